```python
import math
import jax, jax.numpy as jnp
from jax import lax
import numpy as np

D_MODEL = 1024
BATCH = 4
SEQ = 4096
DEPTH = 1

MLA_HEADS = 8
MLA_Q_RANK = 256
MLA_KV_RANK = 128
MLA_NOPE = 64
MLA_ROPE = 32
MLA_V = 64
SWA_Q_HEADS = 16
SWA_KV_HEADS = 2
SWA_HEAD = 64
WINDOW = 128
BLOCK = 128
REL_BUCKETS = 32
REL_MAX_DIST = 128
D_FF = 4 * D_MODEL
ROPE_THETA = 10000.0
EPS = 1e-6

IN_SIZES = (MLA_Q_RANK, MLA_KV_RANK, MLA_ROPE,
            SWA_Q_HEADS * SWA_HEAD, SWA_KV_HEADS * SWA_HEAD, SWA_KV_HEADS * SWA_HEAD,
            D_MODEL, D_MODEL)
D_IN = sum(IN_SIZES)

kernel_name = "hybrid_mla_swa_gated_adaln_block"


def rmsnorm(x, g):
    xf = x.astype(jnp.float32)
    y = xf * lax.rsqrt(jnp.mean(xf * xf, axis=-1, keepdims=True) + EPS)
    return (y * g.astype(jnp.float32)).astype(x.dtype)


def modulate(h, shift, scale):
    return h * (1 + scale[:, None, :]) + shift[:, None, :]


def apply_rope(t, pos):
    half = t.shape[-1] // 2
    inv = ROPE_THETA ** (-jnp.arange(half, dtype=jnp.float32) / half)
    ang = pos.astype(jnp.float32)[..., None] * inv
    cos = jnp.cos(ang)[:, :, None, :]
    sin = jnp.sin(ang)[:, :, None, :]
    t1 = t[..., :half].astype(jnp.float32)
    t2 = t[..., half:].astype(jnp.float32)
    return jnp.concatenate([t1 * cos - t2 * sin, t1 * sin + t2 * cos], axis=-1).astype(t.dtype)


def rel_bucket(rel):
    n = jnp.maximum(rel, 0)
    max_exact = REL_BUCKETS // 2
    nf = jnp.maximum(n, 1).astype(jnp.float32)
    large = max_exact + (jnp.log(nf / max_exact) / math.log(REL_MAX_DIST / max_exact)
                         * (REL_BUCKETS - max_exact)).astype(jnp.int32)
    large = jnp.minimum(large, REL_BUCKETS - 1)
    return jnp.where(n < max_exact, n, large)


def mla_attention(q_lat, kv_lat, k_rope_raw, pos, g_q, g_kv, w_uq, w_ukv):
    B, S, _ = q_lat.shape
    q = jnp.einsum('bsr,rhd->bshd', rmsnorm(q_lat, g_q), w_uq)
    q_nope = q[..., :MLA_NOPE]
    q_pe = apply_rope(q[..., MLA_NOPE:], pos)
    kv = jnp.einsum('bsr,rhd->bshd', rmsnorm(kv_lat, g_kv), w_ukv)
    k_nope = kv[..., :MLA_NOPE]
    v = kv[..., MLA_NOPE:]
    k_pe = apply_rope(k_rope_raw[:, :, None, :], pos)[:, :, 0, :]
    scale = (MLA_NOPE + MLA_ROPE) ** -0.5
    nblk = S // BLOCK
    qn = q_nope.reshape(B, nblk, BLOCK, MLA_HEADS, MLA_NOPE).transpose(1, 0, 2, 3, 4)
    qp = q_pe.reshape(B, nblk, BLOCK, MLA_HEADS, MLA_ROPE).transpose(1, 0, 2, 3, 4)
    k_idx = jnp.arange(S)

    def one_block(args):
        i, qn_i, qp_i = args
        s = (jnp.einsum('bqhd,bkhd->bhqk', qn_i, k_nope)
             + jnp.einsum('bqhd,bkd->bhqk', qp_i, k_pe)).astype(jnp.float32) * scale
        q_idx = i * BLOCK + jnp.arange(BLOCK)
        mask = k_idx[None, :] <= q_idx[:, None]
        s = jnp.where(mask[None, None], s, -jnp.inf)
        p = jax.nn.softmax(s, axis=-1).astype(v.dtype)
        return jnp.einsum('bhqk,bkhd->bqhd', p, v)

    out = lax.map(one_block, (jnp.arange(nblk), qn, qp))
    return out.transpose(1, 0, 2, 3, 4).reshape(B, S, MLA_HEADS * MLA_V)


def swa_attention(q, k, v, pos, sinks, rel_bias):
    B, S = q.shape[0], q.shape[1]
    nblk = S // BLOCK
    G = SWA_Q_HEADS // SWA_KV_HEADS
    qb = q.reshape(B, nblk, BLOCK, SWA_KV_HEADS, G, SWA_HEAD)

    def band(t):
        tb = t.reshape((B, nblk, BLOCK) + t.shape[2:])
        pad = [(0, 0), (1, 0)] + [(0, 0)] * (tb.ndim - 2)
        prev = jnp.pad(tb, pad)[:, :-1]
        return jnp.concatenate([prev, tb], axis=2)

    kb, vb = band(k), band(v)
    s = jnp.einsum('bnqhgd,bnkhd->bnhgqk', qb, kb).astype(jnp.float32) * (SWA_HEAD ** -0.5)
    pq = pos.reshape(B, nblk, BLOCK)
    pk = band(pos)
    bucket = rel_bucket(pq[..., :, None] - pk[..., None, :])
    bias = rel_bias[:, bucket]
    bias = bias.reshape((SWA_KV_HEADS, G) + bias.shape[1:]).transpose(2, 3, 0, 1, 4, 5)
    s = s + bias.astype(jnp.float32)
    a = jnp.arange(BLOCK)[:, None]
    b = jnp.arange(2 * BLOCK)[None, :]
    dist = BLOCK + a - b
    band_ok = (dist >= 0) & (dist < WINDOW)
    blk = jnp.arange(nblk)[:, None, None]
    k_ok = (blk * BLOCK - BLOCK + b[None]) >= 0
    mask = band_ok[None] & k_ok
    s = jnp.where(mask[None, :, None, None], s, -jnp.inf)
    sink = jnp.broadcast_to(
        sinks.reshape(SWA_KV_HEADS, G)[None, None, :, :, None, None].astype(jnp.float32),
        s.shape[:-1] + (1,))
    p = jax.nn.softmax(jnp.concatenate([s, sink], axis=-1), axis=-1)[..., :-1]
    out = jnp.einsum('bnhgqk,bnkhd->bnqhgd', p.astype(vb.dtype), vb)
    return out.reshape(B, S, SWA_Q_HEADS * SWA_HEAD)


def setup_inputs(seed: int = 0) -> dict:
    key = jax.random.key(seed)
    ks = jax.random.split(key, 24)
    f32 = jnp.float32
    nrm = lambda k, shape, s: jax.random.normal(k, shape, f32) * s
    x = jax.random.normal(ks[0], (BATCH, SEQ, D_MODEL), f32)
    c = jax.random.normal(ks[1], (BATCH, D_MODEL), f32)
    start = jax.random.randint(ks[2], (BATCH, 1), 0, 1024, dtype=jnp.int32)
    positions = start + jnp.arange(SEQ, dtype=jnp.int32)[None, :]
    return {
        "x": x,
        "c": c,
        "positions": positions,
        "rel_bias": nrm(ks[3], (SWA_Q_HEADS, REL_BUCKETS), 0.5),
        "ada_w": nrm(ks[4], (DEPTH, D_MODEL, 6 * D_MODEL), 0.5 * D_MODEL ** -0.5),
        "ada_b": nrm(ks[5], (DEPTH, 6 * D_MODEL), 0.02),
        "ln_mix_g": 1.0 + nrm(ks[6], (DEPTH, D_MODEL), 0.01),
        "w_in": nrm(ks[7], (DEPTH, D_MODEL, D_IN), D_MODEL ** -0.5),
        "b_gate": nrm(ks[8], (DEPTH, 2 * D_MODEL), 0.02),
        "mla_q_norm_g": 1.0 + nrm(ks[9], (DEPTH, MLA_Q_RANK), 0.01),
        "mla_kv_norm_g": 1.0 + nrm(ks[10], (DEPTH, MLA_KV_RANK), 0.01),
        "w_uq": nrm(ks[11], (DEPTH, MLA_Q_RANK, MLA_HEADS, MLA_NOPE + MLA_ROPE), MLA_Q_RANK ** -0.5),
        "w_ukv": nrm(ks[12], (DEPTH, MLA_KV_RANK, MLA_HEADS, MLA_NOPE + MLA_V), MLA_KV_RANK ** -0.5),
        "swa_sinks": nrm(ks[13], (DEPTH, SWA_Q_HEADS), 0.5),
        "w_o_mla": nrm(ks[14], (DEPTH, MLA_HEADS * MLA_V, D_MODEL), (MLA_HEADS * MLA_V) ** -0.5),
        "w_o_swa": nrm(ks[15], (DEPTH, SWA_Q_HEADS * SWA_HEAD, D_MODEL), (SWA_Q_HEADS * SWA_HEAD) ** -0.5),
        "w_o": nrm(ks[16], (DEPTH, D_MODEL, D_MODEL), D_MODEL ** -0.5),
        "ln_mlp_g": 1.0 + nrm(ks[17], (DEPTH, D_MODEL), 0.01),
        "w_ff1": nrm(ks[18], (DEPTH, D_MODEL, D_FF), D_MODEL ** -0.5),
        "w_ff2": nrm(ks[19], (DEPTH, D_FF, D_MODEL), D_FF ** -0.5),
        "ln_final_g": 1.0 + nrm(ks[20], (D_MODEL,), 0.01),
    }


def reference(x, c, positions, rel_bias, ada_w, ada_b, ln_mix_g, w_in, b_gate,
              mla_q_norm_g, mla_kv_norm_g, w_uq, w_ukv, swa_sinks, w_o_mla, w_o_swa, w_o,
              ln_mlp_g, w_ff1, w_ff2, ln_final_g):
    B, S, D = x.shape
    HQ, HKV = SWA_Q_HEADS, SWA_KV_HEADS
    bounds = []
    acc = 0
    for sz in IN_SIZES[:-1]:
        acc += sz
        bounds.append(acc)
    for l in range(DEPTH):
        mod = jax.nn.silu(c) @ ada_w[l] + ada_b[l]
        sh1, sc1, ga1, sh2, sc2, ga2 = jnp.split(mod, 6, axis=-1)

        h = modulate(rmsnorm(x, ln_mix_g[l]), sh1, sc1)
        proj = h @ w_in[l]
        q_lat, kv_lat, k_rope, q_s, k_s, v_s, gl_a, gl_b = jnp.split(proj, bounds, axis=-1)

        y_mla = mla_attention(q_lat, kv_lat, k_rope, positions,
                              mla_q_norm_g[l], mla_kv_norm_g[l], w_uq[l], w_ukv[l])
        y_swa = swa_attention(q_s.reshape(B, S, HQ, SWA_HEAD),
                              k_s.reshape(B, S, HKV, SWA_HEAD),
                              v_s.reshape(B, S, HKV, SWA_HEAD),
                              positions, swa_sinks[l], rel_bias)
        g_a = jax.nn.sigmoid(gl_a + b_gate[l, :D])
        g_b = jax.nn.sigmoid(gl_b + b_gate[l, D:])
        merged = g_a * (y_mla @ w_o_mla[l]) + g_b * (y_swa @ w_o_swa[l])
        x = x + ga1[:, None, :] * (merged @ w_o[l])

        h = modulate(rmsnorm(x, ln_mlp_g[l]), sh2, sc2)
        u = jnp.square(jax.nn.relu(h @ w_ff1[l]))
        x = x + ga2[:, None, :] * (u @ w_ff2[l])
    return rmsnorm(x, ln_final_g)
```

```python
import functools
import math

import jax
import jax.numpy as jnp
from jax import lax
from jax.experimental import pallas as pl
from jax.experimental.pallas import tpu as pltpu

F32 = jnp.float32
BF16 = jnp.bfloat16

D_MODEL = 1024
MLA_HEADS = 8
MLA_Q_RANK = 256
MLA_KV_RANK = 128
MLA_NOPE = 64
MLA_ROPE = 32
MLA_V = 64
SWA_Q_HEADS = 16
SWA_KV_HEADS = 2
SWA_HEAD = 64
WINDOW = 128
REL_BUCKETS = 32
REL_MAX_DIST = 128
D_FF = 4 * D_MODEL
ROPE_THETA = 10000.0
EPS = 1e-6

LANES = 128
HALF_ROPE = MLA_ROPE // 2
ROPE_A = MLA_NOPE
ROPE_B = MLA_NOPE + HALF_ROPE
N_PAIRS = SWA_Q_HEADS // 2

C_QLAT = 0
C_KVLAT = C_QLAT + MLA_Q_RANK
C_KR = C_KVLAT + MLA_KV_RANK
C_KRS = C_KR + LANES
C_QS = C_KRS + LANES
C_KS = C_QS + SWA_Q_HEADS * SWA_HEAD
C_VS = C_KS + SWA_KV_HEADS * SWA_HEAD
N_PROJ = C_VS + SWA_KV_HEADS * SWA_HEAD

VMEM_LIMIT = 56 * 1024 * 1024

TM_PROJ = 512
TQ_MLA = 512
TK_MLA = 512
TM_TAIL = 256
FF_CHUNK = 1024


def _rms(x, g):
    return x * lax.rsqrt(jnp.mean(x * x, axis=-1, keepdims=True) + EPS) * g


def _dot(a, b):
    return jnp.dot(a, b, preferred_element_type=F32)


def _dot_nt(a, b):
    return lax.dot_general(a, b, (((1,), (1,)), ((), ())), preferred_element_type=F32)


def _adaln_kernel(c_ref, w_ref, b_ref, o_ref):
    c = c_ref[...]
    act = (c * jax.nn.sigmoid(c)).astype(BF16)
    o_ref[...] = _dot(act, w_ref[...].astype(BF16)) + b_ref[...]


def _adaln(c_pad, ada_w, ada_b):
    rows = c_pad.shape[0]
    n_out = ada_w.shape[1]
    return pl.pallas_call(
        _adaln_kernel,
        grid=(n_out // D_MODEL,),
        in_specs=[pl.BlockSpec((rows, D_MODEL), lambda j: (0, 0)),
                  pl.BlockSpec((D_MODEL, D_MODEL), lambda j: (0, j)),
                  pl.BlockSpec((1, D_MODEL), lambda j: (0, j))],
        out_specs=pl.BlockSpec((rows, D_MODEL), lambda j: (0, j)),
        out_shape=jax.ShapeDtypeStruct((rows, n_out), F32),
        compiler_params=pltpu.CompilerParams(dimension_semantics=("arbitrary",)),
        name="adaln",
    )(c_pad, ada_w, ada_b)


def _swa_bias_kernel(rb_ref, o_ref):
    a = lax.broadcasted_iota(jnp.int32, (WINDOW, 2 * WINDOW), 0)
    b = lax.broadcasted_iota(jnp.int32, (WINDOW, 2 * WINDOW), 1)
    dist = WINDOW + a - b
    n = jnp.maximum(dist, 0)
    max_exact = REL_BUCKETS // 2
    nf = jnp.maximum(n, 1).astype(F32)
    large = max_exact + (jnp.log(nf / max_exact) / math.log(REL_MAX_DIST / max_exact)
                         * (REL_BUCKETS - max_exact)).astype(jnp.int32)
    large = jnp.minimum(large, REL_BUCKETS - 1)
    bucket = jnp.where(n < max_exact, n, large)
    band_ok = (dist >= 0) & (dist < WINDOW)
    has_prev = b >= WINDOW
    neg = jnp.float32(-jnp.inf)
    for head in range(SWA_Q_HEADS):
        bias = jnp.zeros((WINDOW, 2 * WINDOW), F32)
        for k in range(REL_BUCKETS):
            bias = jnp.where(bucket == k, rb_ref[head, k], bias)
        pair, parity = head // 2, head % 2
        cols = slice(parity * 2 * WINDOW, (parity + 1) * 2 * WINDOW)
        o_ref[0, pair, :, cols] = jnp.where(band_ok & has_prev, bias, neg)
        o_ref[1, pair, :, cols] = jnp.where(band_ok, bias, neg)


def _swa_bias(rel_bias):
    return pl.pallas_call(
        _swa_bias_kernel,
        in_specs=[pl.BlockSpec(memory_space=pltpu.SMEM)],
        out_specs=pl.BlockSpec(memory_space=pltpu.VMEM),
        out_shape=jax.ShapeDtypeStruct((2, N_PAIRS, WINDOW, 4 * WINDOW), F32),
        name="swa_bias",
    )(rel_bias)


def _proj_kernel(x_ref, sh_ref, sc_ref, g_ref, pos_ref, inv_ref, w1_ref, gq_ref, gkv_ref, wq_ref, wkv_ref,
                 qm_ref, km_ref, vm_ref, qs_ref, ks_ref, vs_ref):
    h = _rms(x_ref[...], g_ref[...]) * (1 + sc_ref[...]) + sh_ref[...]
    p = _dot(h.astype(BF16), w1_ref[...])

    ang = inv_ref[...] * pos_ref[...].astype(F32)
    row = lax.broadcasted_iota(jnp.int32, (HALF_ROPE, LANES), 0)
    lane = lax.broadcasted_iota(jnp.int32, (HALF_ROPE, LANES), 1)
    spread = ((lane == row + ROPE_A) | (lane == row + ROPE_B)).astype(F32)
    cos_f = jnp.dot(jnp.cos(ang).T, spread, precision=lax.Precision.HIGHEST, preferred_element_type=F32)
    sin_f = jnp.dot(jnp.sin(ang).T, spread, precision=lax.Precision.HIGHEST, preferred_element_type=F32)
    lane_t = lax.broadcasted_iota(jnp.int32, cos_f.shape, 1)
    cos_q = jnp.where(lane_t < MLA_NOPE, 1.0, cos_f)

    qn = _rms(p[:, C_QLAT:C_QLAT + MLA_Q_RANK], gq_ref[...]).astype(BF16)
    qq = _dot(qn, wq_ref[...])
    scale = (MLA_NOPE + MLA_ROPE) ** -0.5
    half = MLA_HEADS * LANES
    for hd in range(MLA_HEADS):
        q = qq[:, hd * LANES:(hd + 1) * LANES] * cos_q + qq[:, half + hd * LANES:half + (hd + 1) * LANES] * sin_f
        qm_ref[hd] = (q * scale).astype(BF16)

    kvn = _rms(p[:, C_KVLAT:C_KVLAT + MLA_KV_RANK], gkv_ref[...]).astype(BF16)
    kk = _dot(kvn, wkv_ref[...])
    k_pe = p[:, C_KR:C_KR + LANES] * cos_f + p[:, C_KRS:C_KRS + LANES] * sin_f
    for hd in range(MLA_HEADS):
        km_ref[hd] = (kk[:, hd * LANES:(hd + 1) * LANES] + k_pe).astype(BF16)
    for pr in range(MLA_HEADS // 2):
        vm_ref[pr] = kk[:, half + pr * LANES:half + (pr + 1) * LANES].astype(BF16)

    qs_ref[...] = p[:, C_QS:C_KS].astype(BF16)
    ks_ref[...] = p[:, C_KS:C_VS].astype(BF16)
    vs_ref[...] = p[:, C_VS:N_PROJ].astype(BF16)


def _proj(x, sh1, sc1, ln_g, pos3, inv, w1, gq, gkv, wq, wkv):
    B, S, D = x.shape
    tm = TM_PROJ
    const = lambda shape: pl.BlockSpec(shape, lambda b, i: (0,) * len(shape))
    return pl.pallas_call(
        _proj_kernel,
        grid=(B, S // tm),
        in_specs=[pl.BlockSpec((None, tm, D), lambda b, i: (b, i, 0)),
                  pl.BlockSpec((None, 1, D), lambda b, i: (b, 0, 0)),
                  pl.BlockSpec((None, 1, D), lambda b, i: (b, 0, 0)),
                  const((1, D)),
                  pl.BlockSpec((None, 1, tm), lambda b, i: (b, 0, i)),
                  const((HALF_ROPE, 1)),
                  const(w1.shape), const(gq.shape), const(gkv.shape), const(wq.shape), const(wkv.shape)],
        out_specs=[pl.BlockSpec((None, MLA_HEADS, tm, LANES), lambda b, i: (b, 0, i, 0)),
                   pl.BlockSpec((None, MLA_HEADS, tm, LANES), lambda b, i: (b, 0, i, 0)),
                   pl.BlockSpec((None, MLA_HEADS // 2, tm, LANES), lambda b, i: (b, 0, i, 0)),
                   pl.BlockSpec((None, tm, SWA_Q_HEADS * SWA_HEAD), lambda b, i: (b, i, 0)),
                   pl.BlockSpec((None, tm, LANES), lambda b, i: (b, i, 0)),
                   pl.BlockSpec((None, tm, LANES), lambda b, i: (b, i, 0))],
        out_shape=[jax.ShapeDtypeStruct((B, MLA_HEADS, S, LANES), BF16),
                   jax.ShapeDtypeStruct((B, MLA_HEADS, S, LANES), BF16),
                   jax.ShapeDtypeStruct((B, MLA_HEADS // 2, S, LANES), BF16),
                   jax.ShapeDtypeStruct((B, S, SWA_Q_HEADS * SWA_HEAD), BF16),
                   jax.ShapeDtypeStruct((B, S, LANES), BF16),
                   jax.ShapeDtypeStruct((B, S, LANES), BF16)],
        compiler_params=pltpu.CompilerParams(dimension_semantics=("arbitrary", "arbitrary"),
                                             vmem_limit_bytes=VMEM_LIMIT),
        name="proj",
    )(x, sh1, sc1, ln_g, pos3, inv, w1, gq, gkv, wq, wkv)


def _mla_kernel(q_ref, k_ref, v_ref, o_ref):
    tq, tk = TQ_MLA, TK_MLA
    i = pl.program_id(2)
    q = [q_ref[0], q_ref[1]]

    def step(j, carry, masked):
        off = pl.multiple_of(j * tk, tk)
        v = v_ref[pl.ds(off, tk), :]
        out = []
        for hd in range(2):
            m, l, acc = carry[hd]
            s = _dot_nt(q[hd], k_ref[hd, pl.ds(off, tk), :])
            if masked:
                r = lax.broadcasted_iota(jnp.int32, (tq, tk), 0)
                c = lax.broadcasted_iota(jnp.int32, (tq, tk), 1)
                s = jnp.where(c + (j * tk - i * tq) <= r, s, -jnp.inf)
            m_new = jnp.maximum(m, jnp.max(s, axis=-1, keepdims=True))
            alpha = jnp.exp(m - m_new)
            p = jnp.exp(s - m_new)
            l = alpha * l + jnp.sum(p, axis=-1, keepdims=True)
            acc = alpha * acc + _dot(p.astype(BF16), v)
            out.append((m_new, l, acc))
        return tuple(out)

    init = tuple((jnp.full((tq, 1), -jnp.inf, F32), jnp.zeros((tq, 1), F32), jnp.zeros((tq, LANES), F32))
                 for _ in range(2))
    n_full = i * (tq // tk)
    carry = lax.fori_loop(0, n_full, lambda j, c: step(j, c, False), init)
    for d in range(tq // tk):
        carry = step(n_full + d, carry, True)
    lane = lax.broadcasted_iota(jnp.int32, (tq, LANES), 1)
    o0 = carry[0][2] / carry[0][1]
    o1 = carry[1][2] / carry[1][1]
    o_ref[...] = jnp.where(lane < MLA_V, o0, o1).astype(BF16)


def _mla(qm, km, vm):
    B, H, S, _ = qm.shape
    tq = TQ_MLA
    return pl.pallas_call(
        _mla_kernel,
        grid=(B, H // 2, S // tq),
        in_specs=[pl.BlockSpec((None, 2, tq, LANES), lambda b, p, i: (b, p, i, 0)),
                  pl.BlockSpec((None, 2, S, LANES), lambda b, p, i: (b, p, 0, 0)),
                  pl.BlockSpec((None, None, S, LANES), lambda b, p, i: (b, p, 0, 0))],
        out_specs=pl.BlockSpec((None, tq, LANES), lambda b, p, i: (b, i, p)),
        out_shape=jax.ShapeDtypeStruct((B, S, H * MLA_V), BF16),
        compiler_params=pltpu.CompilerParams(dimension_semantics=("arbitrary",) * 3,
                                             vmem_limit_bytes=VMEM_LIMIT),
        name="mla",
    )(qm, km, vm)


def _swa_kernel(sink_ref, q_ref, kp_ref, kc_ref, vp_ref, vc_ref, bias_ref, o_ref):
    kband = jnp.concatenate([kp_ref[...], kc_ref[...]], axis=0).astype(F32)
    vband = jnp.concatenate([vp_ref[...], vc_ref[...]], axis=0).astype(F32)
    lane = lax.broadcasted_iota(jnp.int32, kband.shape, 1)
    lo = lane < SWA_HEAD
    kswap = pltpu.roll(kband, SWA_HEAD, 1)
    vswap = pltpu.roll(vband, SWA_HEAD, 1)
    lane_o = lax.broadcasted_iota(jnp.int32, (WINDOW, LANES), 1)
    group = SWA_Q_HEADS // SWA_KV_HEADS // 2
    for kvh in range(SWA_KV_HEADS):
        k_lo, k_hi = (kband, kswap) if kvh == 0 else (kswap, kband)
        v_lo, v_hi = (vband, vswap) if kvh == 0 else (vswap, vband)
        kcat = jnp.concatenate([jnp.where(lo, k_lo, 0.0), jnp.where(lo, 0.0, k_hi)], axis=0).astype(BF16)
        vcat = jnp.concatenate([jnp.where(lo, v_lo, 0.0), jnp.where(lo, 0.0, v_hi)], axis=0).astype(BF16)
        for jj in range(group):
            pair = kvh * group + jj
            s = _dot_nt(q_ref[:, pair * LANES:(pair + 1) * LANES], kcat) + bias_ref[pair]
            es, inv_l = [], []
            for parity in range(2):
                sh = s[:, parity * 2 * WINDOW:(parity + 1) * 2 * WINDOW]
                sink = sink_ref[2 * pair + parity]
                m = jnp.maximum(jnp.max(sh, axis=-1, keepdims=True), sink)
                e = jnp.exp(sh - m)
                es.append(e)
                inv_l.append(1.0 / (jnp.sum(e, axis=-1, keepdims=True) + jnp.exp(sink - m)))
            o = _dot(jnp.concatenate(es, axis=1).astype(BF16), vcat)
            o = o * jnp.where(lane_o < SWA_HEAD, inv_l[0], inv_l[1])
            o_ref[:, pair * LANES:(pair + 1) * LANES] = o.astype(BF16)


def _swa(sinks, qs, ks, vs, bias_tbl):
    B, S, W = qs.shape
    nblk = S // WINDOW
    prev = lambda b, n: (b, jnp.maximum(n - 1, 0), 0)
    cur = lambda b, n: (b, n, 0)
    kv_spec = lambda imap: pl.BlockSpec((None, WINDOW, LANES), imap)
    return pl.pallas_call(
        _swa_kernel,
        grid=(B, nblk),
        in_specs=[pl.BlockSpec(memory_space=pltpu.SMEM),
                  pl.BlockSpec((None, WINDOW, W), cur),
                  kv_spec(prev), kv_spec(cur), kv_spec(prev), kv_spec(cur),
                  pl.BlockSpec((None, N_PAIRS, WINDOW, 4 * WINDOW), lambda b, n: (jnp.minimum(n, 1), 0, 0, 0))],
        out_specs=pl.BlockSpec((None, WINDOW, W), cur),
        out_shape=jax.ShapeDtypeStruct((B, S, W), BF16),
        compiler_params=pltpu.CompilerParams(dimension_semantics=("arbitrary", "arbitrary"),
                                             vmem_limit_bytes=VMEM_LIMIT),
        name="swa",
    )(sinks, qs, ks, ks, vs, vs, bias_tbl)


def _tail_kernel(x_ref, ym_ref, ys_ref, sh1_ref, sc1_ref, ga1_ref, sh2_ref, sc2_ref, ga2_ref,
                 gmix_ref, gmlp_ref, gfin_ref, bg_ref, wg_ref, wom_ref, wos_ref, wo_ref, w1_ref, w2_ref, o_ref):
    x = x_ref[...]
    h = (_rms(x, gmix_ref[...]) * (1 + sc1_ref[...]) + sh1_ref[...]).astype(BF16)
    gates = jax.nn.sigmoid(_dot(h, wg_ref[...]) + bg_ref[...])
    merged = (gates[:, :D_MODEL] * _dot(ym_ref[...], wom_ref[...])
              + gates[:, D_MODEL:] * _dot(ys_ref[...], wos_ref[...]))
    x1 = x + ga1_ref[...] * _dot(merged.astype(BF16), wo_ref[...])
    h2 = (_rms(x1, gmlp_ref[...]) * (1 + sc2_ref[...]) + sh2_ref[...]).astype(BF16)
    acc = jnp.zeros_like(x1)
    for c in range(D_FF // FF_CHUNK):
        u = jnp.square(jnp.maximum(_dot(h2, w1_ref[:, c * FF_CHUNK:(c + 1) * FF_CHUNK]), 0.0))
        acc = acc + _dot(u.astype(BF16), w2_ref[c * FF_CHUNK:(c + 1) * FF_CHUNK, :])
    x2 = x1 + ga2_ref[...] * acc
    o_ref[...] = _rms(x2, gfin_ref[...])


def _tail(x, ym, ys, mods, gmix, gmlp, gfin, bg, wg, wom, wos, wo, w1, w2):
    B, S, D = x.shape
    tm = TM_TAIL
    tok = lambda width: pl.BlockSpec((None, tm, width), lambda b, i: (b, i, 0))
    per_b = pl.BlockSpec((None, 1, D), lambda b, i: (b, 0, 0))
    const = lambda a: pl.BlockSpec(a.shape, lambda b, i: (0,) * a.ndim, pipeline_mode=pl.Buffered(1))
    return pl.pallas_call(
        _tail_kernel,
        grid=(B, S // tm),
        in_specs=[tok(D), tok(ym.shape[-1]), tok(ys.shape[-1])] + [per_b] * 6
                 + [const(a) for a in (gmix, gmlp, gfin, bg, wg, wom, wos, wo, w1, w2)],
        out_specs=tok(D),
        out_shape=jax.ShapeDtypeStruct((B, S, D), F32),
        compiler_params=pltpu.CompilerParams(dimension_semantics=("arbitrary", "arbitrary"),
                                             vmem_limit_bytes=VMEM_LIMIT),
        name="tail",
    )(x, ym, ys, *mods, gmix, gmlp, gfin, bg, wg, wom, wos, wo, w1, w2)


def _head_slab(nope, a, b):
    r, h, _ = nope.shape
    pad = jnp.zeros((r, h, LANES - MLA_NOPE - MLA_ROPE), nope.dtype)
    return jnp.concatenate([nope, a, b, pad], axis=-1).reshape(r, h * LANES)


def kernel(x, c, positions, rel_bias, ada_w, ada_b, ln_mix_g, w_in, b_gate, mla_q_norm_g, mla_kv_norm_g,
           w_uq, w_ukv, swa_sinks, w_o_mla, w_o_swa, w_o, ln_mlp_g, w_ff1, w_ff2, ln_final_g):
    B, S, D = x.shape
    assert (B, S, D) == (x.shape[0], 4096, D_MODEL) and ada_w.shape[0] == 1
    l = 0

    c_pad = jnp.pad(c, ((0, 8 - B), (0, 0)))
    mod = _adaln(c_pad, ada_w[l], ada_b[l][None, :])[:B]
    mods = [m[:, None, :] for m in jnp.split(mod, 6, axis=-1)]

    wi = w_in[l]
    o_kv = MLA_Q_RANK
    o_kr = o_kv + MLA_KV_RANK
    o_qs = o_kr + MLA_ROPE
    o_ks = o_qs + SWA_Q_HEADS * SWA_HEAD
    o_vs = o_ks + SWA_KV_HEADS * SWA_HEAD
    o_ga = o_vs + SWA_KV_HEADS * SWA_HEAD
    kr_a, kr_b = wi[:, o_kr:o_kr + HALF_ROPE], wi[:, o_kr + HALF_ROPE:o_qs]
    z = lambda n: jnp.zeros((D, n), wi.dtype)
    w1 = jnp.concatenate([
        wi[:, :o_kr],
        z(ROPE_A), kr_a, kr_b, z(LANES - ROPE_A - MLA_ROPE),
        z(ROPE_A), -kr_b, kr_a, z(LANES - ROPE_A - MLA_ROPE),
        wi[:, o_qs:o_ks] * (SWA_HEAD ** -0.5),
        wi[:, o_ks:o_ga]], axis=1).astype(BF16)
    assert w1.shape[1] == N_PROJ
    wg = wi[:, o_ga:].astype(BF16)

    uq = w_uq[l]
    q_nope, q_a, q_b = uq[..., :MLA_NOPE], uq[..., MLA_NOPE:MLA_NOPE + HALF_ROPE], uq[..., MLA_NOPE + HALF_ROPE:]
    wq = jnp.concatenate([_head_slab(q_nope, q_a, q_b),
                          _head_slab(jnp.zeros_like(q_nope), -q_b, q_a)], axis=1).astype(BF16)
    ukv = w_ukv[l]
    k_nope, v_up = ukv[..., :MLA_NOPE], ukv[..., MLA_NOPE:]
    zr = jnp.zeros(k_nope.shape[:2] + (HALF_ROPE,), ukv.dtype)
    wkv = jnp.concatenate([_head_slab(k_nope, zr, zr),
                           v_up.reshape(MLA_KV_RANK, MLA_HEADS * MLA_V)], axis=1).astype(BF16)

    inv = (ROPE_THETA ** (-jnp.arange(HALF_ROPE, dtype=F32) / HALF_ROPE))[:, None]

    qm, km, vm, qs, ks, vs = _proj(x, mods[0], mods[1], ln_mix_g[l][None, :], positions[:, None, :], inv, w1,
                                   mla_q_norm_g[l][None, :], mla_kv_norm_g[l][None, :], wq, wkv)

    y_mla = _mla(qm, km, vm)
    y_swa = _swa(swa_sinks[l], qs, ks, vs, _swa_bias(rel_bias))

    return _tail(x, y_mla, y_swa, mods, ln_mix_g[l][None, :], ln_mlp_g[l][None, :], ln_final_g[None, :],
                 b_gate[l][None, :], wg, w_o_mla[l].astype(BF16), w_o_swa[l].astype(BF16), w_o[l].astype(BF16),
                 w_ff1[l].astype(BF16), w_ff2[l].astype(BF16))
```

```python
import functools
import math

import jax
import jax.numpy as jnp
from jax import lax
from jax.experimental import pallas as pl
from jax.experimental.pallas import tpu as pltpu

F32 = jnp.float32
BF16 = jnp.bfloat16

D_MODEL = 1024
MLA_HEADS = 8
MLA_Q_RANK = 256
MLA_KV_RANK = 128
MLA_NOPE = 64
MLA_ROPE = 32
MLA_V = 64
SWA_Q_HEADS = 16
SWA_KV_HEADS = 2
SWA_HEAD = 64
WINDOW = 128
REL_BUCKETS = 32
REL_MAX_DIST = 128
D_FF = 4 * D_MODEL
ROPE_THETA = 10000.0
EPS = 1e-6

LANES = 128
HALF_ROPE = MLA_ROPE // 2
ROPE_A = MLA_NOPE
ROPE_B = MLA_NOPE + HALF_ROPE
N_PAIRS = SWA_Q_HEADS // 2
SWA_GROUP = SWA_Q_HEADS // SWA_KV_HEADS

C_QLAT = 0
C_KVLAT = C_QLAT + MLA_Q_RANK
C_KR = C_KVLAT + MLA_KV_RANK
C_KRS = C_KR + LANES
C_QS = C_KRS + LANES
C_KS = C_QS + SWA_Q_HEADS * SWA_HEAD
C_VS = C_KS + SWA_KV_HEADS * SWA_HEAD
N_PROJ = C_VS + SWA_KV_HEADS * SWA_HEAD

VMEM_LIMIT = 56 * 1024 * 1024

TM_PROJ = 512
TQ_MLA = 512
TK_MLA = 512
TM_TAIL = 256
FF_CHUNK = 1024


def _rms(x, g):
    return x * lax.rsqrt(jnp.mean(x * x, axis=-1, keepdims=True) + EPS) * g


def _dot(a, b):
    return jnp.dot(a, b, preferred_element_type=F32)


def _dot_nt(a, b):
    return lax.dot_general(a, b, (((1,), (1,)), ((), ())), preferred_element_type=F32)


def _adaln_kernel(c_ref, w_ref, b_ref, o_ref):
    c = c_ref[...]
    act = (c * jax.nn.sigmoid(c)).astype(BF16)
    o_ref[...] = _dot(act, w_ref[...].astype(BF16)) + b_ref[...]


def _adaln(c_pad, ada_w, ada_b):
    rows = c_pad.shape[0]
    n_out = ada_w.shape[1]
    return pl.pallas_call(
        _adaln_kernel,
        grid=(n_out // D_MODEL,),
        in_specs=[pl.BlockSpec((rows, D_MODEL), lambda j: (0, 0)),
                  pl.BlockSpec((D_MODEL, D_MODEL), lambda j: (0, j)),
                  pl.BlockSpec((1, D_MODEL), lambda j: (0, j))],
        out_specs=pl.BlockSpec((rows, D_MODEL), lambda j: (0, j)),
        out_shape=jax.ShapeDtypeStruct((rows, n_out), F32),
        compiler_params=pltpu.CompilerParams(dimension_semantics=("arbitrary",)),
        name="adaln",
    )(c_pad, ada_w, ada_b)


def _swa_bias_kernel(rb_ref, o_ref):
    a = lax.broadcasted_iota(jnp.int32, (2 * WINDOW, WINDOW), 1)
    b = lax.broadcasted_iota(jnp.int32, (2 * WINDOW, WINDOW), 0)
    dist = WINDOW + a - b
    n = jnp.maximum(dist, 0)
    max_exact = REL_BUCKETS // 2
    nf = jnp.maximum(n, 1).astype(F32)
    large = max_exact + (jnp.log(nf / max_exact) / math.log(REL_MAX_DIST / max_exact)
                         * (REL_BUCKETS - max_exact)).astype(jnp.int32)
    large = jnp.minimum(large, REL_BUCKETS - 1)
    bucket = jnp.where(n < max_exact, n, large)
    band_ok = (dist >= 0) & (dist < WINDOW)
    has_prev = b >= WINDOW
    neg = jnp.float32(-jnp.inf)
    for head in range(SWA_Q_HEADS):
        bias = jnp.zeros((2 * WINDOW, WINDOW), F32)
        for k in range(REL_BUCKETS):
            bias = jnp.where(bucket == k, rb_ref[head, k], bias)
        kvh, pair, parity = head // SWA_GROUP, (head % SWA_GROUP) // 2, head % 2
        rows = slice(parity * 2 * WINDOW, (parity + 1) * 2 * WINDOW)
        cols = slice(pair * WINDOW, (pair + 1) * WINDOW)
        o_ref[0, kvh, rows, cols] = jnp.where(band_ok & has_prev, bias, neg)
        o_ref[1, kvh, rows, cols] = jnp.where(band_ok, bias, neg)


def _swa_bias(rel_bias):
    return pl.pallas_call(
        _swa_bias_kernel,
        in_specs=[pl.BlockSpec(memory_space=pltpu.SMEM)],
        out_specs=pl.BlockSpec(memory_space=pltpu.VMEM),
        out_shape=jax.ShapeDtypeStruct((2, SWA_KV_HEADS, 4 * WINDOW, SWA_GROUP // 2 * WINDOW), F32),
        name="swa_bias",
    )(rel_bias)


def _proj_kernel(x_ref, sh_ref, sc_ref, g_ref, pos_ref, inv_ref, w1_ref, gq_ref, gkv_ref, wq_ref, wkv_ref,
                 qm_ref, km_ref, vm_ref, qs_ref, ks_ref, vs_ref):
    h = _rms(x_ref[...], g_ref[...]) * (1 + sc_ref[...]) + sh_ref[...]
    p = _dot(h.astype(BF16), w1_ref[...])

    ang = inv_ref[...] * pos_ref[...].astype(F32)
    row = lax.broadcasted_iota(jnp.int32, (HALF_ROPE, LANES), 0)
    lane = lax.broadcasted_iota(jnp.int32, (HALF_ROPE, LANES), 1)
    spread = ((lane == row + ROPE_A) | (lane == row + ROPE_B)).astype(F32)
    cos_f = jnp.dot(jnp.cos(ang).T, spread, precision=lax.Precision.HIGHEST, preferred_element_type=F32)
    sin_f = jnp.dot(jnp.sin(ang).T, spread, precision=lax.Precision.HIGHEST, preferred_element_type=F32)
    lane_t = lax.broadcasted_iota(jnp.int32, cos_f.shape, 1)
    cos_q = jnp.where(lane_t < MLA_NOPE, 1.0, cos_f)

    qn = _rms(p[:, C_QLAT:C_QLAT + MLA_Q_RANK], gq_ref[...]).astype(BF16)
    qq = _dot(qn, wq_ref[...])
    scale = (MLA_NOPE + MLA_ROPE) ** -0.5
    half = MLA_HEADS * LANES
    for hd in range(MLA_HEADS):
        q = qq[:, hd * LANES:(hd + 1) * LANES] * cos_q + qq[:, half + hd * LANES:half + (hd + 1) * LANES] * sin_f
        qm_ref[hd] = (q * scale).astype(BF16)

    kvn = _rms(p[:, C_KVLAT:C_KVLAT + MLA_KV_RANK], gkv_ref[...]).astype(BF16)
    kk = _dot(kvn, wkv_ref[...])
    k_pe = p[:, C_KR:C_KR + LANES] * cos_f + p[:, C_KRS:C_KRS + LANES] * sin_f
    for hd in range(MLA_HEADS):
        km_ref[hd] = (kk[:, hd * LANES:(hd + 1) * LANES] + k_pe).astype(BF16)
    for pr in range(MLA_HEADS // 2):
        vm_ref[pr] = kk[:, half + pr * LANES:half + (pr + 1) * LANES].astype(BF16)

    qs_ref[...] = p[:, C_QS:C_KS].astype(BF16)
    ks_ref[...] = p[:, C_KS:C_VS].astype(BF16)
    vs_ref[...] = p[:, C_VS:N_PROJ].astype(BF16)


def _proj(x, sh1, sc1, ln_g, pos3, inv, w1, gq, gkv, wq, wkv):
    B, S, D = x.shape
    tm = TM_PROJ
    const = lambda shape: pl.BlockSpec(shape, lambda b, i: (0,) * len(shape))
    return pl.pallas_call(
        _proj_kernel,
        grid=(B, S // tm),
        in_specs=[pl.BlockSpec((None, tm, D), lambda b, i: (b, i, 0)),
                  pl.BlockSpec((None, 1, D), lambda b, i: (b, 0, 0)),
                  pl.BlockSpec((None, 1, D), lambda b, i: (b, 0, 0)),
                  const((1, D)),
                  pl.BlockSpec((None, 1, tm), lambda b, i: (b, 0, i)),
                  const((HALF_ROPE, 1)),
                  const(w1.shape), const(gq.shape), const(gkv.shape), const(wq.shape), const(wkv.shape)],
        out_specs=[pl.BlockSpec((None, MLA_HEADS, tm, LANES), lambda b, i: (b, 0, i, 0)),
                   pl.BlockSpec((None, MLA_HEADS, tm, LANES), lambda b, i: (b, 0, i, 0)),
                   pl.BlockSpec((None, MLA_HEADS // 2, tm, LANES), lambda b, i: (b, 0, i, 0)),
                   pl.BlockSpec((None, tm, SWA_Q_HEADS * SWA_HEAD), lambda b, i: (b, i, 0)),
                   pl.BlockSpec((None, tm, LANES), lambda b, i: (b, i, 0)),
                   pl.BlockSpec((None, tm, LANES), lambda b, i: (b, i, 0))],
        out_shape=[jax.ShapeDtypeStruct((B, MLA_HEADS, S, LANES), BF16),
                   jax.ShapeDtypeStruct((B, MLA_HEADS, S, LANES), BF16),
                   jax.ShapeDtypeStruct((B, MLA_HEADS // 2, S, LANES), BF16),
                   jax.ShapeDtypeStruct((B, S, SWA_Q_HEADS * SWA_HEAD), BF16),
                   jax.ShapeDtypeStruct((B, S, LANES), BF16),
                   jax.ShapeDtypeStruct((B, S, LANES), BF16)],
        compiler_params=pltpu.CompilerParams(dimension_semantics=("arbitrary", "arbitrary"),
                                             vmem_limit_bytes=VMEM_LIMIT),
        name="proj",
    )(x, sh1, sc1, ln_g, pos3, inv, w1, gq, gkv, wq, wkv)


def _mla_kernel(q_ref, k_ref, v_ref, o_ref):
    tq, tk = TQ_MLA, TK_MLA
    i = pl.program_id(2)
    q = [q_ref[0], q_ref[1]]

    def step(j, carry, masked):
        off = pl.multiple_of(j * tk, tk)
        v = v_ref[pl.ds(off, tk), :]
        out = []
        for hd in range(2):
            m, l, acc = carry[hd]
            s = _dot_nt(q[hd], k_ref[hd, pl.ds(off, tk), :])
            if masked:
                r = lax.broadcasted_iota(jnp.int32, (tq, tk), 0)
                c = lax.broadcasted_iota(jnp.int32, (tq, tk), 1)
                s = jnp.where(c + (j * tk - i * tq) <= r, s, -jnp.inf)
            m_new = jnp.maximum(m, jnp.max(s, axis=-1, keepdims=True))
            alpha = jnp.exp(m - m_new)
            p = jnp.exp(s - m_new)
            l = alpha * l + jnp.sum(p, axis=-1, keepdims=True)
            acc = alpha * acc + _dot(p.astype(BF16), v)
            out.append((m_new, l, acc))
        return tuple(out)

    init = tuple((jnp.full((tq, 1), -jnp.inf, F32), jnp.zeros((tq, 1), F32), jnp.zeros((tq, LANES), F32))
                 for _ in range(2))
    n_full = i * (tq // tk)
    carry = lax.fori_loop(0, n_full, lambda j, c: step(j, c, False), init)
    for d in range(tq // tk):
        carry = step(n_full + d, carry, True)
    lane = lax.broadcasted_iota(jnp.int32, (tq, LANES), 1)
    o0 = carry[0][2] / carry[0][1]
    o1 = carry[1][2] / carry[1][1]
    o_ref[...] = jnp.where(lane < MLA_V, o0, o1).astype(BF16)


def _mla(qm, km, vm):
    B, H, S, _ = qm.shape
    tq = TQ_MLA
    return pl.pallas_call(
        _mla_kernel,
        grid=(B, H // 2, S // tq),
        in_specs=[pl.BlockSpec((None, 2, tq, LANES), lambda b, p, i: (b, p, i, 0)),
                  pl.BlockSpec((None, 2, S, LANES), lambda b, p, i: (b, p, 0, 0)),
                  pl.BlockSpec((None, None, S, LANES), lambda b, p, i: (b, p, 0, 0))],
        out_specs=pl.BlockSpec((None, tq, LANES), lambda b, p, i: (b, i, p)),
        out_shape=jax.ShapeDtypeStruct((B, S, H * MLA_V), BF16),
        compiler_params=pltpu.CompilerParams(dimension_semantics=("arbitrary",) * 3,
                                             vmem_limit_bytes=VMEM_LIMIT),
        name="mla",
    )(qm, km, vm)


def _swa_kernel(sink_ref, q_ref, kp_ref, kc_ref, vp_ref, vc_ref, bias_ref, o_ref):
    band = 2 * WINDOW
    kband = jnp.concatenate([kp_ref[...], kc_ref[...]], axis=0).astype(F32)
    vband_t = jnp.concatenate([vp_ref[...], vc_ref[...]], axis=0).astype(F32).T
    lo = lax.broadcasted_iota(jnp.int32, kband.shape, 1) < SWA_HEAD
    kswap = pltpu.roll(kband, SWA_HEAD, 1)
    pairs = SWA_GROUP // 2
    for kvh in range(SWA_KV_HEADS):
        k_lo, k_hi = (kband, kswap) if kvh == 0 else (kswap, kband)
        kcat = jnp.concatenate([jnp.where(lo, k_lo, 0.0), jnp.where(lo, 0.0, k_hi)], axis=0).astype(BF16)
        qstack = jnp.concatenate([q_ref[:, (kvh * pairs + jj) * LANES:(kvh * pairs + jj + 1) * LANES]
                                  for jj in range(pairs)], axis=0)
        st = _dot_nt(kcat, qstack) + bias_ref[kvh]
        v_t = vband_t[kvh * SWA_HEAD:(kvh + 1) * SWA_HEAD, :].astype(BF16)
        halves = []
        for parity in range(2):
            ps, inv_l = [], []
            for jj in range(pairs):
                blk = st[parity * band:(parity + 1) * band, jj * WINDOW:(jj + 1) * WINDOW]
                sink = sink_ref[kvh * SWA_GROUP + 2 * jj + parity]
                m = jnp.maximum(jnp.max(blk, axis=0, keepdims=True), sink)
                e = jnp.exp(blk - m)
                inv_l.append(1.0 / (jnp.sum(e, axis=0, keepdims=True) + jnp.exp(sink - m)))
                ps.append(e.astype(BF16))
            halves.append(_dot(v_t, jnp.concatenate(ps, axis=1)) * jnp.concatenate(inv_l, axis=1))
        o_t = jnp.concatenate(halves, axis=0)
        for jj in range(pairs):
            pair = kvh * pairs + jj
            o_ref[:, pair * LANES:(pair + 1) * LANES] = o_t[:, jj * WINDOW:(jj + 1) * WINDOW].T.astype(BF16)


def _swa(sinks, qs, ks, vs, bias_tbl):
    B, S, W = qs.shape
    nblk = S // WINDOW
    prev = lambda b, n: (b, jnp.maximum(n - 1, 0), 0)
    cur = lambda b, n: (b, n, 0)
    kv_spec = lambda imap: pl.BlockSpec((None, WINDOW, LANES), imap)
    return pl.pallas_call(
        _swa_kernel,
        grid=(B, nblk),
        in_specs=[pl.BlockSpec(memory_space=pltpu.SMEM),
                  pl.BlockSpec((None, WINDOW, W), cur),
                  kv_spec(prev), kv_spec(cur), kv_spec(prev), kv_spec(cur),
                  pl.BlockSpec((None,) + bias_tbl.shape[1:], lambda b, n: (jnp.minimum(n, 1), 0, 0, 0))],
        out_specs=pl.BlockSpec((None, WINDOW, W), cur),
        out_shape=jax.ShapeDtypeStruct((B, S, W), BF16),
        compiler_params=pltpu.CompilerParams(dimension_semantics=("arbitrary", "arbitrary"),
                                             vmem_limit_bytes=VMEM_LIMIT),
        name="swa",
    )(sinks, qs, ks, ks, vs, vs, bias_tbl)


def _tail_kernel(x_ref, ym_ref, ys_ref, sh1_ref, sc1_ref, ga1_ref, sh2_ref, sc2_ref, ga2_ref,
                 gmix_ref, gmlp_ref, gfin_ref, bg_ref, wg_ref, wom_ref, wos_ref, wo_ref, w1_ref, w2_ref, o_ref):
    x = x_ref[...]
    h = (_rms(x, gmix_ref[...]) * (1 + sc1_ref[...]) + sh1_ref[...]).astype(BF16)
    gates = jax.nn.sigmoid(_dot(h, wg_ref[...]) + bg_ref[...])
    merged = (gates[:, :D_MODEL] * _dot(ym_ref[...], wom_ref[...])
              + gates[:, D_MODEL:] * _dot(ys_ref[...], wos_ref[...]))
    x1 = x + ga1_ref[...] * _dot(merged.astype(BF16), wo_ref[...])
    h2 = (_rms(x1, gmlp_ref[...]) * (1 + sc2_ref[...]) + sh2_ref[...]).astype(BF16)
    acc = jnp.zeros_like(x1)
    for c in range(D_FF // FF_CHUNK):
        u = jnp.square(jnp.maximum(_dot(h2, w1_ref[:, c * FF_CHUNK:(c + 1) * FF_CHUNK]), 0.0))
        acc = acc + _dot(u.astype(BF16), w2_ref[c * FF_CHUNK:(c + 1) * FF_CHUNK, :])
    x2 = x1 + ga2_ref[...] * acc
    o_ref[...] = _rms(x2, gfin_ref[...])


def _tail(x, ym, ys, mods, gmix, gmlp, gfin, bg, wg, wom, wos, wo, w1, w2):
    B, S, D = x.shape
    tm = TM_TAIL
    tok = lambda width: pl.BlockSpec((None, tm, width), lambda b, i: (b, i, 0))
    per_b = pl.BlockSpec((None, 1, D), lambda b, i: (b, 0, 0))
    const = lambda a: pl.BlockSpec(a.shape, lambda b, i: (0,) * a.ndim, pipeline_mode=pl.Buffered(1))
    return pl.pallas_call(
        _tail_kernel,
        grid=(B, S // tm),
        in_specs=[tok(D), tok(ym.shape[-1]), tok(ys.shape[-1])] + [per_b] * 6
                 + [const(a) for a in (gmix, gmlp, gfin, bg, wg, wom, wos, wo, w1, w2)],
        out_specs=tok(D),
        out_shape=jax.ShapeDtypeStruct((B, S, D), F32),
        compiler_params=pltpu.CompilerParams(dimension_semantics=("arbitrary", "arbitrary"),
                                             vmem_limit_bytes=VMEM_LIMIT),
        name="tail",
    )(x, ym, ys, *mods, gmix, gmlp, gfin, bg, wg, wom, wos, wo, w1, w2)


def _head_slab(nope, a, b):
    r, h, _ = nope.shape
    pad = jnp.zeros((r, h, LANES - MLA_NOPE - MLA_ROPE), nope.dtype)
    return jnp.concatenate([nope, a, b, pad], axis=-1).reshape(r, h * LANES)


def kernel(x, c, positions, rel_bias, ada_w, ada_b, ln_mix_g, w_in, b_gate, mla_q_norm_g, mla_kv_norm_g,
           w_uq, w_ukv, swa_sinks, w_o_mla, w_o_swa, w_o, ln_mlp_g, w_ff1, w_ff2, ln_final_g):
    B, S, D = x.shape
    assert (B, S, D) == (x.shape[0], 4096, D_MODEL) and ada_w.shape[0] == 1
    l = 0

    c_pad = jnp.pad(c, ((0, 8 - B), (0, 0)))
    mod = _adaln(c_pad, ada_w[l], ada_b[l][None, :])[:B]
    mods = [m[:, None, :] for m in jnp.split(mod, 6, axis=-1)]

    wi = w_in[l]
    o_kv = MLA_Q_RANK
    o_kr = o_kv + MLA_KV_RANK
    o_qs = o_kr + MLA_ROPE
    o_ks = o_qs + SWA_Q_HEADS * SWA_HEAD
    o_vs = o_ks + SWA_KV_HEADS * SWA_HEAD
    o_ga = o_vs + SWA_KV_HEADS * SWA_HEAD
    kr_a, kr_b = wi[:, o_kr:o_kr + HALF_ROPE], wi[:, o_kr + HALF_ROPE:o_qs]
    z = lambda n: jnp.zeros((D, n), wi.dtype)
    w1 = jnp.concatenate([
        wi[:, :o_kr],
        z(ROPE_A), kr_a, kr_b, z(LANES - ROPE_A - MLA_ROPE),
        z(ROPE_A), -kr_b, kr_a, z(LANES - ROPE_A - MLA_ROPE),
        wi[:, o_qs:o_ks] * (SWA_HEAD ** -0.5),
        wi[:, o_ks:o_ga]], axis=1).astype(BF16)
    assert w1.shape[1] == N_PROJ
    wg = wi[:, o_ga:].astype(BF16)

    uq = w_uq[l]
    q_nope, q_a, q_b = uq[..., :MLA_NOPE], uq[..., MLA_NOPE:MLA_NOPE + HALF_ROPE], uq[..., MLA_NOPE + HALF_ROPE:]
    wq = jnp.concatenate([_head_slab(q_nope, q_a, q_b),
                          _head_slab(jnp.zeros_like(q_nope), -q_b, q_a)], axis=1).astype(BF16)
    ukv = w_ukv[l]
    k_nope, v_up = ukv[..., :MLA_NOPE], ukv[..., MLA_NOPE:]
    zr = jnp.zeros(k_nope.shape[:2] + (HALF_ROPE,), ukv.dtype)
    wkv = jnp.concatenate([_head_slab(k_nope, zr, zr),
                           v_up.reshape(MLA_KV_RANK, MLA_HEADS * MLA_V)], axis=1).astype(BF16)

    inv = (ROPE_THETA ** (-jnp.arange(HALF_ROPE, dtype=F32) / HALF_ROPE))[:, None]

    qm, km, vm, qs, ks, vs = _proj(x, mods[0], mods[1], ln_mix_g[l][None, :], positions[:, None, :], inv, w1,
                                   mla_q_norm_g[l][None, :], mla_kv_norm_g[l][None, :], wq, wkv)

    y_mla = _mla(qm, km, vm)
    y_swa = _swa(swa_sinks[l], qs, ks, vs, _swa_bias(rel_bias))

    return _tail(x, y_mla, y_swa, mods, ln_mix_g[l][None, :], ln_mlp_g[l][None, :], ln_final_g[None, :],
                 b_gate[l][None, :], wg, w_o_mla[l].astype(BF16), w_o_swa[l].astype(BF16), w_o[l].astype(BF16),
                 w_ff1[l].astype(BF16), w_ff2[l].astype(BF16))
```

```python
import functools
import math

import jax
import jax.numpy as jnp
from jax import lax
from jax.experimental import pallas as pl
from jax.experimental.pallas import tpu as pltpu

F32 = jnp.float32
BF16 = jnp.bfloat16

D_MODEL = 1024
MLA_HEADS = 8
MLA_Q_RANK = 256
MLA_KV_RANK = 128
MLA_NOPE = 64
MLA_ROPE = 32
MLA_V = 64
SWA_Q_HEADS = 16
SWA_KV_HEADS = 2
SWA_HEAD = 64
WINDOW = 128
REL_BUCKETS = 32
REL_MAX_DIST = 128
D_FF = 4 * D_MODEL
ROPE_THETA = 10000.0
EPS = 1e-6

LANES = 128
HALF_ROPE = MLA_ROPE // 2
ROPE_A = MLA_NOPE
ROPE_B = MLA_NOPE + HALF_ROPE
N_PAIRS = SWA_Q_HEADS // 2
SWA_GROUP = SWA_Q_HEADS // SWA_KV_HEADS

C_QLAT = 0
C_KVLAT = C_QLAT + MLA_Q_RANK
C_KR = C_KVLAT + MLA_KV_RANK
C_KRS = C_KR + LANES
C_QS = C_KRS + LANES
C_KS = C_QS + SWA_Q_HEADS * SWA_HEAD
C_VS = C_KS + SWA_KV_HEADS * SWA_HEAD
N_PROJ = C_VS + SWA_KV_HEADS * SWA_HEAD

VMEM_LIMIT = 56 * 1024 * 1024

TM_PROJ = 512
TQ_MLA = 512
TK_MLA = 256
TM_TAIL = 256
FF_CHUNK = 1024


def _rms(x, g):
    return x * lax.rsqrt(jnp.mean(x * x, axis=-1, keepdims=True) + EPS) * g


def _dot(a, b):
    return jnp.dot(a, b, preferred_element_type=F32)


def _dot_nt(a, b):
    return lax.dot_general(a, b, (((1,), (1,)), ((), ())), preferred_element_type=F32)


def _adaln_kernel(c_ref, w_ref, b_ref, o_ref):
    c = c_ref[...]
    act = (c * jax.nn.sigmoid(c)).astype(BF16)
    o_ref[...] = _dot(act, w_ref[...].astype(BF16)) + b_ref[...]


def _adaln(c_pad, ada_w, ada_b):
    rows = c_pad.shape[0]
    n_out = ada_w.shape[1]
    return pl.pallas_call(
        _adaln_kernel,
        grid=(n_out // D_MODEL,),
        in_specs=[pl.BlockSpec((rows, D_MODEL), lambda j: (0, 0)),
                  pl.BlockSpec((D_MODEL, D_MODEL), lambda j: (0, j)),
                  pl.BlockSpec((1, D_MODEL), lambda j: (0, j))],
        out_specs=pl.BlockSpec((rows, D_MODEL), lambda j: (0, j)),
        out_shape=jax.ShapeDtypeStruct((rows, n_out), F32),
        compiler_params=pltpu.CompilerParams(dimension_semantics=("arbitrary",)),
        name="adaln",
    )(c_pad, ada_w, ada_b)


def _swa_bias_kernel(rb_ref, o_ref):
    a = lax.broadcasted_iota(jnp.int32, (2 * WINDOW, WINDOW), 1)
    b = lax.broadcasted_iota(jnp.int32, (2 * WINDOW, WINDOW), 0)
    dist = WINDOW + a - b
    n = jnp.maximum(dist, 0)
    max_exact = REL_BUCKETS // 2
    nf = jnp.maximum(n, 1).astype(F32)
    large = max_exact + (jnp.log(nf / max_exact) / math.log(REL_MAX_DIST / max_exact)
                         * (REL_BUCKETS - max_exact)).astype(jnp.int32)
    large = jnp.minimum(large, REL_BUCKETS - 1)
    bucket = jnp.where(n < max_exact, n, large)
    band_ok = (dist >= 0) & (dist < WINDOW)
    has_prev = b >= WINDOW
    neg = jnp.float32(-jnp.inf)
    for head in range(SWA_Q_HEADS):
        bias = jnp.zeros((2 * WINDOW, WINDOW), F32)
        for k in range(REL_BUCKETS):
            bias = jnp.where(bucket == k, rb_ref[head, k], bias)
        kvh, pair, parity = head // SWA_GROUP, (head % SWA_GROUP) // 2, head % 2
        rows = slice(parity * 2 * WINDOW, (parity + 1) * 2 * WINDOW)
        cols = slice(pair * WINDOW, (pair + 1) * WINDOW)
        o_ref[0, kvh, rows, cols] = jnp.where(band_ok & has_prev, bias, neg)
        o_ref[1, kvh, rows, cols] = jnp.where(band_ok, bias, neg)


def _swa_bias(rel_bias):
    return pl.pallas_call(
        _swa_bias_kernel,
        in_specs=[pl.BlockSpec(memory_space=pltpu.SMEM)],
        out_specs=pl.BlockSpec(memory_space=pltpu.VMEM),
        out_shape=jax.ShapeDtypeStruct((2, SWA_KV_HEADS, 4 * WINDOW, SWA_GROUP // 2 * WINDOW), F32),
        name="swa_bias",
    )(rel_bias)


def _proj_kernel(x_ref, sh_ref, sc_ref, g_ref, pos_ref, inv_ref, w1_ref, gq_ref, gkv_ref, wq_ref, wkv_ref,
                 qm_ref, km_ref, vm_ref, qs_ref, ks_ref, vs_ref):
    h = _rms(x_ref[...], g_ref[...]) * (1 + sc_ref[...]) + sh_ref[...]
    p = _dot(h.astype(BF16), w1_ref[...])

    ang = inv_ref[...] * pos_ref[...].astype(F32)
    row = lax.broadcasted_iota(jnp.int32, (HALF_ROPE, LANES), 0)
    lane = lax.broadcasted_iota(jnp.int32, (HALF_ROPE, LANES), 1)
    spread = ((lane == row + ROPE_A) | (lane == row + ROPE_B)).astype(F32)
    cos_f = jnp.dot(jnp.cos(ang).T, spread, precision=lax.Precision.HIGHEST, preferred_element_type=F32)
    sin_f = jnp.dot(jnp.sin(ang).T, spread, precision=lax.Precision.HIGHEST, preferred_element_type=F32)
    lane_t = lax.broadcasted_iota(jnp.int32, cos_f.shape, 1)
    cos_q = jnp.where(lane_t < MLA_NOPE, 1.0, cos_f)

    qn = _rms(p[:, C_QLAT:C_QLAT + MLA_Q_RANK], gq_ref[...]).astype(BF16)
    qq = _dot(qn, wq_ref[...])
    scale = (MLA_NOPE + MLA_ROPE) ** -0.5
    half = MLA_HEADS * LANES
    for hd in range(MLA_HEADS):
        q = qq[:, hd * LANES:(hd + 1) * LANES] * cos_q + qq[:, half + hd * LANES:half + (hd + 1) * LANES] * sin_f
        qm_ref[hd] = (q * scale).astype(BF16)

    kvn = _rms(p[:, C_KVLAT:C_KVLAT + MLA_KV_RANK], gkv_ref[...]).astype(BF16)
    kk = _dot(kvn, wkv_ref[...])
    k_pe = p[:, C_KR:C_KR + LANES] * cos_f + p[:, C_KRS:C_KRS + LANES] * sin_f
    for hd in range(MLA_HEADS):
        km_ref[hd] = (kk[:, hd * LANES:(hd + 1) * LANES] + k_pe).astype(BF16)
    for pr in range(MLA_HEADS // 2):
        v_t = kk[:, half + pr * LANES:half + (pr + 1) * LANES].T.astype(BF16)
        for t in range(v_t.shape[1] // TK_MLA):
            vm_ref[pr, t] = v_t[:, t * TK_MLA:(t + 1) * TK_MLA]

    qs_ref[...] = p[:, C_QS:C_KS].astype(BF16)
    ks_ref[...] = p[:, C_KS:C_VS].astype(BF16)
    vs_ref[...] = p[:, C_VS:N_PROJ].astype(BF16)


def _proj(x, sh1, sc1, ln_g, pos3, inv, w1, gq, gkv, wq, wkv):
    B, S, D = x.shape
    tm = TM_PROJ
    const = lambda shape: pl.BlockSpec(shape, lambda b, i: (0,) * len(shape))
    return pl.pallas_call(
        _proj_kernel,
        grid=(B, S // tm),
        in_specs=[pl.BlockSpec((None, tm, D), lambda b, i: (b, i, 0)),
                  pl.BlockSpec((None, 1, D), lambda b, i: (b, 0, 0)),
                  pl.BlockSpec((None, 1, D), lambda b, i: (b, 0, 0)),
                  const((1, D)),
                  pl.BlockSpec((None, 1, tm), lambda b, i: (b, 0, i)),
                  const((HALF_ROPE, 1)),
                  const(w1.shape), const(gq.shape), const(gkv.shape), const(wq.shape), const(wkv.shape)],
        out_specs=[pl.BlockSpec((None, MLA_HEADS, tm, LANES), lambda b, i: (b, 0, i, 0)),
                   pl.BlockSpec((None, MLA_HEADS, tm, LANES), lambda b, i: (b, 0, i, 0)),
                   pl.BlockSpec((None, MLA_HEADS // 2, tm // TK_MLA, LANES, TK_MLA), lambda b, i: (b, 0, i, 0, 0)),
                   pl.BlockSpec((None, tm, SWA_Q_HEADS * SWA_HEAD), lambda b, i: (b, i, 0)),
                   pl.BlockSpec((None, tm, LANES), lambda b, i: (b, i, 0)),
                   pl.BlockSpec((None, tm, LANES), lambda b, i: (b, i, 0))],
        out_shape=[jax.ShapeDtypeStruct((B, MLA_HEADS, S, LANES), BF16),
                   jax.ShapeDtypeStruct((B, MLA_HEADS, S, LANES), BF16),
                   jax.ShapeDtypeStruct((B, MLA_HEADS // 2, S // TK_MLA, LANES, TK_MLA), BF16),
                   jax.ShapeDtypeStruct((B, S, SWA_Q_HEADS * SWA_HEAD), BF16),
                   jax.ShapeDtypeStruct((B, S, LANES), BF16),
                   jax.ShapeDtypeStruct((B, S, LANES), BF16)],
        compiler_params=pltpu.CompilerParams(dimension_semantics=("arbitrary", "arbitrary"),
                                             vmem_limit_bytes=VMEM_LIMIT),
        name="proj",
    )(x, sh1, sc1, ln_g, pos3, inv, w1, gq, gkv, wq, wkv)


def _mla_kernel(q_ref, k_ref, vt_ref, o_ref, qt_sc, acc_sc):
    tq, tk = TQ_MLA, TK_MLA
    nc = tq // LANES
    i = pl.program_id(2)
    for hd in range(2):
        qt_sc[hd] = q_ref[hd].astype(F32).T.astype(BF16)
    acc_sc[...] = jnp.zeros_like(acc_sc)

    per_tile = tq // tk

    def tile(t, carry, masked):
        blocks = [(t * per_tile + d, hd) for d in range(per_tile) for hd in range(2)]
        sts = [_dot(k_ref[hd, pl.ds(pl.multiple_of(j * tk, tk), tk), :], qt_sc[hd]) for j, hd in blocks]
        carry = list(carry)
        for (j, hd), st in zip(blocks, sts):
            ms, ls = carry[hd]
            if masked:
                r = lax.broadcasted_iota(jnp.int32, (tk, tq), 0)
                c = lax.broadcasted_iota(jnp.int32, (tk, tq), 1)
                st = jnp.where(r + (j * tk - i * tq) <= c, st, -jnp.inf)
            ps, alphas, ms_new, ls_new = [], [], [], []
            for cc in range(nc):
                blk = st[:, cc * LANES:(cc + 1) * LANES]
                m_new = jnp.maximum(ms[cc], jnp.max(blk, axis=0, keepdims=True))
                alpha = jnp.exp(ms[cc] - m_new)
                p = jnp.exp(blk - m_new)
                ls_new.append(alpha * ls[cc] + jnp.sum(p, axis=0, keepdims=True))
                ms_new.append(m_new)
                alphas.append(alpha)
                ps.append(p.astype(BF16))
            rows = slice(hd * MLA_V, (hd + 1) * MLA_V)
            pv = _dot(vt_ref[j, rows, :], jnp.concatenate(ps, axis=1))
            acc_sc[rows, :] = acc_sc[rows, :] * jnp.concatenate(alphas, axis=1) + pv
            carry[hd] = (tuple(ms_new), tuple(ls_new))
        return tuple(carry)

    stat = lambda v: tuple(jnp.full((1, LANES), v, F32) for _ in range(nc))
    init = tuple((stat(-jnp.inf), stat(0.0)) for _ in range(2))
    carry = lax.fori_loop(0, i, lambda t, c: tile(t, c, False), init)
    carry = tile(i, carry, True)
    inv_l = jnp.concatenate([jnp.broadcast_to(1.0 / jnp.concatenate(carry[hd][1], axis=1), (MLA_V, tq))
                             for hd in range(2)], axis=0)
    o_ref[...] = (acc_sc[...] * inv_l).T.astype(BF16)


def _mla(qm, km, vt):
    B, H, S, _ = qm.shape
    tq, tk = TQ_MLA, TK_MLA
    return pl.pallas_call(
        _mla_kernel,
        grid=(B, H // 2, S // tq),
        in_specs=[pl.BlockSpec((None, 2, tq, LANES), lambda b, p, i: (b, p, i, 0)),
                  pl.BlockSpec((None, 2, S, LANES), lambda b, p, i: (b, p, 0, 0)),
                  pl.BlockSpec((None, None, S // tk, LANES, tk), lambda b, p, i: (b, p, 0, 0, 0))],
        out_specs=pl.BlockSpec((None, tq, LANES), lambda b, p, i: (b, i, p)),
        out_shape=jax.ShapeDtypeStruct((B, S, H * MLA_V), BF16),
        scratch_shapes=[pltpu.VMEM((2, LANES, tq), BF16), pltpu.VMEM((LANES, tq), F32)],
        compiler_params=pltpu.CompilerParams(dimension_semantics=("arbitrary",) * 3,
                                             vmem_limit_bytes=VMEM_LIMIT),
        name="mla",
    )(qm, km, vt)


def _swa_kernel(sink_ref, q_ref, kp_ref, kc_ref, vp_ref, vc_ref, bias_ref, o_ref):
    band = 2 * WINDOW
    kband = jnp.concatenate([kp_ref[...], kc_ref[...]], axis=0).astype(F32)
    vband_t = jnp.concatenate([vp_ref[...], vc_ref[...]], axis=0).astype(F32).T
    lo = lax.broadcasted_iota(jnp.int32, kband.shape, 1) < SWA_HEAD
    kswap = pltpu.roll(kband, SWA_HEAD, 1)
    pairs = SWA_GROUP // 2
    for kvh in range(SWA_KV_HEADS):
        k_lo, k_hi = (kband, kswap) if kvh == 0 else (kswap, kband)
        kcat = jnp.concatenate([jnp.where(lo, k_lo, 0.0), jnp.where(lo, 0.0, k_hi)], axis=0).astype(BF16)
        qstack = jnp.concatenate([q_ref[:, (kvh * pairs + jj) * LANES:(kvh * pairs + jj + 1) * LANES]
                                  for jj in range(pairs)], axis=0)
        st = _dot_nt(kcat, qstack) + bias_ref[kvh]
        v_t = vband_t[kvh * SWA_HEAD:(kvh + 1) * SWA_HEAD, :].astype(BF16)
        halves = []
        for parity in range(2):
            ps, inv_l = [], []
            for jj in range(pairs):
                blk = st[parity * band:(parity + 1) * band, jj * WINDOW:(jj + 1) * WINDOW]
                sink = sink_ref[kvh * SWA_GROUP + 2 * jj + parity]
                m = jnp.maximum(jnp.max(blk, axis=0, keepdims=True), sink)
                e = jnp.exp(blk - m)
                inv_l.append(1.0 / (jnp.sum(e, axis=0, keepdims=True) + jnp.exp(sink - m)))
                ps.append(e.astype(BF16))
            halves.append(_dot(v_t, jnp.concatenate(ps, axis=1)) * jnp.concatenate(inv_l, axis=1))
        o_t = jnp.concatenate(halves, axis=0)
        for jj in range(pairs):
            pair = kvh * pairs + jj
            o_ref[:, pair * LANES:(pair + 1) * LANES] = o_t[:, jj * WINDOW:(jj + 1) * WINDOW].T.astype(BF16)


def _swa(sinks, qs, ks, vs, bias_tbl):
    B, S, W = qs.shape
    nblk = S // WINDOW
    prev = lambda b, n: (b, jnp.maximum(n - 1, 0), 0)
    cur = lambda b, n: (b, n, 0)
    kv_spec = lambda imap: pl.BlockSpec((None, WINDOW, LANES), imap)
    return pl.pallas_call(
        _swa_kernel,
        grid=(B, nblk),
        in_specs=[pl.BlockSpec(memory_space=pltpu.SMEM),
                  pl.BlockSpec((None, WINDOW, W), cur),
                  kv_spec(prev), kv_spec(cur), kv_spec(prev), kv_spec(cur),
                  pl.BlockSpec((None,) + bias_tbl.shape[1:], lambda b, n: (jnp.minimum(n, 1), 0, 0, 0))],
        out_specs=pl.BlockSpec((None, WINDOW, W), cur),
        out_shape=jax.ShapeDtypeStruct((B, S, W), BF16),
        compiler_params=pltpu.CompilerParams(dimension_semantics=("arbitrary", "arbitrary"),
                                             vmem_limit_bytes=VMEM_LIMIT),
        name="swa",
    )(sinks, qs, ks, ks, vs, vs, bias_tbl)


def _tail_kernel(x_ref, ym_ref, ys_ref, sh1_ref, sc1_ref, ga1_ref, sh2_ref, sc2_ref, ga2_ref,
                 gmix_ref, gmlp_ref, gfin_ref, bg_ref, wg_ref, wom_ref, wos_ref, wo_ref, w1_ref, w2_ref, o_ref):
    x = x_ref[...]
    h = (_rms(x, gmix_ref[...]) * (1 + sc1_ref[...]) + sh1_ref[...]).astype(BF16)
    gates = jax.nn.sigmoid(_dot(h, wg_ref[...]) + bg_ref[...])
    merged = (gates[:, :D_MODEL] * _dot(ym_ref[...], wom_ref[...])
              + gates[:, D_MODEL:] * _dot(ys_ref[...], wos_ref[...]))
    x1 = x + ga1_ref[...] * _dot(merged.astype(BF16), wo_ref[...])
    h2 = (_rms(x1, gmlp_ref[...]) * (1 + sc2_ref[...]) + sh2_ref[...]).astype(BF16)
    acc = jnp.zeros_like(x1)
    for c in range(D_FF // FF_CHUNK):
        u = jnp.square(jnp.maximum(_dot(h2, w1_ref[:, c * FF_CHUNK:(c + 1) * FF_CHUNK]), 0.0))
        acc = acc + _dot(u.astype(BF16), w2_ref[c * FF_CHUNK:(c + 1) * FF_CHUNK, :])
    x2 = x1 + ga2_ref[...] * acc
    o_ref[...] = _rms(x2, gfin_ref[...])


def _tail(x, ym, ys, mods, gmix, gmlp, gfin, bg, wg, wom, wos, wo, w1, w2):
    B, S, D = x.shape
    tm = TM_TAIL
    tok = lambda width: pl.BlockSpec((None, tm, width), lambda b, i: (b, i, 0))
    per_b = pl.BlockSpec((None, 1, D), lambda b, i: (b, 0, 0))
    const = lambda a: pl.BlockSpec(a.shape, lambda b, i: (0,) * a.ndim, pipeline_mode=pl.Buffered(1))
    return pl.pallas_call(
        _tail_kernel,
        grid=(B, S // tm),
        in_specs=[tok(D), tok(ym.shape[-1]), tok(ys.shape[-1])] + [per_b] * 6
                 + [const(a) for a in (gmix, gmlp, gfin, bg, wg, wom, wos, wo, w1, w2)],
        out_specs=tok(D),
        out_shape=jax.ShapeDtypeStruct((B, S, D), F32),
        compiler_params=pltpu.CompilerParams(dimension_semantics=("arbitrary", "arbitrary"),
                                             vmem_limit_bytes=VMEM_LIMIT),
        name="tail",
    )(x, ym, ys, *mods, gmix, gmlp, gfin, bg, wg, wom, wos, wo, w1, w2)


def _head_slab(nope, a, b):
    r, h, _ = nope.shape
    pad = jnp.zeros((r, h, LANES - MLA_NOPE - MLA_ROPE), nope.dtype)
    return jnp.concatenate([nope, a, b, pad], axis=-1).reshape(r, h * LANES)


def kernel(x, c, positions, rel_bias, ada_w, ada_b, ln_mix_g, w_in, b_gate, mla_q_norm_g, mla_kv_norm_g,
           w_uq, w_ukv, swa_sinks, w_o_mla, w_o_swa, w_o, ln_mlp_g, w_ff1, w_ff2, ln_final_g):
    B, S, D = x.shape
    assert (B, S, D) == (x.shape[0], 4096, D_MODEL) and ada_w.shape[0] == 1
    l = 0

    c_pad = jnp.pad(c, ((0, 8 - B), (0, 0)))
    mod = _adaln(c_pad, ada_w[l], ada_b[l][None, :])[:B]
    mods = [m[:, None, :] for m in jnp.split(mod, 6, axis=-1)]

    wi = w_in[l]
    o_kv = MLA_Q_RANK
    o_kr = o_kv + MLA_KV_RANK
    o_qs = o_kr + MLA_ROPE
    o_ks = o_qs + SWA_Q_HEADS * SWA_HEAD
    o_vs = o_ks + SWA_KV_HEADS * SWA_HEAD
    o_ga = o_vs + SWA_KV_HEADS * SWA_HEAD
    kr_a, kr_b = wi[:, o_kr:o_kr + HALF_ROPE], wi[:, o_kr + HALF_ROPE:o_qs]
    z = lambda n: jnp.zeros((D, n), wi.dtype)
    w1 = jnp.concatenate([
        wi[:, :o_kr],
        z(ROPE_A), kr_a, kr_b, z(LANES - ROPE_A - MLA_ROPE),
        z(ROPE_A), -kr_b, kr_a, z(LANES - ROPE_A - MLA_ROPE),
        wi[:, o_qs:o_ks] * (SWA_HEAD ** -0.5),
        wi[:, o_ks:o_ga]], axis=1).astype(BF16)
    assert w1.shape[1] == N_PROJ
    wg = wi[:, o_ga:].astype(BF16)

    uq = w_uq[l]
    q_nope, q_a, q_b = uq[..., :MLA_NOPE], uq[..., MLA_NOPE:MLA_NOPE + HALF_ROPE], uq[..., MLA_NOPE + HALF_ROPE:]
    wq = jnp.concatenate([_head_slab(q_nope, q_a, q_b),
                          _head_slab(jnp.zeros_like(q_nope), -q_b, q_a)], axis=1).astype(BF16)
    ukv = w_ukv[l]
    k_nope, v_up = ukv[..., :MLA_NOPE], ukv[..., MLA_NOPE:]
    zr = jnp.zeros(k_nope.shape[:2] + (HALF_ROPE,), ukv.dtype)
    wkv = jnp.concatenate([_head_slab(k_nope, zr, zr),
                           v_up.reshape(MLA_KV_RANK, MLA_HEADS * MLA_V)], axis=1).astype(BF16)

    inv = (ROPE_THETA ** (-jnp.arange(HALF_ROPE, dtype=F32) / HALF_ROPE))[:, None]

    qm, km, vm, qs, ks, vs = _proj(x, mods[0], mods[1], ln_mix_g[l][None, :], positions[:, None, :], inv, w1,
                                   mla_q_norm_g[l][None, :], mla_kv_norm_g[l][None, :], wq, wkv)

    y_mla = _mla(qm, km, vm)
    y_swa = _swa(swa_sinks[l], qs, ks, vs, _swa_bias(rel_bias))

    return _tail(x, y_mla, y_swa, mods, ln_mix_g[l][None, :], ln_mlp_g[l][None, :], ln_final_g[None, :],
                 b_gate[l][None, :], wg, w_o_mla[l].astype(BF16), w_o_swa[l].astype(BF16), w_o[l].astype(BF16),
                 w_ff1[l].astype(BF16), w_ff2[l].astype(BF16))
```

```python
import functools
import math

import jax
import jax.numpy as jnp
from jax import lax
from jax.experimental import pallas as pl
from jax.experimental.pallas import tpu as pltpu

F32 = jnp.float32
BF16 = jnp.bfloat16

D_MODEL = 1024
MLA_HEADS = 8
MLA_Q_RANK = 256
MLA_KV_RANK = 128
MLA_NOPE = 64
MLA_ROPE = 32
MLA_V = 64
SWA_Q_HEADS = 16
SWA_KV_HEADS = 2
SWA_HEAD = 64
WINDOW = 128
REL_BUCKETS = 32
REL_MAX_DIST = 128
D_FF = 4 * D_MODEL
ROPE_THETA = 10000.0
EPS = 1e-6

LANES = 128
HALF_ROPE = MLA_ROPE // 2
ROPE_A = MLA_NOPE
ROPE_B = MLA_NOPE + HALF_ROPE
N_PAIRS = SWA_Q_HEADS // 2
SWA_GROUP = SWA_Q_HEADS // SWA_KV_HEADS

C_QLAT = 0
C_KVLAT = C_QLAT + MLA_Q_RANK
C_KR = C_KVLAT + MLA_KV_RANK
C_KRS = C_KR + LANES
C_QS = C_KRS + LANES
C_KS = C_QS + SWA_Q_HEADS * SWA_HEAD
C_VS = C_KS + SWA_KV_HEADS * SWA_HEAD
N_PROJ = C_VS + SWA_KV_HEADS * SWA_HEAD

VMEM_LIMIT = 56 * 1024 * 1024

TM_PROJ = 512
TQ_MLA = 512
TK_MLA = 256
MLA_GROUP = 4
TM_TAIL = 256
FF_CHUNK = 1024


def _rms(x, g):
    return x * lax.rsqrt(jnp.mean(x * x, axis=-1, keepdims=True) + EPS) * g


def _dot(a, b):
    return jnp.dot(a, b, preferred_element_type=F32)


def _dot_nt(a, b):
    return lax.dot_general(a, b, (((1,), (1,)), ((), ())), preferred_element_type=F32)


def _adaln_kernel(c_ref, w_ref, b_ref, o_ref):
    c = c_ref[...]
    act = (c * jax.nn.sigmoid(c)).astype(BF16)
    o_ref[...] = _dot(act, w_ref[...].astype(BF16)) + b_ref[...]


def _adaln(c_pad, ada_w, ada_b):
    rows = c_pad.shape[0]
    n_out = ada_w.shape[1]
    return pl.pallas_call(
        _adaln_kernel,
        grid=(n_out // D_MODEL,),
        in_specs=[pl.BlockSpec((rows, D_MODEL), lambda j: (0, 0)),
                  pl.BlockSpec((D_MODEL, D_MODEL), lambda j: (0, j)),
                  pl.BlockSpec((1, D_MODEL), lambda j: (0, j))],
        out_specs=pl.BlockSpec((rows, D_MODEL), lambda j: (0, j)),
        out_shape=jax.ShapeDtypeStruct((rows, n_out), F32),
        compiler_params=pltpu.CompilerParams(dimension_semantics=("arbitrary",)),
        name="adaln",
    )(c_pad, ada_w, ada_b)


def _swa_bias_kernel(rb_ref, o_ref):
    a = lax.broadcasted_iota(jnp.int32, (2 * WINDOW, WINDOW), 1)
    b = lax.broadcasted_iota(jnp.int32, (2 * WINDOW, WINDOW), 0)
    dist = WINDOW + a - b
    n = jnp.maximum(dist, 0)
    max_exact = REL_BUCKETS // 2
    nf = jnp.maximum(n, 1).astype(F32)
    large = max_exact + (jnp.log(nf / max_exact) / math.log(REL_MAX_DIST / max_exact)
                         * (REL_BUCKETS - max_exact)).astype(jnp.int32)
    large = jnp.minimum(large, REL_BUCKETS - 1)
    bucket = jnp.where(n < max_exact, n, large)
    band_ok = (dist >= 0) & (dist < WINDOW)
    has_prev = b >= WINDOW
    neg = jnp.float32(-jnp.inf)
    for head in range(SWA_Q_HEADS):
        bias = jnp.zeros((2 * WINDOW, WINDOW), F32)
        for k in range(REL_BUCKETS):
            bias = jnp.where(bucket == k, rb_ref[head, k], bias)
        kvh, pair, parity = head // SWA_GROUP, (head % SWA_GROUP) // 2, head % 2
        rows = slice(parity * 2 * WINDOW, (parity + 1) * 2 * WINDOW)
        cols = slice(pair * WINDOW, (pair + 1) * WINDOW)
        o_ref[0, kvh, rows, cols] = jnp.where(band_ok & has_prev, bias, neg)
        o_ref[1, kvh, rows, cols] = jnp.where(band_ok, bias, neg)


def _swa_bias(rel_bias):
    return pl.pallas_call(
        _swa_bias_kernel,
        in_specs=[pl.BlockSpec(memory_space=pltpu.SMEM)],
        out_specs=pl.BlockSpec(memory_space=pltpu.VMEM),
        out_shape=jax.ShapeDtypeStruct((2, SWA_KV_HEADS, 4 * WINDOW, SWA_GROUP // 2 * WINDOW), F32),
        name="swa_bias",
    )(rel_bias)


def _proj_kernel(x_ref, sh_ref, sc_ref, g_ref, pos_ref, inv_ref, w1_ref, gq_ref, gkv_ref, wq_ref, wkv_ref,
                 qm_ref, km_ref, vm_ref, qs_ref, ks_ref, vs_ref):
    h = _rms(x_ref[...], g_ref[...]) * (1 + sc_ref[...]) + sh_ref[...]
    p = _dot(h.astype(BF16), w1_ref[...])

    ang = inv_ref[...] * pos_ref[...].astype(F32)
    row = lax.broadcasted_iota(jnp.int32, (HALF_ROPE, LANES), 0)
    lane = lax.broadcasted_iota(jnp.int32, (HALF_ROPE, LANES), 1)
    spread = ((lane == row + ROPE_A) | (lane == row + ROPE_B)).astype(F32)
    cos_f = jnp.dot(jnp.cos(ang).T, spread, precision=lax.Precision.HIGHEST, preferred_element_type=F32)
    sin_f = jnp.dot(jnp.sin(ang).T, spread, precision=lax.Precision.HIGHEST, preferred_element_type=F32)
    lane_t = lax.broadcasted_iota(jnp.int32, cos_f.shape, 1)
    cos_q = jnp.where(lane_t < MLA_NOPE, 1.0, cos_f)

    qn = _rms(p[:, C_QLAT:C_QLAT + MLA_Q_RANK], gq_ref[...]).astype(BF16)
    qq = _dot(qn, wq_ref[...])
    scale = (MLA_NOPE + MLA_ROPE) ** -0.5 * math.log2(math.e)
    half = MLA_HEADS * LANES
    for hd in range(MLA_HEADS):
        q = qq[:, hd * LANES:(hd + 1) * LANES] * cos_q + qq[:, half + hd * LANES:half + (hd + 1) * LANES] * sin_f
        qm_ref[hd] = (q * scale).astype(BF16)

    kvn = _rms(p[:, C_KVLAT:C_KVLAT + MLA_KV_RANK], gkv_ref[...]).astype(BF16)
    kk = _dot(kvn, wkv_ref[...])
    k_pe = p[:, C_KR:C_KR + LANES] * cos_f + p[:, C_KRS:C_KRS + LANES] * sin_f
    for hd in range(MLA_HEADS):
        km_ref[hd] = (kk[:, hd * LANES:(hd + 1) * LANES] + k_pe).astype(BF16)
    for pr in range(MLA_HEADS // 2):
        v_t = kk[:, half + pr * LANES:half + (pr + 1) * LANES].T.astype(BF16)
        for t in range(v_t.shape[1] // TK_MLA):
            vm_ref[pr, t] = v_t[:, t * TK_MLA:(t + 1) * TK_MLA]

    qs_ref[...] = p[:, C_QS:C_KS].astype(BF16)
    ks_ref[...] = p[:, C_KS:C_VS].astype(BF16)
    vs_ref[...] = p[:, C_VS:N_PROJ].astype(BF16)


def _proj(x, sh1, sc1, ln_g, pos3, inv, w1, gq, gkv, wq, wkv):
    B, S, D = x.shape
    tm = TM_PROJ
    const = lambda shape: pl.BlockSpec(shape, lambda b, i: (0,) * len(shape))
    return pl.pallas_call(
        _proj_kernel,
        grid=(B, S // tm),
        in_specs=[pl.BlockSpec((None, tm, D), lambda b, i: (b, i, 0)),
                  pl.BlockSpec((None, 1, D), lambda b, i: (b, 0, 0)),
                  pl.BlockSpec((None, 1, D), lambda b, i: (b, 0, 0)),
                  const((1, D)),
                  pl.BlockSpec((None, 1, tm), lambda b, i: (b, 0, i)),
                  const((HALF_ROPE, 1)),
                  const(w1.shape), const(gq.shape), const(gkv.shape), const(wq.shape), const(wkv.shape)],
        out_specs=[pl.BlockSpec((None, MLA_HEADS, tm, LANES), lambda b, i: (b, 0, i, 0)),
                   pl.BlockSpec((None, MLA_HEADS, tm, LANES), lambda b, i: (b, 0, i, 0)),
                   pl.BlockSpec((None, MLA_HEADS // 2, tm // TK_MLA, LANES, TK_MLA), lambda b, i: (b, 0, i, 0, 0)),
                   pl.BlockSpec((None, tm, SWA_Q_HEADS * SWA_HEAD), lambda b, i: (b, i, 0)),
                   pl.BlockSpec((None, tm, LANES), lambda b, i: (b, i, 0)),
                   pl.BlockSpec((None, tm, LANES), lambda b, i: (b, i, 0))],
        out_shape=[jax.ShapeDtypeStruct((B, MLA_HEADS, S, LANES), BF16),
                   jax.ShapeDtypeStruct((B, MLA_HEADS, S, LANES), BF16),
                   jax.ShapeDtypeStruct((B, MLA_HEADS // 2, S // TK_MLA, LANES, TK_MLA), BF16),
                   jax.ShapeDtypeStruct((B, S, SWA_Q_HEADS * SWA_HEAD), BF16),
                   jax.ShapeDtypeStruct((B, S, LANES), BF16),
                   jax.ShapeDtypeStruct((B, S, LANES), BF16)],
        compiler_params=pltpu.CompilerParams(dimension_semantics=("arbitrary", "arbitrary"),
                                             vmem_limit_bytes=VMEM_LIMIT),
        name="proj",
    )(x, sh1, sc1, ln_g, pos3, inv, w1, gq, gkv, wq, wkv)


def _mla_kernel(q_ref, k_ref, vt_ref, o_ref, qt_sc, acc_sc):
    tq, tk = TQ_MLA, TK_MLA
    nc = tq // LANES
    i = pl.program_id(2)
    for hd in range(MLA_GROUP):
        qt_sc[hd] = q_ref[hd].astype(F32).T.astype(BF16)
    acc_sc[...] = jnp.zeros_like(acc_sc)

    per_tile = tq // tk

    def tile(t, carry, masked):
        blocks = [(t * per_tile + d, hd) for d in range(per_tile) for hd in range(MLA_GROUP)]
        sts = [_dot(k_ref[hd, pl.ds(pl.multiple_of(j * tk, tk), tk), :], qt_sc[hd]) for j, hd in blocks]
        carry = list(carry)
        for (j, hd), st in zip(blocks, sts):
            ms, ls = carry[hd]
            if masked:
                r = lax.broadcasted_iota(jnp.int32, (tk, tq), 0)
                c = lax.broadcasted_iota(jnp.int32, (tk, tq), 1)
                st = jnp.where(r + (j * tk - i * tq) <= c, st, -jnp.inf)
            ps, alphas, ms_new, ls_new = [], [], [], []
            for cc in range(nc):
                blk = st[:, cc * LANES:(cc + 1) * LANES]
                m_new = jnp.maximum(ms[cc], jnp.max(blk, axis=0, keepdims=True))
                alpha = jnp.exp2(ms[cc] - m_new)
                p = jnp.exp2(blk - m_new)
                ls_new.append(alpha * ls[cc] + jnp.sum(p, axis=0, keepdims=True))
                ms_new.append(m_new)
                alphas.append(alpha)
                ps.append(p.astype(BF16))
            rows = slice(hd * MLA_V, (hd + 1) * MLA_V)
            v_t = vt_ref[hd // 2, j, (hd % 2) * MLA_V:(hd % 2 + 1) * MLA_V, :]
            pv = _dot(v_t, jnp.concatenate(ps, axis=1))
            acc_sc[rows, :] = acc_sc[rows, :] * jnp.concatenate(alphas, axis=1) + pv
            carry[hd] = (tuple(ms_new), tuple(ls_new))
        return tuple(carry)

    stat = lambda v: tuple(jnp.full((1, LANES), v, F32) for _ in range(nc))
    init = tuple((stat(-jnp.inf), stat(0.0)) for _ in range(MLA_GROUP))
    carry = lax.fori_loop(0, i, lambda t, c: tile(t, c, False), init)
    carry = tile(i, carry, True)
    inv_l = jnp.concatenate([jnp.broadcast_to(1.0 / jnp.concatenate(carry[hd][1], axis=1), (MLA_V, tq))
                             for hd in range(MLA_GROUP)], axis=0)
    o_ref[...] = (acc_sc[...] * inv_l).T.astype(BF16)


def _mla(qm, km, vt):
    B, H, S, _ = qm.shape
    tq, tk, g = TQ_MLA, TK_MLA, MLA_GROUP
    return pl.pallas_call(
        _mla_kernel,
        grid=(B, H // g, S // tq),
        in_specs=[pl.BlockSpec((None, g, tq, LANES), lambda b, p, i: (b, p, i, 0)),
                  pl.BlockSpec((None, g, S, LANES), lambda b, p, i: (b, p, 0, 0)),
                  pl.BlockSpec((None, g // 2, S // tk, LANES, tk), lambda b, p, i: (b, p, 0, 0, 0))],
        out_specs=pl.BlockSpec((None, tq, g * MLA_V), lambda b, p, i: (b, i, p)),
        out_shape=jax.ShapeDtypeStruct((B, S, H * MLA_V), BF16),
        scratch_shapes=[pltpu.VMEM((g, LANES, tq), BF16), pltpu.VMEM((g * MLA_V, tq), F32)],
        compiler_params=pltpu.CompilerParams(dimension_semantics=("arbitrary",) * 3,
                                             vmem_limit_bytes=VMEM_LIMIT),
        name="mla",
    )(qm, km, vt)


def _swa_kernel(sink_ref, q_ref, kp_ref, kc_ref, vp_ref, vc_ref, bias_ref, o_ref):
    band = 2 * WINDOW
    kband = jnp.concatenate([kp_ref[...], kc_ref[...]], axis=0).astype(F32)
    vband_t = jnp.concatenate([vp_ref[...], vc_ref[...]], axis=0).astype(F32).T
    lo = lax.broadcasted_iota(jnp.int32, kband.shape, 1) < SWA_HEAD
    kswap = pltpu.roll(kband, SWA_HEAD, 1)
    pairs = SWA_GROUP // 2
    for kvh in range(SWA_KV_HEADS):
        k_lo, k_hi = (kband, kswap) if kvh == 0 else (kswap, kband)
        kcat = jnp.concatenate([jnp.where(lo, k_lo, 0.0), jnp.where(lo, 0.0, k_hi)], axis=0).astype(BF16)
        qstack = jnp.concatenate([q_ref[:, (kvh * pairs + jj) * LANES:(kvh * pairs + jj + 1) * LANES]
                                  for jj in range(pairs)], axis=0)
        st = _dot_nt(kcat, qstack) + bias_ref[kvh]
        v_t = vband_t[kvh * SWA_HEAD:(kvh + 1) * SWA_HEAD, :].astype(BF16)
        halves = []
        for parity in range(2):
            ps, inv_l = [], []
            for jj in range(pairs):
                blk = st[parity * band:(parity + 1) * band, jj * WINDOW:(jj + 1) * WINDOW]
                sink = sink_ref[kvh * SWA_GROUP + 2 * jj + parity]
                m = jnp.maximum(jnp.max(blk, axis=0, keepdims=True), sink)
                e = jnp.exp(blk - m)
                inv_l.append(1.0 / (jnp.sum(e, axis=0, keepdims=True) + jnp.exp(sink - m)))
                ps.append(e.astype(BF16))
            halves.append(_dot(v_t, jnp.concatenate(ps, axis=1)) * jnp.concatenate(inv_l, axis=1))
        o_t = jnp.concatenate(halves, axis=0)
        for jj in range(pairs):
            pair = kvh * pairs + jj
            o_ref[:, pair * LANES:(pair + 1) * LANES] = o_t[:, jj * WINDOW:(jj + 1) * WINDOW].T.astype(BF16)


def _swa(sinks, qs, ks, vs, bias_tbl):
    B, S, W = qs.shape
    nblk = S // WINDOW
    prev = lambda b, n: (b, jnp.maximum(n - 1, 0), 0)
    cur = lambda b, n: (b, n, 0)
    kv_spec = lambda imap: pl.BlockSpec((None, WINDOW, LANES), imap)
    return pl.pallas_call(
        _swa_kernel,
        grid=(B, nblk),
        in_specs=[pl.BlockSpec(memory_space=pltpu.SMEM),
                  pl.BlockSpec((None, WINDOW, W), cur),
                  kv_spec(prev), kv_spec(cur), kv_spec(prev), kv_spec(cur),
                  pl.BlockSpec((None,) + bias_tbl.shape[1:], lambda b, n: (jnp.minimum(n, 1), 0, 0, 0))],
        out_specs=pl.BlockSpec((None, WINDOW, W), cur),
        out_shape=jax.ShapeDtypeStruct((B, S, W), BF16),
        compiler_params=pltpu.CompilerParams(dimension_semantics=("arbitrary", "arbitrary"),
                                             vmem_limit_bytes=VMEM_LIMIT),
        name="swa",
    )(sinks, qs, ks, ks, vs, vs, bias_tbl)


def _tail_kernel(x_ref, ym_ref, ys_ref, sh1_ref, sc1_ref, ga1_ref, sh2_ref, sc2_ref, ga2_ref,
                 gmix_ref, gmlp_ref, gfin_ref, bg_ref, wg_ref, wom_ref, wos_ref, wo_ref, w1_ref, w2_ref, o_ref):
    x = x_ref[...]
    h = (_rms(x, gmix_ref[...]) * (1 + sc1_ref[...]) + sh1_ref[...]).astype(BF16)
    gates = jax.nn.sigmoid(_dot(h, wg_ref[...]) + bg_ref[...])
    merged = (gates[:, :D_MODEL] * _dot(ym_ref[...], wom_ref[...])
              + gates[:, D_MODEL:] * _dot(ys_ref[...], wos_ref[...]))
    x1 = x + ga1_ref[...] * _dot(merged.astype(BF16), wo_ref[...])
    h2 = (_rms(x1, gmlp_ref[...]) * (1 + sc2_ref[...]) + sh2_ref[...]).astype(BF16)
    acc = jnp.zeros_like(x1)
    for c in range(D_FF // FF_CHUNK):
        u = jnp.square(jnp.maximum(_dot(h2, w1_ref[:, c * FF_CHUNK:(c + 1) * FF_CHUNK]), 0.0))
        acc = acc + _dot(u.astype(BF16), w2_ref[c * FF_CHUNK:(c + 1) * FF_CHUNK, :])
    x2 = x1 + ga2_ref[...] * acc
    o_ref[...] = _rms(x2, gfin_ref[...])


def _tail(x, ym, ys, mods, gmix, gmlp, gfin, bg, wg, wom, wos, wo, w1, w2):
    B, S, D = x.shape
    tm = TM_TAIL
    tok = lambda width: pl.BlockSpec((None, tm, width), lambda b, i: (b, i, 0))
    per_b = pl.BlockSpec((None, 1, D), lambda b, i: (b, 0, 0))
    const = lambda a: pl.BlockSpec(a.shape, lambda b, i: (0,) * a.ndim, pipeline_mode=pl.Buffered(1))
    return pl.pallas_call(
        _tail_kernel,
        grid=(B, S // tm),
        in_specs=[tok(D), tok(ym.shape[-1]), tok(ys.shape[-1])] + [per_b] * 6
                 + [const(a) for a in (gmix, gmlp, gfin, bg, wg, wom, wos, wo, w1, w2)],
        out_specs=tok(D),
        out_shape=jax.ShapeDtypeStruct((B, S, D), F32),
        compiler_params=pltpu.CompilerParams(dimension_semantics=("arbitrary", "arbitrary"),
                                             vmem_limit_bytes=VMEM_LIMIT),
        name="tail",
    )(x, ym, ys, *mods, gmix, gmlp, gfin, bg, wg, wom, wos, wo, w1, w2)


def _head_slab(nope, a, b):
    r, h, _ = nope.shape
    pad = jnp.zeros((r, h, LANES - MLA_NOPE - MLA_ROPE), nope.dtype)
    return jnp.concatenate([nope, a, b, pad], axis=-1).reshape(r, h * LANES)


def kernel(x, c, positions, rel_bias, ada_w, ada_b, ln_mix_g, w_in, b_gate, mla_q_norm_g, mla_kv_norm_g,
           w_uq, w_ukv, swa_sinks, w_o_mla, w_o_swa, w_o, ln_mlp_g, w_ff1, w_ff2, ln_final_g):
    B, S, D = x.shape
    assert (B, S, D) == (x.shape[0], 4096, D_MODEL) and ada_w.shape[0] == 1
    l = 0

    c_pad = jnp.pad(c, ((0, 8 - B), (0, 0)))
    mod = _adaln(c_pad, ada_w[l], ada_b[l][None, :])[:B]
    mods = [m[:, None, :] for m in jnp.split(mod, 6, axis=-1)]

    wi = w_in[l]
    o_kv = MLA_Q_RANK
    o_kr = o_kv + MLA_KV_RANK
    o_qs = o_kr + MLA_ROPE
    o_ks = o_qs + SWA_Q_HEADS * SWA_HEAD
    o_vs = o_ks + SWA_KV_HEADS * SWA_HEAD
    o_ga = o_vs + SWA_KV_HEADS * SWA_HEAD
    kr_a, kr_b = wi[:, o_kr:o_kr + HALF_ROPE], wi[:, o_kr + HALF_ROPE:o_qs]
    z = lambda n: jnp.zeros((D, n), wi.dtype)
    w1 = jnp.concatenate([
        wi[:, :o_kr],
        z(ROPE_A), kr_a, kr_b, z(LANES - ROPE_A - MLA_ROPE),
        z(ROPE_A), -kr_b, kr_a, z(LANES - ROPE_A - MLA_ROPE),
        wi[:, o_qs:o_ks] * (SWA_HEAD ** -0.5),
        wi[:, o_ks:o_ga]], axis=1).astype(BF16)
    assert w1.shape[1] == N_PROJ
    wg = wi[:, o_ga:].astype(BF16)

    uq = w_uq[l]
    q_nope, q_a, q_b = uq[..., :MLA_NOPE], uq[..., MLA_NOPE:MLA_NOPE + HALF_ROPE], uq[..., MLA_NOPE + HALF_ROPE:]
    wq = jnp.concatenate([_head_slab(q_nope, q_a, q_b),
                          _head_slab(jnp.zeros_like(q_nope), -q_b, q_a)], axis=1).astype(BF16)
    ukv = w_ukv[l]
    k_nope, v_up = ukv[..., :MLA_NOPE], ukv[..., MLA_NOPE:]
    zr = jnp.zeros(k_nope.shape[:2] + (HALF_ROPE,), ukv.dtype)
    wkv = jnp.concatenate([_head_slab(k_nope, zr, zr),
                           v_up.reshape(MLA_KV_RANK, MLA_HEADS * MLA_V)], axis=1).astype(BF16)

    inv = (ROPE_THETA ** (-jnp.arange(HALF_ROPE, dtype=F32) / HALF_ROPE))[:, None]

    qm, km, vm, qs, ks, vs = _proj(x, mods[0], mods[1], ln_mix_g[l][None, :], positions[:, None, :], inv, w1,
                                   mla_q_norm_g[l][None, :], mla_kv_norm_g[l][None, :], wq, wkv)

    y_mla = _mla(qm, km, vm)
    y_swa = _swa(swa_sinks[l], qs, ks, vs, _swa_bias(rel_bias))

    return _tail(x, y_mla, y_swa, mods, ln_mix_g[l][None, :], ln_mlp_g[l][None, :], ln_final_g[None, :],
                 b_gate[l][None, :], wg, w_o_mla[l].astype(BF16), w_o_swa[l].astype(BF16), w_o[l].astype(BF16),
                 w_ff1[l].astype(BF16), w_ff2[l].astype(BF16))
```

```python
import functools
import math

import jax
import jax.numpy as jnp
from jax import lax
from jax.experimental import pallas as pl
from jax.experimental.pallas import tpu as pltpu

F32 = jnp.float32
BF16 = jnp.bfloat16

D_MODEL = 1024
MLA_HEADS = 8
MLA_Q_RANK = 256
MLA_KV_RANK = 128
MLA_NOPE = 64
MLA_ROPE = 32
MLA_V = 64
SWA_Q_HEADS = 16
SWA_KV_HEADS = 2
SWA_HEAD = 64
WINDOW = 128
REL_BUCKETS = 32
REL_MAX_DIST = 128
D_FF = 4 * D_MODEL
ROPE_THETA = 10000.0
EPS = 1e-6

LANES = 128
HALF_ROPE = MLA_ROPE // 2
ROPE_A = MLA_NOPE
ROPE_B = MLA_NOPE + HALF_ROPE
N_PAIRS = SWA_Q_HEADS // 2
SWA_GROUP = SWA_Q_HEADS // SWA_KV_HEADS

C_QLAT = 0
C_KVLAT = C_QLAT + MLA_Q_RANK
C_KR = C_KVLAT + MLA_KV_RANK
C_KRS = C_KR + LANES
C_QS = C_KRS + LANES
C_KS = C_QS + SWA_Q_HEADS * SWA_HEAD
C_VS = C_KS + SWA_KV_HEADS * SWA_HEAD
N_PROJ = C_VS + SWA_KV_HEADS * SWA_HEAD

VMEM_LIMIT = 56 * 1024 * 1024

TM_PROJ = 512
TQ_MLA = 512
TK_MLA = 256
MLA_GROUP = 4
MLA_LOOKAHEAD = 4
MLA_ACC_ROWS = MLA_V + 16
TM_TAIL = 256
FF_CHUNK = 1024


def _rms(x, g):
    return x * lax.rsqrt(jnp.mean(x * x, axis=-1, keepdims=True) + EPS) * g


def _dot(a, b):
    return jnp.dot(a, b, preferred_element_type=F32)


def _dot_nt(a, b):
    return lax.dot_general(a, b, (((1,), (1,)), ((), ())), preferred_element_type=F32)


def _adaln_kernel(c_ref, w_ref, b_ref, o_ref):
    c = c_ref[...]
    act = (c * jax.nn.sigmoid(c)).astype(BF16)
    o_ref[...] = _dot(act, w_ref[...].astype(BF16)) + b_ref[...]


def _adaln(c_pad, ada_w, ada_b):
    rows = c_pad.shape[0]
    n_out = ada_w.shape[1]
    return pl.pallas_call(
        _adaln_kernel,
        grid=(n_out // D_MODEL,),
        in_specs=[pl.BlockSpec((rows, D_MODEL), lambda j: (0, 0)),
                  pl.BlockSpec((D_MODEL, D_MODEL), lambda j: (0, j)),
                  pl.BlockSpec((1, D_MODEL), lambda j: (0, j))],
        out_specs=pl.BlockSpec((rows, D_MODEL), lambda j: (0, j)),
        out_shape=jax.ShapeDtypeStruct((rows, n_out), F32),
        compiler_params=pltpu.CompilerParams(dimension_semantics=("arbitrary",)),
        name="adaln",
    )(c_pad, ada_w, ada_b)


def _swa_bias_kernel(rb_ref, o_ref):
    a = lax.broadcasted_iota(jnp.int32, (2 * WINDOW, WINDOW), 1)
    b = lax.broadcasted_iota(jnp.int32, (2 * WINDOW, WINDOW), 0)
    dist = WINDOW + a - b
    n = jnp.maximum(dist, 0)
    max_exact = REL_BUCKETS // 2
    nf = jnp.maximum(n, 1).astype(F32)
    large = max_exact + (jnp.log(nf / max_exact) / math.log(REL_MAX_DIST / max_exact)
                         * (REL_BUCKETS - max_exact)).astype(jnp.int32)
    large = jnp.minimum(large, REL_BUCKETS - 1)
    bucket = jnp.where(n < max_exact, n, large)
    band_ok = (dist >= 0) & (dist < WINDOW)
    has_prev = b >= WINDOW
    neg = jnp.float32(-jnp.inf)
    for head in range(SWA_Q_HEADS):
        bias = jnp.zeros((2 * WINDOW, WINDOW), F32)
        for k in range(REL_BUCKETS):
            bias = jnp.where(bucket == k, rb_ref[head, k], bias)
        kvh, pair, parity = head // SWA_GROUP, (head % SWA_GROUP) // 2, head % 2
        rows = slice(parity * 2 * WINDOW, (parity + 1) * 2 * WINDOW)
        cols = slice(pair * WINDOW, (pair + 1) * WINDOW)
        o_ref[0, kvh, rows, cols] = jnp.where(band_ok & has_prev, bias, neg)
        o_ref[1, kvh, rows, cols] = jnp.where(band_ok, bias, neg)


def _swa_bias(rel_bias):
    return pl.pallas_call(
        _swa_bias_kernel,
        in_specs=[pl.BlockSpec(memory_space=pltpu.SMEM)],
        out_specs=pl.BlockSpec(memory_space=pltpu.VMEM),
        out_shape=jax.ShapeDtypeStruct((2, SWA_KV_HEADS, 4 * WINDOW, SWA_GROUP // 2 * WINDOW), F32),
        name="swa_bias",
    )(rel_bias)


def _proj_kernel(x_ref, sh_ref, sc_ref, g_ref, pos_ref, inv_ref, w1_ref, gq_ref, gkv_ref, wq_ref, wkv_ref,
                 qm_ref, km_ref, vm_ref, qs_ref, ks_ref, vs_ref):
    h = _rms(x_ref[...], g_ref[...]) * (1 + sc_ref[...]) + sh_ref[...]
    p = _dot(h.astype(BF16), w1_ref[...])

    ang = inv_ref[...] * pos_ref[...].astype(F32)
    row = lax.broadcasted_iota(jnp.int32, (HALF_ROPE, LANES), 0)
    lane = lax.broadcasted_iota(jnp.int32, (HALF_ROPE, LANES), 1)
    spread = ((lane == row + ROPE_A) | (lane == row + ROPE_B)).astype(F32)
    cos_f = jnp.dot(jnp.cos(ang).T, spread, precision=lax.Precision.HIGHEST, preferred_element_type=F32)
    sin_f = jnp.dot(jnp.sin(ang).T, spread, precision=lax.Precision.HIGHEST, preferred_element_type=F32)
    lane_t = lax.broadcasted_iota(jnp.int32, cos_f.shape, 1)
    cos_q = jnp.where(lane_t < MLA_NOPE, 1.0, cos_f)

    qn = _rms(p[:, C_QLAT:C_QLAT + MLA_Q_RANK], gq_ref[...]).astype(BF16)
    qq = _dot(qn, wq_ref[...])
    scale = (MLA_NOPE + MLA_ROPE) ** -0.5 * math.log2(math.e)
    half = MLA_HEADS * LANES
    for hd in range(MLA_HEADS):
        q = qq[:, hd * LANES:(hd + 1) * LANES] * cos_q + qq[:, half + hd * LANES:half + (hd + 1) * LANES] * sin_f
        qm_ref[hd] = (q * scale).astype(BF16)

    kvn = _rms(p[:, C_KVLAT:C_KVLAT + MLA_KV_RANK], gkv_ref[...]).astype(BF16)
    kk = _dot(kvn, wkv_ref[...])
    k_pe = p[:, C_KR:C_KR + LANES] * cos_f + p[:, C_KRS:C_KRS + LANES] * sin_f
    for hd in range(MLA_HEADS):
        km_ref[hd] = (kk[:, hd * LANES:(hd + 1) * LANES] + k_pe).astype(BF16)
    for pr in range(MLA_HEADS // 2):
        v_t = kk[:, half + pr * LANES:half + (pr + 1) * LANES].T.astype(BF16)
        for t in range(v_t.shape[1] // TK_MLA):
            vm_ref[pr, t] = v_t[:, t * TK_MLA:(t + 1) * TK_MLA]

    qs_ref[...] = p[:, C_QS:C_KS].astype(BF16)
    ks_ref[...] = p[:, C_KS:C_VS].astype(BF16)
    vs_ref[...] = p[:, C_VS:N_PROJ].astype(BF16)


def _proj(x, sh1, sc1, ln_g, pos3, inv, w1, gq, gkv, wq, wkv):
    B, S, D = x.shape
    tm = TM_PROJ
    const = lambda shape: pl.BlockSpec(shape, lambda b, i: (0,) * len(shape))
    return pl.pallas_call(
        _proj_kernel,
        grid=(B, S // tm),
        in_specs=[pl.BlockSpec((None, tm, D), lambda b, i: (b, i, 0)),
                  pl.BlockSpec((None, 1, D), lambda b, i: (b, 0, 0)),
                  pl.BlockSpec((None, 1, D), lambda b, i: (b, 0, 0)),
                  const((1, D)),
                  pl.BlockSpec((None, 1, tm), lambda b, i: (b, 0, i)),
                  const((HALF_ROPE, 1)),
                  const(w1.shape), const(gq.shape), const(gkv.shape), const(wq.shape), const(wkv.shape)],
        out_specs=[pl.BlockSpec((None, MLA_HEADS, tm, LANES), lambda b, i: (b, 0, i, 0)),
                   pl.BlockSpec((None, MLA_HEADS, tm, LANES), lambda b, i: (b, 0, i, 0)),
                   pl.BlockSpec((None, MLA_HEADS // 2, tm // TK_MLA, LANES, TK_MLA), lambda b, i: (b, 0, i, 0, 0)),
                   pl.BlockSpec((None, tm, SWA_Q_HEADS * SWA_HEAD), lambda b, i: (b, i, 0)),
                   pl.BlockSpec((None, tm, LANES), lambda b, i: (b, i, 0)),
                   pl.BlockSpec((None, tm, LANES), lambda b, i: (b, i, 0))],
        out_shape=[jax.ShapeDtypeStruct((B, MLA_HEADS, S, LANES), BF16),
                   jax.ShapeDtypeStruct((B, MLA_HEADS, S, LANES), BF16),
                   jax.ShapeDtypeStruct((B, MLA_HEADS // 2, S // TK_MLA, LANES, TK_MLA), BF16),
                   jax.ShapeDtypeStruct((B, S, SWA_Q_HEADS * SWA_HEAD), BF16),
                   jax.ShapeDtypeStruct((B, S, LANES), BF16),
                   jax.ShapeDtypeStruct((B, S, LANES), BF16)],
        compiler_params=pltpu.CompilerParams(dimension_semantics=("arbitrary", "arbitrary"),
                                             vmem_limit_bytes=VMEM_LIMIT),
        name="proj",
    )(x, sh1, sc1, ln_g, pos3, inv, w1, gq, gkv, wq, wkv)


def _mla_kernel(q_ref, k_ref, vt_ref, o_ref, qt_sc, acc_sc):
    tq, tk = TQ_MLA, TK_MLA
    nc = tq // LANES
    i = pl.program_id(2)
    for hd in range(MLA_GROUP):
        qt_sc[hd] = q_ref[hd].astype(F32).T.astype(BF16)
    acc_sc[...] = jnp.zeros_like(acc_sc)

    per_tile = tq // tk

    ones_rows = (lax.broadcasted_iota(jnp.int32, (MLA_ACC_ROWS - MLA_V, tk), 0) == 0).astype(BF16)
    r_iota = lax.broadcasted_iota(jnp.int32, (tk, LANES), 0)
    c_iota = lax.broadcasted_iota(jnp.int32, (tk, LANES), 1)

    def tile(t, ms_all, diagonal):
        blocks = [(d, hd) for d in range(per_tile) for hd in range(MLA_GROUP)]
        first_cc = [(d * tk) // LANES if diagonal else 0 for d, _ in blocks]

        def scores(n):
            d, hd = blocks[n]
            k = k_ref[hd, pl.ds(pl.multiple_of((t * per_tile + d) * tk, tk), tk), :]
            return _dot(k, qt_sc[hd, :, first_cc[n] * LANES:])

        pending = {n: scores(n) for n in range(min(MLA_LOOKAHEAD, len(blocks)))}
        ms_all = list(ms_all)
        for n, (d, hd) in enumerate(blocks):
            if n + MLA_LOOKAHEAD < len(blocks):
                pending[n + MLA_LOOKAHEAD] = scores(n + MLA_LOOKAHEAD)
            st = pending.pop(n)
            ms = list(ms_all[hd])
            ps, alphas = [], []
            for cc in range(first_cc[n], nc):
                blk = st[:, (cc - first_cc[n]) * LANES:(cc - first_cc[n] + 1) * LANES]
                if diagonal and cc * LANES < (d + 1) * tk - 1:
                    blk = jnp.where(r_iota + (d * tk - cc * LANES) <= c_iota, blk, -jnp.inf)
                m_new = jnp.maximum(ms[cc], jnp.max(blk, axis=0, keepdims=True))
                alphas.append(jnp.exp2(ms[cc] - m_new))
                ps.append(jnp.exp2(blk - m_new).astype(BF16))
                ms[cc] = m_new
            v_t = vt_ref[hd // 2, t * per_tile + d, (hd % 2) * MLA_V:(hd % 2 + 1) * MLA_V, :]
            pv = _dot(jnp.concatenate([v_t, ones_rows], axis=0), jnp.concatenate(ps, axis=1))
            rows = slice(hd * MLA_ACC_ROWS, (hd + 1) * MLA_ACC_ROWS)
            cols = slice(first_cc[n] * LANES, tq)
            acc_sc[rows, cols] = acc_sc[rows, cols] * jnp.concatenate(alphas, axis=1) + pv
            ms_all[hd] = tuple(ms)
        return tuple(ms_all)

    init = tuple(tuple(jnp.full((1, LANES), -jnp.inf, F32) for _ in range(nc)) for _ in range(MLA_GROUP))
    ms_all = lax.fori_loop(0, i, lambda t, c: tile(t, c, False), init)
    tile(i, ms_all, True)
    outs = []
    for hd in range(MLA_GROUP):
        acc = acc_sc[hd * MLA_ACC_ROWS:(hd + 1) * MLA_ACC_ROWS, :]
        outs.append(acc[:MLA_V] / acc[MLA_V:MLA_V + 1])
    o_ref[...] = jnp.concatenate(outs, axis=0).T.astype(BF16)


def _mla(qm, km, vt):
    B, H, S, _ = qm.shape
    tq, tk, g = TQ_MLA, TK_MLA, MLA_GROUP
    return pl.pallas_call(
        _mla_kernel,
        grid=(B, H // g, S // tq),
        in_specs=[pl.BlockSpec((None, g, tq, LANES), lambda b, p, i: (b, p, i, 0)),
                  pl.BlockSpec((None, g, S, LANES), lambda b, p, i: (b, p, 0, 0)),
                  pl.BlockSpec((None, g // 2, S // tk, LANES, tk), lambda b, p, i: (b, p, 0, 0, 0))],
        out_specs=pl.BlockSpec((None, tq, g * MLA_V), lambda b, p, i: (b, i, p)),
        out_shape=jax.ShapeDtypeStruct((B, S, H * MLA_V), BF16),
        scratch_shapes=[pltpu.VMEM((g, LANES, tq), BF16), pltpu.VMEM((g * MLA_ACC_ROWS, tq), F32)],
        compiler_params=pltpu.CompilerParams(dimension_semantics=("arbitrary",) * 3,
                                             vmem_limit_bytes=VMEM_LIMIT),
        name="mla",
    )(qm, km, vt)


def _swa_kernel(sink_ref, q_ref, kp_ref, kc_ref, vp_ref, vc_ref, bias_ref, o_ref):
    band = 2 * WINDOW
    kband = jnp.concatenate([kp_ref[...], kc_ref[...]], axis=0).astype(F32)
    vband_t = jnp.concatenate([vp_ref[...], vc_ref[...]], axis=0).astype(F32).T
    lo = lax.broadcasted_iota(jnp.int32, kband.shape, 1) < SWA_HEAD
    kswap = pltpu.roll(kband, SWA_HEAD, 1)
    pairs = SWA_GROUP // 2
    for kvh in range(SWA_KV_HEADS):
        k_lo, k_hi = (kband, kswap) if kvh == 0 else (kswap, kband)
        kcat = jnp.concatenate([jnp.where(lo, k_lo, 0.0), jnp.where(lo, 0.0, k_hi)], axis=0).astype(BF16)
        qstack = jnp.concatenate([q_ref[:, (kvh * pairs + jj) * LANES:(kvh * pairs + jj + 1) * LANES]
                                  for jj in range(pairs)], axis=0)
        st = _dot_nt(kcat, qstack) + bias_ref[kvh]
        v_t = vband_t[kvh * SWA_HEAD:(kvh + 1) * SWA_HEAD, :].astype(BF16)
        halves = []
        for parity in range(2):
            ps, inv_l = [], []
            for jj in range(pairs):
                blk = st[parity * band:(parity + 1) * band, jj * WINDOW:(jj + 1) * WINDOW]
                sink = sink_ref[kvh * SWA_GROUP + 2 * jj + parity]
                m = jnp.maximum(jnp.max(blk, axis=0, keepdims=True), sink)
                e = jnp.exp(blk - m)
                inv_l.append(1.0 / (jnp.sum(e, axis=0, keepdims=True) + jnp.exp(sink - m)))
                ps.append(e.astype(BF16))
            halves.append(_dot(v_t, jnp.concatenate(ps, axis=1)) * jnp.concatenate(inv_l, axis=1))
        o_t = jnp.concatenate(halves, axis=0)
        for jj in range(pairs):
            pair = kvh * pairs + jj
            o_ref[:, pair * LANES:(pair + 1) * LANES] = o_t[:, jj * WINDOW:(jj + 1) * WINDOW].T.astype(BF16)


def _swa(sinks, qs, ks, vs, bias_tbl):
    B, S, W = qs.shape
    nblk = S // WINDOW
    prev = lambda b, n: (b, jnp.maximum(n - 1, 0), 0)
    cur = lambda b, n: (b, n, 0)
    kv_spec = lambda imap: pl.BlockSpec((None, WINDOW, LANES), imap)
    return pl.pallas_call(
        _swa_kernel,
        grid=(B, nblk),
        in_specs=[pl.BlockSpec(memory_space=pltpu.SMEM),
                  pl.BlockSpec((None, WINDOW, W), cur),
                  kv_spec(prev), kv_spec(cur), kv_spec(prev), kv_spec(cur),
                  pl.BlockSpec((None,) + bias_tbl.shape[1:], lambda b, n: (jnp.minimum(n, 1), 0, 0, 0))],
        out_specs=pl.BlockSpec((None, WINDOW, W), cur),
        out_shape=jax.ShapeDtypeStruct((B, S, W), BF16),
        compiler_params=pltpu.CompilerParams(dimension_semantics=("arbitrary", "arbitrary"),
                                             vmem_limit_bytes=VMEM_LIMIT),
        name="swa",
    )(sinks, qs, ks, ks, vs, vs, bias_tbl)


def _tail_kernel(x_ref, ym_ref, ys_ref, sh1_ref, sc1_ref, ga1_ref, sh2_ref, sc2_ref, ga2_ref,
                 gmix_ref, gmlp_ref, gfin_ref, bg_ref, wg_ref, wom_ref, wos_ref, wo_ref, w1_ref, w2_ref, o_ref):
    x = x_ref[...]
    h = (_rms(x, gmix_ref[...]) * (1 + sc1_ref[...]) + sh1_ref[...]).astype(BF16)
    gates = jax.nn.sigmoid(_dot(h, wg_ref[...]) + bg_ref[...])
    merged = (gates[:, :D_MODEL] * _dot(ym_ref[...], wom_ref[...])
              + gates[:, D_MODEL:] * _dot(ys_ref[...], wos_ref[...]))
    x1 = x + ga1_ref[...] * _dot(merged.astype(BF16), wo_ref[...])
    h2 = (_rms(x1, gmlp_ref[...]) * (1 + sc2_ref[...]) + sh2_ref[...]).astype(BF16)
    acc = jnp.zeros_like(x1)
    for c in range(D_FF // FF_CHUNK):
        u = jnp.square(jnp.maximum(_dot(h2, w1_ref[:, c * FF_CHUNK:(c + 1) * FF_CHUNK]), 0.0))
        acc = acc + _dot(u.astype(BF16), w2_ref[c * FF_CHUNK:(c + 1) * FF_CHUNK, :])
    x2 = x1 + ga2_ref[...] * acc
    o_ref[...] = _rms(x2, gfin_ref[...])


def _tail(x, ym, ys, mods, gmix, gmlp, gfin, bg, wg, wom, wos, wo, w1, w2):
    B, S, D = x.shape
    tm = TM_TAIL
    tok = lambda width: pl.BlockSpec((None, tm, width), lambda b, i: (b, i, 0))
    per_b = pl.BlockSpec((None, 1, D), lambda b, i: (b, 0, 0))
    const = lambda a: pl.BlockSpec(a.shape, lambda b, i: (0,) * a.ndim, pipeline_mode=pl.Buffered(1))
    return pl.pallas_call(
        _tail_kernel,
        grid=(B, S // tm),
        in_specs=[tok(D), tok(ym.shape[-1]), tok(ys.shape[-1])] + [per_b] * 6
                 + [const(a) for a in (gmix, gmlp, gfin, bg, wg, wom, wos, wo, w1, w2)],
        out_specs=tok(D),
        out_shape=jax.ShapeDtypeStruct((B, S, D), F32),
        compiler_params=pltpu.CompilerParams(dimension_semantics=("arbitrary", "arbitrary"),
                                             vmem_limit_bytes=VMEM_LIMIT),
        name="tail",
    )(x, ym, ys, *mods, gmix, gmlp, gfin, bg, wg, wom, wos, wo, w1, w2)


def _head_slab(nope, a, b):
    r, h, _ = nope.shape
    pad = jnp.zeros((r, h, LANES - MLA_NOPE - MLA_ROPE), nope.dtype)
    return jnp.concatenate([nope, a, b, pad], axis=-1).reshape(r, h * LANES)


def kernel(x, c, positions, rel_bias, ada_w, ada_b, ln_mix_g, w_in, b_gate, mla_q_norm_g, mla_kv_norm_g,
           w_uq, w_ukv, swa_sinks, w_o_mla, w_o_swa, w_o, ln_mlp_g, w_ff1, w_ff2, ln_final_g):
    B, S, D = x.shape
    assert (B, S, D) == (x.shape[0], 4096, D_MODEL) and ada_w.shape[0] == 1
    l = 0

    c_pad = jnp.pad(c, ((0, 8 - B), (0, 0)))
    mod = _adaln(c_pad, ada_w[l], ada_b[l][None, :])[:B]
    mods = [m[:, None, :] for m in jnp.split(mod, 6, axis=-1)]

    wi = w_in[l]
    o_kv = MLA_Q_RANK
    o_kr = o_kv + MLA_KV_RANK
    o_qs = o_kr + MLA_ROPE
    o_ks = o_qs + SWA_Q_HEADS * SWA_HEAD
    o_vs = o_ks + SWA_KV_HEADS * SWA_HEAD
    o_ga = o_vs + SWA_KV_HEADS * SWA_HEAD
    kr_a, kr_b = wi[:, o_kr:o_kr + HALF_ROPE], wi[:, o_kr + HALF_ROPE:o_qs]
    z = lambda n: jnp.zeros((D, n), wi.dtype)
    w1 = jnp.concatenate([
        wi[:, :o_kr],
        z(ROPE_A), kr_a, kr_b, z(LANES - ROPE_A - MLA_ROPE),
        z(ROPE_A), -kr_b, kr_a, z(LANES - ROPE_A - MLA_ROPE),
        wi[:, o_qs:o_ks] * (SWA_HEAD ** -0.5),
        wi[:, o_ks:o_ga]], axis=1).astype(BF16)
    assert w1.shape[1] == N_PROJ
    wg = wi[:, o_ga:].astype(BF16)

    uq = w_uq[l]
    q_nope, q_a, q_b = uq[..., :MLA_NOPE], uq[..., MLA_NOPE:MLA_NOPE + HALF_ROPE], uq[..., MLA_NOPE + HALF_ROPE:]
    wq = jnp.concatenate([_head_slab(q_nope, q_a, q_b),
                          _head_slab(jnp.zeros_like(q_nope), -q_b, q_a)], axis=1).astype(BF16)
    ukv = w_ukv[l]
    k_nope, v_up = ukv[..., :MLA_NOPE], ukv[..., MLA_NOPE:]
    zr = jnp.zeros(k_nope.shape[:2] + (HALF_ROPE,), ukv.dtype)
    wkv = jnp.concatenate([_head_slab(k_nope, zr, zr),
                           v_up.reshape(MLA_KV_RANK, MLA_HEADS * MLA_V)], axis=1).astype(BF16)

    inv = (ROPE_THETA ** (-jnp.arange(HALF_ROPE, dtype=F32) / HALF_ROPE))[:, None]

    qm, km, vm, qs, ks, vs = _proj(x, mods[0], mods[1], ln_mix_g[l][None, :], positions[:, None, :], inv, w1,
                                   mla_q_norm_g[l][None, :], mla_kv_norm_g[l][None, :], wq, wkv)

    y_mla = _mla(qm, km, vm)
    y_swa = _swa(swa_sinks[l], qs, ks, vs, _swa_bias(rel_bias))

    return _tail(x, y_mla, y_swa, mods, ln_mix_g[l][None, :], ln_mlp_g[l][None, :], ln_final_g[None, :],
                 b_gate[l][None, :], wg, w_o_mla[l].astype(BF16), w_o_swa[l].astype(BF16), w_o[l].astype(BF16),
                 w_ff1[l].astype(BF16), w_ff2[l].astype(BF16))
```

```python
import functools
import math

import jax
import jax.numpy as jnp
from jax import lax
from jax.experimental import pallas as pl
from jax.experimental.pallas import tpu as pltpu

F32 = jnp.float32
BF16 = jnp.bfloat16

D_MODEL = 1024
MLA_HEADS = 8
MLA_Q_RANK = 256
MLA_KV_RANK = 128
MLA_NOPE = 64
MLA_ROPE = 32
MLA_V = 64
SWA_Q_HEADS = 16
SWA_KV_HEADS = 2
SWA_HEAD = 64
WINDOW = 128
REL_BUCKETS = 32
REL_MAX_DIST = 128
D_FF = 4 * D_MODEL
ROPE_THETA = 10000.0
EPS = 1e-6

LANES = 128
HALF_ROPE = MLA_ROPE // 2
ROPE_A = MLA_NOPE
ROPE_B = MLA_NOPE + HALF_ROPE
N_PAIRS = SWA_Q_HEADS // 2
SWA_GROUP = SWA_Q_HEADS // SWA_KV_HEADS
SWA_STEP_BLOCKS = 2
LOG2E = math.log2(math.e)

C_QLAT = 0
C_KVLAT = C_QLAT + MLA_Q_RANK
C_KR = C_KVLAT + MLA_KV_RANK
C_KRS = C_KR + LANES
C_QS = C_KRS + LANES
C_KS = C_QS + SWA_Q_HEADS * SWA_HEAD
C_VS = C_KS + SWA_KV_HEADS * SWA_HEAD
N_PROJ = C_VS + SWA_KV_HEADS * SWA_HEAD

VMEM_LIMIT = 56 * 1024 * 1024

TM_PROJ = 512
TQ_MLA = 512
TK_MLA = 256
MLA_GROUP = 4
MLA_LOOKAHEAD = 4
MLA_ACC_ROWS = MLA_V + 16
TM_TAIL = 256
FF_CHUNK = 1024


def _rms(x, g):
    return x * lax.rsqrt(jnp.mean(x * x, axis=-1, keepdims=True) + EPS) * g


def _dot(a, b):
    return jnp.dot(a, b, preferred_element_type=F32)


def _dot_nt(a, b):
    return lax.dot_general(a, b, (((1,), (1,)), ((), ())), preferred_element_type=F32)


def _adaln_kernel(c_ref, w_ref, b_ref, o_ref):
    c = c_ref[...]
    act = (c * jax.nn.sigmoid(c)).astype(BF16)
    o_ref[...] = _dot(act, w_ref[...].astype(BF16)) + b_ref[...]


def _adaln(c_pad, ada_w, ada_b):
    rows = c_pad.shape[0]
    n_out = ada_w.shape[2]
    return pl.pallas_call(
        _adaln_kernel,
        grid=(n_out // D_MODEL,),
        in_specs=[pl.BlockSpec((rows, D_MODEL), lambda j: (0, 0)),
                  pl.BlockSpec((None, D_MODEL, D_MODEL), lambda j: (0, 0, j)),
                  pl.BlockSpec((1, D_MODEL), lambda j: (0, j))],
        out_specs=pl.BlockSpec((rows, D_MODEL), lambda j: (0, j)),
        out_shape=jax.ShapeDtypeStruct((rows, n_out), F32),
        compiler_params=pltpu.CompilerParams(dimension_semantics=("arbitrary",)),
        name="adaln",
    )(c_pad, ada_w, ada_b)


def _swa_bias_kernel(rb_ref, o_ref):
    a = lax.broadcasted_iota(jnp.int32, (2 * WINDOW, WINDOW), 1)
    b = lax.broadcasted_iota(jnp.int32, (2 * WINDOW, WINDOW), 0)
    dist = WINDOW + a - b
    n = jnp.maximum(dist, 0)
    max_exact = REL_BUCKETS // 2
    nf = jnp.maximum(n, 1).astype(F32)
    large = max_exact + (jnp.log(nf / max_exact) / math.log(REL_MAX_DIST / max_exact)
                         * (REL_BUCKETS - max_exact)).astype(jnp.int32)
    large = jnp.minimum(large, REL_BUCKETS - 1)
    bucket = jnp.where(n < max_exact, n, large)
    band_ok = (dist >= 0) & (dist < WINDOW)
    has_prev = b >= WINDOW
    neg = jnp.float32(-jnp.inf)
    for head in range(SWA_Q_HEADS):
        bias = jnp.zeros((2 * WINDOW, WINDOW), F32)
        for k in range(REL_BUCKETS):
            bias = jnp.where(bucket == k, rb_ref[head, k], bias)
        kvh, pair, parity = head // SWA_GROUP, (head % SWA_GROUP) // 2, head % 2
        rows = slice(parity * 2 * WINDOW, (parity + 1) * 2 * WINDOW)
        cols = slice(pair * WINDOW, (pair + 1) * WINDOW)
        o_ref[0, kvh, rows, cols] = jnp.where(band_ok & has_prev, bias * LOG2E, neg)
        o_ref[1, kvh, rows, cols] = jnp.where(band_ok, bias * LOG2E, neg)


def _swa_bias(rel_bias):
    return pl.pallas_call(
        _swa_bias_kernel,
        in_specs=[pl.BlockSpec(memory_space=pltpu.SMEM)],
        out_specs=pl.BlockSpec(memory_space=pltpu.VMEM),
        out_shape=jax.ShapeDtypeStruct((2, SWA_KV_HEADS, 4 * WINDOW, SWA_GROUP // 2 * WINDOW), F32),
        name="swa_bias",
    )(rel_bias)


def _proj_kernel(x_ref, sh_ref, sc_ref, g_ref, pos_ref, inv_ref, w1_ref, gq_ref, gkv_ref, wq_ref, wkv_ref,
                 qm_ref, km_ref, vm_ref, qs_ref, ks_ref, vs_ref):
    h = _rms(x_ref[...], g_ref[...]) * (1 + sc_ref[...]) + sh_ref[...]
    p = _dot(h.astype(BF16), w1_ref[...])

    ang = inv_ref[...] * pos_ref[...].astype(F32)
    row = lax.broadcasted_iota(jnp.int32, (HALF_ROPE, LANES), 0)
    lane = lax.broadcasted_iota(jnp.int32, (HALF_ROPE, LANES), 1)
    spread = ((lane == row + ROPE_A) | (lane == row + ROPE_B)).astype(F32)
    cos_f = jnp.dot(jnp.cos(ang).T, spread, precision=lax.Precision.HIGHEST, preferred_element_type=F32)
    sin_f = jnp.dot(jnp.sin(ang).T, spread, precision=lax.Precision.HIGHEST, preferred_element_type=F32)
    lane_t = lax.broadcasted_iota(jnp.int32, cos_f.shape, 1)
    cos_q = jnp.where(lane_t < MLA_NOPE, 1.0, cos_f)

    qn = _rms(p[:, C_QLAT:C_QLAT + MLA_Q_RANK], gq_ref[...]).astype(BF16)
    qq = _dot(qn, wq_ref[...])
    scale = (MLA_NOPE + MLA_ROPE) ** -0.5 * LOG2E
    half = MLA_HEADS * LANES
    for hd in range(MLA_HEADS):
        q = qq[:, hd * LANES:(hd + 1) * LANES] * cos_q + qq[:, half + hd * LANES:half + (hd + 1) * LANES] * sin_f
        qm_ref[hd] = (q * scale).astype(BF16)

    kvn = _rms(p[:, C_KVLAT:C_KVLAT + MLA_KV_RANK], gkv_ref[...]).astype(BF16)
    kk = _dot(kvn, wkv_ref[...])
    k_pe = p[:, C_KR:C_KR + LANES] * cos_f + p[:, C_KRS:C_KRS + LANES] * sin_f
    for hd in range(MLA_HEADS):
        km_ref[hd] = (kk[:, hd * LANES:(hd + 1) * LANES] + k_pe).astype(BF16)
    for pr in range(MLA_HEADS // 2):
        v_t = kk[:, half + pr * LANES:half + (pr + 1) * LANES].T.astype(BF16)
        for t in range(v_t.shape[1] // TK_MLA):
            vm_ref[pr, t] = v_t[:, t * TK_MLA:(t + 1) * TK_MLA]

    qs_ref[...] = (p[:, C_QS:C_KS] * (SWA_HEAD ** -0.5 * LOG2E)).astype(BF16)
    ks_ref[...] = p[:, C_KS:C_VS].astype(BF16)
    vs_ref[...] = p[:, C_VS:N_PROJ].astype(BF16)


def _proj(x, sh1, sc1, ln_g, pos3, inv, w1, gq, gkv, wq, wkv):
    B, S, D = x.shape
    tm = TM_PROJ
    const = lambda shape: pl.BlockSpec(shape, lambda b, i: (0,) * len(shape))
    return pl.pallas_call(
        _proj_kernel,
        grid=(B, S // tm),
        in_specs=[pl.BlockSpec((None, tm, D), lambda b, i: (b, i, 0)),
                  pl.BlockSpec((None, 1, D), lambda b, i: (b, 0, 0)),
                  pl.BlockSpec((None, 1, D), lambda b, i: (b, 0, 0)),
                  const((1, D)),
                  pl.BlockSpec((None, 1, tm), lambda b, i: (b, 0, i)),
                  const((HALF_ROPE, 1)),
                  const(w1.shape), const(gq.shape), const(gkv.shape), const(wq.shape), const(wkv.shape)],
        out_specs=[pl.BlockSpec((None, MLA_HEADS, tm, LANES), lambda b, i: (b, 0, i, 0)),
                   pl.BlockSpec((None, MLA_HEADS, tm, LANES), lambda b, i: (b, 0, i, 0)),
                   pl.BlockSpec((None, MLA_HEADS // 2, tm // TK_MLA, LANES, TK_MLA), lambda b, i: (b, 0, i, 0, 0)),
                   pl.BlockSpec((None, tm, SWA_Q_HEADS * SWA_HEAD), lambda b, i: (b, i, 0)),
                   pl.BlockSpec((None, tm, LANES), lambda b, i: (b, i, 0)),
                   pl.BlockSpec((None, tm, LANES), lambda b, i: (b, i, 0))],
        out_shape=[jax.ShapeDtypeStruct((B, MLA_HEADS, S, LANES), BF16),
                   jax.ShapeDtypeStruct((B, MLA_HEADS, S, LANES), BF16),
                   jax.ShapeDtypeStruct((B, MLA_HEADS // 2, S // TK_MLA, LANES, TK_MLA), BF16),
                   jax.ShapeDtypeStruct((B, S, SWA_Q_HEADS * SWA_HEAD), BF16),
                   jax.ShapeDtypeStruct((B, S, LANES), BF16),
                   jax.ShapeDtypeStruct((B, S, LANES), BF16)],
        compiler_params=pltpu.CompilerParams(dimension_semantics=("arbitrary", "arbitrary"),
                                             vmem_limit_bytes=VMEM_LIMIT),
        name="proj",
    )(x, sh1, sc1, ln_g, pos3, inv, w1, gq, gkv, wq, wkv)


def _mla_kernel(q_ref, k_ref, vt_ref, o_ref, qt_sc, acc_sc):
    tq, tk = TQ_MLA, TK_MLA
    nc = tq // LANES
    i = pl.program_id(2)
    for hd in range(MLA_GROUP):
        qt_sc[hd] = q_ref[hd].astype(F32).T.astype(BF16)
    acc_sc[...] = jnp.zeros_like(acc_sc)

    per_tile = tq // tk

    ones_rows = (lax.broadcasted_iota(jnp.int32, (MLA_ACC_ROWS - MLA_V, tk), 0) == 0).astype(BF16)
    r_iota = lax.broadcasted_iota(jnp.int32, (tk, LANES), 0)
    c_iota = lax.broadcasted_iota(jnp.int32, (tk, LANES), 1)

    def tile(t, ms_all, diagonal):
        blocks = [(d, hd) for d in range(per_tile) for hd in range(MLA_GROUP)]
        first_cc = [(d * tk) // LANES if diagonal else 0 for d, _ in blocks]

        def scores(n):
            d, hd = blocks[n]
            k = k_ref[hd, pl.ds(pl.multiple_of((t * per_tile + d) * tk, tk), tk), :]
            return _dot(k, qt_sc[hd, :, first_cc[n] * LANES:])

        pending = {n: scores(n) for n in range(min(MLA_LOOKAHEAD, len(blocks)))}
        ms_all = list(ms_all)
        for n, (d, hd) in enumerate(blocks):
            if n + MLA_LOOKAHEAD < len(blocks):
                pending[n + MLA_LOOKAHEAD] = scores(n + MLA_LOOKAHEAD)
            st = pending.pop(n)
            ms = list(ms_all[hd])
            ps, alphas = [], []
            for cc in range(first_cc[n], nc):
                blk = st[:, (cc - first_cc[n]) * LANES:(cc - first_cc[n] + 1) * LANES]
                if diagonal and cc * LANES < (d + 1) * tk - 1:
                    blk = jnp.where(r_iota + (d * tk - cc * LANES) <= c_iota, blk, -jnp.inf)
                m_new = jnp.maximum(ms[cc], jnp.max(blk, axis=0, keepdims=True))
                alphas.append(jnp.exp2(ms[cc] - m_new))
                ps.append(jnp.exp2(blk - m_new).astype(BF16))
                ms[cc] = m_new
            v_t = vt_ref[hd // 2, t * per_tile + d, (hd % 2) * MLA_V:(hd % 2 + 1) * MLA_V, :]
            pv = _dot(jnp.concatenate([v_t, ones_rows], axis=0), jnp.concatenate(ps, axis=1))
            rows = slice(hd * MLA_ACC_ROWS, (hd + 1) * MLA_ACC_ROWS)
            cols = slice(first_cc[n] * LANES, tq)
            acc_sc[rows, cols] = acc_sc[rows, cols] * jnp.concatenate(alphas, axis=1) + pv
            ms_all[hd] = tuple(ms)
        return tuple(ms_all)

    init = tuple(tuple(jnp.full((1, LANES), -jnp.inf, F32) for _ in range(nc)) for _ in range(MLA_GROUP))
    ms_all = lax.fori_loop(0, i, lambda t, c: tile(t, c, False), init)
    tile(i, ms_all, True)
    outs = []
    for hd in range(MLA_GROUP):
        acc = acc_sc[hd * MLA_ACC_ROWS:(hd + 1) * MLA_ACC_ROWS, :]
        outs.append(acc[:MLA_V] / acc[MLA_V:MLA_V + 1])
    o_ref[...] = jnp.concatenate(outs, axis=0).T.astype(BF16)


def _mla(qm, km, vt):
    B, H, S, _ = qm.shape
    tq, tk, g = TQ_MLA, TK_MLA, MLA_GROUP
    return pl.pallas_call(
        _mla_kernel,
        grid=(B, H // g, S // tq),
        in_specs=[pl.BlockSpec((None, g, tq, LANES), lambda b, p, i: (b, p, i, 0)),
                  pl.BlockSpec((None, g, S, LANES), lambda b, p, i: (b, p, 0, 0)),
                  pl.BlockSpec((None, g // 2, S // tk, LANES, tk), lambda b, p, i: (b, p, 0, 0, 0))],
        out_specs=pl.BlockSpec((None, tq, g * MLA_V), lambda b, p, i: (b, i, p)),
        out_shape=jax.ShapeDtypeStruct((B, S, H * MLA_V), BF16),
        scratch_shapes=[pltpu.VMEM((g, LANES, tq), BF16), pltpu.VMEM((g * MLA_ACC_ROWS, tq), F32)],
        compiler_params=pltpu.CompilerParams(dimension_semantics=("arbitrary",) * 3,
                                             vmem_limit_bytes=VMEM_LIMIT),
        name="mla",
    )(qm, km, vt)


def _swa_kernel(sink_ref, q_ref, kp_ref, kc_ref, vp_ref, vc_ref, bias0_ref, bias_ref, o_ref):
    band = 2 * WINDOW
    pairs = SWA_GROUP // 2
    kband = jnp.concatenate([kp_ref[...], kc_ref[...]], axis=0).astype(F32)
    vband_t = jnp.concatenate([vp_ref[...], vc_ref[...]], axis=0).astype(F32).T
    lo = lax.broadcasted_iota(jnp.int32, kband.shape, 1) < SWA_HEAD
    kswap = pltpu.roll(kband, SWA_HEAD, 1)
    k_even = [jnp.where(lo, kband, 0.0).astype(BF16), jnp.where(lo, kswap, 0.0).astype(BF16)]
    k_odd = [jnp.where(lo, 0.0, kswap).astype(BF16), jnp.where(lo, 0.0, kband).astype(BF16)]
    ones_rows = (lax.broadcasted_iota(jnp.int32, (16, band), 0) == 0).astype(BF16)
    chains = [(j, kvh) for j in range(SWA_STEP_BLOCKS) for kvh in range(SWA_KV_HEADS)]

    def scores(j, kvh):
        keys = slice(j * WINDOW, j * WINDOW + band)
        kcat = jnp.concatenate([k_even[kvh][keys], k_odd[kvh][keys]], axis=0)
        qstack = jnp.concatenate([q_ref[j * WINDOW:(j + 1) * WINDOW, (kvh * pairs + jj) * LANES:
                                        (kvh * pairs + jj + 1) * LANES] for jj in range(pairs)], axis=0)
        bias = bias0_ref if j == 0 else bias_ref
        return _dot_nt(kcat, qstack) + bias[kvh]

    sts = [scores(j, kvh) for j, kvh in chains]
    for (j, kvh), st in zip(chains, sts):
        v_t = vband_t[kvh * SWA_HEAD:(kvh + 1) * SWA_HEAD, j * WINDOW:j * WINDOW + band].astype(BF16)
        v_aug = jnp.concatenate([v_t, ones_rows], axis=0)
        halves = []
        for parity in range(2):
            ps, sink_terms = [], []
            for jj in range(pairs):
                blk = st[parity * band:(parity + 1) * band, jj * WINDOW:(jj + 1) * WINDOW]
                sink = sink_ref[kvh * SWA_GROUP + 2 * jj + parity] * LOG2E
                m = jnp.maximum(jnp.max(blk, axis=0, keepdims=True), sink)
                ps.append(jnp.exp2(blk - m).astype(BF16))
                sink_terms.append(jnp.exp2(sink - m))
            pv = _dot(v_aug, jnp.concatenate(ps, axis=1))
            halves.append(pv[:SWA_HEAD] / (pv[SWA_HEAD:SWA_HEAD + 1] + jnp.concatenate(sink_terms, axis=1)))
        o_t = jnp.concatenate(halves, axis=0)
        for jj in range(pairs):
            pair = kvh * pairs + jj
            o_ref[j * WINDOW:(j + 1) * WINDOW, pair * LANES:(pair + 1) * LANES] = (
                o_t[:, jj * WINDOW:(jj + 1) * WINDOW].T.astype(BF16))


def _swa(sinks, qs, ks, vs, bias_tbl):
    B, S, W = qs.shape
    nb = SWA_STEP_BLOCKS
    prev = lambda b, n: (b, jnp.maximum(nb * n - 1, 0), 0)
    cur = lambda b, n: (b, n, 0)
    tbl_block = (None,) + bias_tbl.shape[1:]
    return pl.pallas_call(
        _swa_kernel,
        grid=(B, S // (nb * WINDOW)),
        in_specs=[pl.BlockSpec(memory_space=pltpu.SMEM),
                  pl.BlockSpec((None, nb * WINDOW, W), cur),
                  pl.BlockSpec((None, WINDOW, LANES), prev), pl.BlockSpec((None, nb * WINDOW, LANES), cur),
                  pl.BlockSpec((None, WINDOW, LANES), prev), pl.BlockSpec((None, nb * WINDOW, LANES), cur),
                  pl.BlockSpec(tbl_block, lambda b, n: (jnp.minimum(n, 1), 0, 0, 0)),
                  pl.BlockSpec(tbl_block, lambda b, n: (1, 0, 0, 0))],
        out_specs=pl.BlockSpec((None, nb * WINDOW, W), cur),
        out_shape=jax.ShapeDtypeStruct((B, S, W), BF16),
        compiler_params=pltpu.CompilerParams(dimension_semantics=("arbitrary", "arbitrary"),
                                             vmem_limit_bytes=VMEM_LIMIT),
        name="swa",
    )(sinks, qs, ks, ks, vs, vs, bias_tbl, bias_tbl)


def _tail_kernel(x_ref, ym_ref, ys_ref, sh1_ref, sc1_ref, ga1_ref, sh2_ref, sc2_ref, ga2_ref,
                 gmix_ref, gmlp_ref, gfin_ref, bg_ref, wg_ref, wom_ref, wos_ref, wo_ref, w1_ref, w2_ref, o_ref):
    x = x_ref[...]
    h = (_rms(x, gmix_ref[...]) * (1 + sc1_ref[...]) + sh1_ref[...]).astype(BF16)
    gates = jax.nn.sigmoid(_dot(h, wg_ref[...]) + bg_ref[...])
    merged = (gates[:, :D_MODEL] * _dot(ym_ref[...], wom_ref[...])
              + gates[:, D_MODEL:] * _dot(ys_ref[...], wos_ref[...]))
    x1 = x + ga1_ref[...] * _dot(merged.astype(BF16), wo_ref[...])
    h2 = (_rms(x1, gmlp_ref[...]) * (1 + sc2_ref[...]) + sh2_ref[...]).astype(BF16)
    acc = jnp.zeros_like(x1)
    for c in range(D_FF // FF_CHUNK):
        u = jnp.square(jnp.maximum(_dot(h2, w1_ref[:, c * FF_CHUNK:(c + 1) * FF_CHUNK]), 0.0))
        acc = acc + _dot(u.astype(BF16), w2_ref[c * FF_CHUNK:(c + 1) * FF_CHUNK, :])
    x2 = x1 + ga2_ref[...] * acc
    o_ref[...] = _rms(x2, gfin_ref[...])


def _tail(x, ym, ys, mods, gmix, gmlp, gfin, bg, wg, wom, wos, wo, w1, w2):
    B, S, D = x.shape
    tm = TM_TAIL
    tok = lambda width: pl.BlockSpec((None, tm, width), lambda b, i: (b, i, 0))
    per_b = pl.BlockSpec((None, 1, D), lambda b, i: (b, 0, 0))
    const = lambda a: pl.BlockSpec(a.shape, lambda b, i: (0,) * a.ndim, pipeline_mode=pl.Buffered(1))
    return pl.pallas_call(
        _tail_kernel,
        grid=(B, S // tm),
        in_specs=[tok(D), tok(ym.shape[-1]), tok(ys.shape[-1])] + [per_b] * 6
                 + [const(a) for a in (gmix, gmlp, gfin, bg, wg, wom, wos, wo, w1, w2)],
        out_specs=tok(D),
        out_shape=jax.ShapeDtypeStruct((B, S, D), F32),
        compiler_params=pltpu.CompilerParams(dimension_semantics=("arbitrary", "arbitrary"),
                                             vmem_limit_bytes=VMEM_LIMIT),
        name="tail",
    )(x, ym, ys, *mods, gmix, gmlp, gfin, bg, wg, wom, wos, wo, w1, w2)


def _head_slab(nope, a, b):
    r, h, _ = nope.shape
    pad = jnp.zeros((r, h, LANES - MLA_NOPE - MLA_ROPE), nope.dtype)
    return jnp.concatenate([nope, a, b, pad], axis=-1).reshape(r, h * LANES)


def kernel(x, c, positions, rel_bias, ada_w, ada_b, ln_mix_g, w_in, b_gate, mla_q_norm_g, mla_kv_norm_g,
           w_uq, w_ukv, swa_sinks, w_o_mla, w_o_swa, w_o, ln_mlp_g, w_ff1, w_ff2, ln_final_g):
    B, S, D = x.shape
    assert (B, S, D) == (x.shape[0], 4096, D_MODEL) and ada_w.shape[0] == 1
    l = 0

    c_pad = jnp.pad(c, ((0, 8 - B), (0, 0)))
    mod = _adaln(c_pad, ada_w, ada_b)[:B]
    mods = [m[:, None, :] for m in jnp.split(mod, 6, axis=-1)]

    wi = w_in[l]
    o_kv = MLA_Q_RANK
    o_kr = o_kv + MLA_KV_RANK
    o_qs = o_kr + MLA_ROPE
    o_ks = o_qs + SWA_Q_HEADS * SWA_HEAD
    o_vs = o_ks + SWA_KV_HEADS * SWA_HEAD
    o_ga = o_vs + SWA_KV_HEADS * SWA_HEAD
    kr_a, kr_b = wi[:, o_kr:o_kr + HALF_ROPE], wi[:, o_kr + HALF_ROPE:o_qs]
    z = lambda n: jnp.zeros((D, n), wi.dtype)
    w1 = jnp.concatenate([
        wi[:, :o_kr],
        z(ROPE_A), kr_a, kr_b, z(LANES - ROPE_A - MLA_ROPE),
        z(ROPE_A), -kr_b, kr_a, z(LANES - ROPE_A - MLA_ROPE),
        wi[:, o_qs:o_ga]], axis=1).astype(BF16)
    assert w1.shape[1] == N_PROJ
    wg = wi[:, o_ga:].astype(BF16)

    uq = w_uq[l]
    q_nope, q_a, q_b = uq[..., :MLA_NOPE], uq[..., MLA_NOPE:MLA_NOPE + HALF_ROPE], uq[..., MLA_NOPE + HALF_ROPE:]
    wq = jnp.concatenate([_head_slab(q_nope, q_a, q_b),
                          _head_slab(jnp.zeros_like(q_nope), -q_b, q_a)], axis=1).astype(BF16)
    ukv = w_ukv[l]
    k_nope, v_up = ukv[..., :MLA_NOPE], ukv[..., MLA_NOPE:]
    zr = jnp.zeros(k_nope.shape[:2] + (HALF_ROPE,), ukv.dtype)
    wkv = jnp.concatenate([_head_slab(k_nope, zr, zr),
                           v_up.reshape(MLA_KV_RANK, MLA_HEADS * MLA_V)], axis=1).astype(BF16)

    inv = (ROPE_THETA ** (-jnp.arange(HALF_ROPE, dtype=F32) / HALF_ROPE))[:, None]

    qm, km, vm, qs, ks, vs = _proj(x, mods[0], mods[1], ln_mix_g[l][None, :], positions[:, None, :], inv, w1,
                                   mla_q_norm_g[l][None, :], mla_kv_norm_g[l][None, :], wq, wkv)

    y_mla = _mla(qm, km, vm)
    y_swa = _swa(swa_sinks[l], qs, ks, vs, _swa_bias(rel_bias))

    return _tail(x, y_mla, y_swa, mods, ln_mix_g[l][None, :], ln_mlp_g[l][None, :], ln_final_g[None, :],
                 b_gate[l][None, :], wg, w_o_mla[l].astype(BF16), w_o_swa[l].astype(BF16), w_o[l].astype(BF16),
                 w_ff1[l].astype(BF16), w_ff2[l].astype(BF16))
```

```python
import functools
import math

import jax
import jax.numpy as jnp
from jax import lax
from jax.experimental import pallas as pl
from jax.experimental.pallas import tpu as pltpu

F32 = jnp.float32
BF16 = jnp.bfloat16

D_MODEL = 1024
MLA_HEADS = 8
MLA_Q_RANK = 256
MLA_KV_RANK = 128
MLA_NOPE = 64
MLA_ROPE = 32
MLA_V = 64
SWA_Q_HEADS = 16
SWA_KV_HEADS = 2
SWA_HEAD = 64
WINDOW = 128
REL_BUCKETS = 32
REL_MAX_DIST = 128
D_FF = 4 * D_MODEL
ROPE_THETA = 10000.0
EPS = 1e-6

LANES = 128
HALF_ROPE = MLA_ROPE // 2
ROPE_A = MLA_NOPE
ROPE_B = MLA_NOPE + HALF_ROPE
N_PAIRS = SWA_Q_HEADS // 2
SWA_GROUP = SWA_Q_HEADS // SWA_KV_HEADS
SWA_STEP_BLOCKS = 2
LOG2E = math.log2(math.e)

C_QLAT = 0
C_KVLAT = C_QLAT + MLA_Q_RANK
C_KR = C_KVLAT + MLA_KV_RANK
C_KRS = C_KR + LANES
C_QS = C_KRS + LANES
C_KS = C_QS + SWA_Q_HEADS * SWA_HEAD
C_VS = C_KS + SWA_KV_HEADS * SWA_HEAD
N_PROJ = C_VS + SWA_KV_HEADS * SWA_HEAD

VMEM_LIMIT = 56 * 1024 * 1024

TM_PROJ = 512
TQ_MLA = 1024
TK_MLA = 256
MLA_GROUP = 4
MLA_LOOKAHEAD = 4
MLA_ACC_ROWS = MLA_V + 16
TM_TAIL = 256
FF_CHUNK = 1024


def _rms(x, g):
    return x * lax.rsqrt(jnp.mean(x * x, axis=-1, keepdims=True) + EPS) * g


def _dot(a, b):
    return jnp.dot(a, b, preferred_element_type=F32)


def _dot_nt(a, b):
    return lax.dot_general(a, b, (((1,), (1,)), ((), ())), preferred_element_type=F32)


def _adaln_kernel(c_ref, w_ref, b_ref, o_ref):
    c = c_ref[...]
    act = (c * jax.nn.sigmoid(c)).astype(BF16)
    o_ref[...] = _dot(act, w_ref[...].astype(BF16)) + b_ref[...]


def _adaln(c_pad, ada_w, ada_b):
    rows = c_pad.shape[0]
    n_out = ada_w.shape[2]
    return pl.pallas_call(
        _adaln_kernel,
        grid=(n_out // D_MODEL,),
        in_specs=[pl.BlockSpec((rows, D_MODEL), lambda j: (0, 0)),
                  pl.BlockSpec((None, D_MODEL, D_MODEL), lambda j: (0, 0, j)),
                  pl.BlockSpec((1, D_MODEL), lambda j: (0, j))],
        out_specs=pl.BlockSpec((rows, D_MODEL), lambda j: (0, j)),
        out_shape=jax.ShapeDtypeStruct((rows, n_out), F32),
        compiler_params=pltpu.CompilerParams(dimension_semantics=("arbitrary",)),
        name="adaln",
    )(c_pad, ada_w, ada_b)


def _swa_bias_kernel(rb_ref, o_ref):
    a = lax.broadcasted_iota(jnp.int32, (2 * WINDOW, WINDOW), 1)
    b = lax.broadcasted_iota(jnp.int32, (2 * WINDOW, WINDOW), 0)
    dist = WINDOW + a - b
    n = jnp.maximum(dist, 0)
    max_exact = REL_BUCKETS // 2
    nf = jnp.maximum(n, 1).astype(F32)
    large = max_exact + (jnp.log(nf / max_exact) / math.log(REL_MAX_DIST / max_exact)
                         * (REL_BUCKETS - max_exact)).astype(jnp.int32)
    large = jnp.minimum(large, REL_BUCKETS - 1)
    bucket = jnp.where(n < max_exact, n, large)
    band_ok = (dist >= 0) & (dist < WINDOW)
    has_prev = b >= WINDOW
    neg = jnp.float32(-jnp.inf)
    for head in range(SWA_Q_HEADS):
        bias = jnp.zeros((2 * WINDOW, WINDOW), F32)
        for k in range(REL_BUCKETS):
            bias = jnp.where(bucket == k, rb_ref[head, k], bias)
        kvh, pair, parity = head // SWA_GROUP, (head % SWA_GROUP) // 2, head % 2
        rows = slice(parity * 2 * WINDOW, (parity + 1) * 2 * WINDOW)
        cols = slice(pair * WINDOW, (pair + 1) * WINDOW)
        o_ref[0, kvh, rows, cols] = jnp.where(band_ok & has_prev, bias * LOG2E, neg)
        o_ref[1, kvh, rows, cols] = jnp.where(band_ok, bias * LOG2E, neg)


def _swa_bias(rel_bias):
    return pl.pallas_call(
        _swa_bias_kernel,
        in_specs=[pl.BlockSpec(memory_space=pltpu.SMEM)],
        out_specs=pl.BlockSpec(memory_space=pltpu.VMEM),
        out_shape=jax.ShapeDtypeStruct((2, SWA_KV_HEADS, 4 * WINDOW, SWA_GROUP // 2 * WINDOW), F32),
        name="swa_bias",
    )(rel_bias)


def _proj_kernel(x_ref, sh_ref, sc_ref, g_ref, pos_ref, inv_ref, w1_ref, gq_ref, gkv_ref, wq_ref, wkv_ref,
                 qm_ref, km_ref, vm_ref, qs_ref, ks_ref, vs_ref):
    h = _rms(x_ref[...], g_ref[...]) * (1 + sc_ref[...]) + sh_ref[...]
    p = _dot_nt(h.astype(BF16), w1_ref[...])

    ang = inv_ref[...] * pos_ref[...].astype(F32)
    row = lax.broadcasted_iota(jnp.int32, (HALF_ROPE, LANES), 0)
    lane = lax.broadcasted_iota(jnp.int32, (HALF_ROPE, LANES), 1)
    spread = ((lane == row + ROPE_A) | (lane == row + ROPE_B)).astype(F32)
    cos_f = jnp.dot(jnp.cos(ang).T, spread, precision=lax.Precision.HIGHEST, preferred_element_type=F32)
    sin_f = jnp.dot(jnp.sin(ang).T, spread, precision=lax.Precision.HIGHEST, preferred_element_type=F32)
    lane_t = lax.broadcasted_iota(jnp.int32, cos_f.shape, 1)
    cos_q = jnp.where(lane_t < MLA_NOPE, 1.0, cos_f)

    qn = _rms(p[:, C_QLAT:C_QLAT + MLA_Q_RANK], gq_ref[...]).astype(BF16)
    qq = _dot(qn, wq_ref[...])
    scale = (MLA_NOPE + MLA_ROPE) ** -0.5 * LOG2E
    half = MLA_HEADS * LANES
    for hd in range(MLA_HEADS):
        q = qq[:, hd * LANES:(hd + 1) * LANES] * cos_q + qq[:, half + hd * LANES:half + (hd + 1) * LANES] * sin_f
        qm_ref[hd] = (q * scale).astype(BF16)

    kvn = _rms(p[:, C_KVLAT:C_KVLAT + MLA_KV_RANK], gkv_ref[...]).astype(BF16)
    kk = _dot(kvn, wkv_ref[...])
    k_pe = p[:, C_KR:C_KR + LANES] * cos_f + p[:, C_KRS:C_KRS + LANES] * sin_f
    for hd in range(MLA_HEADS):
        km_ref[hd] = (kk[:, hd * LANES:(hd + 1) * LANES] + k_pe).astype(BF16)
    for pr in range(MLA_HEADS // 2):
        v_t = kk[:, half + pr * LANES:half + (pr + 1) * LANES].T.astype(BF16)
        for t in range(v_t.shape[1] // TK_MLA):
            vm_ref[pr, t] = v_t[:, t * TK_MLA:(t + 1) * TK_MLA]

    qs_ref[...] = (p[:, C_QS:C_KS] * (SWA_HEAD ** -0.5 * LOG2E)).astype(BF16)
    ks_ref[...] = p[:, C_KS:C_VS].astype(BF16)
    vs_ref[...] = p[:, C_VS:N_PROJ].astype(BF16)


def _proj(x, sh1, sc1, ln_g, pos3, inv, w1, gq, gkv, wq, wkv):
    B, S, D = x.shape
    tm = TM_PROJ
    const = lambda shape: pl.BlockSpec(shape, lambda b, i: (0,) * len(shape))
    return pl.pallas_call(
        _proj_kernel,
        grid=(B, S // tm),
        in_specs=[pl.BlockSpec((None, tm, D), lambda b, i: (b, i, 0)),
                  pl.BlockSpec((None, 1, D), lambda b, i: (b, 0, 0)),
                  pl.BlockSpec((None, 1, D), lambda b, i: (b, 0, 0)),
                  const((1, D)),
                  pl.BlockSpec((None, 1, tm), lambda b, i: (b, 0, i)),
                  const((HALF_ROPE, 1)),
                  const(w1.shape), const(gq.shape), const(gkv.shape), const(wq.shape), const(wkv.shape)],
        out_specs=[pl.BlockSpec((None, MLA_HEADS, tm, LANES), lambda b, i: (b, 0, i, 0)),
                   pl.BlockSpec((None, MLA_HEADS, tm, LANES), lambda b, i: (b, 0, i, 0)),
                   pl.BlockSpec((None, MLA_HEADS // 2, tm // TK_MLA, LANES, TK_MLA), lambda b, i: (b, 0, i, 0, 0)),
                   pl.BlockSpec((None, tm, SWA_Q_HEADS * SWA_HEAD), lambda b, i: (b, i, 0)),
                   pl.BlockSpec((None, tm, LANES), lambda b, i: (b, i, 0)),
                   pl.BlockSpec((None, tm, LANES), lambda b, i: (b, i, 0))],
        out_shape=[jax.ShapeDtypeStruct((B, MLA_HEADS, S, LANES), BF16),
                   jax.ShapeDtypeStruct((B, MLA_HEADS, S, LANES), BF16),
                   jax.ShapeDtypeStruct((B, MLA_HEADS // 2, S // TK_MLA, LANES, TK_MLA), BF16),
                   jax.ShapeDtypeStruct((B, S, SWA_Q_HEADS * SWA_HEAD), BF16),
                   jax.ShapeDtypeStruct((B, S, LANES), BF16),
                   jax.ShapeDtypeStruct((B, S, LANES), BF16)],
        compiler_params=pltpu.CompilerParams(dimension_semantics=("arbitrary", "arbitrary"),
                                             vmem_limit_bytes=VMEM_LIMIT),
        name="proj",
    )(x, sh1, sc1, ln_g, pos3, inv, w1, gq, gkv, wq, wkv)


def _mla_kernel(q_ref, k_ref, vt_ref, o_ref, qt_sc, acc_sc):
    tq, tk = TQ_MLA, TK_MLA
    nc = tq // LANES
    i = pl.program_id(2)
    for hd in range(MLA_GROUP):
        qt_sc[hd] = q_ref[hd].astype(F32).T.astype(BF16)
    acc_sc[...] = jnp.zeros_like(acc_sc)

    per_tile = tq // tk

    ones_rows = (lax.broadcasted_iota(jnp.int32, (MLA_ACC_ROWS - MLA_V, tk), 0) == 0).astype(BF16)
    r_iota = lax.broadcasted_iota(jnp.int32, (tk, LANES), 0)
    c_iota = lax.broadcasted_iota(jnp.int32, (tk, LANES), 1)

    def tile(t, ms_all, diagonal):
        blocks = [(d, hd) for d in range(per_tile) for hd in range(MLA_GROUP)]
        first_cc = [(d * tk) // LANES if diagonal else 0 for d, _ in blocks]

        def scores(n):
            d, hd = blocks[n]
            k = k_ref[hd, pl.ds(pl.multiple_of((t * per_tile + d) * tk, tk), tk), :]
            return _dot(k, qt_sc[hd, :, first_cc[n] * LANES:])

        pending = {n: scores(n) for n in range(min(MLA_LOOKAHEAD, len(blocks)))}
        ms_all = list(ms_all)
        for n, (d, hd) in enumerate(blocks):
            if n + MLA_LOOKAHEAD < len(blocks):
                pending[n + MLA_LOOKAHEAD] = scores(n + MLA_LOOKAHEAD)
            st = pending.pop(n)
            ms = list(ms_all[hd])
            ps, alphas = [], []
            for cc in range(first_cc[n], nc):
                blk = st[:, (cc - first_cc[n]) * LANES:(cc - first_cc[n] + 1) * LANES]
                if diagonal and cc * LANES < (d + 1) * tk - 1:
                    blk = jnp.where(r_iota + (d * tk - cc * LANES) <= c_iota, blk, -jnp.inf)
                m_new = jnp.maximum(ms[cc], jnp.max(blk, axis=0, keepdims=True))
                alphas.append(jnp.exp2(ms[cc] - m_new))
                ps.append(jnp.exp2(blk - m_new).astype(BF16))
                ms[cc] = m_new
            v_t = vt_ref[hd // 2, t * per_tile + d, (hd % 2) * MLA_V:(hd % 2 + 1) * MLA_V, :]
            pv = _dot(jnp.concatenate([v_t, ones_rows], axis=0), jnp.concatenate(ps, axis=1))
            rows = slice(hd * MLA_ACC_ROWS, (hd + 1) * MLA_ACC_ROWS)
            cols = slice(first_cc[n] * LANES, tq)
            acc_sc[rows, cols] = acc_sc[rows, cols] * jnp.concatenate(alphas, axis=1) + pv
            ms_all[hd] = tuple(ms)
        return tuple(ms_all)

    init = tuple(tuple(jnp.full((1, LANES), -jnp.inf, F32) for _ in range(nc)) for _ in range(MLA_GROUP))
    ms_all = lax.fori_loop(0, i, lambda t, c: tile(t, c, False), init)
    tile(i, ms_all, True)
    outs = []
    for hd in range(MLA_GROUP):
        acc = acc_sc[hd * MLA_ACC_ROWS:(hd + 1) * MLA_ACC_ROWS, :]
        outs.append(acc[:MLA_V] / acc[MLA_V:MLA_V + 1])
    o_ref[...] = jnp.concatenate(outs, axis=0).T.astype(BF16)


def _mla(qm, km, vt):
    B, H, S, _ = qm.shape
    tq, tk, g = TQ_MLA, TK_MLA, MLA_GROUP
    return pl.pallas_call(
        _mla_kernel,
        grid=(B, H // g, S // tq),
        in_specs=[pl.BlockSpec((None, g, tq, LANES), lambda b, p, i: (b, p, i, 0)),
                  pl.BlockSpec((None, g, S, LANES), lambda b, p, i: (b, p, 0, 0)),
                  pl.BlockSpec((None, g // 2, S // tk, LANES, tk), lambda b, p, i: (b, p, 0, 0, 0))],
        out_specs=pl.BlockSpec((None, tq, g * MLA_V), lambda b, p, i: (b, i, p)),
        out_shape=jax.ShapeDtypeStruct((B, S, H * MLA_V), BF16),
        scratch_shapes=[pltpu.VMEM((g, LANES, tq), BF16), pltpu.VMEM((g * MLA_ACC_ROWS, tq), F32)],
        compiler_params=pltpu.CompilerParams(dimension_semantics=("arbitrary",) * 3,
                                             vmem_limit_bytes=VMEM_LIMIT),
        name="mla",
    )(qm, km, vt)


def _swa_kernel(sink_ref, q_ref, kp_ref, kc_ref, vp_ref, vc_ref, bias0_ref, bias_ref, o_ref):
    band = 2 * WINDOW
    pairs = SWA_GROUP // 2
    kband = jnp.concatenate([kp_ref[...], kc_ref[...]], axis=0).astype(F32)
    vband_t = jnp.concatenate([vp_ref[...], vc_ref[...]], axis=0).astype(F32).T
    lo = lax.broadcasted_iota(jnp.int32, kband.shape, 1) < SWA_HEAD
    kswap = pltpu.roll(kband, SWA_HEAD, 1)
    k_even = [jnp.where(lo, kband, 0.0).astype(BF16), jnp.where(lo, kswap, 0.0).astype(BF16)]
    k_odd = [jnp.where(lo, 0.0, kswap).astype(BF16), jnp.where(lo, 0.0, kband).astype(BF16)]
    ones_rows = (lax.broadcasted_iota(jnp.int32, (16, band), 0) == 0).astype(BF16)
    chains = [(j, kvh) for j in range(SWA_STEP_BLOCKS) for kvh in range(SWA_KV_HEADS)]

    def scores(j, kvh):
        keys = slice(j * WINDOW, j * WINDOW + band)
        kcat = jnp.concatenate([k_even[kvh][keys], k_odd[kvh][keys]], axis=0)
        qstack = jnp.concatenate([q_ref[j * WINDOW:(j + 1) * WINDOW, (kvh * pairs + jj) * LANES:
                                        (kvh * pairs + jj + 1) * LANES] for jj in range(pairs)], axis=0)
        bias = bias0_ref if j == 0 else bias_ref
        return _dot_nt(kcat, qstack) + bias[kvh]

    sts = [scores(j, kvh) for j, kvh in chains]
    for (j, kvh), st in zip(chains, sts):
        v_t = vband_t[kvh * SWA_HEAD:(kvh + 1) * SWA_HEAD, j * WINDOW:j * WINDOW + band].astype(BF16)
        v_aug = jnp.concatenate([v_t, ones_rows], axis=0)
        halves = []
        for parity in range(2):
            ps, sink_terms = [], []
            for jj in range(pairs):
                blk = st[parity * band:(parity + 1) * band, jj * WINDOW:(jj + 1) * WINDOW]
                sink = sink_ref[kvh * SWA_GROUP + 2 * jj + parity] * LOG2E
                m = jnp.maximum(jnp.max(blk, axis=0, keepdims=True), sink)
                ps.append(jnp.exp2(blk - m).astype(BF16))
                sink_terms.append(jnp.exp2(sink - m))
            pv = _dot(v_aug, jnp.concatenate(ps, axis=1))
            halves.append(pv[:SWA_HEAD] / (pv[SWA_HEAD:SWA_HEAD + 1] + jnp.concatenate(sink_terms, axis=1)))
        o_t = jnp.concatenate(halves, axis=0)
        for jj in range(pairs):
            pair = kvh * pairs + jj
            o_ref[j * WINDOW:(j + 1) * WINDOW, pair * LANES:(pair + 1) * LANES] = (
                o_t[:, jj * WINDOW:(jj + 1) * WINDOW].T.astype(BF16))


def _swa(sinks, qs, ks, vs, bias_tbl):
    B, S, W = qs.shape
    nb = SWA_STEP_BLOCKS
    prev = lambda b, n: (b, jnp.maximum(nb * n - 1, 0), 0)
    cur = lambda b, n: (b, n, 0)
    tbl_block = (None,) + bias_tbl.shape[1:]
    return pl.pallas_call(
        _swa_kernel,
        grid=(B, S // (nb * WINDOW)),
        in_specs=[pl.BlockSpec(memory_space=pltpu.SMEM),
                  pl.BlockSpec((None, nb * WINDOW, W), cur),
                  pl.BlockSpec((None, WINDOW, LANES), prev), pl.BlockSpec((None, nb * WINDOW, LANES), cur),
                  pl.BlockSpec((None, WINDOW, LANES), prev), pl.BlockSpec((None, nb * WINDOW, LANES), cur),
                  pl.BlockSpec(tbl_block, lambda b, n: (jnp.minimum(n, 1), 0, 0, 0)),
                  pl.BlockSpec(tbl_block, lambda b, n: (1, 0, 0, 0))],
        out_specs=pl.BlockSpec((None, nb * WINDOW, W), cur),
        out_shape=jax.ShapeDtypeStruct((B, S, W), BF16),
        compiler_params=pltpu.CompilerParams(dimension_semantics=("arbitrary", "arbitrary"),
                                             vmem_limit_bytes=VMEM_LIMIT),
        name="swa",
    )(sinks, qs, ks, ks, vs, vs, bias_tbl, bias_tbl)


def _tail_kernel(x_ref, ym_ref, ys_ref, sh1_ref, sc1_ref, ga1_ref, sh2_ref, sc2_ref, ga2_ref,
                 gmix_ref, gmlp_ref, gfin_ref, bg_ref, wg_ref, wom_ref, wos_ref, wo_ref, w1_ref, w2_ref, o_ref):
    x = x_ref[...]
    h = (_rms(x, gmix_ref[...]) * (1 + sc1_ref[...]) + sh1_ref[...]).astype(BF16)
    gates = jax.nn.sigmoid(_dot_nt(h, wg_ref[...]) + bg_ref[...])
    merged = (gates[:, :D_MODEL] * _dot(ym_ref[...], wom_ref[...])
              + gates[:, D_MODEL:] * _dot(ys_ref[...], wos_ref[...]))
    x1 = x + ga1_ref[...] * _dot(merged.astype(BF16), wo_ref[...])
    h2 = (_rms(x1, gmlp_ref[...]) * (1 + sc2_ref[...]) + sh2_ref[...]).astype(BF16)
    acc = jnp.zeros_like(x1)
    for c in range(D_FF // FF_CHUNK):
        u = jnp.square(jnp.maximum(_dot(h2, w1_ref[:, c * FF_CHUNK:(c + 1) * FF_CHUNK]), 0.0))
        acc = acc + _dot(u.astype(BF16), w2_ref[c * FF_CHUNK:(c + 1) * FF_CHUNK, :])
    x2 = x1 + ga2_ref[...] * acc
    o_ref[...] = _rms(x2, gfin_ref[...])


def _tail(x, ym, ys, mods, gmix, gmlp, gfin, bg, wg, wom, wos, wo, w1, w2):
    B, S, D = x.shape
    tm = TM_TAIL
    tok = lambda width: pl.BlockSpec((None, tm, width), lambda b, i: (b, i, 0))
    per_b = pl.BlockSpec((None, 1, D), lambda b, i: (b, 0, 0))
    const = lambda a: pl.BlockSpec(a.shape, lambda b, i: (0,) * a.ndim, pipeline_mode=pl.Buffered(1))
    return pl.pallas_call(
        _tail_kernel,
        grid=(B, S // tm),
        in_specs=[tok(D), tok(ym.shape[-1]), tok(ys.shape[-1])] + [per_b] * 6
                 + [const(a) for a in (gmix, gmlp, gfin, bg, wg, wom, wos, wo, w1, w2)],
        out_specs=tok(D),
        out_shape=jax.ShapeDtypeStruct((B, S, D), F32),
        compiler_params=pltpu.CompilerParams(dimension_semantics=("arbitrary", "arbitrary"),
                                             vmem_limit_bytes=VMEM_LIMIT),
        name="tail",
    )(x, ym, ys, *mods, gmix, gmlp, gfin, bg, wg, wom, wos, wo, w1, w2)


def _head_slab(nope, a, b):
    r, h, _ = nope.shape
    pad = jnp.zeros((r, h, LANES - MLA_NOPE - MLA_ROPE), nope.dtype)
    return jnp.concatenate([nope, a, b, pad], axis=-1).reshape(r, h * LANES)


def kernel(x, c, positions, rel_bias, ada_w, ada_b, ln_mix_g, w_in, b_gate, mla_q_norm_g, mla_kv_norm_g,
           w_uq, w_ukv, swa_sinks, w_o_mla, w_o_swa, w_o, ln_mlp_g, w_ff1, w_ff2, ln_final_g):
    B, S, D = x.shape
    assert (B, S, D) == (x.shape[0], 4096, D_MODEL) and ada_w.shape[0] == 1
    l = 0

    c_pad = jnp.pad(c, ((0, 8 - B), (0, 0)))
    mod = _adaln(c_pad, ada_w, ada_b)[:B]
    mods = [m[:, None, :] for m in jnp.split(mod, 6, axis=-1)]

    wi = w_in[l].T
    o_kv = MLA_Q_RANK
    o_kr = o_kv + MLA_KV_RANK
    o_qs = o_kr + MLA_ROPE
    o_ks = o_qs + SWA_Q_HEADS * SWA_HEAD
    o_vs = o_ks + SWA_KV_HEADS * SWA_HEAD
    o_ga = o_vs + SWA_KV_HEADS * SWA_HEAD
    kr_a, kr_b = wi[o_kr:o_kr + HALF_ROPE], wi[o_kr + HALF_ROPE:o_qs]
    z = lambda n: jnp.zeros((n, D), wi.dtype)
    w1 = jnp.concatenate([
        wi[:o_kr],
        z(ROPE_A), kr_a, kr_b, z(LANES - ROPE_A - MLA_ROPE),
        z(ROPE_A), -kr_b, kr_a, z(LANES - ROPE_A - MLA_ROPE),
        wi[o_qs:o_ga]], axis=0).astype(BF16)
    assert w1.shape[0] == N_PROJ
    wg = wi[o_ga:].astype(BF16)

    uq = w_uq[l]
    q_nope, q_a, q_b = uq[..., :MLA_NOPE], uq[..., MLA_NOPE:MLA_NOPE + HALF_ROPE], uq[..., MLA_NOPE + HALF_ROPE:]
    wq = jnp.concatenate([_head_slab(q_nope, q_a, q_b),
                          _head_slab(jnp.zeros_like(q_nope), -q_b, q_a)], axis=1).astype(BF16)
    ukv = w_ukv[l]
    k_nope, v_up = ukv[..., :MLA_NOPE], ukv[..., MLA_NOPE:]
    zr = jnp.zeros(k_nope.shape[:2] + (HALF_ROPE,), ukv.dtype)
    wkv = jnp.concatenate([_head_slab(k_nope, zr, zr),
                           v_up.reshape(MLA_KV_RANK, MLA_HEADS * MLA_V)], axis=1).astype(BF16)

    inv = (ROPE_THETA ** (-jnp.arange(HALF_ROPE, dtype=F32) / HALF_ROPE))[:, None]

    qm, km, vm, qs, ks, vs = _proj(x, mods[0], mods[1], ln_mix_g[l][None, :], positions[:, None, :], inv, w1,
                                   mla_q_norm_g[l][None, :], mla_kv_norm_g[l][None, :], wq, wkv)

    y_mla = _mla(qm, km, vm)
    y_swa = _swa(swa_sinks[l], qs, ks, vs, _swa_bias(rel_bias))

    return _tail(x, y_mla, y_swa, mods, ln_mix_g[l][None, :], ln_mlp_g[l][None, :], ln_final_g[None, :],
                 b_gate[l][None, :], wg, w_o_mla[l].astype(BF16), w_o_swa[l].astype(BF16), w_o[l].astype(BF16),
                 w_ff1[l].astype(BF16), w_ff2[l].astype(BF16))
```

```python
import functools
import math

import jax
import jax.numpy as jnp
import numpy as np
from jax import lax
from jax.experimental import pallas as pl
from jax.experimental.pallas import tpu as pltpu

F32 = jnp.float32
BF16 = jnp.bfloat16

D_MODEL = 1024
MLA_HEADS = 8
MLA_Q_RANK = 256
MLA_KV_RANK = 128
MLA_NOPE = 64
MLA_ROPE = 32
MLA_V = 64
SWA_Q_HEADS = 16
SWA_KV_HEADS = 2
SWA_HEAD = 64
WINDOW = 128
REL_BUCKETS = 32
REL_MAX_DIST = 128
D_FF = 4 * D_MODEL
ROPE_THETA = 10000.0
EPS = 1e-6

LANES = 128
HALF_ROPE = MLA_ROPE // 2
ROPE_A = MLA_NOPE
ROPE_B = MLA_NOPE + HALF_ROPE
N_PAIRS = SWA_Q_HEADS // 2
SWA_GROUP = SWA_Q_HEADS // SWA_KV_HEADS
SWA_STEP_BLOCKS = 2
LOG2E = math.log2(math.e)

C_QLAT = 0
C_KVLAT = C_QLAT + MLA_Q_RANK
C_KR = C_KVLAT + MLA_KV_RANK
C_QS = C_KR + LANES
C_KS = C_QS + SWA_Q_HEADS * SWA_HEAD
C_VS = C_KS + SWA_KV_HEADS * SWA_HEAD
N_PROJ = C_VS + SWA_KV_HEADS * SWA_HEAD

VMEM_LIMIT = 56 * 1024 * 1024

TM_PROJ = 512
TQ_MLA = 1024
TK_MLA = 256
MLA_GROUP = 4
MLA_LOOKAHEAD = 4
MLA_ACC_ROWS = MLA_V + 16
TM_TAIL = 512
TAIL_SUB = 256
FF_CHUNK = 1024


def _rms(x, g):
    return x * lax.rsqrt(jnp.mean(x * x, axis=-1, keepdims=True) + EPS) * g


def _dot(a, b):
    return jnp.dot(a, b, preferred_element_type=F32)


def _dot_nt(a, b):
    return lax.dot_general(a, b, (((1,), (1,)), ((), ())), preferred_element_type=F32)


def _adaln_kernel(c_ref, w_ref, b_ref, o_ref):
    c = c_ref[...]
    act = (c * jax.nn.sigmoid(c)).astype(BF16)
    o_ref[...] = _dot(act, w_ref[...].astype(BF16)) + b_ref[...]


def _adaln(c_pad, ada_w, ada_b):
    rows = c_pad.shape[0]
    n_out = ada_w.shape[2]
    return pl.pallas_call(
        _adaln_kernel,
        grid=(n_out // D_MODEL,),
        in_specs=[pl.BlockSpec((rows, D_MODEL), lambda j: (0, 0)),
                  pl.BlockSpec((None, D_MODEL, D_MODEL), lambda j: (0, 0, j)),
                  pl.BlockSpec((1, D_MODEL), lambda j: (0, j))],
        out_specs=pl.BlockSpec((rows, D_MODEL), lambda j: (0, j)),
        out_shape=jax.ShapeDtypeStruct((rows, n_out), F32),
        compiler_params=pltpu.CompilerParams(dimension_semantics=("arbitrary",)),
        name="adaln",
    )(c_pad, ada_w, ada_b)


def _swa_bias_kernel(rb_ref, o_ref):
    a = lax.broadcasted_iota(jnp.int32, (2 * WINDOW, WINDOW), 1)
    b = lax.broadcasted_iota(jnp.int32, (2 * WINDOW, WINDOW), 0)
    dist = WINDOW + a - b
    n = jnp.maximum(dist, 0)
    max_exact = REL_BUCKETS // 2
    nf = jnp.maximum(n, 1).astype(F32)
    large = max_exact + jnp.floor(jnp.log(nf / max_exact) / math.log(REL_MAX_DIST / max_exact)
                                  * (REL_BUCKETS - max_exact)).astype(jnp.int32)
    large = jnp.minimum(large, REL_BUCKETS - 1)
    bucket = jnp.where(n < max_exact, n, large)
    band_ok = (dist >= 0) & (dist < WINDOW)
    has_prev = b >= WINDOW
    neg = jnp.float32(-jnp.inf)
    for head in range(SWA_Q_HEADS):
        bias = jnp.zeros((2 * WINDOW, WINDOW), F32)
        for k in range(REL_BUCKETS):
            bias = jnp.where(bucket == k, rb_ref[head, k], bias)
        kvh, pair, parity = head // SWA_GROUP, (head % SWA_GROUP) // 2, head % 2
        rows = slice(parity * 2 * WINDOW, (parity + 1) * 2 * WINDOW)
        cols = slice(pair * WINDOW, (pair + 1) * WINDOW)
        o_ref[0, kvh, rows, cols] = jnp.where(band_ok & has_prev, bias * LOG2E, neg)
        o_ref[1, kvh, rows, cols] = jnp.where(band_ok, bias * LOG2E, neg)


def _swa_bias(rel_bias):
    return pl.pallas_call(
        _swa_bias_kernel,
        in_specs=[pl.BlockSpec(memory_space=pltpu.SMEM)],
        out_specs=pl.BlockSpec(memory_space=pltpu.VMEM),
        out_shape=jax.ShapeDtypeStruct((2, SWA_KV_HEADS, 4 * WINDOW, SWA_GROUP // 2 * WINDOW), F32),
        name="swa_bias",
    )(rel_bias)


def _proj_kernel(x_ref, sh_ref, sc_ref, g_ref, pos_ref, inv_ref, spread_ref, w1_ref, gq_ref, gkv_ref, wq_ref,
                 wkv_ref, qm_ref, km_ref, vm_ref, qs_ref, ks_ref, vs_ref):
    h = _rms(x_ref[...], g_ref[...]) * (1 + sc_ref[...]) + sh_ref[...]
    p = _dot_nt(h.astype(BF16), w1_ref[...])

    ang = inv_ref[...] * pos_ref[...].astype(F32)
    cs = jnp.concatenate([jnp.cos(ang), jnp.sin(ang)], axis=0)
    hi = cs.astype(BF16).astype(F32)
    mid = (cs - hi).astype(BF16).astype(F32)
    lo = (cs - hi - mid).astype(BF16).astype(F32)
    terms = jnp.concatenate([hi, mid, lo, jnp.zeros_like(hi)], axis=0)
    tables = _dot(terms.T.astype(BF16), spread_ref[...])
    cos_f, sin_f = tables[:, :LANES], tables[:, LANES:]
    lane_t = lax.broadcasted_iota(jnp.int32, cos_f.shape, 1)
    cos_q = jnp.where(lane_t < MLA_NOPE, 1.0, cos_f)
    first_half = lane_t < ROPE_B

    def rope(t, cos):
        partner = jnp.where(first_half, pltpu.roll(t, LANES - HALF_ROPE, 1), pltpu.roll(t, HALF_ROPE, 1))
        return t * cos + partner * sin_f

    qn = _rms(p[:, C_QLAT:C_QLAT + MLA_Q_RANK], gq_ref[...]).astype(BF16)
    qq = _dot(qn, wq_ref[...])
    scale = (MLA_NOPE + MLA_ROPE) ** -0.5 * LOG2E
    half = MLA_HEADS * LANES
    for hd in range(MLA_HEADS):
        qm_ref[hd] = (rope(qq[:, hd * LANES:(hd + 1) * LANES], cos_q) * scale).astype(BF16)

    kvn = _rms(p[:, C_KVLAT:C_KVLAT + MLA_KV_RANK], gkv_ref[...]).astype(BF16)
    kk = _dot(kvn, wkv_ref[...])
    k_pe = rope(p[:, C_KR:C_KR + LANES], cos_f)
    for hd in range(MLA_HEADS):
        km_ref[hd] = (kk[:, hd * LANES:(hd + 1) * LANES] + k_pe).astype(BF16)
    for pr in range(MLA_HEADS // 2):
        v_t = kk[:, half + pr * LANES:half + (pr + 1) * LANES].T.astype(BF16)
        for t in range(v_t.shape[1] // TK_MLA):
            vm_ref[pr, t] = v_t[:, t * TK_MLA:(t + 1) * TK_MLA]

    qs_ref[...] = (p[:, C_QS:C_KS] * (SWA_HEAD ** -0.5 * LOG2E)).astype(BF16)
    ks_ref[...] = p[:, C_KS:C_VS].astype(BF16)
    vs_ref[...] = p[:, C_VS:N_PROJ].astype(BF16)


def _proj(x, sh1, sc1, ln_g, pos3, inv, spread, w1, gq, gkv, wq, wkv):
    B, S, D = x.shape
    tm = TM_PROJ
    const = lambda shape: pl.BlockSpec(shape, lambda b, i: (0,) * len(shape))
    return pl.pallas_call(
        _proj_kernel,
        grid=(B, S // tm),
        in_specs=[pl.BlockSpec((None, tm, D), lambda b, i: (b, i, 0)),
                  pl.BlockSpec((None, 1, D), lambda b, i: (b, 0, 0)),
                  pl.BlockSpec((None, 1, D), lambda b, i: (b, 0, 0)),
                  const((1, D)),
                  pl.BlockSpec((None, 1, tm), lambda b, i: (b, 0, i)),
                  const((HALF_ROPE, 1)), const(spread.shape),
                  const(w1.shape), const(gq.shape), const(gkv.shape), const(wq.shape), const(wkv.shape)],
        out_specs=[pl.BlockSpec((None, MLA_HEADS, tm, LANES), lambda b, i: (b, 0, i, 0)),
                   pl.BlockSpec((None, MLA_HEADS, tm, LANES), lambda b, i: (b, 0, i, 0)),
                   pl.BlockSpec((None, MLA_HEADS // 2, tm // TK_MLA, LANES, TK_MLA), lambda b, i: (b, 0, i, 0, 0)),
                   pl.BlockSpec((None, tm, SWA_Q_HEADS * SWA_HEAD), lambda b, i: (b, i, 0)),
                   pl.BlockSpec((None, tm, LANES), lambda b, i: (b, i, 0)),
                   pl.BlockSpec((None, tm, LANES), lambda b, i: (b, i, 0))],
        out_shape=[jax.ShapeDtypeStruct((B, MLA_HEADS, S, LANES), BF16),
                   jax.ShapeDtypeStruct((B, MLA_HEADS, S, LANES), BF16),
                   jax.ShapeDtypeStruct((B, MLA_HEADS // 2, S // TK_MLA, LANES, TK_MLA), BF16),
                   jax.ShapeDtypeStruct((B, S, SWA_Q_HEADS * SWA_HEAD), BF16),
                   jax.ShapeDtypeStruct((B, S, LANES), BF16),
                   jax.ShapeDtypeStruct((B, S, LANES), BF16)],
        compiler_params=pltpu.CompilerParams(dimension_semantics=("arbitrary", "arbitrary"),
                                             vmem_limit_bytes=VMEM_LIMIT),
        name="proj",
    )(x, sh1, sc1, ln_g, pos3, inv, spread, w1, gq, gkv, wq, wkv)


def _mla_kernel(q_ref, k_ref, vt_ref, o_ref, qt_sc, acc_sc):
    tq, tk = TQ_MLA, TK_MLA
    nc = tq // LANES
    i = pl.program_id(2)
    for hd in range(MLA_GROUP):
        qt_sc[hd] = q_ref[hd].astype(F32).T.astype(BF16)
    acc_sc[...] = jnp.zeros_like(acc_sc)

    per_tile = tq // tk

    ones_rows = (lax.broadcasted_iota(jnp.int32, (MLA_ACC_ROWS - MLA_V, tk), 0) == 0).astype(BF16)
    r_iota = lax.broadcasted_iota(jnp.int32, (tk, LANES), 0)
    c_iota = lax.broadcasted_iota(jnp.int32, (tk, LANES), 1)

    def tile(t, ms_all, diagonal):
        blocks = [(d, hd) for d in range(per_tile) for hd in range(MLA_GROUP)]
        first_cc = [(d * tk) // LANES if diagonal else 0 for d, _ in blocks]

        def scores(n):
            d, hd = blocks[n]
            k = k_ref[hd, pl.ds(pl.multiple_of((t * per_tile + d) * tk, tk), tk), :]
            return _dot(k, qt_sc[hd, :, first_cc[n] * LANES:])

        pending = {n: scores(n) for n in range(min(MLA_LOOKAHEAD, len(blocks)))}
        ms_all = list(ms_all)
        for n, (d, hd) in enumerate(blocks):
            if n + MLA_LOOKAHEAD < len(blocks):
                pending[n + MLA_LOOKAHEAD] = scores(n + MLA_LOOKAHEAD)
            st = pending.pop(n)
            ms = list(ms_all[hd])
            ps, alphas = [], []
            for cc in range(first_cc[n], nc):
                blk = st[:, (cc - first_cc[n]) * LANES:(cc - first_cc[n] + 1) * LANES]
                if diagonal and cc * LANES < (d + 1) * tk - 1:
                    blk = jnp.where(r_iota + (d * tk - cc * LANES) <= c_iota, blk, -jnp.inf)
                m_new = jnp.maximum(ms[cc], jnp.max(blk, axis=0, keepdims=True))
                alphas.append(jnp.exp2(ms[cc] - m_new))
                ps.append(jnp.exp2(blk - m_new).astype(BF16))
                ms[cc] = m_new
            v_t = vt_ref[hd // 2, t * per_tile + d, (hd % 2) * MLA_V:(hd % 2 + 1) * MLA_V, :]
            pv = _dot(jnp.concatenate([v_t, ones_rows], axis=0), jnp.concatenate(ps, axis=1))
            rows = slice(hd * MLA_ACC_ROWS, (hd + 1) * MLA_ACC_ROWS)
            cols = slice(first_cc[n] * LANES, tq)
            acc_sc[rows, cols] = acc_sc[rows, cols] * jnp.concatenate(alphas, axis=1) + pv
            ms_all[hd] = tuple(ms)
        return tuple(ms_all)

    init = tuple(tuple(jnp.full((1, LANES), -jnp.inf, F32) for _ in range(nc)) for _ in range(MLA_GROUP))
    ms_all = lax.fori_loop(0, i, lambda t, c: tile(t, c, False), init)
    tile(i, ms_all, True)
    outs = []
    for hd in range(MLA_GROUP):
        acc = acc_sc[hd * MLA_ACC_ROWS:(hd + 1) * MLA_ACC_ROWS, :]
        outs.append(acc[:MLA_V] / acc[MLA_V:MLA_V + 1])
    o_ref[...] = jnp.concatenate(outs, axis=0).T.astype(BF16)


def _mla(qm, km, vt):
    B, H, S, _ = qm.shape
    tq, tk, g = TQ_MLA, TK_MLA, MLA_GROUP
    return pl.pallas_call(
        _mla_kernel,
        grid=(B, H // g, S // tq),
        in_specs=[pl.BlockSpec((None, g, tq, LANES), lambda b, p, i: (b, p, i, 0)),
                  pl.BlockSpec((None, g, S, LANES), lambda b, p, i: (b, p, 0, 0)),
                  pl.BlockSpec((None, g // 2, S // tk, LANES, tk), lambda b, p, i: (b, p, 0, 0, 0))],
        out_specs=pl.BlockSpec((None, tq, g * MLA_V), lambda b, p, i: (b, i, p)),
        out_shape=jax.ShapeDtypeStruct((B, S, H * MLA_V), BF16),
        scratch_shapes=[pltpu.VMEM((g, LANES, tq), BF16), pltpu.VMEM((g * MLA_ACC_ROWS, tq), F32)],
        compiler_params=pltpu.CompilerParams(dimension_semantics=("arbitrary",) * 3,
                                             vmem_limit_bytes=VMEM_LIMIT),
        name="mla",
    )(qm, km, vt)


def _swa_kernel(sink_ref, q_ref, kp_ref, kc_ref, vp_ref, vc_ref, bias0_ref, bias_ref, o_ref):
    band = 2 * WINDOW
    pairs = SWA_GROUP // 2
    kband = jnp.concatenate([kp_ref[...], kc_ref[...]], axis=0).astype(F32)
    vband_t = jnp.concatenate([vp_ref[...], vc_ref[...]], axis=0).astype(F32).T
    lo = lax.broadcasted_iota(jnp.int32, kband.shape, 1) < SWA_HEAD
    kswap = pltpu.roll(kband, SWA_HEAD, 1)
    k_even = [jnp.where(lo, kband, 0.0).astype(BF16), jnp.where(lo, kswap, 0.0).astype(BF16)]
    k_odd = [jnp.where(lo, 0.0, kswap).astype(BF16), jnp.where(lo, 0.0, kband).astype(BF16)]
    ones_rows = (lax.broadcasted_iota(jnp.int32, (16, band), 0) == 0).astype(BF16)
    chains = [(j, kvh) for j in range(SWA_STEP_BLOCKS) for kvh in range(SWA_KV_HEADS)]

    def scores(j, kvh):
        keys = slice(j * WINDOW, j * WINDOW + band)
        kcat = jnp.concatenate([k_even[kvh][keys], k_odd[kvh][keys]], axis=0)
        qstack = jnp.concatenate([q_ref[j * WINDOW:(j + 1) * WINDOW, (kvh * pairs + jj) * LANES:
                                        (kvh * pairs + jj + 1) * LANES] for jj in range(pairs)], axis=0)
        bias = bias0_ref if j == 0 else bias_ref
        return _dot_nt(kcat, qstack) + bias[kvh]

    sts = [scores(j, kvh) for j, kvh in chains]
    for (j, kvh), st in zip(chains, sts):
        v_t = vband_t[kvh * SWA_HEAD:(kvh + 1) * SWA_HEAD, j * WINDOW:j * WINDOW + band].astype(BF16)
        v_aug = jnp.concatenate([v_t, ones_rows], axis=0)
        halves = []
        for parity in range(2):
            ps, sink_terms = [], []
            for jj in range(pairs):
                blk = st[parity * band:(parity + 1) * band, jj * WINDOW:(jj + 1) * WINDOW]
                sink = sink_ref[kvh * SWA_GROUP + 2 * jj + parity] * LOG2E
                m = jnp.maximum(jnp.max(blk, axis=0, keepdims=True), sink)
                ps.append(jnp.exp2(blk - m).astype(BF16))
                sink_terms.append(jnp.exp2(sink - m))
            pv = _dot(v_aug, jnp.concatenate(ps, axis=1))
            halves.append(pv[:SWA_HEAD] / (pv[SWA_HEAD:SWA_HEAD + 1] + jnp.concatenate(sink_terms, axis=1)))
        o_t = jnp.concatenate(halves, axis=0)
        for jj in range(pairs):
            pair = kvh * pairs + jj
            o_ref[j * WINDOW:(j + 1) * WINDOW, pair * LANES:(pair + 1) * LANES] = (
                o_t[:, jj * WINDOW:(jj + 1) * WINDOW].T.astype(BF16))


def _swa(sinks, qs, ks, vs, bias_tbl):
    B, S, W = qs.shape
    nb = SWA_STEP_BLOCKS
    prev = lambda b, n: (b, jnp.maximum(nb * n - 1, 0), 0)
    cur = lambda b, n: (b, n, 0)
    tbl_block = (None,) + bias_tbl.shape[1:]
    return pl.pallas_call(
        _swa_kernel,
        grid=(B, S // (nb * WINDOW)),
        in_specs=[pl.BlockSpec(memory_space=pltpu.SMEM),
                  pl.BlockSpec((None, nb * WINDOW, W), cur),
                  pl.BlockSpec((None, WINDOW, LANES), prev), pl.BlockSpec((None, nb * WINDOW, LANES), cur),
                  pl.BlockSpec((None, WINDOW, LANES), prev), pl.BlockSpec((None, nb * WINDOW, LANES), cur),
                  pl.BlockSpec(tbl_block, lambda b, n: (jnp.minimum(n, 1), 0, 0, 0)),
                  pl.BlockSpec(tbl_block, lambda b, n: (1, 0, 0, 0))],
        out_specs=pl.BlockSpec((None, nb * WINDOW, W), cur),
        out_shape=jax.ShapeDtypeStruct((B, S, W), BF16),
        compiler_params=pltpu.CompilerParams(dimension_semantics=("arbitrary", "arbitrary"),
                                             vmem_limit_bytes=VMEM_LIMIT),
        name="swa",
    )(sinks, qs, ks, ks, vs, vs, bias_tbl, bias_tbl)


def _tail_kernel(x_ref, ym_ref, ys_ref, sh1_ref, sc1_ref, ga1_ref, sh2_ref, sc2_ref, ga2_ref,
                 gmix_ref, gmlp_ref, gfin_ref, bg_ref, wg_ref, wom_ref, wos_ref, wo_ref, w1_ref, w2_ref, o_ref):
    subs = [slice(k * TAIL_SUB, (k + 1) * TAIL_SUB) for k in range(TM_TAIL // TAIL_SUB)]
    n_ff = D_FF // FF_CHUNK
    xs = [x_ref[r, :] for r in subs]
    branches = [(_dot(ym_ref[r, :], wom_ref[...]), _dot(ys_ref[r, :], wos_ref[...])) for r in subs]
    hs = [(_rms(x, gmix_ref[...]) * (1 + sc1_ref[...]) + sh1_ref[...]).astype(BF16) for x in xs]
    logits = [_dot_nt(h, wg_ref[...]) for h in hs]
    x1s = []
    for x, (a, b), gl in zip(xs, branches, logits):
        gates = jax.nn.sigmoid(gl + bg_ref[...])
        merged = gates[:, :D_MODEL] * a + gates[:, D_MODEL:] * b
        x1s.append(x + ga1_ref[...] * _dot(merged.astype(BF16), wo_ref[...]))
    h2s = [(_rms(x1, gmlp_ref[...]) * (1 + sc2_ref[...]) + sh2_ref[...]).astype(BF16) for x1 in x1s]

    jobs = [(k, c) for k in range(len(subs)) for c in range(n_ff)]
    up = lambda k, c: _dot(h2s[k], w1_ref[:, c * FF_CHUNK:(c + 1) * FF_CHUNK])
    pending = {0: up(*jobs[0])}
    accs = [None] * len(subs)
    for n, (k, c) in enumerate(jobs):
        if n + 1 < len(jobs):
            pending[n + 1] = up(*jobs[n + 1])
        u = jnp.square(jnp.maximum(pending.pop(n), 0.0)).astype(BF16)
        down = _dot(u, w2_ref[c * FF_CHUNK:(c + 1) * FF_CHUNK, :])
        accs[k] = down if accs[k] is None else accs[k] + down
        if c == n_ff - 1:
            o_ref[subs[k], :] = _rms(x1s[k] + ga2_ref[...] * accs[k], gfin_ref[...])


def _tail(x, ym, ys, mods, gmix, gmlp, gfin, bg, wg, wom, wos, wo, w1, w2):
    B, S, D = x.shape
    tm = TM_TAIL
    tok = lambda width: pl.BlockSpec((None, tm, width), lambda b, i: (b, i, 0))
    per_b = pl.BlockSpec((None, 1, D), lambda b, i: (b, 0, 0))
    const = lambda a: pl.BlockSpec(a.shape, lambda b, i: (0,) * a.ndim, pipeline_mode=pl.Buffered(1))
    return pl.pallas_call(
        _tail_kernel,
        grid=(B, S // tm),
        in_specs=[tok(D), tok(ym.shape[-1]), tok(ys.shape[-1])] + [per_b] * 6
                 + [const(a) for a in (gmix, gmlp, gfin, bg, wg, wom, wos, wo, w1, w2)],
        out_specs=tok(D),
        out_shape=jax.ShapeDtypeStruct((B, S, D), F32),
        compiler_params=pltpu.CompilerParams(dimension_semantics=("arbitrary", "arbitrary"),
                                             vmem_limit_bytes=VMEM_LIMIT),
        name="tail",
    )(x, ym, ys, *mods, gmix, gmlp, gfin, bg, wg, wom, wos, wo, w1, w2)


def _head_slab(nope, a, b):
    r, h, _ = nope.shape
    pad = jnp.zeros((r, h, LANES - MLA_NOPE - MLA_ROPE), nope.dtype)
    return jnp.concatenate([nope, a, b, pad], axis=-1).reshape(r, h * LANES)


def _rope_spread():
    m = np.zeros((LANES, 2 * LANES), np.float32)
    for term in range(3):
        for j in range(HALF_ROPE):
            m[term * MLA_ROPE + j, [ROPE_A + j, ROPE_B + j]] = 1.0
            m[term * MLA_ROPE + HALF_ROPE + j, LANES + ROPE_A + j] = -1.0
            m[term * MLA_ROPE + HALF_ROPE + j, LANES + ROPE_B + j] = 1.0
    return jnp.asarray(m, BF16)


def kernel(x, c, positions, rel_bias, ada_w, ada_b, ln_mix_g, w_in, b_gate, mla_q_norm_g, mla_kv_norm_g,
           w_uq, w_ukv, swa_sinks, w_o_mla, w_o_swa, w_o, ln_mlp_g, w_ff1, w_ff2, ln_final_g):
    B, S, D = x.shape
    assert (B, S, D) == (x.shape[0], 4096, D_MODEL) and ada_w.shape[0] == 1
    l = 0

    c_pad = jnp.pad(c, ((0, 8 - B), (0, 0)))
    mod = _adaln(c_pad, ada_w, ada_b)[:B]
    mods = [m[:, None, :] for m in jnp.split(mod, 6, axis=-1)]

    wi = w_in[l].T
    o_kv = MLA_Q_RANK
    o_kr = o_kv + MLA_KV_RANK
    o_qs = o_kr + MLA_ROPE
    o_ks = o_qs + SWA_Q_HEADS * SWA_HEAD
    o_vs = o_ks + SWA_KV_HEADS * SWA_HEAD
    o_ga = o_vs + SWA_KV_HEADS * SWA_HEAD
    kr_a, kr_b = wi[o_kr:o_kr + HALF_ROPE], wi[o_kr + HALF_ROPE:o_qs]
    z = lambda n: jnp.zeros((n, D), wi.dtype)
    w1 = jnp.concatenate([
        wi[:o_kr],
        z(ROPE_A), kr_a, kr_b, z(LANES - ROPE_A - MLA_ROPE),
        wi[o_qs:o_ga]], axis=0).astype(BF16)
    assert w1.shape[0] == N_PROJ
    wg = wi[o_ga:].astype(BF16)

    uq = w_uq[l]
    q_nope, q_a, q_b = uq[..., :MLA_NOPE], uq[..., MLA_NOPE:MLA_NOPE + HALF_ROPE], uq[..., MLA_NOPE + HALF_ROPE:]
    wq = _head_slab(q_nope, q_a, q_b).astype(BF16)
    ukv = w_ukv[l]
    k_nope, v_up = ukv[..., :MLA_NOPE], ukv[..., MLA_NOPE:]
    zr = jnp.zeros(k_nope.shape[:2] + (HALF_ROPE,), ukv.dtype)
    wkv = jnp.concatenate([_head_slab(k_nope, zr, zr),
                           v_up.reshape(MLA_KV_RANK, MLA_HEADS * MLA_V)], axis=1).astype(BF16)

    inv = (ROPE_THETA ** (-jnp.arange(HALF_ROPE, dtype=F32) / HALF_ROPE))[:, None]

    qm, km, vm, qs, ks, vs = _proj(x, mods[0], mods[1], ln_mix_g[l][None, :], positions[:, None, :], inv,
                                   _rope_spread(), w1, mla_q_norm_g[l][None, :], mla_kv_norm_g[l][None, :], wq, wkv)

    y_mla = _mla(qm, km, vm)
    y_swa = _swa(swa_sinks[l], qs, ks, vs, _swa_bias(rel_bias))

    return _tail(x, y_mla, y_swa, mods, ln_mix_g[l][None, :], ln_mlp_g[l][None, :], ln_final_g[None, :],
                 b_gate[l][None, :], wg, w_o_mla[l].astype(BF16), w_o_swa[l].astype(BF16), w_o[l].astype(BF16),
                 w_ff1[l].astype(BF16), w_ff2[l].astype(BF16))
```

```python
import functools
import math

import jax
import jax.numpy as jnp
import numpy as np
from jax import lax
from jax.experimental import pallas as pl
from jax.experimental.pallas import tpu as pltpu

F32 = jnp.float32
BF16 = jnp.bfloat16

D_MODEL = 1024
MLA_HEADS = 8
MLA_Q_RANK = 256
MLA_KV_RANK = 128
MLA_NOPE = 64
MLA_ROPE = 32
MLA_V = 64
SWA_Q_HEADS = 16
SWA_KV_HEADS = 2
SWA_HEAD = 64
WINDOW = 128
REL_BUCKETS = 32
REL_MAX_DIST = 128
D_FF = 4 * D_MODEL
ROPE_THETA = 10000.0
EPS = 1e-6

LANES = 128
HALF_ROPE = MLA_ROPE // 2
ROPE_A = MLA_NOPE
ROPE_B = MLA_NOPE + HALF_ROPE
N_PAIRS = SWA_Q_HEADS // 2
SWA_GROUP = SWA_Q_HEADS // SWA_KV_HEADS
SWA_STEP_BLOCKS = 8
SWA_LOOKAHEAD = 3
LOG2E = math.log2(math.e)

C_QLAT = 0
C_KVLAT = C_QLAT + MLA_Q_RANK
C_KR = C_KVLAT + MLA_KV_RANK
C_QS = C_KR + LANES
C_KS = C_QS + SWA_Q_HEADS * SWA_HEAD
C_VS = C_KS + SWA_KV_HEADS * SWA_HEAD
N_PROJ = C_VS + SWA_KV_HEADS * SWA_HEAD

VMEM_LIMIT = 56 * 1024 * 1024

TM_PROJ = 512
PROJ_SUB = 256
TQ_MLA = 1024
TK_MLA = 256
MLA_GROUP = 4
MLA_LOOKAHEAD = 4
MLA_ACC_ROWS = MLA_V + 16
TM_TAIL = 512
TAIL_SUB = 256
FF_CHUNK = 1024


def _rms(x, g):
    return x * lax.rsqrt(jnp.mean(x * x, axis=-1, keepdims=True) + EPS) * g


def _dot(a, b):
    return jnp.dot(a, b, preferred_element_type=F32)


def _dot_nt(a, b):
    return lax.dot_general(a, b, (((1,), (1,)), ((), ())), preferred_element_type=F32)


def _adaln_kernel(c_ref, w_ref, b_ref, o_ref):
    c = c_ref[...]
    act = (c * jax.nn.sigmoid(c)).astype(BF16)
    o_ref[...] = _dot(act, w_ref[...].astype(BF16)) + b_ref[...]


def _adaln(c_pad, ada_w, ada_b):
    rows = c_pad.shape[0]
    n_out = ada_w.shape[2]
    return pl.pallas_call(
        _adaln_kernel,
        grid=(n_out // D_MODEL,),
        in_specs=[pl.BlockSpec((rows, D_MODEL), lambda j: (0, 0)),
                  pl.BlockSpec((None, D_MODEL, D_MODEL), lambda j: (0, 0, j)),
                  pl.BlockSpec((1, D_MODEL), lambda j: (0, j))],
        out_specs=pl.BlockSpec((rows, D_MODEL), lambda j: (0, j)),
        out_shape=jax.ShapeDtypeStruct((rows, n_out), F32),
        compiler_params=pltpu.CompilerParams(dimension_semantics=("arbitrary",)),
        name="adaln",
    )(c_pad, ada_w, ada_b)


def _swa_bias_kernel(rb_ref, o_ref):
    a = lax.broadcasted_iota(jnp.int32, (2 * WINDOW, WINDOW), 1)
    b = lax.broadcasted_iota(jnp.int32, (2 * WINDOW, WINDOW), 0)
    dist = WINDOW + a - b
    n = jnp.maximum(dist, 0)
    max_exact = REL_BUCKETS // 2
    nf = jnp.maximum(n, 1).astype(F32)
    large = max_exact + jnp.floor(jnp.log(nf / max_exact) / math.log(REL_MAX_DIST / max_exact)
                                  * (REL_BUCKETS - max_exact)).astype(jnp.int32)
    large = jnp.minimum(large, REL_BUCKETS - 1)
    bucket = jnp.where(n < max_exact, n, large)
    band_ok = (dist >= 0) & (dist < WINDOW)
    has_prev = b >= WINDOW
    neg = jnp.float32(-jnp.inf)
    for head in range(SWA_Q_HEADS):
        bias = jnp.zeros((2 * WINDOW, WINDOW), F32)
        for k in range(REL_BUCKETS):
            bias = jnp.where(bucket == k, rb_ref[head, k], bias)
        kvh, pair, parity = head // SWA_GROUP, (head % SWA_GROUP) // 2, head % 2
        rows = slice(parity * 2 * WINDOW, (parity + 1) * 2 * WINDOW)
        cols = slice(pair * WINDOW, (pair + 1) * WINDOW)
        o_ref[0, kvh, rows, cols] = jnp.where(band_ok & has_prev, bias * LOG2E, neg)
        o_ref[1, kvh, rows, cols] = jnp.where(band_ok, bias * LOG2E, neg)


def _swa_bias(rel_bias):
    return pl.pallas_call(
        _swa_bias_kernel,
        in_specs=[pl.BlockSpec(memory_space=pltpu.SMEM)],
        out_specs=pl.BlockSpec(memory_space=pltpu.VMEM),
        out_shape=jax.ShapeDtypeStruct((2, SWA_KV_HEADS, 4 * WINDOW, SWA_GROUP // 2 * WINDOW), F32),
        name="swa_bias",
    )(rel_bias)


def _proj_kernel(x_ref, sh_ref, sc_ref, g_ref, pos_ref, inv_ref, spread_ref, w1_ref, gq_ref, gkv_ref, wq_ref,
                 wkv_ref, qm_ref, km_ref, vm_ref, qs_ref, ks_ref, vs_ref):
    subs = [slice(k * PROJ_SUB, (k + 1) * PROJ_SUB) for k in range(TM_PROJ // PROJ_SUB)]
    ps = []
    for r in subs:
        h = _rms(x_ref[r, :], g_ref[...]) * (1 + sc_ref[...]) + sh_ref[...]
        ps.append(_dot_nt(h.astype(BF16), w1_ref[...]))

    ang = inv_ref[...] * pos_ref[...].astype(F32)
    cs = jnp.concatenate([jnp.cos(ang), jnp.sin(ang)], axis=0)
    hi = cs.astype(BF16).astype(F32)
    mid = (cs - hi).astype(BF16).astype(F32)
    lo = (cs - hi - mid).astype(BF16).astype(F32)
    terms = jnp.concatenate([hi, mid, lo, jnp.zeros_like(hi)], axis=0)
    tables = _dot(terms.T.astype(BF16), spread_ref[...])
    lane_t = lax.broadcasted_iota(jnp.int32, (PROJ_SUB, LANES), 1)
    scale = (MLA_NOPE + MLA_ROPE) ** -0.5 * LOG2E
    half = MLA_HEADS * LANES

    ups = []
    for p in ps:
        qn = _rms(p[:, C_QLAT:C_QLAT + MLA_Q_RANK], gq_ref[...]).astype(BF16)
        kvn = _rms(p[:, C_KVLAT:C_KVLAT + MLA_KV_RANK], gkv_ref[...]).astype(BF16)
        ups.append((_dot(qn, wq_ref[...]), _dot(kvn, wkv_ref[...])))

    for k, (r, p, (qq, kk)) in enumerate(zip(subs, ps, ups)):
        cos_f, sin_f = tables[r, :LANES], tables[r, LANES:]
        cos_q = jnp.where(lane_t < MLA_NOPE, 1.0, cos_f)

        def rope(t, cos):
            return t * cos + pltpu.roll(t, LANES - HALF_ROPE, 1) * sin_f

        for hd in range(MLA_HEADS):
            qm_ref[hd, r, :] = (rope(qq[:, hd * LANES:(hd + 1) * LANES], cos_q) * scale).astype(BF16)
        k_pe = rope(p[:, C_KR:C_KR + LANES], cos_f)
        for hd in range(MLA_HEADS):
            km_ref[hd, r, :] = (kk[:, hd * LANES:(hd + 1) * LANES] + k_pe).astype(BF16)
        for pr in range(MLA_HEADS // 2):
            v_t = kk[:, half + pr * LANES:half + (pr + 1) * LANES].T.astype(BF16)
            for t in range(PROJ_SUB // TK_MLA):
                vm_ref[pr, k * (PROJ_SUB // TK_MLA) + t] = v_t[:, t * TK_MLA:(t + 1) * TK_MLA]
        qs_ref[r, :] = (p[:, C_QS:C_KS] * (SWA_HEAD ** -0.5 * LOG2E)).astype(BF16)
        ks_ref[r, :] = p[:, C_KS:C_VS].astype(BF16)
        vs_ref[r, :] = p[:, C_VS:N_PROJ].astype(BF16)


def _proj(x, sh1, sc1, ln_g, pos3, inv, spread, w1, gq, gkv, wq, wkv):
    B, S, D = x.shape
    tm = TM_PROJ
    const = lambda shape: pl.BlockSpec(shape, lambda b, i: (0,) * len(shape))
    return pl.pallas_call(
        _proj_kernel,
        grid=(B, S // tm),
        in_specs=[pl.BlockSpec((None, tm, D), lambda b, i: (b, i, 0)),
                  pl.BlockSpec((None, 1, D), lambda b, i: (b, 0, 0)),
                  pl.BlockSpec((None, 1, D), lambda b, i: (b, 0, 0)),
                  const((1, D)),
                  pl.BlockSpec((None, 1, tm), lambda b, i: (b, 0, i)),
                  const((HALF_ROPE, 1)), const(spread.shape),
                  const(w1.shape), const(gq.shape), const(gkv.shape), const(wq.shape), const(wkv.shape)],
        out_specs=[pl.BlockSpec((None, MLA_HEADS, tm, LANES), lambda b, i: (b, 0, i, 0)),
                   pl.BlockSpec((None, MLA_HEADS, tm, LANES), lambda b, i: (b, 0, i, 0)),
                   pl.BlockSpec((None, MLA_HEADS // 2, tm // TK_MLA, LANES, TK_MLA), lambda b, i: (b, 0, i, 0, 0)),
                   pl.BlockSpec((None, tm, SWA_Q_HEADS * SWA_HEAD), lambda b, i: (b, i, 0)),
                   pl.BlockSpec((None, tm, LANES), lambda b, i: (b, i, 0)),
                   pl.BlockSpec((None, tm, LANES), lambda b, i: (b, i, 0))],
        out_shape=[jax.ShapeDtypeStruct((B, MLA_HEADS, S, LANES), BF16),
                   jax.ShapeDtypeStruct((B, MLA_HEADS, S, LANES), BF16),
                   jax.ShapeDtypeStruct((B, MLA_HEADS // 2, S // TK_MLA, LANES, TK_MLA), BF16),
                   jax.ShapeDtypeStruct((B, S, SWA_Q_HEADS * SWA_HEAD), BF16),
                   jax.ShapeDtypeStruct((B, S, LANES), BF16),
                   jax.ShapeDtypeStruct((B, S, LANES), BF16)],
        compiler_params=pltpu.CompilerParams(dimension_semantics=("arbitrary", "arbitrary"),
                                             vmem_limit_bytes=VMEM_LIMIT),
        name="proj",
    )(x, sh1, sc1, ln_g, pos3, inv, spread, w1, gq, gkv, wq, wkv)


def _mla_kernel(q_ref, k_ref, vt_ref, o_ref, qt_sc, acc_sc):
    tq, tk = TQ_MLA, TK_MLA
    nc = tq // LANES
    i = pl.program_id(2)
    for hd in range(MLA_GROUP):
        qt_sc[hd] = q_ref[hd].astype(F32).T.astype(BF16)
    acc_sc[...] = jnp.zeros_like(acc_sc)

    per_tile = tq // tk

    ones_rows = (lax.broadcasted_iota(jnp.int32, (MLA_ACC_ROWS - MLA_V, tk), 0) == 0).astype(BF16)
    r_iota = lax.broadcasted_iota(jnp.int32, (tk, LANES), 0)
    c_iota = lax.broadcasted_iota(jnp.int32, (tk, LANES), 1)

    def tile(t, ms_all, diagonal):
        blocks = [(d, hd) for d in range(per_tile) for hd in range(MLA_GROUP)]
        first_cc = [(d * tk) // LANES if diagonal else 0 for d, _ in blocks]

        def scores(n):
            d, hd = blocks[n]
            k = k_ref[hd, pl.ds(pl.multiple_of((t * per_tile + d) * tk, tk), tk), :]
            return _dot(k, qt_sc[hd, :, first_cc[n] * LANES:])

        pending = {n: scores(n) for n in range(min(MLA_LOOKAHEAD, len(blocks)))}
        ms_all = list(ms_all)
        for n, (d, hd) in enumerate(blocks):
            if n + MLA_LOOKAHEAD < len(blocks):
                pending[n + MLA_LOOKAHEAD] = scores(n + MLA_LOOKAHEAD)
            st = pending.pop(n)
            ms = list(ms_all[hd])
            ps, alphas = [], []
            for cc in range(first_cc[n], nc):
                blk = st[:, (cc - first_cc[n]) * LANES:(cc - first_cc[n] + 1) * LANES]
                if diagonal and cc * LANES < (d + 1) * tk - 1:
                    blk = jnp.where(r_iota + (d * tk - cc * LANES) <= c_iota, blk, -jnp.inf)
                m_new = jnp.maximum(ms[cc], jnp.max(blk, axis=0, keepdims=True))
                alphas.append(jnp.exp2(ms[cc] - m_new))
                ps.append(jnp.exp2(blk - m_new).astype(BF16))
                ms[cc] = m_new
            v_t = vt_ref[hd // 2, t * per_tile + d, (hd % 2) * MLA_V:(hd % 2 + 1) * MLA_V, :]
            pv = _dot(jnp.concatenate([v_t, ones_rows], axis=0), jnp.concatenate(ps, axis=1))
            rows = slice(hd * MLA_ACC_ROWS, (hd + 1) * MLA_ACC_ROWS)
            cols = slice(first_cc[n] * LANES, tq)
            acc_sc[rows, cols] = acc_sc[rows, cols] * jnp.concatenate(alphas, axis=1) + pv
            ms_all[hd] = tuple(ms)
        return tuple(ms_all)

    init = tuple(tuple(jnp.full((1, LANES), -jnp.inf, F32) for _ in range(nc)) for _ in range(MLA_GROUP))
    ms_all = lax.fori_loop(0, i, lambda t, c: tile(t, c, False), init)
    tile(i, ms_all, True)
    outs = []
    for hd in range(MLA_GROUP):
        acc = acc_sc[hd * MLA_ACC_ROWS:(hd + 1) * MLA_ACC_ROWS, :]
        outs.append(acc[:MLA_V] / acc[MLA_V:MLA_V + 1])
    o_ref[...] = jnp.concatenate(outs, axis=0).T.astype(BF16)


def _mla(qm, km, vt):
    B, H, S, _ = qm.shape
    tq, tk, g = TQ_MLA, TK_MLA, MLA_GROUP
    return pl.pallas_call(
        _mla_kernel,
        grid=(B, H // g, S // tq),
        in_specs=[pl.BlockSpec((None, g, tq, LANES), lambda b, p, i: (b, p, i, 0)),
                  pl.BlockSpec((None, g, S, LANES), lambda b, p, i: (b, p, 0, 0)),
                  pl.BlockSpec((None, g // 2, S // tk, LANES, tk), lambda b, p, i: (b, p, 0, 0, 0))],
        out_specs=pl.BlockSpec((None, tq, g * MLA_V), lambda b, p, i: (b, i, p)),
        out_shape=jax.ShapeDtypeStruct((B, S, H * MLA_V), BF16),
        scratch_shapes=[pltpu.VMEM((g, LANES, tq), BF16), pltpu.VMEM((g * MLA_ACC_ROWS, tq), F32)],
        compiler_params=pltpu.CompilerParams(dimension_semantics=("arbitrary",) * 3,
                                             vmem_limit_bytes=VMEM_LIMIT),
        name="mla",
    )(qm, km, vt)


def _swa_kernel(sink_ref, q_ref, kp_ref, kc_ref, vp_ref, vc_ref, bias0_ref, bias_ref, o_ref):
    band = 2 * WINDOW
    pairs = SWA_GROUP // 2
    kband = jnp.concatenate([kp_ref[...], kc_ref[...]], axis=0).astype(F32)
    vband_t = jnp.concatenate([vp_ref[...], vc_ref[...]], axis=0).astype(F32).T
    lo = lax.broadcasted_iota(jnp.int32, kband.shape, 1) < SWA_HEAD
    kswap = pltpu.roll(kband, SWA_HEAD, 1)
    k_even = [jnp.where(lo, kband, 0.0).astype(BF16), jnp.where(lo, kswap, 0.0).astype(BF16)]
    k_odd = [jnp.where(lo, 0.0, kswap).astype(BF16), jnp.where(lo, 0.0, kband).astype(BF16)]
    ones_rows = (lax.broadcasted_iota(jnp.int32, (16, band), 0) == 0).astype(BF16)
    chains = [(j, kvh) for j in range(SWA_STEP_BLOCKS) for kvh in range(SWA_KV_HEADS)]

    def scores(j, kvh):
        keys = slice(j * WINDOW, j * WINDOW + band)
        kcat = jnp.concatenate([k_even[kvh][keys], k_odd[kvh][keys]], axis=0)
        qstack = jnp.concatenate([q_ref[j * WINDOW:(j + 1) * WINDOW, (kvh * pairs + jj) * LANES:
                                        (kvh * pairs + jj + 1) * LANES] for jj in range(pairs)], axis=0)
        bias = bias0_ref if j == 0 else bias_ref
        return _dot_nt(kcat, qstack) + bias[kvh]

    pending = {n: scores(*chains[n]) for n in range(min(SWA_LOOKAHEAD, len(chains)))}
    for n, (j, kvh) in enumerate(chains):
        if n + SWA_LOOKAHEAD < len(chains):
            pending[n + SWA_LOOKAHEAD] = scores(*chains[n + SWA_LOOKAHEAD])
        st = pending.pop(n)
        v_t = vband_t[kvh * SWA_HEAD:(kvh + 1) * SWA_HEAD, j * WINDOW:j * WINDOW + band].astype(BF16)
        v_aug = jnp.concatenate([v_t, ones_rows], axis=0)
        halves = []
        for parity in range(2):
            ps, sink_terms = [], []
            for jj in range(pairs):
                blk = st[parity * band:(parity + 1) * band, jj * WINDOW:(jj + 1) * WINDOW]
                sink = sink_ref[kvh * SWA_GROUP + 2 * jj + parity] * LOG2E
                m = jnp.maximum(jnp.max(blk, axis=0, keepdims=True), sink)
                ps.append(jnp.exp2(blk - m).astype(BF16))
                sink_terms.append(jnp.exp2(sink - m))
            pv = _dot(v_aug, jnp.concatenate(ps, axis=1))
            halves.append(pv[:SWA_HEAD] / (pv[SWA_HEAD:SWA_HEAD + 1] + jnp.concatenate(sink_terms, axis=1)))
        o_t = jnp.concatenate(halves, axis=0)
        for jj in range(pairs):
            pair = kvh * pairs + jj
            o_ref[j * WINDOW:(j + 1) * WINDOW, pair * LANES:(pair + 1) * LANES] = (
                o_t[:, jj * WINDOW:(jj + 1) * WINDOW].T.astype(BF16))


def _swa(sinks, qs, ks, vs, bias_tbl):
    B, S, W = qs.shape
    nb = SWA_STEP_BLOCKS
    prev = lambda b, n: (b, jnp.maximum(nb * n - 1, 0), 0)
    cur = lambda b, n: (b, n, 0)
    tbl_block = (None,) + bias_tbl.shape[1:]
    return pl.pallas_call(
        _swa_kernel,
        grid=(B, S // (nb * WINDOW)),
        in_specs=[pl.BlockSpec(memory_space=pltpu.SMEM),
                  pl.BlockSpec((None, nb * WINDOW, W), cur),
                  pl.BlockSpec((None, WINDOW, LANES), prev), pl.BlockSpec((None, nb * WINDOW, LANES), cur),
                  pl.BlockSpec((None, WINDOW, LANES), prev), pl.BlockSpec((None, nb * WINDOW, LANES), cur),
                  pl.BlockSpec(tbl_block, lambda b, n: (jnp.minimum(n, 1), 0, 0, 0)),
                  pl.BlockSpec(tbl_block, lambda b, n: (1, 0, 0, 0))],
        out_specs=pl.BlockSpec((None, nb * WINDOW, W), cur),
        out_shape=jax.ShapeDtypeStruct((B, S, W), BF16),
        compiler_params=pltpu.CompilerParams(dimension_semantics=("arbitrary", "arbitrary"),
                                             vmem_limit_bytes=VMEM_LIMIT),
        name="swa",
    )(sinks, qs, ks, ks, vs, vs, bias_tbl, bias_tbl)


def _tail_kernel(x_ref, ym_ref, ys_ref, sh1_ref, sc1_ref, ga1_ref, sh2_ref, sc2_ref, ga2_ref,
                 gmix_ref, gmlp_ref, gfin_ref, bg_ref, wg_ref, wom_ref, wos_ref, wo_ref, w1_ref, w2_ref, o_ref):
    subs = [slice(k * TAIL_SUB, (k + 1) * TAIL_SUB) for k in range(TM_TAIL // TAIL_SUB)]
    n_ff = D_FF // FF_CHUNK
    xs = [x_ref[r, :] for r in subs]
    branches = [(_dot(ym_ref[r, :], wom_ref[...]), _dot(ys_ref[r, :], wos_ref[...])) for r in subs]
    hs = [(_rms(x, gmix_ref[...]) * (1 + sc1_ref[...]) + sh1_ref[...]).astype(BF16) for x in xs]
    logits = [_dot_nt(h, wg_ref[...]) for h in hs]
    x1s = []
    for x, (a, b), gl in zip(xs, branches, logits):
        gates = jax.nn.sigmoid(gl + bg_ref[...])
        merged = gates[:, :D_MODEL] * a + gates[:, D_MODEL:] * b
        x1s.append(x + ga1_ref[...] * _dot(merged.astype(BF16), wo_ref[...]))
    h2s = [(_rms(x1, gmlp_ref[...]) * (1 + sc2_ref[...]) + sh2_ref[...]).astype(BF16) for x1 in x1s]

    jobs = [(k, c) for k in range(len(subs)) for c in range(n_ff)]
    up = lambda k, c: _dot(h2s[k], w1_ref[:, c * FF_CHUNK:(c + 1) * FF_CHUNK])
    pending = {0: up(*jobs[0])}
    accs = [None] * len(subs)
    for n, (k, c) in enumerate(jobs):
        if n + 1 < len(jobs):
            pending[n + 1] = up(*jobs[n + 1])
        u = jnp.square(jnp.maximum(pending.pop(n), 0.0)).astype(BF16)
        down = _dot(u, w2_ref[c * FF_CHUNK:(c + 1) * FF_CHUNK, :])
        accs[k] = down if accs[k] is None else accs[k] + down
        if c == n_ff - 1:
            o_ref[subs[k], :] = _rms(x1s[k] + ga2_ref[...] * accs[k], gfin_ref[...])


def _tail(x, ym, ys, mods, gmix, gmlp, gfin, bg, wg, wom, wos, wo, w1, w2):
    B, S, D = x.shape
    tm = TM_TAIL
    tok = lambda width: pl.BlockSpec((None, tm, width), lambda b, i: (b, i, 0))
    per_b = pl.BlockSpec((None, 1, D), lambda b, i: (b, 0, 0))
    const = lambda a: pl.BlockSpec(a.shape, lambda b, i: (0,) * a.ndim, pipeline_mode=pl.Buffered(1))
    return pl.pallas_call(
        _tail_kernel,
        grid=(B, S // tm),
        in_specs=[tok(D), tok(ym.shape[-1]), tok(ys.shape[-1])] + [per_b] * 6
                 + [const(a) for a in (gmix, gmlp, gfin, bg, wg, wom, wos, wo, w1, w2)],
        out_specs=tok(D),
        out_shape=jax.ShapeDtypeStruct((B, S, D), F32),
        compiler_params=pltpu.CompilerParams(dimension_semantics=("arbitrary", "arbitrary"),
                                             vmem_limit_bytes=VMEM_LIMIT),
        name="tail",
    )(x, ym, ys, *mods, gmix, gmlp, gfin, bg, wg, wom, wos, wo, w1, w2)


def _head_slab(nope, a, b):
    r, h, _ = nope.shape
    pad = jnp.zeros((r, h, LANES - MLA_NOPE - MLA_ROPE - HALF_ROPE), nope.dtype)
    return jnp.concatenate([nope, a, b, a, pad], axis=-1).reshape(r, h * LANES)


def _rope_spread():
    m = np.zeros((LANES, 2 * LANES), np.float32)
    for term in range(3):
        for j in range(HALF_ROPE):
            m[term * MLA_ROPE + j, [ROPE_A + j, ROPE_B + j]] = 1.0
            m[term * MLA_ROPE + HALF_ROPE + j, LANES + ROPE_A + j] = -1.0
            m[term * MLA_ROPE + HALF_ROPE + j, LANES + ROPE_B + j] = 1.0
    return jnp.asarray(m, BF16)


def kernel(x, c, positions, rel_bias, ada_w, ada_b, ln_mix_g, w_in, b_gate, mla_q_norm_g, mla_kv_norm_g,
           w_uq, w_ukv, swa_sinks, w_o_mla, w_o_swa, w_o, ln_mlp_g, w_ff1, w_ff2, ln_final_g):
    B, S, D = x.shape
    assert (B, S, D) == (x.shape[0], 4096, D_MODEL) and ada_w.shape[0] == 1
    l = 0

    c_pad = jnp.pad(c, ((0, 8 - B), (0, 0)))
    mod = _adaln(c_pad, ada_w, ada_b)[:B]
    mods = [m[:, None, :] for m in jnp.split(mod, 6, axis=-1)]

    wi = w_in[l].T
    o_kv = MLA_Q_RANK
    o_kr = o_kv + MLA_KV_RANK
    o_qs = o_kr + MLA_ROPE
    o_ks = o_qs + SWA_Q_HEADS * SWA_HEAD
    o_vs = o_ks + SWA_KV_HEADS * SWA_HEAD
    o_ga = o_vs + SWA_KV_HEADS * SWA_HEAD
    kr_a, kr_b = wi[o_kr:o_kr + HALF_ROPE], wi[o_kr + HALF_ROPE:o_qs]
    z = lambda n: jnp.zeros((n, D), wi.dtype)
    w1 = jnp.concatenate([
        wi[:o_kr],
        z(ROPE_A), kr_a, kr_b, kr_a, z(LANES - ROPE_A - MLA_ROPE - HALF_ROPE),
        wi[o_qs:o_ga]], axis=0).astype(BF16)
    assert w1.shape[0] == N_PROJ
    wg = wi[o_ga:].astype(BF16)

    uq = w_uq[l]
    q_nope, q_a, q_b = uq[..., :MLA_NOPE], uq[..., MLA_NOPE:MLA_NOPE + HALF_ROPE], uq[..., MLA_NOPE + HALF_ROPE:]
    wq = _head_slab(q_nope, q_a, q_b).astype(BF16)
    ukv = w_ukv[l]
    k_nope, v_up = ukv[..., :MLA_NOPE], ukv[..., MLA_NOPE:]
    zr = jnp.zeros(k_nope.shape[:2] + (HALF_ROPE,), ukv.dtype)
    wkv = jnp.concatenate([_head_slab(k_nope, zr, zr),
                           v_up.reshape(MLA_KV_RANK, MLA_HEADS * MLA_V)], axis=1).astype(BF16)

    inv = (ROPE_THETA ** (-jnp.arange(HALF_ROPE, dtype=F32) / HALF_ROPE))[:, None]

    qm, km, vm, qs, ks, vs = _proj(x, mods[0], mods[1], ln_mix_g[l][None, :], positions[:, None, :], inv,
                                   _rope_spread(), w1, mla_q_norm_g[l][None, :], mla_kv_norm_g[l][None, :], wq, wkv)

    y_mla = _mla(qm, km, vm)
    y_swa = _swa(swa_sinks[l], qs, ks, vs, _swa_bias(rel_bias))

    return _tail(x, y_mla, y_swa, mods, ln_mix_g[l][None, :], ln_mlp_g[l][None, :], ln_final_g[None, :],
                 b_gate[l][None, :], wg, w_o_mla[l].astype(BF16), w_o_swa[l].astype(BF16), w_o[l].astype(BF16),
                 w_ff1[l].astype(BF16), w_ff2[l].astype(BF16))
```

```python
import functools
import math

import jax
import jax.numpy as jnp
import numpy as np
from jax import lax
from jax.experimental import pallas as pl
from jax.experimental.pallas import tpu as pltpu

F32 = jnp.float32
BF16 = jnp.bfloat16

D_MODEL = 1024
MLA_HEADS = 8
MLA_Q_RANK = 256
MLA_KV_RANK = 128
MLA_NOPE = 64
MLA_ROPE = 32
MLA_V = 64
SWA_Q_HEADS = 16
SWA_KV_HEADS = 2
SWA_HEAD = 64
WINDOW = 128
REL_BUCKETS = 32
REL_MAX_DIST = 128
D_FF = 4 * D_MODEL
ROPE_THETA = 10000.0
EPS = 1e-6

LANES = 128
HALF_ROPE = MLA_ROPE // 2
ROPE_A = MLA_NOPE
ROPE_B = MLA_NOPE + HALF_ROPE
N_PAIRS = SWA_Q_HEADS // 2
SWA_GROUP = SWA_Q_HEADS // SWA_KV_HEADS
SWA_STEP_BLOCKS = 8
SWA_LOOKAHEAD = 3
LOG2E = math.log2(math.e)

C_QLAT = 0
C_KVLAT = C_QLAT + MLA_Q_RANK
C_KR = C_KVLAT + MLA_KV_RANK
C_QS = C_KR + LANES
C_KS = C_QS + SWA_Q_HEADS * SWA_HEAD
C_VS = C_KS + SWA_KV_HEADS * SWA_HEAD
N_PROJ = C_VS + SWA_KV_HEADS * SWA_HEAD

VMEM_LIMIT = 56 * 1024 * 1024

TM_PROJ = 512
PROJ_SUB = 256
TQ_MLA = 1024
TK_MLA = 256
MLA_GROUP = 4
MLA_LOOKAHEAD = 2
MLA_ACC_ROWS = MLA_V + 16
TM_TAIL = 512
TAIL_SUB = 256
FF_CHUNK = 1024


def _rms(x, g):
    return x * lax.rsqrt(jnp.mean(x * x, axis=-1, keepdims=True) + EPS) * g


def _dot(a, b):
    return jnp.dot(a, b, preferred_element_type=F32)


def _dot_nt(a, b):
    return lax.dot_general(a, b, (((1,), (1,)), ((), ())), preferred_element_type=F32)


def _adaln_kernel(c_ref, w_ref, b_ref, o_ref):
    c = c_ref[...]
    act = (c * jax.nn.sigmoid(c)).astype(BF16)
    o_ref[...] = _dot(act, w_ref[...].astype(BF16)) + b_ref[...]


def _adaln(c_pad, ada_w, ada_b):
    rows = c_pad.shape[0]
    n_out = ada_w.shape[2]
    return pl.pallas_call(
        _adaln_kernel,
        grid=(n_out // D_MODEL,),
        in_specs=[pl.BlockSpec((rows, D_MODEL), lambda j: (0, 0)),
                  pl.BlockSpec((None, D_MODEL, D_MODEL), lambda j: (0, 0, j)),
                  pl.BlockSpec((1, D_MODEL), lambda j: (0, j))],
        out_specs=pl.BlockSpec((rows, D_MODEL), lambda j: (0, j)),
        out_shape=jax.ShapeDtypeStruct((rows, n_out), F32),
        compiler_params=pltpu.CompilerParams(dimension_semantics=("arbitrary",)),
        name="adaln",
    )(c_pad, ada_w, ada_b)


def _swa_bias_kernel(rb_ref, o_ref):
    a = lax.broadcasted_iota(jnp.int32, (2 * WINDOW, WINDOW), 1)
    b = lax.broadcasted_iota(jnp.int32, (2 * WINDOW, WINDOW), 0)
    dist = WINDOW + a - b
    n = jnp.maximum(dist, 0)
    max_exact = REL_BUCKETS // 2
    nf = jnp.maximum(n, 1).astype(F32)
    large = max_exact + jnp.floor(jnp.log(nf / max_exact) / math.log(REL_MAX_DIST / max_exact)
                                  * (REL_BUCKETS - max_exact)).astype(jnp.int32)
    large = jnp.minimum(large, REL_BUCKETS - 1)
    bucket = jnp.where(n < max_exact, n, large)
    band_ok = (dist >= 0) & (dist < WINDOW)
    has_prev = b >= WINDOW
    neg = jnp.float32(-jnp.inf)
    for head in range(SWA_Q_HEADS):
        bias = jnp.zeros((2 * WINDOW, WINDOW), F32)
        for k in range(REL_BUCKETS):
            bias = jnp.where(bucket == k, rb_ref[head, k], bias)
        kvh, pair, parity = head // SWA_GROUP, (head % SWA_GROUP) // 2, head % 2
        rows = slice(parity * 2 * WINDOW, (parity + 1) * 2 * WINDOW)
        cols = slice(pair * WINDOW, (pair + 1) * WINDOW)
        o_ref[0, kvh, rows, cols] = jnp.where(band_ok & has_prev, bias * LOG2E, neg)
        o_ref[1, kvh, rows, cols] = jnp.where(band_ok, bias * LOG2E, neg)


def _swa_bias(rel_bias):
    return pl.pallas_call(
        _swa_bias_kernel,
        in_specs=[pl.BlockSpec(memory_space=pltpu.SMEM)],
        out_specs=pl.BlockSpec(memory_space=pltpu.VMEM),
        out_shape=jax.ShapeDtypeStruct((2, SWA_KV_HEADS, 4 * WINDOW, SWA_GROUP // 2 * WINDOW), F32),
        name="swa_bias",
    )(rel_bias)


def _proj_kernel(x_ref, sh_ref, sc_ref, g_ref, pos_ref, inv_ref, spread_ref, w1_ref, gq_ref, gkv_ref, wq_ref,
                 wkv_ref, qm_ref, km_ref, vm_ref, qs_ref, ks_ref, vs_ref):
    subs = [slice(k * PROJ_SUB, (k + 1) * PROJ_SUB) for k in range(TM_PROJ // PROJ_SUB)]
    ps = []
    for r in subs:
        h = _rms(x_ref[r, :], g_ref[...]) * (1 + sc_ref[...]) + sh_ref[...]
        ps.append(_dot_nt(h.astype(BF16), w1_ref[...]))

    ang = inv_ref[...] * pos_ref[...].astype(F32)
    cs = jnp.concatenate([jnp.cos(ang), jnp.sin(ang)], axis=0)
    hi = cs.astype(BF16).astype(F32)
    mid = (cs - hi).astype(BF16).astype(F32)
    lo = (cs - hi - mid).astype(BF16).astype(F32)
    terms = jnp.concatenate([hi, mid, lo, jnp.zeros_like(hi)], axis=0)
    tables = _dot(terms.T.astype(BF16), spread_ref[...])
    lane_t = lax.broadcasted_iota(jnp.int32, (PROJ_SUB, LANES), 1)
    scale = (MLA_NOPE + MLA_ROPE) ** -0.5 * LOG2E
    half = MLA_HEADS * LANES

    ups = []
    for p in ps:
        qn = _rms(p[:, C_QLAT:C_QLAT + MLA_Q_RANK], gq_ref[...]).astype(BF16)
        kvn = _rms(p[:, C_KVLAT:C_KVLAT + MLA_KV_RANK], gkv_ref[...]).astype(BF16)
        ups.append((_dot(qn, wq_ref[...]), _dot(kvn, wkv_ref[...])))

    for k, (r, p, (qq, kk)) in enumerate(zip(subs, ps, ups)):
        cos_f, sin_f = tables[r, :LANES], tables[r, LANES:]
        cos_q = jnp.where(lane_t < MLA_NOPE, 1.0, cos_f)

        def rope(t, cos):
            return t * cos + pltpu.roll(t, LANES - HALF_ROPE, 1) * sin_f

        for hd in range(MLA_HEADS):
            qm_ref[hd, r, :] = (rope(qq[:, hd * LANES:(hd + 1) * LANES], cos_q) * scale).astype(BF16)
        k_pe = rope(p[:, C_KR:C_KR + LANES], cos_f)
        for hd in range(MLA_HEADS):
            km_ref[hd, r, :] = (kk[:, hd * LANES:(hd + 1) * LANES] + k_pe).astype(BF16)
        for pr in range(MLA_HEADS // 2):
            v_t = kk[:, half + pr * LANES:half + (pr + 1) * LANES].T.astype(BF16)
            for t in range(PROJ_SUB // TK_MLA):
                vm_ref[pr, k * (PROJ_SUB // TK_MLA) + t] = v_t[:, t * TK_MLA:(t + 1) * TK_MLA]
        qs_ref[r, :] = (p[:, C_QS:C_KS] * (SWA_HEAD ** -0.5 * LOG2E)).astype(BF16)
        ks_ref[r, :] = p[:, C_KS:C_VS].astype(BF16)
        vs_ref[r, :] = p[:, C_VS:N_PROJ].astype(BF16)


def _proj(x, sh1, sc1, ln_g, pos3, inv, spread, w1, gq, gkv, wq, wkv):
    B, S, D = x.shape
    tm = TM_PROJ
    const = lambda shape: pl.BlockSpec(shape, lambda b, i: (0,) * len(shape))
    return pl.pallas_call(
        _proj_kernel,
        grid=(B, S // tm),
        in_specs=[pl.BlockSpec((None, tm, D), lambda b, i: (b, i, 0)),
                  pl.BlockSpec((None, 1, D), lambda b, i: (b, 0, 0)),
                  pl.BlockSpec((None, 1, D), lambda b, i: (b, 0, 0)),
                  const((1, D)),
                  pl.BlockSpec((None, 1, tm), lambda b, i: (b, 0, i)),
                  const((HALF_ROPE, 1)), const(spread.shape),
                  const(w1.shape), const(gq.shape), const(gkv.shape), const(wq.shape), const(wkv.shape)],
        out_specs=[pl.BlockSpec((None, MLA_HEADS, tm, LANES), lambda b, i: (b, 0, i, 0)),
                   pl.BlockSpec((None, MLA_HEADS, tm, LANES), lambda b, i: (b, 0, i, 0)),
                   pl.BlockSpec((None, MLA_HEADS // 2, tm // TK_MLA, LANES, TK_MLA), lambda b, i: (b, 0, i, 0, 0)),
                   pl.BlockSpec((None, tm, SWA_Q_HEADS * SWA_HEAD), lambda b, i: (b, i, 0)),
                   pl.BlockSpec((None, tm, LANES), lambda b, i: (b, i, 0)),
                   pl.BlockSpec((None, tm, LANES), lambda b, i: (b, i, 0))],
        out_shape=[jax.ShapeDtypeStruct((B, MLA_HEADS, S, LANES), BF16),
                   jax.ShapeDtypeStruct((B, MLA_HEADS, S, LANES), BF16),
                   jax.ShapeDtypeStruct((B, MLA_HEADS // 2, S // TK_MLA, LANES, TK_MLA), BF16),
                   jax.ShapeDtypeStruct((B, S, SWA_Q_HEADS * SWA_HEAD), BF16),
                   jax.ShapeDtypeStruct((B, S, LANES), BF16),
                   jax.ShapeDtypeStruct((B, S, LANES), BF16)],
        compiler_params=pltpu.CompilerParams(dimension_semantics=("arbitrary", "arbitrary"),
                                             vmem_limit_bytes=VMEM_LIMIT),
        name="proj",
    )(x, sh1, sc1, ln_g, pos3, inv, spread, w1, gq, gkv, wq, wkv)


def _mla_kernel(q_ref, k_ref, vt_ref, o_ref, qt_sc, acc_sc, st_sc):
    tq, tk = TQ_MLA, TK_MLA
    nc = tq // LANES
    i = pl.program_id(2)
    for hd in range(MLA_GROUP):
        qt_sc[hd] = q_ref[hd].astype(F32).T.astype(BF16)
    acc_sc[...] = jnp.zeros_like(acc_sc)

    per_tile = tq // tk

    ones_rows = (lax.broadcasted_iota(jnp.int32, (MLA_ACC_ROWS - MLA_V, tk), 0) == 0).astype(BF16)
    r_iota = lax.broadcasted_iota(jnp.int32, (tk, LANES), 0)
    c_iota = lax.broadcasted_iota(jnp.int32, (tk, LANES), 1)

    def tile(t, ms_all, diagonal):
        blocks = [(d, hd) for d in range(per_tile) for hd in range(MLA_GROUP)]
        first_cc = [(d * tk) // LANES if diagonal else 0 for d, _ in blocks]

        def scores(n):
            d, hd = blocks[n]
            k = k_ref[hd, pl.ds(pl.multiple_of((t * per_tile + d) * tk, tk), tk), :]
            st_sc[n % (MLA_LOOKAHEAD + 1), :, first_cc[n] * LANES:] = _dot(k, qt_sc[hd, :, first_cc[n] * LANES:])

        for n in range(min(MLA_LOOKAHEAD, len(blocks))):
            scores(n)
        ms_all = list(ms_all)
        for n, (d, hd) in enumerate(blocks):
            if n + MLA_LOOKAHEAD < len(blocks):
                scores(n + MLA_LOOKAHEAD)
            ms = list(ms_all[hd])
            ps, alphas = [], []
            for cc in range(first_cc[n], nc):
                blk = st_sc[n % (MLA_LOOKAHEAD + 1), :, cc * LANES:(cc + 1) * LANES]
                if diagonal and cc * LANES < (d + 1) * tk - 1:
                    blk = jnp.where(r_iota + (d * tk - cc * LANES) <= c_iota, blk, -jnp.inf)
                m_new = jnp.maximum(ms[cc], jnp.max(blk, axis=0, keepdims=True))
                alphas.append(jnp.exp2(ms[cc] - m_new))
                ps.append(jnp.exp2(blk - m_new).astype(BF16))
                ms[cc] = m_new
            v_t = vt_ref[hd // 2, t * per_tile + d, (hd % 2) * MLA_V:(hd % 2 + 1) * MLA_V, :]
            pv = _dot(jnp.concatenate([v_t, ones_rows], axis=0), jnp.concatenate(ps, axis=1))
            rows = slice(hd * MLA_ACC_ROWS, (hd + 1) * MLA_ACC_ROWS)
            cols = slice(first_cc[n] * LANES, tq)
            acc_sc[rows, cols] = acc_sc[rows, cols] * jnp.concatenate(alphas, axis=1) + pv
            ms_all[hd] = tuple(ms)
        return tuple(ms_all)

    init = tuple(tuple(jnp.full((1, LANES), -jnp.inf, F32) for _ in range(nc)) for _ in range(MLA_GROUP))
    ms_all = lax.fori_loop(0, i, lambda t, c: tile(t, c, False), init)
    tile(i, ms_all, True)
    outs = []
    for hd in range(MLA_GROUP):
        acc = acc_sc[hd * MLA_ACC_ROWS:(hd + 1) * MLA_ACC_ROWS, :]
        outs.append(acc[:MLA_V] / acc[MLA_V:MLA_V + 1])
    o_ref[...] = jnp.concatenate(outs, axis=0).T.astype(BF16)


def _mla(qm, km, vt):
    B, H, S, _ = qm.shape
    tq, tk, g = TQ_MLA, TK_MLA, MLA_GROUP
    return pl.pallas_call(
        _mla_kernel,
        grid=(B, H // g, S // tq),
        in_specs=[pl.BlockSpec((None, g, tq, LANES), lambda b, p, i: (b, p, i, 0)),
                  pl.BlockSpec((None, g, S, LANES), lambda b, p, i: (b, p, 0, 0)),
                  pl.BlockSpec((None, g // 2, S // tk, LANES, tk), lambda b, p, i: (b, p, 0, 0, 0))],
        out_specs=pl.BlockSpec((None, tq, g * MLA_V), lambda b, p, i: (b, i, p)),
        out_shape=jax.ShapeDtypeStruct((B, S, H * MLA_V), BF16),
        scratch_shapes=[pltpu.VMEM((g, LANES, tq), BF16), pltpu.VMEM((g * MLA_ACC_ROWS, tq), F32),
                        pltpu.VMEM((MLA_LOOKAHEAD + 1, tk, tq), F32)],
        compiler_params=pltpu.CompilerParams(dimension_semantics=("arbitrary",) * 3,
                                             vmem_limit_bytes=VMEM_LIMIT),
        name="mla",
    )(qm, km, vt)


def _swa_kernel(sink_ref, q_ref, kp_ref, kc_ref, vp_ref, vc_ref, bias0_ref, bias_ref, o_ref):
    band = 2 * WINDOW
    pairs = SWA_GROUP // 2
    kband = jnp.concatenate([kp_ref[...], kc_ref[...]], axis=0).astype(F32)
    vband_t = jnp.concatenate([vp_ref[...], vc_ref[...]], axis=0).astype(F32).T
    lo = lax.broadcasted_iota(jnp.int32, kband.shape, 1) < SWA_HEAD
    kswap = pltpu.roll(kband, SWA_HEAD, 1)
    k_even = [jnp.where(lo, kband, 0.0).astype(BF16), jnp.where(lo, kswap, 0.0).astype(BF16)]
    k_odd = [jnp.where(lo, 0.0, kswap).astype(BF16), jnp.where(lo, 0.0, kband).astype(BF16)]
    ones_rows = (lax.broadcasted_iota(jnp.int32, (16, band), 0) == 0).astype(BF16)
    chains = [(j, kvh) for j in range(SWA_STEP_BLOCKS) for kvh in range(SWA_KV_HEADS)]

    def scores(j, kvh):
        keys = slice(j * WINDOW, j * WINDOW + band)
        kcat = jnp.concatenate([k_even[kvh][keys], k_odd[kvh][keys]], axis=0)
        qstack = jnp.concatenate([q_ref[j * WINDOW:(j + 1) * WINDOW, (kvh * pairs + jj) * LANES:
                                        (kvh * pairs + jj + 1) * LANES] for jj in range(pairs)], axis=0)
        bias = bias0_ref if j == 0 else bias_ref
        return _dot_nt(kcat, qstack) + bias[kvh]

    pending = {n: scores(*chains[n]) for n in range(min(SWA_LOOKAHEAD, len(chains)))}
    for n, (j, kvh) in enumerate(chains):
        if n + SWA_LOOKAHEAD < len(chains):
            pending[n + SWA_LOOKAHEAD] = scores(*chains[n + SWA_LOOKAHEAD])
        st = pending.pop(n)
        v_t = vband_t[kvh * SWA_HEAD:(kvh + 1) * SWA_HEAD, j * WINDOW:j * WINDOW + band].astype(BF16)
        v_aug = jnp.concatenate([v_t, ones_rows], axis=0)
        halves = []
        for parity in range(2):
            ps, sink_terms = [], []
            for jj in range(pairs):
                blk = st[parity * band:(parity + 1) * band, jj * WINDOW:(jj + 1) * WINDOW]
                sink = sink_ref[kvh * SWA_GROUP + 2 * jj + parity] * LOG2E
                m = jnp.maximum(jnp.max(blk, axis=0, keepdims=True), sink)
                ps.append(jnp.exp2(blk - m).astype(BF16))
                sink_terms.append(jnp.exp2(sink - m))
            pv = _dot(v_aug, jnp.concatenate(ps, axis=1))
            halves.append(pv[:SWA_HEAD] / (pv[SWA_HEAD:SWA_HEAD + 1] + jnp.concatenate(sink_terms, axis=1)))
        o_t = jnp.concatenate(halves, axis=0)
        for jj in range(pairs):
            pair = kvh * pairs + jj
            o_ref[j * WINDOW:(j + 1) * WINDOW, pair * LANES:(pair + 1) * LANES] = (
                o_t[:, jj * WINDOW:(jj + 1) * WINDOW].T.astype(BF16))


def _swa(sinks, qs, ks, vs, bias_tbl):
    B, S, W = qs.shape
    nb = SWA_STEP_BLOCKS
    prev = lambda b, n: (b, jnp.maximum(nb * n - 1, 0), 0)
    cur = lambda b, n: (b, n, 0)
    tbl_block = (None,) + bias_tbl.shape[1:]
    return pl.pallas_call(
        _swa_kernel,
        grid=(B, S // (nb * WINDOW)),
        in_specs=[pl.BlockSpec(memory_space=pltpu.SMEM),
                  pl.BlockSpec((None, nb * WINDOW, W), cur),
                  pl.BlockSpec((None, WINDOW, LANES), prev), pl.BlockSpec((None, nb * WINDOW, LANES), cur),
                  pl.BlockSpec((None, WINDOW, LANES), prev), pl.BlockSpec((None, nb * WINDOW, LANES), cur),
                  pl.BlockSpec(tbl_block, lambda b, n: (jnp.minimum(n, 1), 0, 0, 0)),
                  pl.BlockSpec(tbl_block, lambda b, n: (1, 0, 0, 0))],
        out_specs=pl.BlockSpec((None, nb * WINDOW, W), cur),
        out_shape=jax.ShapeDtypeStruct((B, S, W), BF16),
        compiler_params=pltpu.CompilerParams(dimension_semantics=("arbitrary", "arbitrary"),
                                             vmem_limit_bytes=VMEM_LIMIT),
        name="swa",
    )(sinks, qs, ks, ks, vs, vs, bias_tbl, bias_tbl)


def _tail_kernel(x_ref, ym_ref, ys_ref, sh1_ref, sc1_ref, ga1_ref, sh2_ref, sc2_ref, ga2_ref,
                 gmix_ref, gmlp_ref, gfin_ref, bg_ref, wg_ref, wom_ref, wos_ref, wo_ref, w1_ref, w2_ref, o_ref):
    subs = [slice(k * TAIL_SUB, (k + 1) * TAIL_SUB) for k in range(TM_TAIL // TAIL_SUB)]
    n_ff = D_FF // FF_CHUNK
    xs = [x_ref[r, :] for r in subs]
    branches = [(_dot(ym_ref[r, :], wom_ref[...]), _dot(ys_ref[r, :], wos_ref[...])) for r in subs]
    hs = [(_rms(x, gmix_ref[...]) * (1 + sc1_ref[...]) + sh1_ref[...]).astype(BF16) for x in xs]
    logits = [_dot_nt(h, wg_ref[...]) for h in hs]
    x1s = []
    for x, (a, b), gl in zip(xs, branches, logits):
        gates = jax.nn.sigmoid(gl + bg_ref[...])
        merged = gates[:, :D_MODEL] * a + gates[:, D_MODEL:] * b
        x1s.append(x + ga1_ref[...] * _dot(merged.astype(BF16), wo_ref[...]))
    h2s = [(_rms(x1, gmlp_ref[...]) * (1 + sc2_ref[...]) + sh2_ref[...]).astype(BF16) for x1 in x1s]

    jobs = [(k, c) for k in range(len(subs)) for c in range(n_ff)]
    up = lambda k, c: _dot(h2s[k], w1_ref[:, c * FF_CHUNK:(c + 1) * FF_CHUNK])
    pending = {0: up(*jobs[0])}
    accs = [None] * len(subs)
    for n, (k, c) in enumerate(jobs):
        if n + 1 < len(jobs):
            pending[n + 1] = up(*jobs[n + 1])
        u = jnp.square(jnp.maximum(pending.pop(n), 0.0)).astype(BF16)
        down = _dot(u, w2_ref[c * FF_CHUNK:(c + 1) * FF_CHUNK, :])
        accs[k] = down if accs[k] is None else accs[k] + down
        if c == n_ff - 1:
            o_ref[subs[k], :] = _rms(x1s[k] + ga2_ref[...] * accs[k], gfin_ref[...])


def _tail(x, ym, ys, mods, gmix, gmlp, gfin, bg, wg, wom, wos, wo, w1, w2):
    B, S, D = x.shape
    tm = TM_TAIL
    tok = lambda width: pl.BlockSpec((None, tm, width), lambda b, i: (b, i, 0))
    per_b = pl.BlockSpec((None, 1, D), lambda b, i: (b, 0, 0))
    const = lambda a: pl.BlockSpec(a.shape, lambda b, i: (0,) * a.ndim, pipeline_mode=pl.Buffered(1))
    return pl.pallas_call(
        _tail_kernel,
        grid=(B, S // tm),
        in_specs=[tok(D), tok(ym.shape[-1]), tok(ys.shape[-1])] + [per_b] * 6
                 + [const(a) for a in (gmix, gmlp, gfin, bg, wg, wom, wos, wo, w1, w2)],
        out_specs=tok(D),
        out_shape=jax.ShapeDtypeStruct((B, S, D), F32),
        compiler_params=pltpu.CompilerParams(dimension_semantics=("arbitrary", "arbitrary"),
                                             vmem_limit_bytes=VMEM_LIMIT),
        name="tail",
    )(x, ym, ys, *mods, gmix, gmlp, gfin, bg, wg, wom, wos, wo, w1, w2)


def _head_slab(nope, a, b):
    r, h, _ = nope.shape
    pad = jnp.zeros((r, h, LANES - MLA_NOPE - MLA_ROPE - HALF_ROPE), nope.dtype)
    return jnp.concatenate([nope, a, b, a, pad], axis=-1).reshape(r, h * LANES)


def _rope_spread():
    m = np.zeros((LANES, 2 * LANES), np.float32)
    for term in range(3):
        for j in range(HALF_ROPE):
            m[term * MLA_ROPE + j, [ROPE_A + j, ROPE_B + j]] = 1.0
            m[term * MLA_ROPE + HALF_ROPE + j, LANES + ROPE_A + j] = -1.0
            m[term * MLA_ROPE + HALF_ROPE + j, LANES + ROPE_B + j] = 1.0
    return jnp.asarray(m, BF16)


def kernel(x, c, positions, rel_bias, ada_w, ada_b, ln_mix_g, w_in, b_gate, mla_q_norm_g, mla_kv_norm_g,
           w_uq, w_ukv, swa_sinks, w_o_mla, w_o_swa, w_o, ln_mlp_g, w_ff1, w_ff2, ln_final_g):
    B, S, D = x.shape
    assert (B, S, D) == (x.shape[0], 4096, D_MODEL) and ada_w.shape[0] == 1
    l = 0

    c_pad = jnp.pad(c, ((0, 8 - B), (0, 0)))
    mod = _adaln(c_pad, ada_w, ada_b)[:B]
    mods = [m[:, None, :] for m in jnp.split(mod, 6, axis=-1)]

    wi = w_in[l].T
    o_kv = MLA_Q_RANK
    o_kr = o_kv + MLA_KV_RANK
    o_qs = o_kr + MLA_ROPE
    o_ks = o_qs + SWA_Q_HEADS * SWA_HEAD
    o_vs = o_ks + SWA_KV_HEADS * SWA_HEAD
    o_ga = o_vs + SWA_KV_HEADS * SWA_HEAD
    kr_a, kr_b = wi[o_kr:o_kr + HALF_ROPE], wi[o_kr + HALF_ROPE:o_qs]
    z = lambda n: jnp.zeros((n, D), wi.dtype)
    w1 = jnp.concatenate([
        wi[:o_kr],
        z(ROPE_A), kr_a, kr_b, kr_a, z(LANES - ROPE_A - MLA_ROPE - HALF_ROPE),
        wi[o_qs:o_ga]], axis=0).astype(BF16)
    assert w1.shape[0] == N_PROJ
    wg = wi[o_ga:].astype(BF16)

    uq = w_uq[l]
    q_nope, q_a, q_b = uq[..., :MLA_NOPE], uq[..., MLA_NOPE:MLA_NOPE + HALF_ROPE], uq[..., MLA_NOPE + HALF_ROPE:]
    wq = _head_slab(q_nope, q_a, q_b).astype(BF16)
    ukv = w_ukv[l]
    k_nope, v_up = ukv[..., :MLA_NOPE], ukv[..., MLA_NOPE:]
    zr = jnp.zeros(k_nope.shape[:2] + (HALF_ROPE,), ukv.dtype)
    wkv = jnp.concatenate([_head_slab(k_nope, zr, zr),
                           v_up.reshape(MLA_KV_RANK, MLA_HEADS * MLA_V)], axis=1).astype(BF16)

    inv = (ROPE_THETA ** (-jnp.arange(HALF_ROPE, dtype=F32) / HALF_ROPE))[:, None]

    qm, km, vm, qs, ks, vs = _proj(x, mods[0], mods[1], ln_mix_g[l][None, :], positions[:, None, :], inv,
                                   _rope_spread(), w1, mla_q_norm_g[l][None, :], mla_kv_norm_g[l][None, :], wq, wkv)

    y_mla = _mla(qm, km, vm)
    y_swa = _swa(swa_sinks[l], qs, ks, vs, _swa_bias(rel_bias))

    return _tail(x, y_mla, y_swa, mods, ln_mix_g[l][None, :], ln_mlp_g[l][None, :], ln_final_g[None, :],
                 b_gate[l][None, :], wg, w_o_mla[l].astype(BF16), w_o_swa[l].astype(BF16), w_o[l].astype(BF16),
                 w_ff1[l].astype(BF16), w_ff2[l].astype(BF16))
```

```python
import functools
import math

import jax
import jax.numpy as jnp
import numpy as np
from jax import lax
from jax.experimental import pallas as pl
from jax.experimental.pallas import tpu as pltpu

F32 = jnp.float32
BF16 = jnp.bfloat16

D_MODEL = 1024
MLA_HEADS = 8
MLA_Q_RANK = 256
MLA_KV_RANK = 128
MLA_NOPE = 64
MLA_ROPE = 32
MLA_V = 64
SWA_Q_HEADS = 16
SWA_KV_HEADS = 2
SWA_HEAD = 64
WINDOW = 128
REL_BUCKETS = 32
REL_MAX_DIST = 128
D_FF = 4 * D_MODEL
ROPE_THETA = 10000.0
EPS = 1e-6

LANES = 128
HALF_ROPE = MLA_ROPE // 2
ROPE_A = MLA_NOPE
ROPE_B = MLA_NOPE + HALF_ROPE
N_PAIRS = SWA_Q_HEADS // 2
SWA_GROUP = SWA_Q_HEADS // SWA_KV_HEADS
SWA_STEP_BLOCKS = 8
SWA_LOOKAHEAD = 2
LOG2E = math.log2(math.e)

C_QLAT = 0
C_KVLAT = C_QLAT + MLA_Q_RANK
C_KR = C_KVLAT + MLA_KV_RANK
C_QS = C_KR + LANES
C_KS = C_QS + SWA_Q_HEADS * SWA_HEAD
C_VS = C_KS + SWA_KV_HEADS * SWA_HEAD
N_PROJ = C_VS + SWA_KV_HEADS * SWA_HEAD

VMEM_LIMIT = 56 * 1024 * 1024

TM_PROJ = 512
PROJ_SUB = 256
TQ_MLA = 1024
TK_MLA = 256
MLA_GROUP = 4
MLA_LOOKAHEAD = 2
MLA_ACC_ROWS = MLA_V + 16
TM_TAIL = 512
TAIL_SUB = 256
FF_CHUNK = 1024


def _rms(x, g):
    return x * lax.rsqrt(jnp.mean(x * x, axis=-1, keepdims=True) + EPS) * g


def _dot(a, b):
    return jnp.dot(a, b, preferred_element_type=F32)


def _dot_nt(a, b):
    return lax.dot_general(a, b, (((1,), (1,)), ((), ())), preferred_element_type=F32)


def _adaln_kernel(c_ref, w_ref, b_ref, o_ref):
    c = c_ref[...]
    act = (c * jax.nn.sigmoid(c)).astype(BF16)
    o_ref[...] = _dot(act, w_ref[...].astype(BF16)) + b_ref[...]


def _adaln(c_pad, ada_w, ada_b):
    rows = c_pad.shape[0]
    n_out = ada_w.shape[2]
    return pl.pallas_call(
        _adaln_kernel,
        grid=(n_out // D_MODEL,),
        in_specs=[pl.BlockSpec((rows, D_MODEL), lambda j: (0, 0)),
                  pl.BlockSpec((None, D_MODEL, D_MODEL), lambda j: (0, 0, j)),
                  pl.BlockSpec((1, D_MODEL), lambda j: (0, j))],
        out_specs=pl.BlockSpec((rows, D_MODEL), lambda j: (0, j)),
        out_shape=jax.ShapeDtypeStruct((rows, n_out), F32),
        compiler_params=pltpu.CompilerParams(dimension_semantics=("arbitrary",)),
        name="adaln",
    )(c_pad, ada_w, ada_b)


def _swa_bias_kernel(rb_ref, o_ref):
    a = lax.broadcasted_iota(jnp.int32, (2 * WINDOW, WINDOW), 1)
    b = lax.broadcasted_iota(jnp.int32, (2 * WINDOW, WINDOW), 0)
    dist = WINDOW + a - b
    n = jnp.maximum(dist, 0)
    max_exact = REL_BUCKETS // 2
    nf = jnp.maximum(n, 1).astype(F32)
    large = max_exact + jnp.floor(jnp.log(nf / max_exact) / math.log(REL_MAX_DIST / max_exact)
                                  * (REL_BUCKETS - max_exact)).astype(jnp.int32)
    large = jnp.minimum(large, REL_BUCKETS - 1)
    bucket = jnp.where(n < max_exact, n, large)
    band_ok = (dist >= 0) & (dist < WINDOW)
    has_prev = b >= WINDOW
    neg = jnp.float32(-jnp.inf)
    for head in range(SWA_Q_HEADS):
        bias = jnp.zeros((2 * WINDOW, WINDOW), F32)
        for k in range(REL_BUCKETS):
            bias = jnp.where(bucket == k, rb_ref[head, k], bias)
        kvh, pair, parity = head // SWA_GROUP, (head % SWA_GROUP) // 2, head % 2
        rows = slice(parity * 2 * WINDOW, (parity + 1) * 2 * WINDOW)
        cols = slice(pair * WINDOW, (pair + 1) * WINDOW)
        o_ref[0, kvh, rows, cols] = jnp.where(band_ok & has_prev, bias * LOG2E, neg)
        o_ref[1, kvh, rows, cols] = jnp.where(band_ok, bias * LOG2E, neg)


def _swa_bias(rel_bias):
    return pl.pallas_call(
        _swa_bias_kernel,
        in_specs=[pl.BlockSpec(memory_space=pltpu.SMEM)],
        out_specs=pl.BlockSpec(memory_space=pltpu.VMEM),
        out_shape=jax.ShapeDtypeStruct((2, SWA_KV_HEADS, 4 * WINDOW, SWA_GROUP // 2 * WINDOW), F32),
        name="swa_bias",
    )(rel_bias)


def _proj_kernel(x_ref, sh_ref, sc_ref, g_ref, pos_ref, inv_ref, spread_ref, w1_ref, gq_ref, gkv_ref, wq_ref,
                 wkv_ref, qm_ref, km_ref, vm_ref, qs_ref, ks_ref, vs_ref):
    subs = [slice(k * PROJ_SUB, (k + 1) * PROJ_SUB) for k in range(TM_PROJ // PROJ_SUB)]
    ps = []
    for r in subs:
        h = _rms(x_ref[r, :], g_ref[...]) * (1 + sc_ref[...]) + sh_ref[...]
        ps.append(_dot_nt(h.astype(BF16), w1_ref[...]))

    ang = inv_ref[...] * pos_ref[...].astype(F32)
    cs = jnp.concatenate([jnp.cos(ang), jnp.sin(ang)], axis=0)
    hi = cs.astype(BF16).astype(F32)
    mid = (cs - hi).astype(BF16).astype(F32)
    lo = (cs - hi - mid).astype(BF16).astype(F32)
    terms = jnp.concatenate([hi, mid, lo, jnp.zeros_like(hi)], axis=0)
    tables = _dot(terms.T.astype(BF16), spread_ref[...])
    lane_t = lax.broadcasted_iota(jnp.int32, (PROJ_SUB, LANES), 1)
    scale = (MLA_NOPE + MLA_ROPE) ** -0.5 * LOG2E
    half = MLA_HEADS * LANES

    ups = []
    for p in ps:
        qn = _rms(p[:, C_QLAT:C_QLAT + MLA_Q_RANK], gq_ref[...]).astype(BF16)
        kvn = _rms(p[:, C_KVLAT:C_KVLAT + MLA_KV_RANK], gkv_ref[...]).astype(BF16)
        ups.append((_dot(qn, wq_ref[...]), _dot(kvn, wkv_ref[...])))

    for k, (r, p, (qq, kk)) in enumerate(zip(subs, ps, ups)):
        cos_f, sin_f = tables[r, :LANES], tables[r, LANES:]
        cos_q = jnp.where(lane_t < MLA_NOPE, 1.0, cos_f)

        def rope(t, cos):
            return t * cos + pltpu.roll(t, LANES - HALF_ROPE, 1) * sin_f

        for hd in range(MLA_HEADS):
            qm_ref[hd, r, :] = (rope(qq[:, hd * LANES:(hd + 1) * LANES], cos_q) * scale).astype(BF16)
        k_pe = rope(p[:, C_KR:C_KR + LANES], cos_f)
        for hd in range(MLA_HEADS):
            km_ref[hd, r, :] = (kk[:, hd * LANES:(hd + 1) * LANES] + k_pe).astype(BF16)
        for pr in range(MLA_HEADS // 2):
            v_t = kk[:, half + pr * LANES:half + (pr + 1) * LANES].T.astype(BF16)
            for t in range(PROJ_SUB // TK_MLA):
                vm_ref[pr, k * (PROJ_SUB // TK_MLA) + t] = v_t[:, t * TK_MLA:(t + 1) * TK_MLA]
        qs_ref[r, :] = (p[:, C_QS:C_KS] * (SWA_HEAD ** -0.5 * LOG2E)).astype(BF16)
        ks_ref[r, :] = p[:, C_KS:C_VS].astype(BF16)
        vs_ref[r, :] = p[:, C_VS:N_PROJ].astype(BF16)


def _proj(x, sh1, sc1, ln_g, pos3, inv, spread, w1, gq, gkv, wq, wkv):
    B, S, D = x.shape
    tm = TM_PROJ
    const = lambda shape: pl.BlockSpec(shape, lambda b, i: (0,) * len(shape))
    return pl.pallas_call(
        _proj_kernel,
        grid=(B, S // tm),
        in_specs=[pl.BlockSpec((None, tm, D), lambda b, i: (b, i, 0)),
                  pl.BlockSpec((None, 1, D), lambda b, i: (b, 0, 0)),
                  pl.BlockSpec((None, 1, D), lambda b, i: (b, 0, 0)),
                  const((1, D)),
                  pl.BlockSpec((None, 1, tm), lambda b, i: (b, 0, i)),
                  const((HALF_ROPE, 1)), const(spread.shape),
                  const(w1.shape), const(gq.shape), const(gkv.shape), const(wq.shape), const(wkv.shape)],
        out_specs=[pl.BlockSpec((None, MLA_HEADS, tm, LANES), lambda b, i: (b, 0, i, 0)),
                   pl.BlockSpec((None, MLA_HEADS, tm, LANES), lambda b, i: (b, 0, i, 0)),
                   pl.BlockSpec((None, MLA_HEADS // 2, tm // TK_MLA, LANES, TK_MLA), lambda b, i: (b, 0, i, 0, 0)),
                   pl.BlockSpec((None, tm, SWA_Q_HEADS * SWA_HEAD), lambda b, i: (b, i, 0)),
                   pl.BlockSpec((None, tm, LANES), lambda b, i: (b, i, 0)),
                   pl.BlockSpec((None, tm, LANES), lambda b, i: (b, i, 0))],
        out_shape=[jax.ShapeDtypeStruct((B, MLA_HEADS, S, LANES), BF16),
                   jax.ShapeDtypeStruct((B, MLA_HEADS, S, LANES), BF16),
                   jax.ShapeDtypeStruct((B, MLA_HEADS // 2, S // TK_MLA, LANES, TK_MLA), BF16),
                   jax.ShapeDtypeStruct((B, S, SWA_Q_HEADS * SWA_HEAD), BF16),
                   jax.ShapeDtypeStruct((B, S, LANES), BF16),
                   jax.ShapeDtypeStruct((B, S, LANES), BF16)],
        compiler_params=pltpu.CompilerParams(dimension_semantics=("arbitrary", "arbitrary"),
                                             vmem_limit_bytes=VMEM_LIMIT),
        name="proj",
    )(x, sh1, sc1, ln_g, pos3, inv, spread, w1, gq, gkv, wq, wkv)


def _mla_kernel(q_ref, k_ref, vt_ref, o_ref, qt_sc, acc_sc, *st_slots):
    tq, tk = TQ_MLA, TK_MLA
    nc = tq // LANES
    i = pl.program_id(2)
    for hd in range(MLA_GROUP):
        qt_sc[hd] = q_ref[hd].astype(F32).T.astype(BF16)
    acc_sc[...] = jnp.zeros_like(acc_sc)

    per_tile = tq // tk

    ones_rows = (lax.broadcasted_iota(jnp.int32, (MLA_ACC_ROWS - MLA_V, tk), 0) == 0).astype(BF16)
    r_iota = lax.broadcasted_iota(jnp.int32, (tk, LANES), 0)
    c_iota = lax.broadcasted_iota(jnp.int32, (tk, LANES), 1)

    def tile(t, ms_all, diagonal):
        blocks = [(d, hd) for d in range(per_tile) for hd in range(MLA_GROUP)]
        first_cc = [(d * tk) // LANES if diagonal else 0 for d, _ in blocks]

        def scores(n):
            d, hd = blocks[n]
            k = k_ref[hd, pl.ds(pl.multiple_of((t * per_tile + d) * tk, tk), tk), :]
            st_slots[n % len(st_slots)][0, :, first_cc[n] * LANES:] = _dot(k, qt_sc[hd, :, first_cc[n] * LANES:])

        for n in range(min(MLA_LOOKAHEAD, len(blocks))):
            scores(n)
        ms_all = list(ms_all)
        for n, (d, hd) in enumerate(blocks):
            if n + MLA_LOOKAHEAD < len(blocks):
                scores(n + MLA_LOOKAHEAD)
            ms = list(ms_all[hd])
            ps, alphas = [], []
            for cc in range(first_cc[n], nc):
                blk = st_slots[n % len(st_slots)][0, :, cc * LANES:(cc + 1) * LANES]
                if diagonal and cc * LANES < (d + 1) * tk - 1:
                    blk = jnp.where(r_iota + (d * tk - cc * LANES) <= c_iota, blk, -jnp.inf)
                m_new = jnp.maximum(ms[cc], jnp.max(blk, axis=0, keepdims=True))
                alphas.append(jnp.exp2(ms[cc] - m_new))
                ps.append(jnp.exp2(blk - m_new).astype(BF16))
                ms[cc] = m_new
            v_t = vt_ref[hd // 2, t * per_tile + d, (hd % 2) * MLA_V:(hd % 2 + 1) * MLA_V, :]
            pv = _dot(jnp.concatenate([v_t, ones_rows], axis=0), jnp.concatenate(ps, axis=1))
            rows = slice(hd * MLA_ACC_ROWS, (hd + 1) * MLA_ACC_ROWS)
            cols = slice(first_cc[n] * LANES, tq)
            acc_sc[rows, cols] = acc_sc[rows, cols] * jnp.concatenate(alphas, axis=1) + pv
            ms_all[hd] = tuple(ms)
        return tuple(ms_all)

    init = tuple(tuple(jnp.full((1, LANES), -jnp.inf, F32) for _ in range(nc)) for _ in range(MLA_GROUP))
    ms_all = lax.fori_loop(0, i, lambda t, c: tile(t, c, False), init)
    tile(i, ms_all, True)
    outs = []
    for hd in range(MLA_GROUP):
        acc = acc_sc[hd * MLA_ACC_ROWS:(hd + 1) * MLA_ACC_ROWS, :]
        outs.append(acc[:MLA_V] / acc[MLA_V:MLA_V + 1])
    o_ref[...] = jnp.concatenate(outs, axis=0).T.astype(BF16)


def _mla(qm, km, vt):
    B, H, S, _ = qm.shape
    tq, tk, g = TQ_MLA, TK_MLA, MLA_GROUP
    return pl.pallas_call(
        _mla_kernel,
        grid=(B, H // g, S // tq),
        in_specs=[pl.BlockSpec((None, g, tq, LANES), lambda b, p, i: (b, p, i, 0)),
                  pl.BlockSpec((None, g, S, LANES), lambda b, p, i: (b, p, 0, 0)),
                  pl.BlockSpec((None, g // 2, S // tk, LANES, tk), lambda b, p, i: (b, p, 0, 0, 0))],
        out_specs=pl.BlockSpec((None, tq, g * MLA_V), lambda b, p, i: (b, i, p)),
        out_shape=jax.ShapeDtypeStruct((B, S, H * MLA_V), BF16),
        scratch_shapes=[pltpu.VMEM((g, LANES, tq), BF16), pltpu.VMEM((g * MLA_ACC_ROWS, tq), F32),
                        ] + [pltpu.VMEM((1, tk, tq), F32)] * (MLA_LOOKAHEAD + 1),
        compiler_params=pltpu.CompilerParams(dimension_semantics=("arbitrary",) * 3,
                                             vmem_limit_bytes=VMEM_LIMIT),
        name="mla",
    )(qm, km, vt)


def _swa_kernel(sink_ref, q_ref, kp_ref, kc_ref, vp_ref, vc_ref, bias0_ref, bias_ref, o_ref, *st_slots):
    band = 2 * WINDOW
    pairs = SWA_GROUP // 2
    kband = jnp.concatenate([kp_ref[...], kc_ref[...]], axis=0).astype(F32)
    vband_t = jnp.concatenate([vp_ref[...], vc_ref[...]], axis=0).astype(F32).T
    lo = lax.broadcasted_iota(jnp.int32, kband.shape, 1) < SWA_HEAD
    kswap = pltpu.roll(kband, SWA_HEAD, 1)
    k_even = [jnp.where(lo, kband, 0.0).astype(BF16), jnp.where(lo, kswap, 0.0).astype(BF16)]
    k_odd = [jnp.where(lo, 0.0, kswap).astype(BF16), jnp.where(lo, 0.0, kband).astype(BF16)]
    ones_rows = (lax.broadcasted_iota(jnp.int32, (16, band), 0) == 0).astype(BF16)
    chains = [(j, kvh) for j in range(SWA_STEP_BLOCKS) for kvh in range(SWA_KV_HEADS)]

    def scores(n):
        j, kvh = chains[n]
        keys = slice(j * WINDOW, j * WINDOW + band)
        kcat = jnp.concatenate([k_even[kvh][keys], k_odd[kvh][keys]], axis=0)
        qstack = jnp.concatenate([q_ref[j * WINDOW:(j + 1) * WINDOW, (kvh * pairs + jj) * LANES:
                                        (kvh * pairs + jj + 1) * LANES] for jj in range(pairs)], axis=0)
        bias = bias0_ref if j == 0 else bias_ref
        st_slots[n % len(st_slots)][0] = _dot_nt(kcat, qstack) + bias[kvh]

    for n in range(min(SWA_LOOKAHEAD, len(chains))):
        scores(n)
    for n, (j, kvh) in enumerate(chains):
        if n + SWA_LOOKAHEAD < len(chains):
            scores(n + SWA_LOOKAHEAD)
        st = st_slots[n % len(st_slots)].at[jnp.minimum(pl.program_id(1), 0)]
        v_t = vband_t[kvh * SWA_HEAD:(kvh + 1) * SWA_HEAD, j * WINDOW:j * WINDOW + band].astype(BF16)
        v_aug = jnp.concatenate([v_t, ones_rows], axis=0)
        halves = []
        for parity in range(2):
            ps, sink_terms = [], []
            for jj in range(pairs):
                blk = st[parity * band:(parity + 1) * band, jj * WINDOW:(jj + 1) * WINDOW]
                sink = sink_ref[kvh * SWA_GROUP + 2 * jj + parity] * LOG2E
                m = jnp.maximum(jnp.max(blk, axis=0, keepdims=True), sink)
                ps.append(jnp.exp2(blk - m).astype(BF16))
                sink_terms.append(jnp.exp2(sink - m))
            pv = _dot(v_aug, jnp.concatenate(ps, axis=1))
            halves.append(pv[:SWA_HEAD] / (pv[SWA_HEAD:SWA_HEAD + 1] + jnp.concatenate(sink_terms, axis=1)))
        o_t = jnp.concatenate(halves, axis=0)
        for jj in range(pairs):
            pair = kvh * pairs + jj
            o_ref[j * WINDOW:(j + 1) * WINDOW, pair * LANES:(pair + 1) * LANES] = (
                o_t[:, jj * WINDOW:(jj + 1) * WINDOW].T.astype(BF16))


def _swa(sinks, qs, ks, vs, bias_tbl):
    B, S, W = qs.shape
    nb = SWA_STEP_BLOCKS
    prev = lambda b, n: (b, jnp.maximum(nb * n - 1, 0), 0)
    cur = lambda b, n: (b, n, 0)
    tbl_block = (None,) + bias_tbl.shape[1:]
    return pl.pallas_call(
        _swa_kernel,
        grid=(B, S // (nb * WINDOW)),
        in_specs=[pl.BlockSpec(memory_space=pltpu.SMEM),
                  pl.BlockSpec((None, nb * WINDOW, W), cur),
                  pl.BlockSpec((None, WINDOW, LANES), prev), pl.BlockSpec((None, nb * WINDOW, LANES), cur),
                  pl.BlockSpec((None, WINDOW, LANES), prev), pl.BlockSpec((None, nb * WINDOW, LANES), cur),
                  pl.BlockSpec(tbl_block, lambda b, n: (jnp.minimum(n, 1), 0, 0, 0)),
                  pl.BlockSpec(tbl_block, lambda b, n: (1, 0, 0, 0))],
        out_specs=pl.BlockSpec((None, nb * WINDOW, W), cur),
        out_shape=jax.ShapeDtypeStruct((B, S, W), BF16),
        scratch_shapes=[pltpu.VMEM((1,) + bias_tbl.shape[2:], F32)] * (SWA_LOOKAHEAD + 1),
        compiler_params=pltpu.CompilerParams(dimension_semantics=("arbitrary", "arbitrary"),
                                             vmem_limit_bytes=VMEM_LIMIT),
        name="swa",
    )(sinks, qs, ks, ks, vs, vs, bias_tbl, bias_tbl)


def _tail_kernel(x_ref, ym_ref, ys_ref, sh1_ref, sc1_ref, ga1_ref, sh2_ref, sc2_ref, ga2_ref,
                 gmix_ref, gmlp_ref, gfin_ref, bg_ref, wg_ref, wom_ref, wos_ref, wo_ref, w1_ref, w2_ref, o_ref):
    subs = [slice(k * TAIL_SUB, (k + 1) * TAIL_SUB) for k in range(TM_TAIL // TAIL_SUB)]
    n_ff = D_FF // FF_CHUNK
    xs = [x_ref[r, :] for r in subs]
    branches = [(_dot(ym_ref[r, :], wom_ref[...]), _dot(ys_ref[r, :], wos_ref[...])) for r in subs]
    hs = [(_rms(x, gmix_ref[...]) * (1 + sc1_ref[...]) + sh1_ref[...]).astype(BF16) for x in xs]
    logits = [_dot_nt(h, wg_ref[...]) for h in hs]
    x1s = []
    for x, (a, b), gl in zip(xs, branches, logits):
        gates = jax.nn.sigmoid(gl + bg_ref[...])
        merged = gates[:, :D_MODEL] * a + gates[:, D_MODEL:] * b
        x1s.append(x + ga1_ref[...] * _dot(merged.astype(BF16), wo_ref[...]))
    h2s = [(_rms(x1, gmlp_ref[...]) * (1 + sc2_ref[...]) + sh2_ref[...]).astype(BF16) for x1 in x1s]

    jobs = [(k, c) for k in range(len(subs)) for c in range(n_ff)]
    up = lambda k, c: _dot(h2s[k], w1_ref[:, c * FF_CHUNK:(c + 1) * FF_CHUNK])
    pending = {0: up(*jobs[0])}
    accs = [None] * len(subs)
    for n, (k, c) in enumerate(jobs):
        if n + 1 < len(jobs):
            pending[n + 1] = up(*jobs[n + 1])
        u = jnp.square(jnp.maximum(pending.pop(n), 0.0)).astype(BF16)
        down = _dot(u, w2_ref[c * FF_CHUNK:(c + 1) * FF_CHUNK, :])
        accs[k] = down if accs[k] is None else accs[k] + down
        if c == n_ff - 1:
            o_ref[subs[k], :] = _rms(x1s[k] + ga2_ref[...] * accs[k], gfin_ref[...])


def _tail(x, ym, ys, mods, gmix, gmlp, gfin, bg, wg, wom, wos, wo, w1, w2):
    B, S, D = x.shape
    tm = TM_TAIL
    tok = lambda width: pl.BlockSpec((None, tm, width), lambda b, i: (b, i, 0))
    per_b = pl.BlockSpec((None, 1, D), lambda b, i: (b, 0, 0))
    const = lambda a: pl.BlockSpec(a.shape, lambda b, i: (0,) * a.ndim, pipeline_mode=pl.Buffered(1))
    return pl.pallas_call(
        _tail_kernel,
        grid=(B, S // tm),
        in_specs=[tok(D), tok(ym.shape[-1]), tok(ys.shape[-1])] + [per_b] * 6
                 + [const(a) for a in (gmix, gmlp, gfin, bg, wg, wom, wos, wo, w1, w2)],
        out_specs=tok(D),
        out_shape=jax.ShapeDtypeStruct((B, S, D), F32),
        compiler_params=pltpu.CompilerParams(dimension_semantics=("arbitrary", "arbitrary"),
                                             vmem_limit_bytes=VMEM_LIMIT),
        name="tail",
    )(x, ym, ys, *mods, gmix, gmlp, gfin, bg, wg, wom, wos, wo, w1, w2)


def _head_slab(nope, a, b):
    r, h, _ = nope.shape
    pad = jnp.zeros((r, h, LANES - MLA_NOPE - MLA_ROPE - HALF_ROPE), nope.dtype)
    return jnp.concatenate([nope, a, b, a, pad], axis=-1).reshape(r, h * LANES)


def _rope_spread():
    m = np.zeros((LANES, 2 * LANES), np.float32)
    for term in range(3):
        for j in range(HALF_ROPE):
            m[term * MLA_ROPE + j, [ROPE_A + j, ROPE_B + j]] = 1.0
            m[term * MLA_ROPE + HALF_ROPE + j, LANES + ROPE_A + j] = -1.0
            m[term * MLA_ROPE + HALF_ROPE + j, LANES + ROPE_B + j] = 1.0
    return jnp.asarray(m, BF16)


def kernel(x, c, positions, rel_bias, ada_w, ada_b, ln_mix_g, w_in, b_gate, mla_q_norm_g, mla_kv_norm_g,
           w_uq, w_ukv, swa_sinks, w_o_mla, w_o_swa, w_o, ln_mlp_g, w_ff1, w_ff2, ln_final_g):
    B, S, D = x.shape
    assert (B, S, D) == (x.shape[0], 4096, D_MODEL) and ada_w.shape[0] == 1
    l = 0

    c_pad = jnp.pad(c, ((0, 8 - B), (0, 0)))
    mod = _adaln(c_pad, ada_w, ada_b)[:B]
    mods = [m[:, None, :] for m in jnp.split(mod, 6, axis=-1)]

    wi = w_in[l].T
    o_kv = MLA_Q_RANK
    o_kr = o_kv + MLA_KV_RANK
    o_qs = o_kr + MLA_ROPE
    o_ks = o_qs + SWA_Q_HEADS * SWA_HEAD
    o_vs = o_ks + SWA_KV_HEADS * SWA_HEAD
    o_ga = o_vs + SWA_KV_HEADS * SWA_HEAD
    kr_a, kr_b = wi[o_kr:o_kr + HALF_ROPE], wi[o_kr + HALF_ROPE:o_qs]
    z = lambda n: jnp.zeros((n, D), wi.dtype)
    w1 = jnp.concatenate([
        wi[:o_kr],
        z(ROPE_A), kr_a, kr_b, kr_a, z(LANES - ROPE_A - MLA_ROPE - HALF_ROPE),
        wi[o_qs:o_ga]], axis=0).astype(BF16)
    assert w1.shape[0] == N_PROJ
    wg = wi[o_ga:].astype(BF16)

    uq = w_uq[l]
    q_nope, q_a, q_b = uq[..., :MLA_NOPE], uq[..., MLA_NOPE:MLA_NOPE + HALF_ROPE], uq[..., MLA_NOPE + HALF_ROPE:]
    wq = _head_slab(q_nope, q_a, q_b).astype(BF16)
    ukv = w_ukv[l]
    k_nope, v_up = ukv[..., :MLA_NOPE], ukv[..., MLA_NOPE:]
    zr = jnp.zeros(k_nope.shape[:2] + (HALF_ROPE,), ukv.dtype)
    wkv = jnp.concatenate([_head_slab(k_nope, zr, zr),
                           v_up.reshape(MLA_KV_RANK, MLA_HEADS * MLA_V)], axis=1).astype(BF16)

    inv = (ROPE_THETA ** (-jnp.arange(HALF_ROPE, dtype=F32) / HALF_ROPE))[:, None]

    qm, km, vm, qs, ks, vs = _proj(x, mods[0], mods[1], ln_mix_g[l][None, :], positions[:, None, :], inv,
                                   _rope_spread(), w1, mla_q_norm_g[l][None, :], mla_kv_norm_g[l][None, :], wq, wkv)

    y_mla = _mla(qm, km, vm)
    y_swa = _swa(swa_sinks[l], qs, ks, vs, _swa_bias(rel_bias))

    return _tail(x, y_mla, y_swa, mods, ln_mix_g[l][None, :], ln_mlp_g[l][None, :], ln_final_g[None, :],
                 b_gate[l][None, :], wg, w_o_mla[l].astype(BF16), w_o_swa[l].astype(BF16), w_o[l].astype(BF16),
                 w_ff1[l].astype(BF16), w_ff2[l].astype(BF16))
```

```python
import functools
import math

import jax
import jax.numpy as jnp
import numpy as np
from jax import lax
from jax.experimental import pallas as pl
from jax.experimental.pallas import tpu as pltpu

F32 = jnp.float32
BF16 = jnp.bfloat16

D_MODEL = 1024
MLA_HEADS = 8
MLA_Q_RANK = 256
MLA_KV_RANK = 128
MLA_NOPE = 64
MLA_ROPE = 32
MLA_V = 64
SWA_Q_HEADS = 16
SWA_KV_HEADS = 2
SWA_HEAD = 64
WINDOW = 128
REL_BUCKETS = 32
REL_MAX_DIST = 128
D_FF = 4 * D_MODEL
ROPE_THETA = 10000.0
EPS = 1e-6

LANES = 128
HALF_ROPE = MLA_ROPE // 2
ROPE_A = MLA_NOPE
ROPE_B = MLA_NOPE + HALF_ROPE
N_PAIRS = SWA_Q_HEADS // 2
SWA_GROUP = SWA_Q_HEADS // SWA_KV_HEADS
SWA_STEP_BLOCKS = 8
SWA_LOOKAHEAD = 2
LOG2E = math.log2(math.e)

C_QLAT = 0
C_KVLAT = C_QLAT + MLA_Q_RANK
C_KR = C_KVLAT + MLA_KV_RANK
C_QS = C_KR + LANES
C_KS = C_QS + SWA_Q_HEADS * SWA_HEAD
C_VS = C_KS + SWA_KV_HEADS * SWA_HEAD
N_PROJ = C_VS + SWA_KV_HEADS * SWA_HEAD

VMEM_LIMIT = 56 * 1024 * 1024

TM_PROJ = 512
PROJ_SUB = 256
TQ_MLA = 1024
TK_MLA = 256
MLA_GROUP = 4
MLA_LOOKAHEAD = 3
MLA_ACC_ROWS = MLA_V + 16
TM_TAIL = 512
TAIL_SUB = 256
FF_CHUNK = 1024


def _rms(x, g):
    return x * lax.rsqrt(jnp.mean(x * x, axis=-1, keepdims=True) + EPS) * g


def _dot(a, b):
    return jnp.dot(a, b, preferred_element_type=F32)


def _dot_nt(a, b):
    return lax.dot_general(a, b, (((1,), (1,)), ((), ())), preferred_element_type=F32)


def _adaln_kernel(c_ref, w_ref, b_ref, o_ref):
    c = c_ref[...]
    act = (c * jax.nn.sigmoid(c)).astype(BF16)
    o_ref[...] = _dot(act, w_ref[...].astype(BF16)) + b_ref[...]


def _adaln(c_pad, ada_w, ada_b):
    rows = c_pad.shape[0]
    n_out = ada_w.shape[2]
    return pl.pallas_call(
        _adaln_kernel,
        grid=(n_out // D_MODEL,),
        in_specs=[pl.BlockSpec((rows, D_MODEL), lambda j: (0, 0)),
                  pl.BlockSpec((None, D_MODEL, D_MODEL), lambda j: (0, 0, j)),
                  pl.BlockSpec((1, D_MODEL), lambda j: (0, j))],
        out_specs=pl.BlockSpec((rows, D_MODEL), lambda j: (0, j)),
        out_shape=jax.ShapeDtypeStruct((rows, n_out), F32),
        compiler_params=pltpu.CompilerParams(dimension_semantics=("arbitrary",)),
        name="adaln",
    )(c_pad, ada_w, ada_b)


def _swa_bias_kernel(rb_ref, o_ref):
    a = lax.broadcasted_iota(jnp.int32, (2 * WINDOW, WINDOW), 1)
    b = lax.broadcasted_iota(jnp.int32, (2 * WINDOW, WINDOW), 0)
    dist = WINDOW + a - b
    n = jnp.maximum(dist, 0)
    max_exact = REL_BUCKETS // 2
    nf = jnp.maximum(n, 1).astype(F32)
    large = max_exact + jnp.floor(jnp.log(nf / max_exact) / math.log(REL_MAX_DIST / max_exact)
                                  * (REL_BUCKETS - max_exact)).astype(jnp.int32)
    large = jnp.minimum(large, REL_BUCKETS - 1)
    bucket = jnp.where(n < max_exact, n, large)
    band_ok = (dist >= 0) & (dist < WINDOW)
    has_prev = b >= WINDOW
    neg = jnp.float32(-jnp.inf)
    for head in range(SWA_Q_HEADS):
        bias = jnp.zeros((2 * WINDOW, WINDOW), F32)
        for k in range(REL_BUCKETS):
            bias = jnp.where(bucket == k, rb_ref[head, k], bias)
        kvh, pair, parity = head // SWA_GROUP, (head % SWA_GROUP) // 2, head % 2
        rows = slice(parity * 2 * WINDOW, (parity + 1) * 2 * WINDOW)
        cols = slice(pair * WINDOW, (pair + 1) * WINDOW)
        o_ref[0, kvh, rows, cols] = jnp.where(band_ok & has_prev, bias * LOG2E, neg)
        o_ref[1, kvh, rows, cols] = jnp.where(band_ok, bias * LOG2E, neg)


def _swa_bias(rel_bias):
    return pl.pallas_call(
        _swa_bias_kernel,
        in_specs=[pl.BlockSpec(memory_space=pltpu.SMEM)],
        out_specs=pl.BlockSpec(memory_space=pltpu.VMEM),
        out_shape=jax.ShapeDtypeStruct((2, SWA_KV_HEADS, 4 * WINDOW, SWA_GROUP // 2 * WINDOW), F32),
        name="swa_bias",
    )(rel_bias)


def _proj_kernel(x_ref, sh_ref, sc_ref, g_ref, pos_ref, inv_ref, spread_ref, w1_ref, gq_ref, gkv_ref, wq_ref,
                 wkv_ref, qm_ref, km_ref, vm_ref, qs_ref, ks_ref, vs_ref):
    subs = [slice(k * PROJ_SUB, (k + 1) * PROJ_SUB) for k in range(TM_PROJ // PROJ_SUB)]
    ps = []
    for r in subs:
        h = _rms(x_ref[r, :], g_ref[...]) * (1 + sc_ref[...]) + sh_ref[...]
        ps.append(_dot_nt(h.astype(BF16), w1_ref[...]))

    ang = inv_ref[...] * pos_ref[...].astype(F32)
    cs = jnp.concatenate([jnp.cos(ang), jnp.sin(ang)], axis=0)
    hi = cs.astype(BF16).astype(F32)
    mid = (cs - hi).astype(BF16).astype(F32)
    lo = (cs - hi - mid).astype(BF16).astype(F32)
    terms = jnp.concatenate([hi, mid, lo, jnp.zeros_like(hi)], axis=0)
    tables = _dot(terms.T.astype(BF16), spread_ref[...])
    lane_t = lax.broadcasted_iota(jnp.int32, (PROJ_SUB, LANES), 1)
    scale = (MLA_NOPE + MLA_ROPE) ** -0.5 * LOG2E
    half = MLA_HEADS * LANES

    ups = []
    for p in ps:
        qn = _rms(p[:, C_QLAT:C_QLAT + MLA_Q_RANK], gq_ref[...]).astype(BF16)
        kvn = _rms(p[:, C_KVLAT:C_KVLAT + MLA_KV_RANK], gkv_ref[...]).astype(BF16)
        ups.append((_dot(qn, wq_ref[...]), _dot(kvn, wkv_ref[...])))

    for k, (r, p, (qq, kk)) in enumerate(zip(subs, ps, ups)):
        cos_f, sin_f = tables[r, :LANES], tables[r, LANES:]
        cos_q = jnp.where(lane_t < MLA_NOPE, 1.0, cos_f)

        def rope(t, cos):
            return t * cos + pltpu.roll(t, LANES - HALF_ROPE, 1) * sin_f

        for hd in range(MLA_HEADS):
            qm_ref[hd, r, :] = (rope(qq[:, hd * LANES:(hd + 1) * LANES], cos_q) * scale).astype(BF16)
        k_pe = rope(p[:, C_KR:C_KR + LANES], cos_f)
        for hd in range(MLA_HEADS):
            km_ref[hd, r, :] = (kk[:, hd * LANES:(hd + 1) * LANES] + k_pe).astype(BF16)
        for pr in range(MLA_HEADS // 2):
            v_t = kk[:, half + pr * LANES:half + (pr + 1) * LANES].T.astype(BF16)
            for t in range(PROJ_SUB // TK_MLA):
                vm_ref[pr, k * (PROJ_SUB // TK_MLA) + t] = v_t[:, t * TK_MLA:(t + 1) * TK_MLA]
        qs_ref[r, :] = (p[:, C_QS:C_KS] * (SWA_HEAD ** -0.5 * LOG2E)).astype(BF16)
        ks_ref[r, :] = p[:, C_KS:C_VS].astype(BF16)
        vs_ref[r, :] = p[:, C_VS:N_PROJ].astype(BF16)


def _proj(x, sh1, sc1, ln_g, pos3, inv, spread, w1, gq, gkv, wq, wkv):
    B, S, D = x.shape
    tm = TM_PROJ
    const = lambda shape: pl.BlockSpec(shape, lambda b, i: (0,) * len(shape))
    return pl.pallas_call(
        _proj_kernel,
        grid=(B, S // tm),
        in_specs=[pl.BlockSpec((None, tm, D), lambda b, i: (b, i, 0)),
                  pl.BlockSpec((None, 1, D), lambda b, i: (b, 0, 0)),
                  pl.BlockSpec((None, 1, D), lambda b, i: (b, 0, 0)),
                  const((1, D)),
                  pl.BlockSpec((None, 1, tm), lambda b, i: (b, 0, i)),
                  const((HALF_ROPE, 1)), const(spread.shape),
                  const(w1.shape), const(gq.shape), const(gkv.shape), const(wq.shape), const(wkv.shape)],
        out_specs=[pl.BlockSpec((None, MLA_HEADS, tm, LANES), lambda b, i: (b, 0, i, 0)),
                   pl.BlockSpec((None, MLA_HEADS, tm, LANES), lambda b, i: (b, 0, i, 0)),
                   pl.BlockSpec((None, MLA_HEADS // 2, tm // TK_MLA, LANES, TK_MLA), lambda b, i: (b, 0, i, 0, 0)),
                   pl.BlockSpec((None, tm, SWA_Q_HEADS * SWA_HEAD), lambda b, i: (b, i, 0)),
                   pl.BlockSpec((None, tm, LANES), lambda b, i: (b, i, 0)),
                   pl.BlockSpec((None, tm, LANES), lambda b, i: (b, i, 0))],
        out_shape=[jax.ShapeDtypeStruct((B, MLA_HEADS, S, LANES), BF16),
                   jax.ShapeDtypeStruct((B, MLA_HEADS, S, LANES), BF16),
                   jax.ShapeDtypeStruct((B, MLA_HEADS // 2, S // TK_MLA, LANES, TK_MLA), BF16),
                   jax.ShapeDtypeStruct((B, S, SWA_Q_HEADS * SWA_HEAD), BF16),
                   jax.ShapeDtypeStruct((B, S, LANES), BF16),
                   jax.ShapeDtypeStruct((B, S, LANES), BF16)],
        compiler_params=pltpu.CompilerParams(dimension_semantics=("arbitrary", "arbitrary"),
                                             vmem_limit_bytes=VMEM_LIMIT),
        name="proj",
    )(x, sh1, sc1, ln_g, pos3, inv, spread, w1, gq, gkv, wq, wkv)


def _mla_kernel(q_ref, k_ref, vt_ref, o_ref, qt_sc, acc_sc, *st_slots):
    tq, tk = TQ_MLA, TK_MLA
    nc = tq // LANES
    i = pl.program_id(2)
    for hd in range(MLA_GROUP):
        qt_sc[hd] = q_ref[hd].astype(F32).T.astype(BF16)
    acc_sc[...] = jnp.zeros_like(acc_sc)

    per_tile = tq // tk

    ones_rows = (lax.broadcasted_iota(jnp.int32, (MLA_ACC_ROWS - MLA_V, tk), 0) == 0).astype(BF16)
    r_iota = lax.broadcasted_iota(jnp.int32, (tk, LANES), 0)
    c_iota = lax.broadcasted_iota(jnp.int32, (tk, LANES), 1)

    blocks = [(d, hd) for d in range(per_tile) for hd in range(MLA_GROUP)]
    assert len(blocks) % len(st_slots) == 0 and MLA_LOOKAHEAD <= MLA_GROUP

    def scores(t, n, first_col):
        d, hd = blocks[n]
        k = k_ref[hd, pl.ds(pl.multiple_of((t * per_tile + d) * tk, tk), tk), :]
        st_slots[n % len(st_slots)][0, :, first_col:] = _dot(k, qt_sc[hd, :, first_col:])

    def tile(t, ms_all, diagonal):
        first_cc = [(d * tk) // LANES if diagonal else 0 for d, _ in blocks]
        ms_all = list(ms_all)
        for n, (d, hd) in enumerate(blocks):
            ahead = n + MLA_LOOKAHEAD
            if ahead < len(blocks):
                scores(t, ahead, first_cc[ahead] * LANES)
            elif not diagonal:
                scores(t + 1, ahead - len(blocks), 0)
            ms = list(ms_all[hd])
            ps, alphas = [], []
            for cc in range(first_cc[n], nc):
                blk = st_slots[n % len(st_slots)][0, :, cc * LANES:(cc + 1) * LANES]
                if diagonal and cc * LANES < (d + 1) * tk - 1:
                    blk = jnp.where(r_iota + (d * tk - cc * LANES) <= c_iota, blk, -jnp.inf)
                m_new = jnp.maximum(ms[cc], jnp.max(blk, axis=0, keepdims=True))
                alphas.append(jnp.exp2(ms[cc] - m_new))
                ps.append(jnp.exp2(blk - m_new).astype(BF16))
                ms[cc] = m_new
            v_t = vt_ref[hd // 2, t * per_tile + d, (hd % 2) * MLA_V:(hd % 2 + 1) * MLA_V, :]
            pv = _dot(jnp.concatenate([v_t, ones_rows], axis=0), jnp.concatenate(ps, axis=1))
            rows = slice(hd * MLA_ACC_ROWS, (hd + 1) * MLA_ACC_ROWS)
            cols = slice(first_cc[n] * LANES, tq)
            acc_sc[rows, cols] = acc_sc[rows, cols] * jnp.concatenate(alphas, axis=1) + pv
            ms_all[hd] = tuple(ms)
        return tuple(ms_all)

    init = tuple(tuple(jnp.full((1, LANES), -jnp.inf, F32) for _ in range(nc)) for _ in range(MLA_GROUP))
    for n in range(MLA_LOOKAHEAD):
        scores(0, n, 0)
    ms_all = lax.fori_loop(0, i, lambda t, c: tile(t, c, False), init)
    tile(i, ms_all, True)
    outs = []
    for hd in range(MLA_GROUP):
        acc = acc_sc[hd * MLA_ACC_ROWS:(hd + 1) * MLA_ACC_ROWS, :]
        outs.append(acc[:MLA_V] / acc[MLA_V:MLA_V + 1])
    o_ref[...] = jnp.concatenate(outs, axis=0).T.astype(BF16)


def _mla(qm, km, vt):
    B, H, S, _ = qm.shape
    tq, tk, g = TQ_MLA, TK_MLA, MLA_GROUP
    return pl.pallas_call(
        _mla_kernel,
        grid=(B, H // g, S // tq),
        in_specs=[pl.BlockSpec((None, g, tq, LANES), lambda b, p, i: (b, p, i, 0)),
                  pl.BlockSpec((None, g, S, LANES), lambda b, p, i: (b, p, 0, 0)),
                  pl.BlockSpec((None, g // 2, S // tk, LANES, tk), lambda b, p, i: (b, p, 0, 0, 0))],
        out_specs=pl.BlockSpec((None, tq, g * MLA_V), lambda b, p, i: (b, i, p)),
        out_shape=jax.ShapeDtypeStruct((B, S, H * MLA_V), BF16),
        scratch_shapes=[pltpu.VMEM((g, LANES, tq), BF16), pltpu.VMEM((g * MLA_ACC_ROWS, tq), F32),
                        ] + [pltpu.VMEM((1, tk, tq), F32)] * (MLA_LOOKAHEAD + 1),
        compiler_params=pltpu.CompilerParams(dimension_semantics=("arbitrary",) * 3,
                                             vmem_limit_bytes=VMEM_LIMIT),
        name="mla",
    )(qm, km, vt)


def _swa_kernel(sink_ref, q_ref, kp_ref, kc_ref, vp_ref, vc_ref, bias0_ref, bias_ref, o_ref, *st_slots):
    band = 2 * WINDOW
    pairs = SWA_GROUP // 2
    kband = jnp.concatenate([kp_ref[...], kc_ref[...]], axis=0).astype(F32)
    vband_t = jnp.concatenate([vp_ref[...], vc_ref[...]], axis=0).astype(F32).T
    lo = lax.broadcasted_iota(jnp.int32, kband.shape, 1) < SWA_HEAD
    kswap = pltpu.roll(kband, SWA_HEAD, 1)
    k_even = [jnp.where(lo, kband, 0.0).astype(BF16), jnp.where(lo, kswap, 0.0).astype(BF16)]
    k_odd = [jnp.where(lo, 0.0, kswap).astype(BF16), jnp.where(lo, 0.0, kband).astype(BF16)]
    ones_rows = (lax.broadcasted_iota(jnp.int32, (16, band), 0) == 0).astype(BF16)
    chains = [(j, kvh) for j in range(SWA_STEP_BLOCKS) for kvh in range(SWA_KV_HEADS)]

    def scores(n):
        j, kvh = chains[n]
        keys = slice(j * WINDOW, j * WINDOW + band)
        kcat = jnp.concatenate([k_even[kvh][keys], k_odd[kvh][keys]], axis=0)
        qstack = jnp.concatenate([q_ref[j * WINDOW:(j + 1) * WINDOW, (kvh * pairs + jj) * LANES:
                                        (kvh * pairs + jj + 1) * LANES] for jj in range(pairs)], axis=0)
        bias = bias0_ref if j == 0 else bias_ref
        st_slots[n % len(st_slots)][0] = _dot_nt(kcat, qstack) + bias[kvh]

    for n in range(min(SWA_LOOKAHEAD, len(chains))):
        scores(n)
    for n, (j, kvh) in enumerate(chains):
        if n + SWA_LOOKAHEAD < len(chains):
            scores(n + SWA_LOOKAHEAD)
        st = st_slots[n % len(st_slots)].at[jnp.minimum(pl.program_id(1), 0)]
        v_t = vband_t[kvh * SWA_HEAD:(kvh + 1) * SWA_HEAD, j * WINDOW:j * WINDOW + band].astype(BF16)
        v_aug = jnp.concatenate([v_t, ones_rows], axis=0)
        halves = []
        for parity in range(2):
            ps, sink_terms = [], []
            for jj in range(pairs):
                blk = st[parity * band:(parity + 1) * band, jj * WINDOW:(jj + 1) * WINDOW]
                sink = sink_ref[kvh * SWA_GROUP + 2 * jj + parity] * LOG2E
                m = jnp.maximum(jnp.max(blk, axis=0, keepdims=True), sink)
                ps.append(jnp.exp2(blk - m).astype(BF16))
                sink_terms.append(jnp.exp2(sink - m))
            pv = _dot(v_aug, jnp.concatenate(ps, axis=1))
            halves.append(pv[:SWA_HEAD] / (pv[SWA_HEAD:SWA_HEAD + 1] + jnp.concatenate(sink_terms, axis=1)))
        o_t = jnp.concatenate(halves, axis=0)
        for jj in range(pairs):
            pair = kvh * pairs + jj
            o_ref[j * WINDOW:(j + 1) * WINDOW, pair * LANES:(pair + 1) * LANES] = (
                o_t[:, jj * WINDOW:(jj + 1) * WINDOW].T.astype(BF16))


def _swa(sinks, qs, ks, vs, bias_tbl):
    B, S, W = qs.shape
    nb = SWA_STEP_BLOCKS
    prev = lambda b, n: (b, jnp.maximum(nb * n - 1, 0), 0)
    cur = lambda b, n: (b, n, 0)
    tbl_block = (None,) + bias_tbl.shape[1:]
    return pl.pallas_call(
        _swa_kernel,
        grid=(B, S // (nb * WINDOW)),
        in_specs=[pl.BlockSpec(memory_space=pltpu.SMEM),
                  pl.BlockSpec((None, nb * WINDOW, W), cur),
                  pl.BlockSpec((None, WINDOW, LANES), prev), pl.BlockSpec((None, nb * WINDOW, LANES), cur),
                  pl.BlockSpec((None, WINDOW, LANES), prev), pl.BlockSpec((None, nb * WINDOW, LANES), cur),
                  pl.BlockSpec(tbl_block, lambda b, n: (jnp.minimum(n, 1), 0, 0, 0)),
                  pl.BlockSpec(tbl_block, lambda b, n: (1, 0, 0, 0))],
        out_specs=pl.BlockSpec((None, nb * WINDOW, W), cur),
        out_shape=jax.ShapeDtypeStruct((B, S, W), BF16),
        scratch_shapes=[pltpu.VMEM((1,) + bias_tbl.shape[2:], F32)] * (SWA_LOOKAHEAD + 1),
        compiler_params=pltpu.CompilerParams(dimension_semantics=("arbitrary", "arbitrary"),
                                             vmem_limit_bytes=VMEM_LIMIT),
        name="swa",
    )(sinks, qs, ks, ks, vs, vs, bias_tbl, bias_tbl)


def _tail_kernel(x_ref, ym_ref, ys_ref, sh1_ref, sc1_ref, ga1_ref, sh2_ref, sc2_ref, ga2_ref,
                 gmix_ref, gmlp_ref, gfin_ref, bg_ref, wg_ref, wom_ref, wos_ref, wo_ref, w1_ref, w2_ref, o_ref):
    subs = [slice(k * TAIL_SUB, (k + 1) * TAIL_SUB) for k in range(TM_TAIL // TAIL_SUB)]
    n_ff = D_FF // FF_CHUNK
    xs = [x_ref[r, :] for r in subs]
    branches = [(_dot(ym_ref[r, :], wom_ref[...]), _dot(ys_ref[r, :], wos_ref[...])) for r in subs]
    hs = [(_rms(x, gmix_ref[...]) * (1 + sc1_ref[...]) + sh1_ref[...]).astype(BF16) for x in xs]
    logits = [_dot_nt(h, wg_ref[...]) for h in hs]
    x1s = []
    for x, (a, b), gl in zip(xs, branches, logits):
        gates = jax.nn.sigmoid(gl + bg_ref[...])
        merged = gates[:, :D_MODEL] * a + gates[:, D_MODEL:] * b
        x1s.append(x + ga1_ref[...] * _dot(merged.astype(BF16), wo_ref[...]))
    h2s = [(_rms(x1, gmlp_ref[...]) * (1 + sc2_ref[...]) + sh2_ref[...]).astype(BF16) for x1 in x1s]

    jobs = [(k, c) for k in range(len(subs)) for c in range(n_ff)]
    up = lambda k, c: _dot(h2s[k], w1_ref[:, c * FF_CHUNK:(c + 1) * FF_CHUNK])
    pending = {0: up(*jobs[0])}
    accs = [None] * len(subs)
    for n, (k, c) in enumerate(jobs):
        if n + 1 < len(jobs):
            pending[n + 1] = up(*jobs[n + 1])
        u = jnp.square(jnp.maximum(pending.pop(n), 0.0)).astype(BF16)
        down = _dot(u, w2_ref[c * FF_CHUNK:(c + 1) * FF_CHUNK, :])
        accs[k] = down if accs[k] is None else accs[k] + down
        if c == n_ff - 1:
            o_ref[subs[k], :] = _rms(x1s[k] + ga2_ref[...] * accs[k], gfin_ref[...])


def _tail(x, ym, ys, mods, gmix, gmlp, gfin, bg, wg, wom, wos, wo, w1, w2):
    B, S, D = x.shape
    tm = TM_TAIL
    tok = lambda width: pl.BlockSpec((None, tm, width), lambda b, i: (b, i, 0))
    per_b = pl.BlockSpec((None, 1, D), lambda b, i: (b, 0, 0))
    const = lambda a: pl.BlockSpec(a.shape, lambda b, i: (0,) * a.ndim, pipeline_mode=pl.Buffered(1))
    return pl.pallas_call(
        _tail_kernel,
        grid=(B, S // tm),
        in_specs=[tok(D), tok(ym.shape[-1]), tok(ys.shape[-1])] + [per_b] * 6
                 + [const(a) for a in (gmix, gmlp, gfin, bg, wg, wom, wos, wo, w1, w2)],
        out_specs=tok(D),
        out_shape=jax.ShapeDtypeStruct((B, S, D), F32),
        compiler_params=pltpu.CompilerParams(dimension_semantics=("arbitrary", "arbitrary"),
                                             vmem_limit_bytes=VMEM_LIMIT),
        name="tail",
    )(x, ym, ys, *mods, gmix, gmlp, gfin, bg, wg, wom, wos, wo, w1, w2)


def _head_slab(nope, a, b):
    r, h, _ = nope.shape
    pad = jnp.zeros((r, h, LANES - MLA_NOPE - MLA_ROPE - HALF_ROPE), nope.dtype)
    return jnp.concatenate([nope, a, b, a, pad], axis=-1).reshape(r, h * LANES)


def _rope_spread():
    m = np.zeros((LANES, 2 * LANES), np.float32)
    for term in range(3):
        for j in range(HALF_ROPE):
            m[term * MLA_ROPE + j, [ROPE_A + j, ROPE_B + j]] = 1.0
            m[term * MLA_ROPE + HALF_ROPE + j, LANES + ROPE_A + j] = -1.0
            m[term * MLA_ROPE + HALF_ROPE + j, LANES + ROPE_B + j] = 1.0
    return jnp.asarray(m, BF16)


def kernel(x, c, positions, rel_bias, ada_w, ada_b, ln_mix_g, w_in, b_gate, mla_q_norm_g, mla_kv_norm_g,
           w_uq, w_ukv, swa_sinks, w_o_mla, w_o_swa, w_o, ln_mlp_g, w_ff1, w_ff2, ln_final_g):
    B, S, D = x.shape
    assert (B, S, D) == (x.shape[0], 4096, D_MODEL) and ada_w.shape[0] == 1
    l = 0

    c_pad = jnp.pad(c, ((0, 8 - B), (0, 0)))
    mod = _adaln(c_pad, ada_w, ada_b)[:B]
    mods = [m[:, None, :] for m in jnp.split(mod, 6, axis=-1)]

    wi = w_in[l].T
    o_kv = MLA_Q_RANK
    o_kr = o_kv + MLA_KV_RANK
    o_qs = o_kr + MLA_ROPE
    o_ks = o_qs + SWA_Q_HEADS * SWA_HEAD
    o_vs = o_ks + SWA_KV_HEADS * SWA_HEAD
    o_ga = o_vs + SWA_KV_HEADS * SWA_HEAD
    kr_a, kr_b = wi[o_kr:o_kr + HALF_ROPE], wi[o_kr + HALF_ROPE:o_qs]
    z = lambda n: jnp.zeros((n, D), wi.dtype)
    w1 = jnp.concatenate([
        wi[:o_kr],
        z(ROPE_A), kr_a, kr_b, kr_a, z(LANES - ROPE_A - MLA_ROPE - HALF_ROPE),
        wi[o_qs:o_ga]], axis=0).astype(BF16)
    assert w1.shape[0] == N_PROJ
    wg = wi[o_ga:].astype(BF16)

    uq = w_uq[l]
    q_nope, q_a, q_b = uq[..., :MLA_NOPE], uq[..., MLA_NOPE:MLA_NOPE + HALF_ROPE], uq[..., MLA_NOPE + HALF_ROPE:]
    wq = _head_slab(q_nope, q_a, q_b).astype(BF16)
    ukv = w_ukv[l]
    k_nope, v_up = ukv[..., :MLA_NOPE], ukv[..., MLA_NOPE:]
    zr = jnp.zeros(k_nope.shape[:2] + (HALF_ROPE,), ukv.dtype)
    wkv = jnp.concatenate([_head_slab(k_nope, zr, zr),
                           v_up.reshape(MLA_KV_RANK, MLA_HEADS * MLA_V)], axis=1).astype(BF16)

    inv = (ROPE_THETA ** (-jnp.arange(HALF_ROPE, dtype=F32) / HALF_ROPE))[:, None]

    qm, km, vm, qs, ks, vs = _proj(x, mods[0], mods[1], ln_mix_g[l][None, :], positions[:, None, :], inv,
                                   _rope_spread(), w1, mla_q_norm_g[l][None, :], mla_kv_norm_g[l][None, :], wq, wkv)

    y_mla = _mla(qm, km, vm)
    y_swa = _swa(swa_sinks[l], qs, ks, vs, _swa_bias(rel_bias))

    return _tail(x, y_mla, y_swa, mods, ln_mix_g[l][None, :], ln_mlp_g[l][None, :], ln_final_g[None, :],
                 b_gate[l][None, :], wg, w_o_mla[l].astype(BF16), w_o_swa[l].astype(BF16), w_o[l].astype(BF16),
                 w_ff1[l].astype(BF16), w_ff2[l].astype(BF16))
```

```python
import functools
import math

import jax
import jax.numpy as jnp
import numpy as np
from jax import lax
from jax.experimental import pallas as pl
from jax.experimental.pallas import tpu as pltpu

F32 = jnp.float32
BF16 = jnp.bfloat16

D_MODEL = 1024
MLA_HEADS = 8
MLA_Q_RANK = 256
MLA_KV_RANK = 128
MLA_NOPE = 64
MLA_ROPE = 32
MLA_V = 64
SWA_Q_HEADS = 16
SWA_KV_HEADS = 2
SWA_HEAD = 64
WINDOW = 128
REL_BUCKETS = 32
REL_MAX_DIST = 128
D_FF = 4 * D_MODEL
ROPE_THETA = 10000.0
EPS = 1e-6

LANES = 128
SCORE_PAD = LANES
HALF_ROPE = MLA_ROPE // 2
ROPE_A = MLA_NOPE
ROPE_B = MLA_NOPE + HALF_ROPE
N_PAIRS = SWA_Q_HEADS // 2
SWA_GROUP = SWA_Q_HEADS // SWA_KV_HEADS
SWA_STEP_BLOCKS = 8
SWA_LOOKAHEAD = 2
LOG2E = math.log2(math.e)

C_QLAT = 0
C_KVLAT = C_QLAT + MLA_Q_RANK
C_KR = C_KVLAT + MLA_KV_RANK
C_QS = C_KR + LANES
C_KS = C_QS + SWA_Q_HEADS * SWA_HEAD
C_VS = C_KS + SWA_KV_HEADS * SWA_HEAD
N_PROJ = C_VS + SWA_KV_HEADS * SWA_HEAD

VMEM_LIMIT = 56 * 1024 * 1024

TM_PROJ = 512
PROJ_SUB = 256
TQ_MLA = 1024
TK_MLA = 256
MLA_GROUP = 4
MLA_LOOKAHEAD = 3
MLA_ACC_ROWS = MLA_V + 16
TM_TAIL = 512
TAIL_SUB = 256
FF_CHUNK = 1024


def _rms(x, g):
    return x * lax.rsqrt(jnp.mean(x * x, axis=-1, keepdims=True) + EPS) * g


def _dot(a, b):
    return jnp.dot(a, b, preferred_element_type=F32)


def _dot_nt(a, b):
    return lax.dot_general(a, b, (((1,), (1,)), ((), ())), preferred_element_type=F32)


def _adaln_kernel(c_ref, w_ref, b_ref, o_ref):
    c = c_ref[...]
    act = (c * jax.nn.sigmoid(c)).astype(BF16)
    o_ref[...] = _dot(act, w_ref[...].astype(BF16)) + b_ref[...]


def _adaln(c_pad, ada_w, ada_b):
    rows = c_pad.shape[0]
    n_out = ada_w.shape[2]
    return pl.pallas_call(
        _adaln_kernel,
        grid=(n_out // D_MODEL,),
        in_specs=[pl.BlockSpec((rows, D_MODEL), lambda j: (0, 0)),
                  pl.BlockSpec((None, D_MODEL, D_MODEL), lambda j: (0, 0, j)),
                  pl.BlockSpec((1, D_MODEL), lambda j: (0, j))],
        out_specs=pl.BlockSpec((rows, D_MODEL), lambda j: (0, j)),
        out_shape=jax.ShapeDtypeStruct((rows, n_out), F32),
        compiler_params=pltpu.CompilerParams(dimension_semantics=("arbitrary",)),
        name="adaln",
    )(c_pad, ada_w, ada_b)


def _swa_bias_kernel(rb_ref, o_ref):
    a = lax.broadcasted_iota(jnp.int32, (2 * WINDOW, WINDOW), 1)
    b = lax.broadcasted_iota(jnp.int32, (2 * WINDOW, WINDOW), 0)
    dist = WINDOW + a - b
    n = jnp.maximum(dist, 0)
    max_exact = REL_BUCKETS // 2
    nf = jnp.maximum(n, 1).astype(F32)
    large = max_exact + jnp.floor(jnp.log(nf / max_exact) / math.log(REL_MAX_DIST / max_exact)
                                  * (REL_BUCKETS - max_exact)).astype(jnp.int32)
    large = jnp.minimum(large, REL_BUCKETS - 1)
    bucket = jnp.where(n < max_exact, n, large)
    band_ok = (dist >= 0) & (dist < WINDOW)
    has_prev = b >= WINDOW
    neg = jnp.float32(-jnp.inf)
    for head in range(SWA_Q_HEADS):
        bias = jnp.zeros((2 * WINDOW, WINDOW), F32)
        for k in range(REL_BUCKETS):
            bias = jnp.where(bucket == k, rb_ref[head, k], bias)
        kvh, pair, parity = head // SWA_GROUP, (head % SWA_GROUP) // 2, head % 2
        rows = slice(parity * 2 * WINDOW, (parity + 1) * 2 * WINDOW)
        cols = slice(pair * WINDOW, (pair + 1) * WINDOW)
        o_ref[0, kvh, rows, cols] = jnp.where(band_ok & has_prev, bias * LOG2E, neg)
        o_ref[1, kvh, rows, cols] = jnp.where(band_ok, bias * LOG2E, neg)


def _swa_bias(rel_bias):
    return pl.pallas_call(
        _swa_bias_kernel,
        in_specs=[pl.BlockSpec(memory_space=pltpu.SMEM)],
        out_specs=pl.BlockSpec(memory_space=pltpu.VMEM),
        out_shape=jax.ShapeDtypeStruct((2, SWA_KV_HEADS, 4 * WINDOW, SWA_GROUP // 2 * WINDOW), F32),
        name="swa_bias",
    )(rel_bias)


def _proj_kernel(x_ref, sh_ref, sc_ref, g_ref, pos_ref, inv_ref, spread_ref, w1_ref, gq_ref, gkv_ref, wq_ref,
                 wkv_ref, qm_ref, km_ref, vm_ref, qs_ref, ks_ref, vs_ref):
    subs = [slice(k * PROJ_SUB, (k + 1) * PROJ_SUB) for k in range(TM_PROJ // PROJ_SUB)]
    ps = []
    for r in subs:
        h = _rms(x_ref[r, :], g_ref[...]) * (1 + sc_ref[...]) + sh_ref[...]
        ps.append(_dot_nt(h.astype(BF16), w1_ref[...]))

    ang = inv_ref[...] * pos_ref[...].astype(F32)
    cs = jnp.concatenate([jnp.cos(ang), jnp.sin(ang)], axis=0)
    hi = cs.astype(BF16).astype(F32)
    mid = (cs - hi).astype(BF16).astype(F32)
    lo = (cs - hi - mid).astype(BF16).astype(F32)
    terms = jnp.concatenate([hi, mid, lo, jnp.zeros_like(hi)], axis=0)
    tables = _dot(terms.T.astype(BF16), spread_ref[...])
    lane_t = lax.broadcasted_iota(jnp.int32, (PROJ_SUB, LANES), 1)
    scale = (MLA_NOPE + MLA_ROPE) ** -0.5 * LOG2E
    half = MLA_HEADS * LANES

    ups = []
    for p in ps:
        qn = _rms(p[:, C_QLAT:C_QLAT + MLA_Q_RANK], gq_ref[...]).astype(BF16)
        kvn = _rms(p[:, C_KVLAT:C_KVLAT + MLA_KV_RANK], gkv_ref[...]).astype(BF16)
        ups.append((_dot(qn, wq_ref[...]), _dot(kvn, wkv_ref[...])))

    for k, (r, p, (qq, kk)) in enumerate(zip(subs, ps, ups)):
        cos_f, sin_f = tables[r, :LANES], tables[r, LANES:]
        cos_q = jnp.where(lane_t < MLA_NOPE, 1.0, cos_f)

        def rope(t, cos):
            return t * cos + pltpu.roll(t, LANES - HALF_ROPE, 1) * sin_f

        for hd in range(MLA_HEADS):
            qm_ref[hd, r, :] = (rope(qq[:, hd * LANES:(hd + 1) * LANES], cos_q) * scale).astype(BF16)
        k_pe = rope(p[:, C_KR:C_KR + LANES], cos_f)
        for hd in range(MLA_HEADS):
            km_ref[hd, r, :] = (kk[:, hd * LANES:(hd + 1) * LANES] + k_pe).astype(BF16)
        for pr in range(MLA_HEADS // 2):
            v_t = kk[:, half + pr * LANES:half + (pr + 1) * LANES].T.astype(BF16)
            for t in range(PROJ_SUB // TK_MLA):
                vm_ref[pr, k * (PROJ_SUB // TK_MLA) + t] = v_t[:, t * TK_MLA:(t + 1) * TK_MLA]
        qs_ref[r, :] = (p[:, C_QS:C_KS] * (SWA_HEAD ** -0.5 * LOG2E)).astype(BF16)
        ks_ref[r, :] = p[:, C_KS:C_VS].astype(BF16)
        vs_ref[r, :] = p[:, C_VS:N_PROJ].astype(BF16)


def _proj(x, sh1, sc1, ln_g, pos3, inv, spread, w1, gq, gkv, wq, wkv):
    B, S, D = x.shape
    tm = TM_PROJ
    const = lambda shape: pl.BlockSpec(shape, lambda b, i: (0,) * len(shape))
    return pl.pallas_call(
        _proj_kernel,
        grid=(B, S // tm),
        in_specs=[pl.BlockSpec((None, tm, D), lambda b, i: (b, i, 0)),
                  pl.BlockSpec((None, 1, D), lambda b, i: (b, 0, 0)),
                  pl.BlockSpec((None, 1, D), lambda b, i: (b, 0, 0)),
                  const((1, D)),
                  pl.BlockSpec((None, 1, tm), lambda b, i: (b, 0, i)),
                  const((HALF_ROPE, 1)), const(spread.shape),
                  const(w1.shape), const(gq.shape), const(gkv.shape), const(wq.shape), const(wkv.shape)],
        out_specs=[pl.BlockSpec((None, MLA_HEADS, tm, LANES), lambda b, i: (b, 0, i, 0)),
                   pl.BlockSpec((None, MLA_HEADS, tm, LANES), lambda b, i: (b, 0, i, 0)),
                   pl.BlockSpec((None, MLA_HEADS // 2, tm // TK_MLA, LANES, TK_MLA), lambda b, i: (b, 0, i, 0, 0)),
                   pl.BlockSpec((None, tm, SWA_Q_HEADS * SWA_HEAD), lambda b, i: (b, i, 0)),
                   pl.BlockSpec((None, tm, LANES), lambda b, i: (b, i, 0)),
                   pl.BlockSpec((None, tm, LANES), lambda b, i: (b, i, 0))],
        out_shape=[jax.ShapeDtypeStruct((B, MLA_HEADS, S, LANES), BF16),
                   jax.ShapeDtypeStruct((B, MLA_HEADS, S, LANES), BF16),
                   jax.ShapeDtypeStruct((B, MLA_HEADS // 2, S // TK_MLA, LANES, TK_MLA), BF16),
                   jax.ShapeDtypeStruct((B, S, SWA_Q_HEADS * SWA_HEAD), BF16),
                   jax.ShapeDtypeStruct((B, S, LANES), BF16),
                   jax.ShapeDtypeStruct((B, S, LANES), BF16)],
        compiler_params=pltpu.CompilerParams(dimension_semantics=("arbitrary", "arbitrary"),
                                             vmem_limit_bytes=VMEM_LIMIT),
        name="proj",
    )(x, sh1, sc1, ln_g, pos3, inv, spread, w1, gq, gkv, wq, wkv)


def _mla_kernel(q_ref, k_ref, vt_ref, o_ref, qt_sc, acc_sc, *st_slots):
    tq, tk = TQ_MLA, TK_MLA
    nc = tq // LANES
    i = pl.program_id(2)
    for hd in range(MLA_GROUP):
        qt_sc[hd] = q_ref[hd].astype(F32).T.astype(BF16)
    acc_sc[...] = jnp.zeros_like(acc_sc)

    per_tile = tq // tk

    ones_rows = (lax.broadcasted_iota(jnp.int32, (MLA_ACC_ROWS - MLA_V, tk), 0) == 0).astype(BF16)
    r_iota = lax.broadcasted_iota(jnp.int32, (tk, LANES), 0)
    c_iota = lax.broadcasted_iota(jnp.int32, (tk, LANES), 1)

    blocks = [(d, hd) for d in range(per_tile) for hd in range(MLA_GROUP)]
    assert len(blocks) % len(st_slots) == 0 and MLA_LOOKAHEAD <= MLA_GROUP

    def scores(t, n, first_col):
        d, hd = blocks[n]
        k = k_ref[hd, pl.ds(pl.multiple_of((t * per_tile + d) * tk, tk), tk), :]
        st_slots[n % len(st_slots)][0, :, first_col:tq] = _dot(k, qt_sc[hd, :, first_col:])

    def tile(t, ms_all, diagonal):
        first_cc = [(d * tk) // LANES if diagonal else 0 for d, _ in blocks]
        ms_all = list(ms_all)
        for n, (d, hd) in enumerate(blocks):
            ahead = n + MLA_LOOKAHEAD
            if ahead < len(blocks):
                scores(t, ahead, first_cc[ahead] * LANES)
            elif not diagonal:
                scores(t + 1, ahead - len(blocks), 0)
            ms = list(ms_all[hd])
            ps, alphas = [], []
            for cc in range(first_cc[n], nc):
                blk = st_slots[n % len(st_slots)][0, :, cc * LANES:(cc + 1) * LANES]
                if diagonal and cc * LANES < (d + 1) * tk - 1:
                    blk = jnp.where(r_iota + (d * tk - cc * LANES) <= c_iota, blk, -jnp.inf)
                m_new = jnp.maximum(ms[cc], jnp.max(blk, axis=0, keepdims=True))
                alphas.append(jnp.exp2(ms[cc] - m_new))
                ps.append(jnp.exp2(blk - m_new).astype(BF16))
                ms[cc] = m_new
            v_t = vt_ref[hd // 2, t * per_tile + d, (hd % 2) * MLA_V:(hd % 2 + 1) * MLA_V, :]
            pv = _dot(jnp.concatenate([v_t, ones_rows], axis=0), jnp.concatenate(ps, axis=1))
            rows = slice(hd * MLA_ACC_ROWS, (hd + 1) * MLA_ACC_ROWS)
            cols = slice(first_cc[n] * LANES, tq)
            acc_sc[rows, cols] = acc_sc[rows, cols] * jnp.concatenate(alphas, axis=1) + pv
            ms_all[hd] = tuple(ms)
        return tuple(ms_all)

    init = tuple(tuple(jnp.full((1, LANES), -jnp.inf, F32) for _ in range(nc)) for _ in range(MLA_GROUP))
    for n in range(MLA_LOOKAHEAD):
        scores(0, n, 0)
    ms_all = lax.fori_loop(0, i, lambda t, c: tile(t, c, False), init)
    tile(i, ms_all, True)
    outs = []
    for hd in range(MLA_GROUP):
        acc = acc_sc[hd * MLA_ACC_ROWS:(hd + 1) * MLA_ACC_ROWS, :]
        outs.append(acc[:MLA_V] / acc[MLA_V:MLA_V + 1])
    o_ref[...] = jnp.concatenate(outs, axis=0).T.astype(BF16)


def _mla(qm, km, vt):
    B, H, S, _ = qm.shape
    tq, tk, g = TQ_MLA, TK_MLA, MLA_GROUP
    return pl.pallas_call(
        _mla_kernel,
        grid=(B, H // g, S // tq),
        in_specs=[pl.BlockSpec((None, g, tq, LANES), lambda b, p, i: (b, p, i, 0)),
                  pl.BlockSpec((None, g, S, LANES), lambda b, p, i: (b, p, 0, 0)),
                  pl.BlockSpec((None, g // 2, S // tk, LANES, tk), lambda b, p, i: (b, p, 0, 0, 0))],
        out_specs=pl.BlockSpec((None, tq, g * MLA_V), lambda b, p, i: (b, i, p)),
        out_shape=jax.ShapeDtypeStruct((B, S, H * MLA_V), BF16),
        scratch_shapes=[pltpu.VMEM((g, LANES, tq), BF16), pltpu.VMEM((g * MLA_ACC_ROWS, tq), F32),
                        ] + [pltpu.VMEM((1, tk, tq + SCORE_PAD), F32)] * (MLA_LOOKAHEAD + 1),
        compiler_params=pltpu.CompilerParams(dimension_semantics=("arbitrary",) * 3,
                                             vmem_limit_bytes=VMEM_LIMIT),
        name="mla",
    )(qm, km, vt)


def _swa_kernel(sink_ref, q_ref, kp_ref, kc_ref, vp_ref, vc_ref, bias0_ref, bias_ref, o_ref, *st_slots):
    band = 2 * WINDOW
    pairs = SWA_GROUP // 2
    kband = jnp.concatenate([kp_ref[...], kc_ref[...]], axis=0).astype(F32)
    vband_t = jnp.concatenate([vp_ref[...], vc_ref[...]], axis=0).astype(F32).T
    lo = lax.broadcasted_iota(jnp.int32, kband.shape, 1) < SWA_HEAD
    kswap = pltpu.roll(kband, SWA_HEAD, 1)
    k_even = [jnp.where(lo, kband, 0.0).astype(BF16), jnp.where(lo, kswap, 0.0).astype(BF16)]
    k_odd = [jnp.where(lo, 0.0, kswap).astype(BF16), jnp.where(lo, 0.0, kband).astype(BF16)]
    ones_rows = (lax.broadcasted_iota(jnp.int32, (16, band), 0) == 0).astype(BF16)
    chains = [(j, kvh) for j in range(SWA_STEP_BLOCKS) for kvh in range(SWA_KV_HEADS)]

    def scores(n):
        j, kvh = chains[n]
        keys = slice(j * WINDOW, j * WINDOW + band)
        kcat = jnp.concatenate([k_even[kvh][keys], k_odd[kvh][keys]], axis=0)
        qstack = jnp.concatenate([q_ref[j * WINDOW:(j + 1) * WINDOW, (kvh * pairs + jj) * LANES:
                                        (kvh * pairs + jj + 1) * LANES] for jj in range(pairs)], axis=0)
        bias = bias0_ref if j == 0 else bias_ref
        st_slots[n % len(st_slots)][0, :, :pairs * WINDOW] = _dot_nt(kcat, qstack) + bias[kvh]

    for n in range(min(SWA_LOOKAHEAD, len(chains))):
        scores(n)
    for n, (j, kvh) in enumerate(chains):
        if n + SWA_LOOKAHEAD < len(chains):
            scores(n + SWA_LOOKAHEAD)
        st = st_slots[n % len(st_slots)].at[jnp.minimum(pl.program_id(1), 0)]
        v_t = vband_t[kvh * SWA_HEAD:(kvh + 1) * SWA_HEAD, j * WINDOW:j * WINDOW + band].astype(BF16)
        v_aug = jnp.concatenate([v_t, ones_rows], axis=0)
        halves = []
        for parity in range(2):
            ps, sink_terms = [], []
            for jj in range(pairs):
                blk = st[parity * band:(parity + 1) * band, jj * WINDOW:(jj + 1) * WINDOW]
                sink = sink_ref[kvh * SWA_GROUP + 2 * jj + parity] * LOG2E
                m = jnp.maximum(jnp.max(blk, axis=0, keepdims=True), sink)
                ps.append(jnp.exp2(blk - m).astype(BF16))
                sink_terms.append(jnp.exp2(sink - m))
            pv = _dot(v_aug, jnp.concatenate(ps, axis=1))
            halves.append(pv[:SWA_HEAD] / (pv[SWA_HEAD:SWA_HEAD + 1] + jnp.concatenate(sink_terms, axis=1)))
        o_t = jnp.concatenate(halves, axis=0)
        for jj in range(pairs):
            pair = kvh * pairs + jj
            o_ref[j * WINDOW:(j + 1) * WINDOW, pair * LANES:(pair + 1) * LANES] = (
                o_t[:, jj * WINDOW:(jj + 1) * WINDOW].T.astype(BF16))


def _swa(sinks, qs, ks, vs, bias_tbl):
    B, S, W = qs.shape
    nb = SWA_STEP_BLOCKS
    prev = lambda b, n: (b, jnp.maximum(nb * n - 1, 0), 0)
    cur = lambda b, n: (b, n, 0)
    tbl_block = (None,) + bias_tbl.shape[1:]
    return pl.pallas_call(
        _swa_kernel,
        grid=(B, S // (nb * WINDOW)),
        in_specs=[pl.BlockSpec(memory_space=pltpu.SMEM),
                  pl.BlockSpec((None, nb * WINDOW, W), cur),
                  pl.BlockSpec((None, WINDOW, LANES), prev), pl.BlockSpec((None, nb * WINDOW, LANES), cur),
                  pl.BlockSpec((None, WINDOW, LANES), prev), pl.BlockSpec((None, nb * WINDOW, LANES), cur),
                  pl.BlockSpec(tbl_block, lambda b, n: (jnp.minimum(n, 1), 0, 0, 0)),
                  pl.BlockSpec(tbl_block, lambda b, n: (1, 0, 0, 0))],
        out_specs=pl.BlockSpec((None, nb * WINDOW, W), cur),
        out_shape=jax.ShapeDtypeStruct((B, S, W), BF16),
        scratch_shapes=[pltpu.VMEM((1, bias_tbl.shape[2], bias_tbl.shape[3] + SCORE_PAD), F32)] * (SWA_LOOKAHEAD + 1),
        compiler_params=pltpu.CompilerParams(dimension_semantics=("arbitrary", "arbitrary"),
                                             vmem_limit_bytes=VMEM_LIMIT),
        name="swa",
    )(sinks, qs, ks, ks, vs, vs, bias_tbl, bias_tbl)


def _tail_kernel(x_ref, ym_ref, ys_ref, sh1_ref, sc1_ref, ga1_ref, sh2_ref, sc2_ref, ga2_ref,
                 gmix_ref, gmlp_ref, gfin_ref, bg_ref, wg_ref, wom_ref, wos_ref, wo_ref, w1_ref, w2_ref, o_ref):
    subs = [slice(k * TAIL_SUB, (k + 1) * TAIL_SUB) for k in range(TM_TAIL // TAIL_SUB)]
    n_ff = D_FF // FF_CHUNK
    xs = [x_ref[r, :] for r in subs]
    branches = [(_dot(ym_ref[r, :], wom_ref[...]), _dot(ys_ref[r, :], wos_ref[...])) for r in subs]
    hs = [(_rms(x, gmix_ref[...]) * (1 + sc1_ref[...]) + sh1_ref[...]).astype(BF16) for x in xs]
    logits = [_dot_nt(h, wg_ref[...]) for h in hs]
    x1s = []
    for x, (a, b), gl in zip(xs, branches, logits):
        gates = jax.nn.sigmoid(gl + bg_ref[...])
        merged = gates[:, :D_MODEL] * a + gates[:, D_MODEL:] * b
        x1s.append(x + ga1_ref[...] * _dot(merged.astype(BF16), wo_ref[...]))
    h2s = [(_rms(x1, gmlp_ref[...]) * (1 + sc2_ref[...]) + sh2_ref[...]).astype(BF16) for x1 in x1s]

    jobs = [(k, c) for k in range(len(subs)) for c in range(n_ff)]
    up = lambda k, c: _dot(h2s[k], w1_ref[:, c * FF_CHUNK:(c + 1) * FF_CHUNK])
    pending = {0: up(*jobs[0])}
    accs = [None] * len(subs)
    for n, (k, c) in enumerate(jobs):
        if n + 1 < len(jobs):
            pending[n + 1] = up(*jobs[n + 1])
        u = jnp.square(jnp.maximum(pending.pop(n), 0.0)).astype(BF16)
        down = _dot(u, w2_ref[c * FF_CHUNK:(c + 1) * FF_CHUNK, :])
        accs[k] = down if accs[k] is None else accs[k] + down
        if c == n_ff - 1:
            o_ref[subs[k], :] = _rms(x1s[k] + ga2_ref[...] * accs[k], gfin_ref[...])


def _tail(x, ym, ys, mods, gmix, gmlp, gfin, bg, wg, wom, wos, wo, w1, w2):
    B, S, D = x.shape
    tm = TM_TAIL
    tok = lambda width: pl.BlockSpec((None, tm, width), lambda b, i: (b, i, 0))
    per_b = pl.BlockSpec((None, 1, D), lambda b, i: (b, 0, 0))
    const = lambda a: pl.BlockSpec(a.shape, lambda b, i: (0,) * a.ndim, pipeline_mode=pl.Buffered(1))
    return pl.pallas_call(
        _tail_kernel,
        grid=(B, S // tm),
        in_specs=[tok(D), tok(ym.shape[-1]), tok(ys.shape[-1])] + [per_b] * 6
                 + [const(a) for a in (gmix, gmlp, gfin, bg, wg, wom, wos, wo, w1, w2)],
        out_specs=tok(D),
        out_shape=jax.ShapeDtypeStruct((B, S, D), F32),
        compiler_params=pltpu.CompilerParams(dimension_semantics=("arbitrary", "arbitrary"),
                                             vmem_limit_bytes=VMEM_LIMIT),
        name="tail",
    )(x, ym, ys, *mods, gmix, gmlp, gfin, bg, wg, wom, wos, wo, w1, w2)


def _head_slab(nope, a, b):
    r, h, _ = nope.shape
    pad = jnp.zeros((r, h, LANES - MLA_NOPE - MLA_ROPE - HALF_ROPE), nope.dtype)
    return jnp.concatenate([nope, a, b, a, pad], axis=-1).reshape(r, h * LANES)


def _rope_spread():
    m = np.zeros((LANES, 2 * LANES), np.float32)
    for term in range(3):
        for j in range(HALF_ROPE):
            m[term * MLA_ROPE + j, [ROPE_A + j, ROPE_B + j]] = 1.0
            m[term * MLA_ROPE + HALF_ROPE + j, LANES + ROPE_A + j] = -1.0
            m[term * MLA_ROPE + HALF_ROPE + j, LANES + ROPE_B + j] = 1.0
    return jnp.asarray(m, BF16)


def kernel(x, c, positions, rel_bias, ada_w, ada_b, ln_mix_g, w_in, b_gate, mla_q_norm_g, mla_kv_norm_g,
           w_uq, w_ukv, swa_sinks, w_o_mla, w_o_swa, w_o, ln_mlp_g, w_ff1, w_ff2, ln_final_g):
    B, S, D = x.shape
    assert (B, S, D) == (x.shape[0], 4096, D_MODEL) and ada_w.shape[0] == 1
    l = 0

    c_pad = jnp.pad(c, ((0, 8 - B), (0, 0)))
    mod = _adaln(c_pad, ada_w, ada_b)[:B]
    mods = [m[:, None, :] for m in jnp.split(mod, 6, axis=-1)]

    wi = w_in[l].T
    o_kv = MLA_Q_RANK
    o_kr = o_kv + MLA_KV_RANK
    o_qs = o_kr + MLA_ROPE
    o_ks = o_qs + SWA_Q_HEADS * SWA_HEAD
    o_vs = o_ks + SWA_KV_HEADS * SWA_HEAD
    o_ga = o_vs + SWA_KV_HEADS * SWA_HEAD
    kr_a, kr_b = wi[o_kr:o_kr + HALF_ROPE], wi[o_kr + HALF_ROPE:o_qs]
    z = lambda n: jnp.zeros((n, D), wi.dtype)
    w1 = jnp.concatenate([
        wi[:o_kr],
        z(ROPE_A), kr_a, kr_b, kr_a, z(LANES - ROPE_A - MLA_ROPE - HALF_ROPE),
        wi[o_qs:o_ga]], axis=0).astype(BF16)
    assert w1.shape[0] == N_PROJ
    wg = wi[o_ga:].astype(BF16)

    uq = w_uq[l]
    q_nope, q_a, q_b = uq[..., :MLA_NOPE], uq[..., MLA_NOPE:MLA_NOPE + HALF_ROPE], uq[..., MLA_NOPE + HALF_ROPE:]
    wq = _head_slab(q_nope, q_a, q_b).astype(BF16)
    ukv = w_ukv[l]
    k_nope, v_up = ukv[..., :MLA_NOPE], ukv[..., MLA_NOPE:]
    zr = jnp.zeros(k_nope.shape[:2] + (HALF_ROPE,), ukv.dtype)
    wkv = jnp.concatenate([_head_slab(k_nope, zr, zr),
                           v_up.reshape(MLA_KV_RANK, MLA_HEADS * MLA_V)], axis=1).astype(BF16)

    inv = (ROPE_THETA ** (-jnp.arange(HALF_ROPE, dtype=F32) / HALF_ROPE))[:, None]

    qm, km, vm, qs, ks, vs = _proj(x, mods[0], mods[1], ln_mix_g[l][None, :], positions[:, None, :], inv,
                                   _rope_spread(), w1, mla_q_norm_g[l][None, :], mla_kv_norm_g[l][None, :], wq, wkv)

    y_mla = _mla(qm, km, vm)
    y_swa = _swa(swa_sinks[l], qs, ks, vs, _swa_bias(rel_bias))

    return _tail(x, y_mla, y_swa, mods, ln_mix_g[l][None, :], ln_mlp_g[l][None, :], ln_final_g[None, :],
                 b_gate[l][None, :], wg, w_o_mla[l].astype(BF16), w_o_swa[l].astype(BF16), w_o[l].astype(BF16),
                 w_ff1[l].astype(BF16), w_ff2[l].astype(BF16))
```

```python
import functools
import math

import jax
import jax.numpy as jnp
import numpy as np
from jax import lax
from jax.experimental import pallas as pl
from jax.experimental.pallas import tpu as pltpu

F32 = jnp.float32
BF16 = jnp.bfloat16

D_MODEL = 1024
MLA_HEADS = 8
MLA_Q_RANK = 256
MLA_KV_RANK = 128
MLA_NOPE = 64
MLA_ROPE = 32
MLA_V = 64
SWA_Q_HEADS = 16
SWA_KV_HEADS = 2
SWA_HEAD = 64
WINDOW = 128
REL_BUCKETS = 32
REL_MAX_DIST = 128
D_FF = 4 * D_MODEL
ROPE_THETA = 10000.0
EPS = 1e-6

LANES = 128
SCORE_PAD = LANES
HALF_ROPE = MLA_ROPE // 2
ROPE_A = MLA_NOPE
ROPE_B = MLA_NOPE + HALF_ROPE
N_PAIRS = SWA_Q_HEADS // 2
SWA_GROUP = SWA_Q_HEADS // SWA_KV_HEADS
SWA_STEP_BLOCKS = 8
SWA_LOOKAHEAD = 2
LOG2E = math.log2(math.e)

C_QLAT = 0
C_KVLAT = C_QLAT + MLA_Q_RANK
C_KR = C_KVLAT + MLA_KV_RANK
C_QS = C_KR + LANES
C_KS = C_QS + SWA_Q_HEADS * SWA_HEAD
C_VS = C_KS + SWA_KV_HEADS * SWA_HEAD
N_PROJ = C_VS + SWA_KV_HEADS * SWA_HEAD

VMEM_LIMIT = 56 * 1024 * 1024

TM_PROJ = 512
PROJ_SUB = 256
TQ_MLA = 1024
TK_MLA = 256
MLA_GROUP = 4
MLA_LOOKAHEAD = 3
MLA_ACC_ROWS = MLA_V + 16
TM_TAIL = 512
TAIL_SUB = 256
FF_CHUNK = 1024


def _rms(x, g):
    return x * lax.rsqrt(jnp.mean(x * x, axis=-1, keepdims=True) + EPS) * g


def _dot(a, b):
    return jnp.dot(a, b, preferred_element_type=F32)


def _dot_nt(a, b):
    return lax.dot_general(a, b, (((1,), (1,)), ((), ())), preferred_element_type=F32)


def _adaln_kernel(c_ref, w_ref, b_ref, o_ref):
    c = c_ref[...]
    act = (c * jax.nn.sigmoid(c)).astype(BF16)
    o_ref[...] = _dot(act, w_ref[...].astype(BF16)) + b_ref[...]


def _adaln(c_pad, ada_w, ada_b):
    rows = c_pad.shape[0]
    n_out = ada_w.shape[2]
    return pl.pallas_call(
        _adaln_kernel,
        grid=(n_out // D_MODEL,),
        in_specs=[pl.BlockSpec((rows, D_MODEL), lambda j: (0, 0)),
                  pl.BlockSpec((None, D_MODEL, D_MODEL), lambda j: (0, 0, j)),
                  pl.BlockSpec((1, D_MODEL), lambda j: (0, j))],
        out_specs=pl.BlockSpec((rows, D_MODEL), lambda j: (0, j)),
        out_shape=jax.ShapeDtypeStruct((rows, n_out), F32),
        compiler_params=pltpu.CompilerParams(dimension_semantics=("arbitrary",)),
        name="adaln",
    )(c_pad, ada_w, ada_b)


def _swa_bias_kernel(rb_ref, o_ref):
    a = lax.broadcasted_iota(jnp.int32, (2 * WINDOW, WINDOW), 1)
    b = lax.broadcasted_iota(jnp.int32, (2 * WINDOW, WINDOW), 0)
    dist = WINDOW + a - b
    n = jnp.maximum(dist, 0)
    max_exact = REL_BUCKETS // 2
    nf = jnp.maximum(n, 1).astype(F32)
    large = max_exact + jnp.floor(jnp.log(nf / max_exact) / math.log(REL_MAX_DIST / max_exact)
                                  * (REL_BUCKETS - max_exact)).astype(jnp.int32)
    large = jnp.minimum(large, REL_BUCKETS - 1)
    bucket = jnp.where(n < max_exact, n, large)
    band_ok = (dist >= 0) & (dist < WINDOW)
    has_prev = b >= WINDOW
    neg = jnp.float32(-jnp.inf)
    for head in range(SWA_Q_HEADS):
        bias = jnp.zeros((2 * WINDOW, WINDOW), F32)
        for k in range(REL_BUCKETS):
            bias = jnp.where(bucket == k, rb_ref[head, k], bias)
        kvh, pair, parity = head // SWA_GROUP, (head % SWA_GROUP) // 2, head % 2
        rows = slice(parity * 2 * WINDOW, (parity + 1) * 2 * WINDOW)
        cols = slice(pair * WINDOW, (pair + 1) * WINDOW)
        o_ref[0, kvh, rows, cols] = jnp.where(band_ok & has_prev, bias * LOG2E, neg)
        o_ref[1, kvh, rows, cols] = jnp.where(band_ok, bias * LOG2E, neg)


def _swa_bias(rel_bias):
    return pl.pallas_call(
        _swa_bias_kernel,
        in_specs=[pl.BlockSpec(memory_space=pltpu.SMEM)],
        out_specs=pl.BlockSpec(memory_space=pltpu.VMEM),
        out_shape=jax.ShapeDtypeStruct((2, SWA_KV_HEADS, 4 * WINDOW, SWA_GROUP // 2 * WINDOW), F32),
        name="swa_bias",
    )(rel_bias)


def _proj_kernel(x_ref, sh_ref, sc_ref, g_ref, pos_ref, inv_ref, spread_ref, w1_ref, gq_ref, gkv_ref, wq_ref,
                 wkv_ref, *rest):
    n_cast = (len(rest) - 6) // 2
    cast_in, (qm_ref, km_ref, vm_ref, qs_ref, ks_ref, vs_ref), cast_out = (
        rest[:n_cast], rest[n_cast:n_cast + 6], rest[n_cast + 6:])
    for src, dst in zip(cast_in, cast_out):
        dst[...] = src[...].astype(BF16)
    subs = [slice(k * PROJ_SUB, (k + 1) * PROJ_SUB) for k in range(TM_PROJ // PROJ_SUB)]
    ps = []
    for r in subs:
        h = _rms(x_ref[r, :], g_ref[...]) * (1 + sc_ref[...]) + sh_ref[...]
        ps.append(_dot_nt(h.astype(BF16), w1_ref[...]))

    ang = inv_ref[...] * pos_ref[...].astype(F32)
    cs = jnp.concatenate([jnp.cos(ang), jnp.sin(ang)], axis=0)
    hi = cs.astype(BF16).astype(F32)
    mid = (cs - hi).astype(BF16).astype(F32)
    lo = (cs - hi - mid).astype(BF16).astype(F32)
    terms = jnp.concatenate([hi, mid, lo, jnp.zeros_like(hi)], axis=0)
    tables = _dot(terms.T.astype(BF16), spread_ref[...])
    lane_t = lax.broadcasted_iota(jnp.int32, (PROJ_SUB, LANES), 1)
    scale = (MLA_NOPE + MLA_ROPE) ** -0.5 * LOG2E
    half = MLA_HEADS * LANES

    ups = []
    for p in ps:
        qn = _rms(p[:, C_QLAT:C_QLAT + MLA_Q_RANK], gq_ref[...]).astype(BF16)
        kvn = _rms(p[:, C_KVLAT:C_KVLAT + MLA_KV_RANK], gkv_ref[...]).astype(BF16)
        ups.append((_dot(qn, wq_ref[...]), _dot(kvn, wkv_ref[...])))

    for k, (r, p, (qq, kk)) in enumerate(zip(subs, ps, ups)):
        cos_f, sin_f = tables[r, :LANES], tables[r, LANES:]
        cos_q = jnp.where(lane_t < MLA_NOPE, 1.0, cos_f)

        def rope(t, cos):
            return t * cos + pltpu.roll(t, LANES - HALF_ROPE, 1) * sin_f

        for hd in range(MLA_HEADS):
            qm_ref[hd, r, :] = (rope(qq[:, hd * LANES:(hd + 1) * LANES], cos_q) * scale).astype(BF16)
        k_pe = rope(p[:, C_KR:C_KR + LANES], cos_f)
        for hd in range(MLA_HEADS):
            km_ref[hd, r, :] = (kk[:, hd * LANES:(hd + 1) * LANES] + k_pe).astype(BF16)
        for pr in range(MLA_HEADS // 2):
            v_t = kk[:, half + pr * LANES:half + (pr + 1) * LANES].T.astype(BF16)
            for t in range(PROJ_SUB // TK_MLA):
                vm_ref[pr, k * (PROJ_SUB // TK_MLA) + t] = v_t[:, t * TK_MLA:(t + 1) * TK_MLA]
        qs_ref[r, :] = (p[:, C_QS:C_KS] * (SWA_HEAD ** -0.5 * LOG2E)).astype(BF16)
        ks_ref[r, :] = p[:, C_KS:C_VS].astype(BF16)
        vs_ref[r, :] = p[:, C_VS:N_PROJ].astype(BF16)


def _proj(x, sh1, sc1, ln_g, pos3, inv, spread, w1, gq, gkv, wq, wkv, tail_weights):
    B, S, D = x.shape
    tm = TM_PROJ
    steps = B * (S // tm)
    const = lambda shape: pl.BlockSpec(shape, lambda b, i: (0,) * len(shape))
    slab = lambda w: w.shape[1] // steps
    cast_in = [pl.BlockSpec((None, slab(w), w.shape[2]), lambda b, i: (0, b * (S // tm) + i, 0)) for w in tail_weights]
    cast_out = [pl.BlockSpec((slab(w), w.shape[2]), lambda b, i: (b * (S // tm) + i, 0)) for w in tail_weights]
    assert all(w.shape[1] % steps == 0 and slab(w) % 16 == 0 for w in tail_weights)
    return pl.pallas_call(
        _proj_kernel,
        grid=(B, S // tm),
        in_specs=[pl.BlockSpec((None, tm, D), lambda b, i: (b, i, 0)),
                  pl.BlockSpec((None, 1, D), lambda b, i: (b, 0, 0)),
                  pl.BlockSpec((None, 1, D), lambda b, i: (b, 0, 0)),
                  const((1, D)),
                  pl.BlockSpec((None, 1, tm), lambda b, i: (b, 0, i)),
                  const((HALF_ROPE, 1)), const(spread.shape),
                  const(w1.shape), const(gq.shape), const(gkv.shape), const(wq.shape), const(wkv.shape)] + cast_in,
        out_specs=[pl.BlockSpec((None, MLA_HEADS, tm, LANES), lambda b, i: (b, 0, i, 0)),
                   pl.BlockSpec((None, MLA_HEADS, tm, LANES), lambda b, i: (b, 0, i, 0)),
                   pl.BlockSpec((None, MLA_HEADS // 2, tm // TK_MLA, LANES, TK_MLA), lambda b, i: (b, 0, i, 0, 0)),
                   pl.BlockSpec((None, tm, SWA_Q_HEADS * SWA_HEAD), lambda b, i: (b, i, 0)),
                   pl.BlockSpec((None, tm, LANES), lambda b, i: (b, i, 0)),
                   pl.BlockSpec((None, tm, LANES), lambda b, i: (b, i, 0))] + cast_out,
        out_shape=[jax.ShapeDtypeStruct((B, MLA_HEADS, S, LANES), BF16),
                   jax.ShapeDtypeStruct((B, MLA_HEADS, S, LANES), BF16),
                   jax.ShapeDtypeStruct((B, MLA_HEADS // 2, S // TK_MLA, LANES, TK_MLA), BF16),
                   jax.ShapeDtypeStruct((B, S, SWA_Q_HEADS * SWA_HEAD), BF16),
                   jax.ShapeDtypeStruct((B, S, LANES), BF16),
                   jax.ShapeDtypeStruct((B, S, LANES), BF16)]
                  + [jax.ShapeDtypeStruct(w.shape[1:], BF16) for w in tail_weights],
        compiler_params=pltpu.CompilerParams(dimension_semantics=("arbitrary", "arbitrary"),
                                             vmem_limit_bytes=VMEM_LIMIT),
        name="proj",
    )(x, sh1, sc1, ln_g, pos3, inv, spread, w1, gq, gkv, wq, wkv, *tail_weights)


def _mla_kernel(q_ref, k_ref, vt_ref, o_ref, qt_sc, acc_sc, *st_slots):
    tq, tk = TQ_MLA, TK_MLA
    nc = tq // LANES
    i = pl.program_id(2)
    for hd in range(MLA_GROUP):
        qt_sc[hd] = q_ref[hd].astype(F32).T.astype(BF16)
    acc_sc[...] = jnp.zeros_like(acc_sc)

    per_tile = tq // tk

    ones_rows = (lax.broadcasted_iota(jnp.int32, (MLA_ACC_ROWS - MLA_V, tk), 0) == 0).astype(BF16)
    r_iota = lax.broadcasted_iota(jnp.int32, (tk, LANES), 0)
    c_iota = lax.broadcasted_iota(jnp.int32, (tk, LANES), 1)

    blocks = [(d, hd) for d in range(per_tile) for hd in range(MLA_GROUP)]
    assert len(blocks) % len(st_slots) == 0 and MLA_LOOKAHEAD <= MLA_GROUP

    def scores(t, n, first_col):
        d, hd = blocks[n]
        k = k_ref[hd, pl.ds(pl.multiple_of((t * per_tile + d) * tk, tk), tk), :]
        st_slots[n % len(st_slots)][0, :, first_col:tq] = _dot(k, qt_sc[hd, :, first_col:])

    def tile(t, ms_all, diagonal):
        first_cc = [(d * tk) // LANES if diagonal else 0 for d, _ in blocks]
        ms_all = list(ms_all)
        for n, (d, hd) in enumerate(blocks):
            ahead = n + MLA_LOOKAHEAD
            if ahead < len(blocks):
                scores(t, ahead, first_cc[ahead] * LANES)
            elif not diagonal:
                scores(t + 1, ahead - len(blocks), 0)
            ms = list(ms_all[hd])
            ps, alphas = [], []
            for cc in range(first_cc[n], nc):
                blk = st_slots[n % len(st_slots)][0, :, cc * LANES:(cc + 1) * LANES]
                if diagonal and cc * LANES < (d + 1) * tk - 1:
                    blk = jnp.where(r_iota + (d * tk - cc * LANES) <= c_iota, blk, -jnp.inf)
                m_new = jnp.maximum(ms[cc], jnp.max(blk, axis=0, keepdims=True))
                alphas.append(jnp.exp2(ms[cc] - m_new))
                ps.append(jnp.exp2(blk - m_new).astype(BF16))
                ms[cc] = m_new
            v_t = vt_ref[hd // 2, t * per_tile + d, (hd % 2) * MLA_V:(hd % 2 + 1) * MLA_V, :]
            pv = _dot(jnp.concatenate([v_t, ones_rows], axis=0), jnp.concatenate(ps, axis=1))
            rows = slice(hd * MLA_ACC_ROWS, (hd + 1) * MLA_ACC_ROWS)
            cols = slice(first_cc[n] * LANES, tq)
            acc_sc[rows, cols] = acc_sc[rows, cols] * jnp.concatenate(alphas, axis=1) + pv
            ms_all[hd] = tuple(ms)
        return tuple(ms_all)

    init = tuple(tuple(jnp.full((1, LANES), -jnp.inf, F32) for _ in range(nc)) for _ in range(MLA_GROUP))
    for n in range(MLA_LOOKAHEAD):
        scores(0, n, 0)
    ms_all = lax.fori_loop(0, i, lambda t, c: tile(t, c, False), init)
    tile(i, ms_all, True)
    outs = []
    for hd in range(MLA_GROUP):
        acc = acc_sc[hd * MLA_ACC_ROWS:(hd + 1) * MLA_ACC_ROWS, :]
        outs.append(acc[:MLA_V] / acc[MLA_V:MLA_V + 1])
    o_ref[...] = jnp.concatenate(outs, axis=0).T.astype(BF16)


def _mla(qm, km, vt):
    B, H, S, _ = qm.shape
    tq, tk, g = TQ_MLA, TK_MLA, MLA_GROUP
    return pl.pallas_call(
        _mla_kernel,
        grid=(B, H // g, S // tq),
        in_specs=[pl.BlockSpec((None, g, tq, LANES), lambda b, p, i: (b, p, i, 0)),
                  pl.BlockSpec((None, g, S, LANES), lambda b, p, i: (b, p, 0, 0)),
                  pl.BlockSpec((None, g // 2, S // tk, LANES, tk), lambda b, p, i: (b, p, 0, 0, 0))],
        out_specs=pl.BlockSpec((None, tq, g * MLA_V), lambda b, p, i: (b, i, p)),
        out_shape=jax.ShapeDtypeStruct((B, S, H * MLA_V), BF16),
        scratch_shapes=[pltpu.VMEM((g, LANES, tq), BF16), pltpu.VMEM((g * MLA_ACC_ROWS, tq), F32),
                        ] + [pltpu.VMEM((1, tk, tq + SCORE_PAD), F32)] * (MLA_LOOKAHEAD + 1),
        compiler_params=pltpu.CompilerParams(dimension_semantics=("arbitrary",) * 3,
                                             vmem_limit_bytes=VMEM_LIMIT),
        name="mla",
    )(qm, km, vt)


def _swa_kernel(sink_ref, q_ref, kp_ref, kc_ref, vp_ref, vc_ref, bias0_ref, bias_ref, o_ref, *st_slots):
    band = 2 * WINDOW
    pairs = SWA_GROUP // 2
    kband = jnp.concatenate([kp_ref[...], kc_ref[...]], axis=0).astype(F32)
    vband_t = jnp.concatenate([vp_ref[...], vc_ref[...]], axis=0).astype(F32).T
    lo = lax.broadcasted_iota(jnp.int32, kband.shape, 1) < SWA_HEAD
    kswap = pltpu.roll(kband, SWA_HEAD, 1)
    k_even = [jnp.where(lo, kband, 0.0).astype(BF16), jnp.where(lo, kswap, 0.0).astype(BF16)]
    k_odd = [jnp.where(lo, 0.0, kswap).astype(BF16), jnp.where(lo, 0.0, kband).astype(BF16)]
    ones_rows = (lax.broadcasted_iota(jnp.int32, (16, band), 0) == 0).astype(BF16)
    chains = [(j, kvh) for j in range(SWA_STEP_BLOCKS) for kvh in range(SWA_KV_HEADS)]

    def scores(n):
        j, kvh = chains[n]
        keys = slice(j * WINDOW, j * WINDOW + band)
        kcat = jnp.concatenate([k_even[kvh][keys], k_odd[kvh][keys]], axis=0)
        qstack = jnp.concatenate([q_ref[j * WINDOW:(j + 1) * WINDOW, (kvh * pairs + jj) * LANES:
                                        (kvh * pairs + jj + 1) * LANES] for jj in range(pairs)], axis=0)
        bias = bias0_ref if j == 0 else bias_ref
        st_slots[n % len(st_slots)][0, :, :pairs * WINDOW] = _dot_nt(kcat, qstack) + bias[kvh]

    for n in range(min(SWA_LOOKAHEAD, len(chains))):
        scores(n)
    for n, (j, kvh) in enumerate(chains):
        if n + SWA_LOOKAHEAD < len(chains):
            scores(n + SWA_LOOKAHEAD)
        st = st_slots[n % len(st_slots)].at[jnp.minimum(pl.program_id(1), 0)]
        v_t = vband_t[kvh * SWA_HEAD:(kvh + 1) * SWA_HEAD, j * WINDOW:j * WINDOW + band].astype(BF16)
        v_aug = jnp.concatenate([v_t, ones_rows], axis=0)
        halves = []
        for parity in range(2):
            ps, sink_terms = [], []
            for jj in range(pairs):
                blk = st[parity * band:(parity + 1) * band, jj * WINDOW:(jj + 1) * WINDOW]
                sink = sink_ref[kvh * SWA_GROUP + 2 * jj + parity] * LOG2E
                m = jnp.maximum(jnp.max(blk, axis=0, keepdims=True), sink)
                ps.append(jnp.exp2(blk - m).astype(BF16))
                sink_terms.append(jnp.exp2(sink - m))
            pv = _dot(v_aug, jnp.concatenate(ps, axis=1))
            halves.append(pv[:SWA_HEAD] / (pv[SWA_HEAD:SWA_HEAD + 1] + jnp.concatenate(sink_terms, axis=1)))
        o_t = jnp.concatenate(halves, axis=0)
        for jj in range(pairs):
            pair = kvh * pairs + jj
            o_ref[j * WINDOW:(j + 1) * WINDOW, pair * LANES:(pair + 1) * LANES] = (
                o_t[:, jj * WINDOW:(jj + 1) * WINDOW].T.astype(BF16))


def _swa(sinks, qs, ks, vs, bias_tbl):
    B, S, W = qs.shape
    nb = SWA_STEP_BLOCKS
    prev = lambda b, n: (b, jnp.maximum(nb * n - 1, 0), 0)
    cur = lambda b, n: (b, n, 0)
    tbl_block = (None,) + bias_tbl.shape[1:]
    return pl.pallas_call(
        _swa_kernel,
        grid=(B, S // (nb * WINDOW)),
        in_specs=[pl.BlockSpec(memory_space=pltpu.SMEM),
                  pl.BlockSpec((None, nb * WINDOW, W), cur),
                  pl.BlockSpec((None, WINDOW, LANES), prev), pl.BlockSpec((None, nb * WINDOW, LANES), cur),
                  pl.BlockSpec((None, WINDOW, LANES), prev), pl.BlockSpec((None, nb * WINDOW, LANES), cur),
                  pl.BlockSpec(tbl_block, lambda b, n: (jnp.minimum(n, 1), 0, 0, 0)),
                  pl.BlockSpec(tbl_block, lambda b, n: (1, 0, 0, 0))],
        out_specs=pl.BlockSpec((None, nb * WINDOW, W), cur),
        out_shape=jax.ShapeDtypeStruct((B, S, W), BF16),
        scratch_shapes=[pltpu.VMEM((1, bias_tbl.shape[2], bias_tbl.shape[3] + SCORE_PAD), F32)] * (SWA_LOOKAHEAD + 1),
        compiler_params=pltpu.CompilerParams(dimension_semantics=("arbitrary", "arbitrary"),
                                             vmem_limit_bytes=VMEM_LIMIT),
        name="swa",
    )(sinks, qs, ks, ks, vs, vs, bias_tbl, bias_tbl)


def _tail_kernel(x_ref, ym_ref, ys_ref, sh1_ref, sc1_ref, ga1_ref, sh2_ref, sc2_ref, ga2_ref,
                 gmix_ref, gmlp_ref, gfin_ref, bg_ref, wg_ref, wom_ref, wos_ref, wo_ref, w1_ref, w2_ref, o_ref):
    subs = [slice(k * TAIL_SUB, (k + 1) * TAIL_SUB) for k in range(TM_TAIL // TAIL_SUB)]
    n_ff = D_FF // FF_CHUNK
    xs = [x_ref[r, :] for r in subs]
    branches = [(_dot(ym_ref[r, :], wom_ref[...]), _dot(ys_ref[r, :], wos_ref[...])) for r in subs]
    hs = [(_rms(x, gmix_ref[...]) * (1 + sc1_ref[...]) + sh1_ref[...]).astype(BF16) for x in xs]
    logits = [_dot_nt(h, wg_ref[...]) for h in hs]
    x1s = []
    for x, (a, b), gl in zip(xs, branches, logits):
        gates = jax.nn.sigmoid(gl + bg_ref[...])
        merged = gates[:, :D_MODEL] * a + gates[:, D_MODEL:] * b
        x1s.append(x + ga1_ref[...] * _dot(merged.astype(BF16), wo_ref[...]))
    h2s = [(_rms(x1, gmlp_ref[...]) * (1 + sc2_ref[...]) + sh2_ref[...]).astype(BF16) for x1 in x1s]

    jobs = [(k, c) for k in range(len(subs)) for c in range(n_ff)]
    up = lambda k, c: _dot(h2s[k], w1_ref[:, c * FF_CHUNK:(c + 1) * FF_CHUNK])
    pending = {0: up(*jobs[0])}
    accs = [None] * len(subs)
    for n, (k, c) in enumerate(jobs):
        if n + 1 < len(jobs):
            pending[n + 1] = up(*jobs[n + 1])
        u = jnp.square(jnp.maximum(pending.pop(n), 0.0)).astype(BF16)
        down = _dot(u, w2_ref[c * FF_CHUNK:(c + 1) * FF_CHUNK, :])
        accs[k] = down if accs[k] is None else accs[k] + down
        if c == n_ff - 1:
            o_ref[subs[k], :] = _rms(x1s[k] + ga2_ref[...] * accs[k], gfin_ref[...])


def _tail(x, ym, ys, mods, gmix, gmlp, gfin, bg, wg, wom, wos, wo, w1, w2):
    B, S, D = x.shape
    tm = TM_TAIL
    tok = lambda width: pl.BlockSpec((None, tm, width), lambda b, i: (b, i, 0))
    per_b = pl.BlockSpec((None, 1, D), lambda b, i: (b, 0, 0))
    const = lambda a: pl.BlockSpec(a.shape, lambda b, i: (0,) * a.ndim, pipeline_mode=pl.Buffered(1))
    return pl.pallas_call(
        _tail_kernel,
        grid=(B, S // tm),
        in_specs=[tok(D), tok(ym.shape[-1]), tok(ys.shape[-1])] + [per_b] * 6
                 + [const(a) for a in (gmix, gmlp, gfin, bg, wg, wom, wos, wo, w1, w2)],
        out_specs=tok(D),
        out_shape=jax.ShapeDtypeStruct((B, S, D), F32),
        compiler_params=pltpu.CompilerParams(dimension_semantics=("arbitrary", "arbitrary"),
                                             vmem_limit_bytes=VMEM_LIMIT),
        name="tail",
    )(x, ym, ys, *mods, gmix, gmlp, gfin, bg, wg, wom, wos, wo, w1, w2)


def _head_slab(nope, a, b):
    r, h, _ = nope.shape
    pad = jnp.zeros((r, h, LANES - MLA_NOPE - MLA_ROPE - HALF_ROPE), nope.dtype)
    return jnp.concatenate([nope, a, b, a, pad], axis=-1).reshape(r, h * LANES)


def _rope_spread():
    m = np.zeros((LANES, 2 * LANES), np.float32)
    for term in range(3):
        for j in range(HALF_ROPE):
            m[term * MLA_ROPE + j, [ROPE_A + j, ROPE_B + j]] = 1.0
            m[term * MLA_ROPE + HALF_ROPE + j, LANES + ROPE_A + j] = -1.0
            m[term * MLA_ROPE + HALF_ROPE + j, LANES + ROPE_B + j] = 1.0
    return jnp.asarray(m, BF16)


def kernel(x, c, positions, rel_bias, ada_w, ada_b, ln_mix_g, w_in, b_gate, mla_q_norm_g, mla_kv_norm_g,
           w_uq, w_ukv, swa_sinks, w_o_mla, w_o_swa, w_o, ln_mlp_g, w_ff1, w_ff2, ln_final_g):
    B, S, D = x.shape
    assert (B, S, D) == (x.shape[0], 4096, D_MODEL) and ada_w.shape[0] == 1
    l = 0

    c_pad = jnp.pad(c, ((0, 8 - B), (0, 0)))
    mod = _adaln(c_pad, ada_w, ada_b)[:B]
    mods = [m[:, None, :] for m in jnp.split(mod, 6, axis=-1)]

    wi = w_in[l].T
    o_kv = MLA_Q_RANK
    o_kr = o_kv + MLA_KV_RANK
    o_qs = o_kr + MLA_ROPE
    o_ks = o_qs + SWA_Q_HEADS * SWA_HEAD
    o_vs = o_ks + SWA_KV_HEADS * SWA_HEAD
    o_ga = o_vs + SWA_KV_HEADS * SWA_HEAD
    kr_a, kr_b = wi[o_kr:o_kr + HALF_ROPE], wi[o_kr + HALF_ROPE:o_qs]
    z = lambda n: jnp.zeros((n, D), wi.dtype)
    w1 = jnp.concatenate([
        wi[:o_kr],
        z(ROPE_A), kr_a, kr_b, kr_a, z(LANES - ROPE_A - MLA_ROPE - HALF_ROPE),
        wi[o_qs:o_ga]], axis=0).astype(BF16)
    assert w1.shape[0] == N_PROJ
    wg = wi[o_ga:].astype(BF16)

    uq = w_uq[l]
    q_nope, q_a, q_b = uq[..., :MLA_NOPE], uq[..., MLA_NOPE:MLA_NOPE + HALF_ROPE], uq[..., MLA_NOPE + HALF_ROPE:]
    wq = _head_slab(q_nope, q_a, q_b).astype(BF16)
    ukv = w_ukv[l]
    k_nope, v_up = ukv[..., :MLA_NOPE], ukv[..., MLA_NOPE:]
    zr = jnp.zeros(k_nope.shape[:2] + (HALF_ROPE,), ukv.dtype)
    wkv = jnp.concatenate([_head_slab(k_nope, zr, zr),
                           v_up.reshape(MLA_KV_RANK, MLA_HEADS * MLA_V)], axis=1).astype(BF16)

    inv = (ROPE_THETA ** (-jnp.arange(HALF_ROPE, dtype=F32) / HALF_ROPE))[:, None]

    qm, km, vm, qs, ks, vs, wom, wos, wo, wf1, wf2 = _proj(
        x, mods[0], mods[1], ln_mix_g[l][None, :], positions[:, None, :], inv, _rope_spread(), w1,
        mla_q_norm_g[l][None, :], mla_kv_norm_g[l][None, :], wq, wkv, (w_o_mla, w_o_swa, w_o, w_ff1, w_ff2))

    y_mla = _mla(qm, km, vm)
    y_swa = _swa(swa_sinks[l], qs, ks, vs, _swa_bias(rel_bias))

    return _tail(x, y_mla, y_swa, mods, ln_mix_g[l][None, :], ln_mlp_g[l][None, :], ln_final_g[None, :],
                 b_gate[l][None, :], wg, wom, wos, wo, wf1, wf2)
```

```python
import functools
import math

import jax
import jax.numpy as jnp
import numpy as np
from jax import lax
from jax.experimental import pallas as pl
from jax.experimental.pallas import tpu as pltpu

F32 = jnp.float32
BF16 = jnp.bfloat16

D_MODEL = 1024
MLA_HEADS = 8
MLA_Q_RANK = 256
MLA_KV_RANK = 128
MLA_NOPE = 64
MLA_ROPE = 32
MLA_V = 64
SWA_Q_HEADS = 16
SWA_KV_HEADS = 2
SWA_HEAD = 64
WINDOW = 128
REL_BUCKETS = 32
REL_MAX_DIST = 128
D_FF = 4 * D_MODEL
ROPE_THETA = 10000.0
EPS = 1e-6

LANES = 128
SCORE_PAD = LANES
HALF_ROPE = MLA_ROPE // 2
ROPE_A = MLA_NOPE
ROPE_B = MLA_NOPE + HALF_ROPE
N_PAIRS = SWA_Q_HEADS // 2
SWA_GROUP = SWA_Q_HEADS // SWA_KV_HEADS
SWA_STEP_BLOCKS = 8
SWA_LOOKAHEAD = 2
LOG2E = math.log2(math.e)

C_QLAT = 0
C_KVLAT = C_QLAT + MLA_Q_RANK
C_KR = C_KVLAT + MLA_KV_RANK
C_QS = C_KR + LANES
C_KS = C_QS + SWA_Q_HEADS * SWA_HEAD
C_VS = C_KS + SWA_KV_HEADS * SWA_HEAD
N_PROJ = C_VS + SWA_KV_HEADS * SWA_HEAD

VMEM_LIMIT = 56 * 1024 * 1024

TM_PROJ = 512
PROJ_SUB = 256
TQ_MLA = 1024
TK_MLA = 256
MLA_GROUP = 4
MLA_LOOKAHEAD = 3
MLA_ACC_ROWS = MLA_V + 16
TM_TAIL = 512
TAIL_SUB = 256
FF_CHUNK = 1024


def _rms(x, g):
    return x * lax.rsqrt(jnp.mean(x * x, axis=-1, keepdims=True) + EPS) * g


def _dot(a, b):
    return jnp.dot(a, b, preferred_element_type=F32)


def _dot_nt(a, b):
    return lax.dot_general(a, b, (((1,), (1,)), ((), ())), preferred_element_type=F32)


ADALN_STEPS = 8


def _adaln_bias_kernel(rb_ref, c_ref, w_ref, b_ref, o_ref, tbl_ref):
    c = c_ref[...]
    act = (c * jax.nn.sigmoid(c)).astype(BF16)
    o_ref[...] = _dot(act, w_ref[...].astype(BF16)) + b_ref[...]

    a = lax.broadcasted_iota(jnp.int32, (2 * WINDOW, WINDOW), 1)
    b = lax.broadcasted_iota(jnp.int32, (2 * WINDOW, WINDOW), 0)
    dist = WINDOW + a - b
    n = jnp.maximum(dist, 0)
    max_exact = REL_BUCKETS // 2
    nf = jnp.maximum(n, 1).astype(F32)
    large = max_exact + jnp.floor(jnp.log(nf / max_exact) / math.log(REL_MAX_DIST / max_exact)
                                  * (REL_BUCKETS - max_exact)).astype(jnp.int32)
    large = jnp.minimum(large, REL_BUCKETS - 1)
    bucket = jnp.where(n < max_exact, n, large)
    band_ok = (dist >= 0) & (dist < WINDOW)
    has_prev = b >= WINDOW
    neg = jnp.float32(-jnp.inf)
    heads = SWA_Q_HEADS // ADALN_STEPS
    for h in range(heads):
        head = pl.program_id(0) * heads + h
        bias = jnp.zeros((2 * WINDOW, WINDOW), F32)
        for k in range(REL_BUCKETS):
            bias = jnp.where(bucket == k, rb_ref[head, k], bias)
        tbl_ref[0, h] = jnp.where(band_ok & has_prev, bias * LOG2E, neg)
        tbl_ref[1, h] = jnp.where(band_ok, bias * LOG2E, neg)


def _adaln_bias(rel_bias, c_pad, ada_w, ada_b):
    rows = c_pad.shape[0]
    n_out = ada_w.shape[2]
    cols = n_out // ADALN_STEPS
    heads = SWA_Q_HEADS // ADALN_STEPS
    return pl.pallas_call(
        _adaln_bias_kernel,
        grid=(ADALN_STEPS,),
        in_specs=[pl.BlockSpec(memory_space=pltpu.SMEM),
                  pl.BlockSpec((rows, D_MODEL), lambda j: (0, 0)),
                  pl.BlockSpec((None, D_MODEL, cols), lambda j: (0, 0, j)),
                  pl.BlockSpec((1, cols), lambda j: (0, j))],
        out_specs=[pl.BlockSpec((rows, cols), lambda j: (0, j)),
                   pl.BlockSpec((2, heads, 2 * WINDOW, WINDOW), lambda j: (0, j, 0, 0))],
        out_shape=[jax.ShapeDtypeStruct((rows, n_out), F32),
                   jax.ShapeDtypeStruct((2, SWA_Q_HEADS, 2 * WINDOW, WINDOW), F32)],
        compiler_params=pltpu.CompilerParams(dimension_semantics=("arbitrary",)),
        name="adaln_bias",
    )(rel_bias, c_pad, ada_w, ada_b)


def _proj_kernel(x_ref, sh_ref, sc_ref, g_ref, pos_ref, inv_ref, spread_ref, w1_ref, gq_ref, gkv_ref, wq_ref,
                 wkv_ref, *rest):
    n_cast = (len(rest) - 6) // 2
    cast_in, (qm_ref, km_ref, vm_ref, qs_ref, ks_ref, vs_ref), cast_out = (
        rest[:n_cast], rest[n_cast:n_cast + 6], rest[n_cast + 6:])
    for src, dst in zip(cast_in, cast_out):
        dst[...] = src[...].astype(BF16)
    subs = [slice(k * PROJ_SUB, (k + 1) * PROJ_SUB) for k in range(TM_PROJ // PROJ_SUB)]
    ps = []
    for r in subs:
        h = _rms(x_ref[r, :], g_ref[...]) * (1 + sc_ref[...]) + sh_ref[...]
        ps.append(_dot_nt(h.astype(BF16), w1_ref[...]))

    ang = inv_ref[...] * pos_ref[...].astype(F32)
    cs = jnp.concatenate([jnp.cos(ang), jnp.sin(ang)], axis=0)
    hi = cs.astype(BF16).astype(F32)
    mid = (cs - hi).astype(BF16).astype(F32)
    lo = (cs - hi - mid).astype(BF16).astype(F32)
    terms = jnp.concatenate([hi, mid, lo, jnp.zeros_like(hi)], axis=0)
    tables = _dot(terms.T.astype(BF16), spread_ref[...])
    lane_t = lax.broadcasted_iota(jnp.int32, (PROJ_SUB, LANES), 1)
    scale = (MLA_NOPE + MLA_ROPE) ** -0.5 * LOG2E
    half = MLA_HEADS * LANES

    ups = []
    for p in ps:
        qn = _rms(p[:, C_QLAT:C_QLAT + MLA_Q_RANK], gq_ref[...]).astype(BF16)
        kvn = _rms(p[:, C_KVLAT:C_KVLAT + MLA_KV_RANK], gkv_ref[...]).astype(BF16)
        ups.append((_dot(qn, wq_ref[...]), _dot(kvn, wkv_ref[...])))

    for k, (r, p, (qq, kk)) in enumerate(zip(subs, ps, ups)):
        cos_f, sin_f = tables[r, :LANES], tables[r, LANES:]
        cos_q = jnp.where(lane_t < MLA_NOPE, 1.0, cos_f)

        def rope(t, cos):
            return t * cos + pltpu.roll(t, LANES - HALF_ROPE, 1) * sin_f

        for hd in range(MLA_HEADS):
            qm_ref[hd, r, :] = (rope(qq[:, hd * LANES:(hd + 1) * LANES], cos_q) * scale).astype(BF16)
        k_pe = rope(p[:, C_KR:C_KR + LANES], cos_f)
        for hd in range(MLA_HEADS):
            km_ref[hd, r, :] = (kk[:, hd * LANES:(hd + 1) * LANES] + k_pe).astype(BF16)
        for pr in range(MLA_HEADS // 2):
            v_t = kk[:, half + pr * LANES:half + (pr + 1) * LANES].T.astype(BF16)
            for t in range(PROJ_SUB // TK_MLA):
                vm_ref[pr, k * (PROJ_SUB // TK_MLA) + t] = v_t[:, t * TK_MLA:(t + 1) * TK_MLA]
        qs_ref[r, :] = (p[:, C_QS:C_KS] * (SWA_HEAD ** -0.5 * LOG2E)).astype(BF16)
        ks_ref[r, :] = p[:, C_KS:C_VS].astype(BF16)
        vs_ref[r, :] = p[:, C_VS:N_PROJ].astype(BF16)


def _proj(x, sh1, sc1, ln_g, pos3, inv, spread, w1, gq, gkv, wq, wkv, tail_weights):
    B, S, D = x.shape
    tm = TM_PROJ
    steps = B * (S // tm)
    const = lambda shape: pl.BlockSpec(shape, lambda b, i: (0,) * len(shape))
    slab = lambda w: w.shape[1] // steps
    cast_in = [pl.BlockSpec((None, slab(w), w.shape[2]), lambda b, i: (0, b * (S // tm) + i, 0)) for w in tail_weights]
    cast_out = [pl.BlockSpec((slab(w), w.shape[2]), lambda b, i: (b * (S // tm) + i, 0)) for w in tail_weights]
    assert all(w.shape[1] % steps == 0 and slab(w) % 16 == 0 for w in tail_weights)
    return pl.pallas_call(
        _proj_kernel,
        grid=(B, S // tm),
        in_specs=[pl.BlockSpec((None, tm, D), lambda b, i: (b, i, 0)),
                  pl.BlockSpec((None, 1, D), lambda b, i: (b, 0, 0)),
                  pl.BlockSpec((None, 1, D), lambda b, i: (b, 0, 0)),
                  const((1, D)),
                  pl.BlockSpec((None, 1, tm), lambda b, i: (b, 0, i)),
                  const((HALF_ROPE, 1)), const(spread.shape),
                  const(w1.shape), const(gq.shape), const(gkv.shape), const(wq.shape), const(wkv.shape)] + cast_in,
        out_specs=[pl.BlockSpec((None, MLA_HEADS, tm, LANES), lambda b, i: (b, 0, i, 0)),
                   pl.BlockSpec((None, MLA_HEADS, tm, LANES), lambda b, i: (b, 0, i, 0)),
                   pl.BlockSpec((None, MLA_HEADS // 2, tm // TK_MLA, LANES, TK_MLA), lambda b, i: (b, 0, i, 0, 0)),
                   pl.BlockSpec((None, tm, SWA_Q_HEADS * SWA_HEAD), lambda b, i: (b, i, 0)),
                   pl.BlockSpec((None, tm, LANES), lambda b, i: (b, i, 0)),
                   pl.BlockSpec((None, tm, LANES), lambda b, i: (b, i, 0))] + cast_out,
        out_shape=[jax.ShapeDtypeStruct((B, MLA_HEADS, S, LANES), BF16),
                   jax.ShapeDtypeStruct((B, MLA_HEADS, S, LANES), BF16),
                   jax.ShapeDtypeStruct((B, MLA_HEADS // 2, S // TK_MLA, LANES, TK_MLA), BF16),
                   jax.ShapeDtypeStruct((B, S, SWA_Q_HEADS * SWA_HEAD), BF16),
                   jax.ShapeDtypeStruct((B, S, LANES), BF16),
                   jax.ShapeDtypeStruct((B, S, LANES), BF16)]
                  + [jax.ShapeDtypeStruct(w.shape[1:], BF16) for w in tail_weights],
        compiler_params=pltpu.CompilerParams(dimension_semantics=("arbitrary", "arbitrary"),
                                             vmem_limit_bytes=VMEM_LIMIT),
        name="proj",
    )(x, sh1, sc1, ln_g, pos3, inv, spread, w1, gq, gkv, wq, wkv, *tail_weights)


def _mla_kernel(q_ref, k_ref, vt_ref, o_ref, qt_sc, acc_sc, *st_slots):
    tq, tk = TQ_MLA, TK_MLA
    nc = tq // LANES
    i = pl.program_id(2)
    for hd in range(MLA_GROUP):
        qt_sc[hd] = q_ref[hd].astype(F32).T.astype(BF16)
    acc_sc[...] = jnp.zeros_like(acc_sc)

    per_tile = tq // tk

    ones_rows = (lax.broadcasted_iota(jnp.int32, (MLA_ACC_ROWS - MLA_V, tk), 0) == 0).astype(BF16)
    r_iota = lax.broadcasted_iota(jnp.int32, (tk, LANES), 0)
    c_iota = lax.broadcasted_iota(jnp.int32, (tk, LANES), 1)

    blocks = [(d, hd) for d in range(per_tile) for hd in range(MLA_GROUP)]
    assert len(blocks) % len(st_slots) == 0 and MLA_LOOKAHEAD <= MLA_GROUP

    def scores(t, n, first_col):
        d, hd = blocks[n]
        k = k_ref[hd, pl.ds(pl.multiple_of((t * per_tile + d) * tk, tk), tk), :]
        st_slots[n % len(st_slots)][0, :, first_col:tq] = _dot(k, qt_sc[hd, :, first_col:])

    def tile(t, ms_all, diagonal):
        first_cc = [(d * tk) // LANES if diagonal else 0 for d, _ in blocks]
        ms_all = list(ms_all)
        for n, (d, hd) in enumerate(blocks):
            ahead = n + MLA_LOOKAHEAD
            if ahead < len(blocks):
                scores(t, ahead, first_cc[ahead] * LANES)
            elif not diagonal:
                scores(t + 1, ahead - len(blocks), 0)
            ms = list(ms_all[hd])
            ps, alphas = [], []
            for cc in range(first_cc[n], nc):
                blk = st_slots[n % len(st_slots)][0, :, cc * LANES:(cc + 1) * LANES]
                if diagonal and cc * LANES < (d + 1) * tk - 1:
                    blk = jnp.where(r_iota + (d * tk - cc * LANES) <= c_iota, blk, -jnp.inf)
                m_new = jnp.maximum(ms[cc], jnp.max(blk, axis=0, keepdims=True))
                alphas.append(jnp.exp2(ms[cc] - m_new))
                ps.append(jnp.exp2(blk - m_new).astype(BF16))
                ms[cc] = m_new
            v_t = vt_ref[hd // 2, t * per_tile + d, (hd % 2) * MLA_V:(hd % 2 + 1) * MLA_V, :]
            pv = _dot(jnp.concatenate([v_t, ones_rows], axis=0), jnp.concatenate(ps, axis=1))
            rows = slice(hd * MLA_ACC_ROWS, (hd + 1) * MLA_ACC_ROWS)
            cols = slice(first_cc[n] * LANES, tq)
            acc_sc[rows, cols] = acc_sc[rows, cols] * jnp.concatenate(alphas, axis=1) + pv
            ms_all[hd] = tuple(ms)
        return tuple(ms_all)

    init = tuple(tuple(jnp.full((1, LANES), -jnp.inf, F32) for _ in range(nc)) for _ in range(MLA_GROUP))
    for n in range(MLA_LOOKAHEAD):
        scores(0, n, 0)
    ms_all = lax.fori_loop(0, i, lambda t, c: tile(t, c, False), init)
    tile(i, ms_all, True)
    outs = []
    for hd in range(MLA_GROUP):
        acc = acc_sc[hd * MLA_ACC_ROWS:(hd + 1) * MLA_ACC_ROWS, :]
        outs.append(acc[:MLA_V] / acc[MLA_V:MLA_V + 1])
    o_ref[...] = jnp.concatenate(outs, axis=0).T.astype(BF16)


def _mla(qm, km, vt):
    B, H, S, _ = qm.shape
    tq, tk, g = TQ_MLA, TK_MLA, MLA_GROUP
    return pl.pallas_call(
        _mla_kernel,
        grid=(B, H // g, S // tq),
        in_specs=[pl.BlockSpec((None, g, tq, LANES), lambda b, p, i: (b, p, i, 0)),
                  pl.BlockSpec((None, g, S, LANES), lambda b, p, i: (b, p, 0, 0)),
                  pl.BlockSpec((None, g // 2, S // tk, LANES, tk), lambda b, p, i: (b, p, 0, 0, 0))],
        out_specs=pl.BlockSpec((None, tq, g * MLA_V), lambda b, p, i: (b, i, p)),
        out_shape=jax.ShapeDtypeStruct((B, S, H * MLA_V), BF16),
        scratch_shapes=[pltpu.VMEM((g, LANES, tq), BF16), pltpu.VMEM((g * MLA_ACC_ROWS, tq), F32),
                        ] + [pltpu.VMEM((1, tk, tq + SCORE_PAD), F32)] * (MLA_LOOKAHEAD + 1),
        compiler_params=pltpu.CompilerParams(dimension_semantics=("arbitrary",) * 3,
                                             vmem_limit_bytes=VMEM_LIMIT),
        name="mla",
    )(qm, km, vt)


def _swa_kernel(sink_ref, q_ref, kp_ref, kc_ref, vp_ref, vc_ref, bias0_ref, bias_ref, o_ref, *st_slots):
    band = 2 * WINDOW
    pairs = SWA_GROUP // 2
    kband = jnp.concatenate([kp_ref[...], kc_ref[...]], axis=0).astype(F32)
    vband_t = jnp.concatenate([vp_ref[...], vc_ref[...]], axis=0).astype(F32).T
    lo = lax.broadcasted_iota(jnp.int32, kband.shape, 1) < SWA_HEAD
    kswap = pltpu.roll(kband, SWA_HEAD, 1)
    k_even = [jnp.where(lo, kband, 0.0).astype(BF16), jnp.where(lo, kswap, 0.0).astype(BF16)]
    k_odd = [jnp.where(lo, 0.0, kswap).astype(BF16), jnp.where(lo, 0.0, kband).astype(BF16)]
    ones_rows = (lax.broadcasted_iota(jnp.int32, (16, band), 0) == 0).astype(BF16)
    chains = [(j, kvh) for j in range(SWA_STEP_BLOCKS) for kvh in range(SWA_KV_HEADS)]

    def scores(n):
        j, kvh = chains[n]
        keys = slice(j * WINDOW, j * WINDOW + band)
        kcat = jnp.concatenate([k_even[kvh][keys], k_odd[kvh][keys]], axis=0)
        qstack = jnp.concatenate([q_ref[j * WINDOW:(j + 1) * WINDOW, (kvh * pairs + jj) * LANES:
                                        (kvh * pairs + jj + 1) * LANES] for jj in range(pairs)], axis=0)
        bias = bias0_ref if j == 0 else bias_ref
        st = _dot_nt(kcat, qstack)
        for parity in range(2):
            for jj in range(pairs):
                rows, cols = slice(parity * band, (parity + 1) * band), slice(jj * WINDOW, (jj + 1) * WINDOW)
                st_slots[n % len(st_slots)][0, rows, cols] = st[rows, cols] + bias[kvh * SWA_GROUP + 2 * jj + parity]

    for n in range(min(SWA_LOOKAHEAD, len(chains))):
        scores(n)
    for n, (j, kvh) in enumerate(chains):
        if n + SWA_LOOKAHEAD < len(chains):
            scores(n + SWA_LOOKAHEAD)
        st = st_slots[n % len(st_slots)].at[jnp.minimum(pl.program_id(1), 0)]
        v_t = vband_t[kvh * SWA_HEAD:(kvh + 1) * SWA_HEAD, j * WINDOW:j * WINDOW + band].astype(BF16)
        v_aug = jnp.concatenate([v_t, ones_rows], axis=0)
        halves = []
        for parity in range(2):
            ps, sink_terms = [], []
            for jj in range(pairs):
                blk = st[parity * band:(parity + 1) * band, jj * WINDOW:(jj + 1) * WINDOW]
                sink = sink_ref[kvh * SWA_GROUP + 2 * jj + parity] * LOG2E
                m = jnp.maximum(jnp.max(blk, axis=0, keepdims=True), sink)
                ps.append(jnp.exp2(blk - m).astype(BF16))
                sink_terms.append(jnp.exp2(sink - m))
            pv = _dot(v_aug, jnp.concatenate(ps, axis=1))
            halves.append(pv[:SWA_HEAD] / (pv[SWA_HEAD:SWA_HEAD + 1] + jnp.concatenate(sink_terms, axis=1)))
        o_t = jnp.concatenate(halves, axis=0)
        for jj in range(pairs):
            pair = kvh * pairs + jj
            o_ref[j * WINDOW:(j + 1) * WINDOW, pair * LANES:(pair + 1) * LANES] = (
                o_t[:, jj * WINDOW:(jj + 1) * WINDOW].T.astype(BF16))


def _swa(sinks, qs, ks, vs, bias_tbl):
    B, S, W = qs.shape
    nb = SWA_STEP_BLOCKS
    prev = lambda b, n: (b, jnp.maximum(nb * n - 1, 0), 0)
    cur = lambda b, n: (b, n, 0)
    tbl_block = (None,) + bias_tbl.shape[1:]
    return pl.pallas_call(
        _swa_kernel,
        grid=(B, S // (nb * WINDOW)),
        in_specs=[pl.BlockSpec(memory_space=pltpu.SMEM),
                  pl.BlockSpec((None, nb * WINDOW, W), cur),
                  pl.BlockSpec((None, WINDOW, LANES), prev), pl.BlockSpec((None, nb * WINDOW, LANES), cur),
                  pl.BlockSpec((None, WINDOW, LANES), prev), pl.BlockSpec((None, nb * WINDOW, LANES), cur),
                  pl.BlockSpec(tbl_block, lambda b, n: (jnp.minimum(n, 1), 0, 0, 0)),
                  pl.BlockSpec(tbl_block, lambda b, n: (1, 0, 0, 0))],
        out_specs=pl.BlockSpec((None, nb * WINDOW, W), cur),
        out_shape=jax.ShapeDtypeStruct((B, S, W), BF16),
        scratch_shapes=[pltpu.VMEM((1, 4 * WINDOW, SWA_GROUP // 2 * WINDOW + SCORE_PAD), F32)] * (SWA_LOOKAHEAD + 1),
        compiler_params=pltpu.CompilerParams(dimension_semantics=("arbitrary", "arbitrary"),
                                             vmem_limit_bytes=VMEM_LIMIT),
        name="swa",
    )(sinks, qs, ks, ks, vs, vs, bias_tbl, bias_tbl)


def _tail_kernel(x_ref, ym_ref, ys_ref, sh1_ref, sc1_ref, ga1_ref, sh2_ref, sc2_ref, ga2_ref,
                 gmix_ref, gmlp_ref, gfin_ref, bg_ref, wg_ref, wom_ref, wos_ref, wo_ref, w1_ref, w2_ref, o_ref):
    subs = [slice(k * TAIL_SUB, (k + 1) * TAIL_SUB) for k in range(TM_TAIL // TAIL_SUB)]
    n_ff = D_FF // FF_CHUNK
    xs = [x_ref[r, :] for r in subs]
    branches = [(_dot(ym_ref[r, :], wom_ref[...]), _dot(ys_ref[r, :], wos_ref[...])) for r in subs]
    hs = [(_rms(x, gmix_ref[...]) * (1 + sc1_ref[...]) + sh1_ref[...]).astype(BF16) for x in xs]
    logits = [_dot_nt(h, wg_ref[...]) for h in hs]
    x1s = []
    for x, (a, b), gl in zip(xs, branches, logits):
        gates = jax.nn.sigmoid(gl + bg_ref[...])
        merged = gates[:, :D_MODEL] * a + gates[:, D_MODEL:] * b
        x1s.append(x + ga1_ref[...] * _dot(merged.astype(BF16), wo_ref[...]))
    h2s = [(_rms(x1, gmlp_ref[...]) * (1 + sc2_ref[...]) + sh2_ref[...]).astype(BF16) for x1 in x1s]

    jobs = [(k, c) for k in range(len(subs)) for c in range(n_ff)]
    up = lambda k, c: _dot(h2s[k], w1_ref[:, c * FF_CHUNK:(c + 1) * FF_CHUNK])
    pending = {0: up(*jobs[0])}
    accs = [None] * len(subs)
    for n, (k, c) in enumerate(jobs):
        if n + 1 < len(jobs):
            pending[n + 1] = up(*jobs[n + 1])
        u = jnp.square(jnp.maximum(pending.pop(n), 0.0)).astype(BF16)
        down = _dot(u, w2_ref[c * FF_CHUNK:(c + 1) * FF_CHUNK, :])
        accs[k] = down if accs[k] is None else accs[k] + down
        if c == n_ff - 1:
            o_ref[subs[k], :] = _rms(x1s[k] + ga2_ref[...] * accs[k], gfin_ref[...])


def _tail(x, ym, ys, mods, gmix, gmlp, gfin, bg, wg, wom, wos, wo, w1, w2):
    B, S, D = x.shape
    tm = TM_TAIL
    tok = lambda width: pl.BlockSpec((None, tm, width), lambda b, i: (b, i, 0))
    per_b = pl.BlockSpec((None, 1, D), lambda b, i: (b, 0, 0))
    const = lambda a: pl.BlockSpec(a.shape, lambda b, i: (0,) * a.ndim, pipeline_mode=pl.Buffered(1))
    return pl.pallas_call(
        _tail_kernel,
        grid=(B, S // tm),
        in_specs=[tok(D), tok(ym.shape[-1]), tok(ys.shape[-1])] + [per_b] * 6
                 + [const(a) for a in (gmix, gmlp, gfin, bg, wg, wom, wos, wo, w1, w2)],
        out_specs=tok(D),
        out_shape=jax.ShapeDtypeStruct((B, S, D), F32),
        compiler_params=pltpu.CompilerParams(dimension_semantics=("arbitrary", "arbitrary"),
                                             vmem_limit_bytes=VMEM_LIMIT),
        name="tail",
    )(x, ym, ys, *mods, gmix, gmlp, gfin, bg, wg, wom, wos, wo, w1, w2)


def _head_slab(nope, a, b):
    r, h, _ = nope.shape
    pad = jnp.zeros((r, h, LANES - MLA_NOPE - MLA_ROPE - HALF_ROPE), nope.dtype)
    return jnp.concatenate([nope, a, b, a, pad], axis=-1).reshape(r, h * LANES)


def _rope_spread():
    m = np.zeros((LANES, 2 * LANES), np.float32)
    for term in range(3):
        for j in range(HALF_ROPE):
            m[term * MLA_ROPE + j, [ROPE_A + j, ROPE_B + j]] = 1.0
            m[term * MLA_ROPE + HALF_ROPE + j, LANES + ROPE_A + j] = -1.0
            m[term * MLA_ROPE + HALF_ROPE + j, LANES + ROPE_B + j] = 1.0
    return jnp.asarray(m, BF16)


def kernel(x, c, positions, rel_bias, ada_w, ada_b, ln_mix_g, w_in, b_gate, mla_q_norm_g, mla_kv_norm_g,
           w_uq, w_ukv, swa_sinks, w_o_mla, w_o_swa, w_o, ln_mlp_g, w_ff1, w_ff2, ln_final_g):
    B, S, D = x.shape
    assert (B, S, D) == (x.shape[0], 4096, D_MODEL) and ada_w.shape[0] == 1
    l = 0

    c_pad = jnp.pad(c, ((0, 8 - B), (0, 0)))
    mod, bias_tbl = _adaln_bias(rel_bias, c_pad, ada_w, ada_b)
    mod = mod[:B]
    mods = [m[:, None, :] for m in jnp.split(mod, 6, axis=-1)]

    wi = w_in[l].T
    o_kv = MLA_Q_RANK
    o_kr = o_kv + MLA_KV_RANK
    o_qs = o_kr + MLA_ROPE
    o_ks = o_qs + SWA_Q_HEADS * SWA_HEAD
    o_vs = o_ks + SWA_KV_HEADS * SWA_HEAD
    o_ga = o_vs + SWA_KV_HEADS * SWA_HEAD
    kr_a, kr_b = wi[o_kr:o_kr + HALF_ROPE], wi[o_kr + HALF_ROPE:o_qs]
    z = lambda n: jnp.zeros((n, D), wi.dtype)
    w1 = jnp.concatenate([
        wi[:o_kr],
        z(ROPE_A), kr_a, kr_b, kr_a, z(LANES - ROPE_A - MLA_ROPE - HALF_ROPE),
        wi[o_qs:o_ga]], axis=0).astype(BF16)
    assert w1.shape[0] == N_PROJ
    wg = wi[o_ga:].astype(BF16)

    uq = w_uq[l]
    q_nope, q_a, q_b = uq[..., :MLA_NOPE], uq[..., MLA_NOPE:MLA_NOPE + HALF_ROPE], uq[..., MLA_NOPE + HALF_ROPE:]
    wq = _head_slab(q_nope, q_a, q_b).astype(BF16)
    ukv = w_ukv[l]
    k_nope, v_up = ukv[..., :MLA_NOPE], ukv[..., MLA_NOPE:]
    zr = jnp.zeros(k_nope.shape[:2] + (HALF_ROPE,), ukv.dtype)
    wkv = jnp.concatenate([_head_slab(k_nope, zr, zr),
                           v_up.reshape(MLA_KV_RANK, MLA_HEADS * MLA_V)], axis=1).astype(BF16)

    inv = (ROPE_THETA ** (-jnp.arange(HALF_ROPE, dtype=F32) / HALF_ROPE))[:, None]

    qm, km, vm, qs, ks, vs, wom, wos, wo, wf1, wf2 = _proj(
        x, mods[0], mods[1], ln_mix_g[l][None, :], positions[:, None, :], inv, _rope_spread(), w1,
        mla_q_norm_g[l][None, :], mla_kv_norm_g[l][None, :], wq, wkv, (w_o_mla, w_o_swa, w_o, w_ff1, w_ff2))

    y_mla = _mla(qm, km, vm)
    y_swa = _swa(swa_sinks[l], qs, ks, vs, bias_tbl)

    return _tail(x, y_mla, y_swa, mods, ln_mix_g[l][None, :], ln_mlp_g[l][None, :], ln_final_g[None, :],
                 b_gate[l][None, :], wg, wom, wos, wo, wf1, wf2)
```

```python
import functools
import math

import jax
import jax.numpy as jnp
import numpy as np
from jax import lax
from jax.experimental import pallas as pl
from jax.experimental.pallas import tpu as pltpu

F32 = jnp.float32
BF16 = jnp.bfloat16

D_MODEL = 1024
MLA_HEADS = 8
MLA_Q_RANK = 256
MLA_KV_RANK = 128
MLA_NOPE = 64
MLA_ROPE = 32
MLA_V = 64
SWA_Q_HEADS = 16
SWA_KV_HEADS = 2
SWA_HEAD = 64
WINDOW = 128
REL_BUCKETS = 32
REL_MAX_DIST = 128
D_FF = 4 * D_MODEL
ROPE_THETA = 10000.0
EPS = 1e-6

LANES = 128
SCORE_PAD = LANES
HALF_ROPE = MLA_ROPE // 2
ROPE_A = MLA_NOPE
ROPE_B = MLA_NOPE + HALF_ROPE
N_PAIRS = SWA_Q_HEADS // 2
SWA_GROUP = SWA_Q_HEADS // SWA_KV_HEADS
SWA_STEP_BLOCKS = 16
SWA_LOOKAHEAD = 2
LOG2E = math.log2(math.e)

C_QLAT = 0
C_KVLAT = C_QLAT + MLA_Q_RANK
C_KR = C_KVLAT + MLA_KV_RANK
C_QS = C_KR + LANES
C_KS = C_QS + SWA_Q_HEADS * SWA_HEAD
C_VS = C_KS + SWA_KV_HEADS * SWA_HEAD
N_PROJ = C_VS + SWA_KV_HEADS * SWA_HEAD

VMEM_LIMIT = 56 * 1024 * 1024

TM_PROJ = 512
PROJ_SUB = 256
TQ_MLA = 1024
TK_MLA = 256
MLA_GROUP = 4
MLA_LOOKAHEAD = 3
MLA_ACC_ROWS = MLA_V + 16
TM_TAIL = 512
TAIL_SUB = 256
FF_CHUNK = 1024


def _rms(x, g):
    return x * lax.rsqrt(jnp.mean(x * x, axis=-1, keepdims=True) + EPS) * g


def _dot(a, b):
    return jnp.dot(a, b, preferred_element_type=F32)


def _dot_nt(a, b):
    return lax.dot_general(a, b, (((1,), (1,)), ((), ())), preferred_element_type=F32)


ADALN_STEPS = 8


def _adaln_bias_kernel(rb_ref, c_ref, w_ref, b_ref, o_ref, tbl_ref):
    c = c_ref[...]
    act = (c * jax.nn.sigmoid(c)).astype(BF16)
    o_ref[...] = _dot(act, w_ref[...].astype(BF16)) + b_ref[...]

    a = lax.broadcasted_iota(jnp.int32, (2 * WINDOW, WINDOW), 1)
    b = lax.broadcasted_iota(jnp.int32, (2 * WINDOW, WINDOW), 0)
    dist = WINDOW + a - b
    n = jnp.maximum(dist, 0)
    max_exact = REL_BUCKETS // 2
    nf = jnp.maximum(n, 1).astype(F32)
    large = max_exact + jnp.floor(jnp.log(nf / max_exact) / math.log(REL_MAX_DIST / max_exact)
                                  * (REL_BUCKETS - max_exact)).astype(jnp.int32)
    large = jnp.minimum(large, REL_BUCKETS - 1)
    bucket = jnp.where(n < max_exact, n, large)
    band_ok = (dist >= 0) & (dist < WINDOW)
    has_prev = b >= WINDOW
    neg = jnp.float32(-jnp.inf)
    heads = SWA_Q_HEADS // ADALN_STEPS
    for h in range(heads):
        head = pl.program_id(0) * heads + h
        bias = jnp.zeros((2 * WINDOW, WINDOW), F32)
        for k in range(REL_BUCKETS):
            bias = jnp.where(bucket == k, rb_ref[head, k], bias)
        tbl_ref[0, h] = jnp.where(band_ok & has_prev, bias * LOG2E, neg)
        tbl_ref[1, h] = jnp.where(band_ok, bias * LOG2E, neg)


def _adaln_bias(rel_bias, c_pad, ada_w, ada_b):
    rows = c_pad.shape[0]
    n_out = ada_w.shape[2]
    cols = n_out // ADALN_STEPS
    heads = SWA_Q_HEADS // ADALN_STEPS
    return pl.pallas_call(
        _adaln_bias_kernel,
        grid=(ADALN_STEPS,),
        in_specs=[pl.BlockSpec(memory_space=pltpu.SMEM),
                  pl.BlockSpec((rows, D_MODEL), lambda j: (0, 0)),
                  pl.BlockSpec((None, D_MODEL, cols), lambda j: (0, 0, j)),
                  pl.BlockSpec((1, cols), lambda j: (0, j))],
        out_specs=[pl.BlockSpec((rows, cols), lambda j: (0, j)),
                   pl.BlockSpec((2, heads, 2 * WINDOW, WINDOW), lambda j: (0, j, 0, 0))],
        out_shape=[jax.ShapeDtypeStruct((rows, n_out), F32),
                   jax.ShapeDtypeStruct((2, SWA_Q_HEADS, 2 * WINDOW, WINDOW), F32)],
        compiler_params=pltpu.CompilerParams(dimension_semantics=("arbitrary",)),
        name="adaln_bias",
    )(rel_bias, c_pad, ada_w, ada_b)


def _proj_kernel(cast_pieces, x_ref, sh_ref, sc_ref, g_ref, pos_ref, inv_ref, spread_ref, w1_ref, gq_ref, gkv_ref,
                 wq_ref, wkv_ref, *rest):
    n_in = sum(cast_pieces)
    cast_in, (qm_ref, km_ref, vm_ref, qs_ref, ks_ref, vs_ref), cast_out = (
        rest[:n_in], rest[n_in:n_in + 6], rest[n_in + 6:])
    srcs = iter(cast_in)
    for pieces, dst in zip(cast_pieces, cast_out):
        rows = dst.shape[0] // pieces
        for k in range(pieces):
            dst[k * rows:(k + 1) * rows, :] = next(srcs)[...].astype(BF16)
    subs = [slice(k * PROJ_SUB, (k + 1) * PROJ_SUB) for k in range(TM_PROJ // PROJ_SUB)]
    ps = []
    for r in subs:
        h = _rms(x_ref[r, :], g_ref[...]) * (1 + sc_ref[...]) + sh_ref[...]
        ps.append(_dot_nt(h.astype(BF16), w1_ref[...]))

    ang = inv_ref[...] * pos_ref[...].astype(F32)
    cs = jnp.concatenate([jnp.cos(ang), jnp.sin(ang)], axis=0)
    hi = cs.astype(BF16).astype(F32)
    mid = (cs - hi).astype(BF16).astype(F32)
    lo = (cs - hi - mid).astype(BF16).astype(F32)
    terms = jnp.concatenate([hi, mid, lo, jnp.zeros_like(hi)], axis=0)
    tables = _dot(terms.T.astype(BF16), spread_ref[...])
    lane_t = lax.broadcasted_iota(jnp.int32, (PROJ_SUB, LANES), 1)
    scale = (MLA_NOPE + MLA_ROPE) ** -0.5 * LOG2E
    half = MLA_HEADS * LANES

    ups = []
    for p in ps:
        qn = _rms(p[:, C_QLAT:C_QLAT + MLA_Q_RANK], gq_ref[...]).astype(BF16)
        kvn = _rms(p[:, C_KVLAT:C_KVLAT + MLA_KV_RANK], gkv_ref[...]).astype(BF16)
        ups.append((_dot(qn, wq_ref[...]), _dot(kvn, wkv_ref[...])))

    for k, (r, p, (qq, kk)) in enumerate(zip(subs, ps, ups)):
        cos_f, sin_f = tables[r, :LANES], tables[r, LANES:]
        cos_q = jnp.where(lane_t < MLA_NOPE, 1.0, cos_f)

        def rope(t, cos):
            return t * cos + pltpu.roll(t, LANES - HALF_ROPE, 1) * sin_f

        for hd in range(MLA_HEADS):
            qm_ref[hd, r, :] = (rope(qq[:, hd * LANES:(hd + 1) * LANES], cos_q) * scale).astype(BF16)
        k_pe = rope(p[:, C_KR:C_KR + LANES], cos_f)
        for hd in range(MLA_HEADS):
            km_ref[hd, r, :] = (kk[:, hd * LANES:(hd + 1) * LANES] + k_pe).astype(BF16)
        for pr in range(MLA_HEADS // 2):
            v_t = kk[:, half + pr * LANES:half + (pr + 1) * LANES].T.astype(BF16)
            for t in range(PROJ_SUB // TK_MLA):
                vm_ref[pr, k * (PROJ_SUB // TK_MLA) + t] = v_t[:, t * TK_MLA:(t + 1) * TK_MLA]
        qs_ref[r, :] = (p[:, C_QS:C_KS] * (SWA_HEAD ** -0.5 * LOG2E)).astype(BF16)
        ks_ref[r, :] = p[:, C_KS:C_VS].astype(BF16)
        vs_ref[r, :] = p[:, C_VS:N_PROJ].astype(BF16)


def _proj(x, sh1, sc1, ln_g, pos3, inv, spread, w1, gq, gkv, wq, wkv, tail_weights):
    B, S, D = x.shape
    tm = TM_PROJ
    steps = B * (S // tm)
    step = lambda b, i: b * (S // tm) + i
    const = lambda shape: pl.BlockSpec(shape, lambda b, i: (0,) * len(shape))
    cast_in, cast_args, cast_out, cast_shapes, cast_pieces = [], [], [], [], []
    for w, row0, rows in tail_weights:
        slab = rows // steps
        piece = math.gcd(row0, slab) if row0 else slab
        assert rows % steps == 0 and piece % 16 == 0
        for k in range(slab // piece):
            cast_in.append(pl.BlockSpec((None, piece, w.shape[2]),
                                        lambda b, i, o=row0 // piece + k, n=slab // piece: (0, o + step(b, i) * n, 0)))
            cast_args.append(w)
        cast_out.append(pl.BlockSpec((slab, w.shape[2]), lambda b, i: (step(b, i), 0)))
        cast_shapes.append(jax.ShapeDtypeStruct((rows, w.shape[2]), BF16))
        cast_pieces.append(slab // piece)
    return pl.pallas_call(
        functools.partial(_proj_kernel, tuple(cast_pieces)),
        grid=(B, S // tm),
        in_specs=[pl.BlockSpec((None, tm, D), lambda b, i: (b, i, 0)),
                  pl.BlockSpec((None, 1, D), lambda b, i: (b, 0, 0)),
                  pl.BlockSpec((None, 1, D), lambda b, i: (b, 0, 0)),
                  const((1, D)),
                  pl.BlockSpec((None, 1, tm), lambda b, i: (b, 0, i)),
                  const((HALF_ROPE, 1)), const(spread.shape),
                  const(w1.shape), const(gq.shape), const(gkv.shape), const(wq.shape), const(wkv.shape)] + cast_in,
        out_specs=[pl.BlockSpec((None, MLA_HEADS, tm, LANES), lambda b, i: (b, 0, i, 0)),
                   pl.BlockSpec((None, MLA_HEADS, tm, LANES), lambda b, i: (b, 0, i, 0)),
                   pl.BlockSpec((None, MLA_HEADS // 2, tm // TK_MLA, LANES, TK_MLA), lambda b, i: (b, 0, i, 0, 0)),
                   pl.BlockSpec((None, tm, SWA_Q_HEADS * SWA_HEAD), lambda b, i: (b, i, 0)),
                   pl.BlockSpec((None, tm, LANES), lambda b, i: (b, i, 0)),
                   pl.BlockSpec((None, tm, LANES), lambda b, i: (b, i, 0))] + cast_out,
        out_shape=[jax.ShapeDtypeStruct((B, MLA_HEADS, S, LANES), BF16),
                   jax.ShapeDtypeStruct((B, MLA_HEADS, S, LANES), BF16),
                   jax.ShapeDtypeStruct((B, MLA_HEADS // 2, S // TK_MLA, LANES, TK_MLA), BF16),
                   jax.ShapeDtypeStruct((B, S, SWA_Q_HEADS * SWA_HEAD), BF16),
                   jax.ShapeDtypeStruct((B, S, LANES), BF16),
                   jax.ShapeDtypeStruct((B, S, LANES), BF16)]
                  + cast_shapes,
        compiler_params=pltpu.CompilerParams(dimension_semantics=("arbitrary", "arbitrary"),
                                             vmem_limit_bytes=VMEM_LIMIT),
        name="proj",
    )(x, sh1, sc1, ln_g, pos3, inv, spread, w1, gq, gkv, wq, wkv, *cast_args)


def _mla_kernel(q_ref, k_ref, vt_ref, o_ref, qt_sc, acc_sc, *st_slots):
    tq, tk = TQ_MLA, TK_MLA
    nc = tq // LANES
    i = pl.program_id(2)
    for hd in range(MLA_GROUP):
        qt_sc[hd] = q_ref[hd].astype(F32).T.astype(BF16)
    acc_sc[...] = jnp.zeros_like(acc_sc)

    per_tile = tq // tk

    ones_rows = (lax.broadcasted_iota(jnp.int32, (MLA_ACC_ROWS - MLA_V, tk), 0) == 0).astype(BF16)
    r_iota = lax.broadcasted_iota(jnp.int32, (tk, LANES), 0)
    c_iota = lax.broadcasted_iota(jnp.int32, (tk, LANES), 1)

    blocks = [(d, hd) for d in range(per_tile) for hd in range(MLA_GROUP)]
    assert len(blocks) % len(st_slots) == 0 and MLA_LOOKAHEAD <= MLA_GROUP

    def scores(t, n, first_col):
        d, hd = blocks[n]
        k = k_ref[hd, pl.ds(pl.multiple_of((t * per_tile + d) * tk, tk), tk), :]
        st_slots[n % len(st_slots)][0, :, first_col:tq] = _dot(k, qt_sc[hd, :, first_col:])

    def tile(t, ms_all, diagonal):
        first_cc = [(d * tk) // LANES if diagonal else 0 for d, _ in blocks]
        ms_all = list(ms_all)
        for n, (d, hd) in enumerate(blocks):
            ahead = n + MLA_LOOKAHEAD
            if ahead < len(blocks):
                scores(t, ahead, first_cc[ahead] * LANES)
            elif not diagonal:
                scores(t + 1, ahead - len(blocks), 0)
            ms = list(ms_all[hd])
            ps, alphas = [], []
            for cc in range(first_cc[n], nc):
                blk = st_slots[n % len(st_slots)][0, :, cc * LANES:(cc + 1) * LANES]
                if diagonal and cc * LANES < (d + 1) * tk - 1:
                    blk = jnp.where(r_iota + (d * tk - cc * LANES) <= c_iota, blk, -jnp.inf)
                m_new = jnp.maximum(ms[cc], jnp.max(blk, axis=0, keepdims=True))
                alphas.append(jnp.exp2(ms[cc] - m_new))
                ps.append(jnp.exp2(blk - m_new).astype(BF16))
                ms[cc] = m_new
            v_t = vt_ref[hd // 2, t * per_tile + d, (hd % 2) * MLA_V:(hd % 2 + 1) * MLA_V, :]
            pv = _dot(jnp.concatenate([v_t, ones_rows], axis=0), jnp.concatenate(ps, axis=1))
            rows = slice(hd * MLA_ACC_ROWS, (hd + 1) * MLA_ACC_ROWS)
            cols = slice(first_cc[n] * LANES, tq)
            acc_sc[rows, cols] = acc_sc[rows, cols] * jnp.concatenate(alphas, axis=1) + pv
            ms_all[hd] = tuple(ms)
        return tuple(ms_all)

    init = tuple(tuple(jnp.full((1, LANES), -jnp.inf, F32) for _ in range(nc)) for _ in range(MLA_GROUP))
    for n in range(MLA_LOOKAHEAD):
        scores(0, n, 0)
    ms_all = lax.fori_loop(0, i, lambda t, c: tile(t, c, False), init)
    tile(i, ms_all, True)
    outs = []
    for hd in range(MLA_GROUP):
        acc = acc_sc[hd * MLA_ACC_ROWS:(hd + 1) * MLA_ACC_ROWS, :]
        outs.append(acc[:MLA_V] / acc[MLA_V:MLA_V + 1])
    o_ref[...] = jnp.concatenate(outs, axis=0).T.astype(BF16)


def _mla(qm, km, vt):
    B, H, S, _ = qm.shape
    tq, tk, g = TQ_MLA, TK_MLA, MLA_GROUP
    return pl.pallas_call(
        _mla_kernel,
        grid=(B, H // g, S // tq),
        in_specs=[pl.BlockSpec((None, g, tq, LANES), lambda b, p, i: (b, p, i, 0)),
                  pl.BlockSpec((None, g, S, LANES), lambda b, p, i: (b, p, 0, 0)),
                  pl.BlockSpec((None, g // 2, S // tk, LANES, tk), lambda b, p, i: (b, p, 0, 0, 0))],
        out_specs=pl.BlockSpec((None, tq, g * MLA_V), lambda b, p, i: (b, i, p)),
        out_shape=jax.ShapeDtypeStruct((B, S, H * MLA_V), BF16),
        scratch_shapes=[pltpu.VMEM((g, LANES, tq), BF16), pltpu.VMEM((g * MLA_ACC_ROWS, tq), F32),
                        ] + [pltpu.VMEM((1, tk, tq + SCORE_PAD), F32)] * (MLA_LOOKAHEAD + 1),
        compiler_params=pltpu.CompilerParams(dimension_semantics=("arbitrary",) * 3,
                                             vmem_limit_bytes=VMEM_LIMIT),
        name="mla",
    )(qm, km, vt)


def _swa_kernel(sink_ref, q_ref, kp_ref, kc_ref, vp_ref, vc_ref, bias0_ref, bias_ref, o_ref, *st_slots):
    band = 2 * WINDOW
    pairs = SWA_GROUP // 2
    kband = jnp.concatenate([kp_ref[...], kc_ref[...]], axis=0).astype(F32)
    vband_t = jnp.concatenate([vp_ref[...], vc_ref[...]], axis=0).astype(F32).T
    lo = lax.broadcasted_iota(jnp.int32, kband.shape, 1) < SWA_HEAD
    kswap = pltpu.roll(kband, SWA_HEAD, 1)
    k_even = [jnp.where(lo, kband, 0.0).astype(BF16), jnp.where(lo, kswap, 0.0).astype(BF16)]
    k_odd = [jnp.where(lo, 0.0, kswap).astype(BF16), jnp.where(lo, 0.0, kband).astype(BF16)]
    ones_rows = (lax.broadcasted_iota(jnp.int32, (16, band), 0) == 0).astype(BF16)
    chains = [(j, kvh) for j in range(SWA_STEP_BLOCKS) for kvh in range(SWA_KV_HEADS)]

    def scores(n):
        j, kvh = chains[n]
        keys = slice(j * WINDOW, j * WINDOW + band)
        kcat = jnp.concatenate([k_even[kvh][keys], k_odd[kvh][keys]], axis=0)
        qstack = jnp.concatenate([q_ref[j * WINDOW:(j + 1) * WINDOW, (kvh * pairs + jj) * LANES:
                                        (kvh * pairs + jj + 1) * LANES] for jj in range(pairs)], axis=0)
        bias = bias0_ref if j == 0 else bias_ref
        st = _dot_nt(kcat, qstack)
        for parity in range(2):
            for jj in range(pairs):
                rows, cols = slice(parity * band, (parity + 1) * band), slice(jj * WINDOW, (jj + 1) * WINDOW)
                st_slots[n % len(st_slots)][0, rows, cols] = st[rows, cols] + bias[kvh * SWA_GROUP + 2 * jj + parity]

    for n in range(min(SWA_LOOKAHEAD, len(chains))):
        scores(n)
    for n, (j, kvh) in enumerate(chains):
        if n + SWA_LOOKAHEAD < len(chains):
            scores(n + SWA_LOOKAHEAD)
        st = st_slots[n % len(st_slots)].at[jnp.minimum(pl.program_id(1), 0)]
        v_t = vband_t[kvh * SWA_HEAD:(kvh + 1) * SWA_HEAD, j * WINDOW:j * WINDOW + band].astype(BF16)
        v_aug = jnp.concatenate([v_t, ones_rows], axis=0)
        halves = []
        for parity in range(2):
            ps, sink_terms = [], []
            for jj in range(pairs):
                blk = st[parity * band:(parity + 1) * band, jj * WINDOW:(jj + 1) * WINDOW]
                sink = sink_ref[kvh * SWA_GROUP + 2 * jj + parity] * LOG2E
                m = jnp.maximum(jnp.max(blk, axis=0, keepdims=True), sink)
                ps.append(jnp.exp2(blk - m).astype(BF16))
                sink_terms.append(jnp.exp2(sink - m))
            pv = _dot(v_aug, jnp.concatenate(ps, axis=1))
            halves.append(pv[:SWA_HEAD] / (pv[SWA_HEAD:SWA_HEAD + 1] + jnp.concatenate(sink_terms, axis=1)))
        o_t = jnp.concatenate(halves, axis=0)
        for jj in range(pairs):
            pair = kvh * pairs + jj
            o_ref[j * WINDOW:(j + 1) * WINDOW, pair * LANES:(pair + 1) * LANES] = (
                o_t[:, jj * WINDOW:(jj + 1) * WINDOW].T.astype(BF16))


def _swa(sinks, qs, ks, vs, bias_tbl):
    B, S, W = qs.shape
    nb = SWA_STEP_BLOCKS
    prev = lambda b, n: (b, jnp.maximum(nb * n - 1, 0), 0)
    cur = lambda b, n: (b, n, 0)
    tbl_block = (None,) + bias_tbl.shape[1:]
    return pl.pallas_call(
        _swa_kernel,
        grid=(B, S // (nb * WINDOW)),
        in_specs=[pl.BlockSpec(memory_space=pltpu.SMEM),
                  pl.BlockSpec((None, nb * WINDOW, W), cur),
                  pl.BlockSpec((None, WINDOW, LANES), prev), pl.BlockSpec((None, nb * WINDOW, LANES), cur),
                  pl.BlockSpec((None, WINDOW, LANES), prev), pl.BlockSpec((None, nb * WINDOW, LANES), cur),
                  pl.BlockSpec(tbl_block, lambda b, n: (jnp.minimum(n, 1), 0, 0, 0)),
                  pl.BlockSpec(tbl_block, lambda b, n: (1, 0, 0, 0))],
        out_specs=pl.BlockSpec((None, nb * WINDOW, W), cur),
        out_shape=jax.ShapeDtypeStruct((B, S, W), BF16),
        scratch_shapes=[pltpu.VMEM((1, 4 * WINDOW, SWA_GROUP // 2 * WINDOW + SCORE_PAD), F32)] * (SWA_LOOKAHEAD + 1),
        compiler_params=pltpu.CompilerParams(dimension_semantics=("arbitrary", "arbitrary"),
                                             vmem_limit_bytes=VMEM_LIMIT),
        name="swa",
    )(sinks, qs, ks, ks, vs, vs, bias_tbl, bias_tbl)


def _tail_kernel(x_ref, ym_ref, ys_ref, sh1_ref, sc1_ref, ga1_ref, sh2_ref, sc2_ref, ga2_ref,
                 gmix_ref, gmlp_ref, gfin_ref, bg_ref, wg_ref, wom_ref, wos_ref, wo_ref, w1_ref, w2_ref, o_ref):
    subs = [slice(k * TAIL_SUB, (k + 1) * TAIL_SUB) for k in range(TM_TAIL // TAIL_SUB)]
    n_ff = D_FF // FF_CHUNK
    xs = [x_ref[r, :] for r in subs]
    branches = [(_dot(ym_ref[r, :], wom_ref[...]), _dot(ys_ref[r, :], wos_ref[...])) for r in subs]
    hs = [(_rms(x, gmix_ref[...]) * (1 + sc1_ref[...]) + sh1_ref[...]).astype(BF16) for x in xs]
    logits = [_dot_nt(h, wg_ref[...]) for h in hs]
    x1s = []
    for x, (a, b), gl in zip(xs, branches, logits):
        gates = jax.nn.sigmoid(gl + bg_ref[...])
        merged = gates[:, :D_MODEL] * a + gates[:, D_MODEL:] * b
        x1s.append(x + ga1_ref[...] * _dot(merged.astype(BF16), wo_ref[...]))
    h2s = [(_rms(x1, gmlp_ref[...]) * (1 + sc2_ref[...]) + sh2_ref[...]).astype(BF16) for x1 in x1s]

    jobs = [(k, c) for k in range(len(subs)) for c in range(n_ff)]
    up = lambda k, c: _dot(h2s[k], w1_ref[:, c * FF_CHUNK:(c + 1) * FF_CHUNK])
    pending = {0: up(*jobs[0])}
    accs = [None] * len(subs)
    for n, (k, c) in enumerate(jobs):
        if n + 1 < len(jobs):
            pending[n + 1] = up(*jobs[n + 1])
        u = jnp.square(jnp.maximum(pending.pop(n), 0.0)).astype(BF16)
        down = _dot(u, w2_ref[c * FF_CHUNK:(c + 1) * FF_CHUNK, :])
        accs[k] = down if accs[k] is None else accs[k] + down
        if c == n_ff - 1:
            o_ref[subs[k], :] = _rms(x1s[k] + ga2_ref[...] * accs[k], gfin_ref[...])


def _tail(x, ym, ys, mods, gmix, gmlp, gfin, bg, wg, wom, wos, wo, w1, w2):
    B, S, D = x.shape
    tm = TM_TAIL
    tok = lambda width: pl.BlockSpec((None, tm, width), lambda b, i: (b, i, 0))
    per_b = pl.BlockSpec((None, 1, D), lambda b, i: (b, 0, 0))
    const = lambda a: pl.BlockSpec(a.shape, lambda b, i: (0,) * a.ndim, pipeline_mode=pl.Buffered(1))
    return pl.pallas_call(
        _tail_kernel,
        grid=(B, S // tm),
        in_specs=[tok(D), tok(ym.shape[-1]), tok(ys.shape[-1])] + [per_b] * 6
                 + [const(a) for a in (gmix, gmlp, gfin, bg, wg, wom, wos, wo, w1, w2)],
        out_specs=tok(D),
        out_shape=jax.ShapeDtypeStruct((B, S, D), F32),
        compiler_params=pltpu.CompilerParams(dimension_semantics=("arbitrary", "arbitrary"),
                                             vmem_limit_bytes=VMEM_LIMIT),
        name="tail",
    )(x, ym, ys, *mods, gmix, gmlp, gfin, bg, wg, wom, wos, wo, w1, w2)


def _head_slab(nope, a, b):
    r, h, _ = nope.shape
    pad = jnp.zeros((r, h, LANES - MLA_NOPE - MLA_ROPE - HALF_ROPE), nope.dtype)
    return jnp.concatenate([nope, a, b, a, pad], axis=-1).reshape(r, h * LANES)


def _rope_spread():
    m = np.zeros((LANES, 2 * LANES), np.float32)
    for term in range(3):
        for j in range(HALF_ROPE):
            m[term * MLA_ROPE + j, [ROPE_A + j, ROPE_B + j]] = 1.0
            m[term * MLA_ROPE + HALF_ROPE + j, LANES + ROPE_A + j] = -1.0
            m[term * MLA_ROPE + HALF_ROPE + j, LANES + ROPE_B + j] = 1.0
    return jnp.asarray(m, BF16)


def kernel(x, c, positions, rel_bias, ada_w, ada_b, ln_mix_g, w_in, b_gate, mla_q_norm_g, mla_kv_norm_g,
           w_uq, w_ukv, swa_sinks, w_o_mla, w_o_swa, w_o, ln_mlp_g, w_ff1, w_ff2, ln_final_g):
    B, S, D = x.shape
    assert (B, S, D) == (x.shape[0], 4096, D_MODEL) and ada_w.shape[0] == 1
    l = 0

    c_pad = jnp.pad(c, ((0, 8 - B), (0, 0)))
    mod, bias_tbl = _adaln_bias(rel_bias, c_pad, ada_w, ada_b)
    mod = mod[:B]
    mods = [m[:, None, :] for m in jnp.split(mod, 6, axis=-1)]

    wi = w_in[l].T
    o_kv = MLA_Q_RANK
    o_kr = o_kv + MLA_KV_RANK
    o_qs = o_kr + MLA_ROPE
    o_ks = o_qs + SWA_Q_HEADS * SWA_HEAD
    o_vs = o_ks + SWA_KV_HEADS * SWA_HEAD
    o_ga = o_vs + SWA_KV_HEADS * SWA_HEAD
    kr_a, kr_b = wi[o_kr:o_kr + HALF_ROPE], wi[o_kr + HALF_ROPE:o_qs]
    z = lambda n: jnp.zeros((n, D), wi.dtype)
    w1 = jnp.concatenate([
        wi[:o_kr],
        z(ROPE_A), kr_a, kr_b, kr_a, z(LANES - ROPE_A - MLA_ROPE - HALF_ROPE),
        wi[o_qs:o_ga]], axis=0).astype(BF16)
    assert w1.shape[0] == N_PROJ

    uq = w_uq[l]
    q_nope, q_a, q_b = uq[..., :MLA_NOPE], uq[..., MLA_NOPE:MLA_NOPE + HALF_ROPE], uq[..., MLA_NOPE + HALF_ROPE:]
    wq = _head_slab(q_nope, q_a, q_b).astype(BF16)
    ukv = w_ukv[l]
    k_nope, v_up = ukv[..., :MLA_NOPE], ukv[..., MLA_NOPE:]
    zr = jnp.zeros(k_nope.shape[:2] + (HALF_ROPE,), ukv.dtype)
    wkv = jnp.concatenate([_head_slab(k_nope, zr, zr),
                           v_up.reshape(MLA_KV_RANK, MLA_HEADS * MLA_V)], axis=1).astype(BF16)

    inv = (ROPE_THETA ** (-jnp.arange(HALF_ROPE, dtype=F32) / HALF_ROPE))[:, None]

    whole = lambda w: (w, 0, w.shape[1])
    gate_rows = (wi[None], o_ga, 2 * D)
    qm, km, vm, qs, ks, vs, wom, wos, wo, wf1, wf2, wg = _proj(
        x, mods[0], mods[1], ln_mix_g[l][None, :], positions[:, None, :], inv, _rope_spread(), w1,
        mla_q_norm_g[l][None, :], mla_kv_norm_g[l][None, :], wq, wkv,
        [whole(w_o_mla), whole(w_o_swa), whole(w_o), whole(w_ff1), whole(w_ff2), gate_rows])

    y_mla = _mla(qm, km, vm)
    y_swa = _swa(swa_sinks[l], qs, ks, vs, bias_tbl)

    return _tail(x, y_mla, y_swa, mods, ln_mix_g[l][None, :], ln_mlp_g[l][None, :], ln_final_g[None, :],
                 b_gate[l][None, :], wg, wom, wos, wo, wf1, wf2)
```

```python
import functools
import math

import jax
import jax.numpy as jnp
import numpy as np
from jax import lax
from jax.experimental import pallas as pl
from jax.experimental.pallas import tpu as pltpu

F32 = jnp.float32
BF16 = jnp.bfloat16

D_MODEL = 1024
MLA_HEADS = 8
MLA_Q_RANK = 256
MLA_KV_RANK = 128
MLA_NOPE = 64
MLA_ROPE = 32
MLA_V = 64
SWA_Q_HEADS = 16
SWA_KV_HEADS = 2
SWA_HEAD = 64
WINDOW = 128
REL_BUCKETS = 32
REL_MAX_DIST = 128
D_FF = 4 * D_MODEL
ROPE_THETA = 10000.0
EPS = 1e-6

LANES = 128
SCORE_PAD = LANES
HALF_ROPE = MLA_ROPE // 2
ROPE_A = MLA_NOPE
ROPE_B = MLA_NOPE + HALF_ROPE
N_PAIRS = SWA_Q_HEADS // 2
SWA_GROUP = SWA_Q_HEADS // SWA_KV_HEADS
SWA_STEP_BLOCKS = 16
SWA_LOOKAHEAD = 2
LOG2E = math.log2(math.e)

C_QLAT = 0
C_KVLAT = C_QLAT + MLA_Q_RANK
C_KR = C_KVLAT + MLA_KV_RANK
C_QS = C_KR + LANES
C_KS = C_QS + SWA_Q_HEADS * SWA_HEAD
C_VS = C_KS + SWA_KV_HEADS * SWA_HEAD
N_PROJ = C_VS + SWA_KV_HEADS * SWA_HEAD

VMEM_LIMIT = 56 * 1024 * 1024

TM_PROJ = 1024
PROJ_SUB = 512
TQ_MLA = 1024
TK_MLA = 256
MLA_GROUP = 4
MLA_LOOKAHEAD = 3
MLA_ACC_ROWS = MLA_V + 16
TM_TAIL = 512
TAIL_SUB = 512
FF_CHUNK = 1024


def _rms(x, g):
    return x * lax.rsqrt(jnp.mean(x * x, axis=-1, keepdims=True) + EPS) * g


def _dot(a, b):
    return jnp.dot(a, b, preferred_element_type=F32)


def _dot_nt(a, b):
    return lax.dot_general(a, b, (((1,), (1,)), ((), ())), preferred_element_type=F32)


ADALN_STEPS = 8


def _adaln_bias_kernel(rb_ref, c_ref, w_ref, b_ref, o_ref, tbl_ref):
    c = c_ref[...]
    act = (c * jax.nn.sigmoid(c)).astype(BF16)
    o_ref[...] = _dot(act, w_ref[...].astype(BF16)) + b_ref[...]

    a = lax.broadcasted_iota(jnp.int32, (2 * WINDOW, WINDOW), 1)
    b = lax.broadcasted_iota(jnp.int32, (2 * WINDOW, WINDOW), 0)
    dist = WINDOW + a - b
    n = jnp.maximum(dist, 0)
    max_exact = REL_BUCKETS // 2
    nf = jnp.maximum(n, 1).astype(F32)
    large = max_exact + jnp.floor(jnp.log(nf / max_exact) / math.log(REL_MAX_DIST / max_exact)
                                  * (REL_BUCKETS - max_exact)).astype(jnp.int32)
    large = jnp.minimum(large, REL_BUCKETS - 1)
    bucket = jnp.where(n < max_exact, n, large)
    band_ok = (dist >= 0) & (dist < WINDOW)
    has_prev = b >= WINDOW
    neg = jnp.float32(-jnp.inf)
    heads = SWA_Q_HEADS // ADALN_STEPS
    for h in range(heads):
        head = pl.program_id(0) * heads + h
        bias = jnp.zeros((2 * WINDOW, WINDOW), F32)
        for k in range(REL_BUCKETS):
            bias = jnp.where(bucket == k, rb_ref[head, k], bias)
        tbl_ref[0, h] = jnp.where(band_ok & has_prev, bias * LOG2E, neg)
        tbl_ref[1, h] = jnp.where(band_ok, bias * LOG2E, neg)


def _adaln_bias(rel_bias, c_pad, ada_w, ada_b):
    rows = c_pad.shape[0]
    n_out = ada_w.shape[2]
    cols = n_out // ADALN_STEPS
    heads = SWA_Q_HEADS // ADALN_STEPS
    return pl.pallas_call(
        _adaln_bias_kernel,
        grid=(ADALN_STEPS,),
        in_specs=[pl.BlockSpec(memory_space=pltpu.SMEM),
                  pl.BlockSpec((rows, D_MODEL), lambda j: (0, 0)),
                  pl.BlockSpec((None, D_MODEL, cols), lambda j: (0, 0, j)),
                  pl.BlockSpec((1, cols), lambda j: (0, j))],
        out_specs=[pl.BlockSpec((rows, cols), lambda j: (0, j)),
                   pl.BlockSpec((2, heads, 2 * WINDOW, WINDOW), lambda j: (0, j, 0, 0))],
        out_shape=[jax.ShapeDtypeStruct((rows, n_out), F32),
                   jax.ShapeDtypeStruct((2, SWA_Q_HEADS, 2 * WINDOW, WINDOW), F32)],
        compiler_params=pltpu.CompilerParams(dimension_semantics=("arbitrary",)),
        name="adaln_bias",
    )(rel_bias, c_pad, ada_w, ada_b)


def _proj_kernel(cast_pieces, x_ref, sh_ref, sc_ref, g_ref, pos_ref, inv_ref, spread_ref, w1_ref, gq_ref, gkv_ref,
                 wq_ref, wkv_ref, *rest):
    n_in = sum(cast_pieces)
    cast_in, (qm_ref, km_ref, vm_ref, qs_ref, ks_ref, vs_ref), cast_out = (
        rest[:n_in], rest[n_in:n_in + 6], rest[n_in + 6:])
    srcs = iter(cast_in)
    for pieces, dst in zip(cast_pieces, cast_out):
        rows = dst.shape[0] // pieces
        for k in range(pieces):
            dst[k * rows:(k + 1) * rows, :] = next(srcs)[...].astype(BF16)
    subs = [slice(k * PROJ_SUB, (k + 1) * PROJ_SUB) for k in range(TM_PROJ // PROJ_SUB)]
    ps = []
    for r in subs:
        h = _rms(x_ref[r, :], g_ref[...]) * (1 + sc_ref[...]) + sh_ref[...]
        ps.append(_dot_nt(h.astype(BF16), w1_ref[...]))

    ang = inv_ref[...] * pos_ref[...].astype(F32)
    cs = jnp.concatenate([jnp.cos(ang), jnp.sin(ang)], axis=0)
    hi = cs.astype(BF16).astype(F32)
    mid = (cs - hi).astype(BF16).astype(F32)
    lo = (cs - hi - mid).astype(BF16).astype(F32)
    terms = jnp.concatenate([hi, mid, lo, jnp.zeros_like(hi)], axis=0)
    tables = _dot(terms.T.astype(BF16), spread_ref[...])
    lane_t = lax.broadcasted_iota(jnp.int32, (PROJ_SUB, LANES), 1)
    scale = (MLA_NOPE + MLA_ROPE) ** -0.5 * LOG2E
    half = MLA_HEADS * LANES

    ups = []
    for p in ps:
        qn = _rms(p[:, C_QLAT:C_QLAT + MLA_Q_RANK], gq_ref[...]).astype(BF16)
        kvn = _rms(p[:, C_KVLAT:C_KVLAT + MLA_KV_RANK], gkv_ref[...]).astype(BF16)
        ups.append((_dot(qn, wq_ref[...]), _dot(kvn, wkv_ref[...])))

    for k, (r, p, (qq, kk)) in enumerate(zip(subs, ps, ups)):
        cos_f, sin_f = tables[r, :LANES], tables[r, LANES:]
        cos_q = jnp.where(lane_t < MLA_NOPE, 1.0, cos_f)

        def rope(t, cos):
            return t * cos + pltpu.roll(t, LANES - HALF_ROPE, 1) * sin_f

        for hd in range(MLA_HEADS):
            qm_ref[hd, r, :] = (rope(qq[:, hd * LANES:(hd + 1) * LANES], cos_q) * scale).astype(BF16)
        k_pe = rope(p[:, C_KR:C_KR + LANES], cos_f)
        for hd in range(MLA_HEADS):
            km_ref[hd, r, :] = (kk[:, hd * LANES:(hd + 1) * LANES] + k_pe).astype(BF16)
        for pr in range(MLA_HEADS // 2):
            v_t = kk[:, half + pr * LANES:half + (pr + 1) * LANES].T.astype(BF16)
            for t in range(PROJ_SUB // TK_MLA):
                vm_ref[pr, k * (PROJ_SUB // TK_MLA) + t] = v_t[:, t * TK_MLA:(t + 1) * TK_MLA]
        qs_ref[r, :] = (p[:, C_QS:C_KS] * (SWA_HEAD ** -0.5 * LOG2E)).astype(BF16)
        ks_ref[r, :] = p[:, C_KS:C_VS].astype(BF16)
        vs_ref[r, :] = p[:, C_VS:N_PROJ].astype(BF16)


def _proj(x, sh1, sc1, ln_g, pos3, inv, spread, w1, gq, gkv, wq, wkv, tail_weights):
    B, S, D = x.shape
    tm = TM_PROJ
    steps = B * (S // tm)
    step = lambda b, i: b * (S // tm) + i
    const = lambda shape: pl.BlockSpec(shape, lambda b, i: (0,) * len(shape))
    cast_in, cast_args, cast_out, cast_shapes, cast_pieces = [], [], [], [], []
    for w, row0, rows in tail_weights:
        slab = rows // steps
        piece = math.gcd(row0, slab) if row0 else slab
        assert rows % steps == 0 and piece % 16 == 0
        for k in range(slab // piece):
            cast_in.append(pl.BlockSpec((None, piece, w.shape[2]),
                                        lambda b, i, o=row0 // piece + k, n=slab // piece: (0, o + step(b, i) * n, 0)))
            cast_args.append(w)
        cast_out.append(pl.BlockSpec((slab, w.shape[2]), lambda b, i: (step(b, i), 0)))
        cast_shapes.append(jax.ShapeDtypeStruct((rows, w.shape[2]), BF16))
        cast_pieces.append(slab // piece)
    return pl.pallas_call(
        functools.partial(_proj_kernel, tuple(cast_pieces)),
        grid=(B, S // tm),
        in_specs=[pl.BlockSpec((None, tm, D), lambda b, i: (b, i, 0)),
                  pl.BlockSpec((None, 1, D), lambda b, i: (b, 0, 0)),
                  pl.BlockSpec((None, 1, D), lambda b, i: (b, 0, 0)),
                  const((1, D)),
                  pl.BlockSpec((None, 1, tm), lambda b, i: (b, 0, i)),
                  const((HALF_ROPE, 1)), const(spread.shape),
                  const(w1.shape), const(gq.shape), const(gkv.shape), const(wq.shape), const(wkv.shape)] + cast_in,
        out_specs=[pl.BlockSpec((None, MLA_HEADS, tm, LANES), lambda b, i: (b, 0, i, 0)),
                   pl.BlockSpec((None, MLA_HEADS, tm, LANES), lambda b, i: (b, 0, i, 0)),
                   pl.BlockSpec((None, MLA_HEADS // 2, tm // TK_MLA, LANES, TK_MLA), lambda b, i: (b, 0, i, 0, 0)),
                   pl.BlockSpec((None, tm, SWA_Q_HEADS * SWA_HEAD), lambda b, i: (b, i, 0)),
                   pl.BlockSpec((None, tm, LANES), lambda b, i: (b, i, 0)),
                   pl.BlockSpec((None, tm, LANES), lambda b, i: (b, i, 0))] + cast_out,
        out_shape=[jax.ShapeDtypeStruct((B, MLA_HEADS, S, LANES), BF16),
                   jax.ShapeDtypeStruct((B, MLA_HEADS, S, LANES), BF16),
                   jax.ShapeDtypeStruct((B, MLA_HEADS // 2, S // TK_MLA, LANES, TK_MLA), BF16),
                   jax.ShapeDtypeStruct((B, S, SWA_Q_HEADS * SWA_HEAD), BF16),
                   jax.ShapeDtypeStruct((B, S, LANES), BF16),
                   jax.ShapeDtypeStruct((B, S, LANES), BF16)]
                  + cast_shapes,
        compiler_params=pltpu.CompilerParams(dimension_semantics=("arbitrary", "arbitrary"),
                                             vmem_limit_bytes=VMEM_LIMIT),
        name="proj",
    )(x, sh1, sc1, ln_g, pos3, inv, spread, w1, gq, gkv, wq, wkv, *cast_args)


def _mla_kernel(q_ref, k_ref, vt_ref, o_ref, qt_sc, acc_sc, *st_slots):
    tq, tk = TQ_MLA, TK_MLA
    nc = tq // LANES
    i = pl.program_id(2)
    for hd in range(MLA_GROUP):
        qt_sc[hd] = q_ref[hd].astype(F32).T.astype(BF16)
    acc_sc[...] = jnp.zeros_like(acc_sc)

    per_tile = tq // tk

    ones_rows = (lax.broadcasted_iota(jnp.int32, (MLA_ACC_ROWS - MLA_V, tk), 0) == 0).astype(BF16)
    r_iota = lax.broadcasted_iota(jnp.int32, (tk, LANES), 0)
    c_iota = lax.broadcasted_iota(jnp.int32, (tk, LANES), 1)

    blocks = [(d, hd) for d in range(per_tile) for hd in range(MLA_GROUP)]
    assert len(blocks) % len(st_slots) == 0 and MLA_LOOKAHEAD <= MLA_GROUP

    def scores(t, n, first_col):
        d, hd = blocks[n]
        k = k_ref[hd, pl.ds(pl.multiple_of((t * per_tile + d) * tk, tk), tk), :]
        st_slots[n % len(st_slots)][0, :, first_col:tq] = _dot(k, qt_sc[hd, :, first_col:])

    def tile(t, ms_all, diagonal):
        first_cc = [(d * tk) // LANES if diagonal else 0 for d, _ in blocks]
        ms_all = list(ms_all)
        for n, (d, hd) in enumerate(blocks):
            ahead = n + MLA_LOOKAHEAD
            if ahead < len(blocks):
                scores(t, ahead, first_cc[ahead] * LANES)
            elif not diagonal:
                scores(t + 1, ahead - len(blocks), 0)
            ms = list(ms_all[hd])
            ps, alphas = [], []
            for cc in range(first_cc[n], nc):
                blk = st_slots[n % len(st_slots)][0, :, cc * LANES:(cc + 1) * LANES]
                if diagonal and cc * LANES < (d + 1) * tk - 1:
                    blk = jnp.where(r_iota + (d * tk - cc * LANES) <= c_iota, blk, -jnp.inf)
                m_new = jnp.maximum(ms[cc], jnp.max(blk, axis=0, keepdims=True))
                alphas.append(jnp.exp2(ms[cc] - m_new))
                ps.append(jnp.exp2(blk - m_new).astype(BF16))
                ms[cc] = m_new
            v_t = vt_ref[hd // 2, t * per_tile + d, (hd % 2) * MLA_V:(hd % 2 + 1) * MLA_V, :]
            pv = _dot(jnp.concatenate([v_t, ones_rows], axis=0), jnp.concatenate(ps, axis=1))
            rows = slice(hd * MLA_ACC_ROWS, (hd + 1) * MLA_ACC_ROWS)
            cols = slice(first_cc[n] * LANES, tq)
            acc_sc[rows, cols] = acc_sc[rows, cols] * jnp.concatenate(alphas, axis=1) + pv
            ms_all[hd] = tuple(ms)
        return tuple(ms_all)

    init = tuple(tuple(jnp.full((1, LANES), -jnp.inf, F32) for _ in range(nc)) for _ in range(MLA_GROUP))
    for n in range(MLA_LOOKAHEAD):
        scores(0, n, 0)
    ms_all = lax.fori_loop(0, i, lambda t, c: tile(t, c, False), init)
    tile(i, ms_all, True)
    outs = []
    for hd in range(MLA_GROUP):
        acc = acc_sc[hd * MLA_ACC_ROWS:(hd + 1) * MLA_ACC_ROWS, :]
        outs.append(acc[:MLA_V] / acc[MLA_V:MLA_V + 1])
    o_ref[...] = jnp.concatenate(outs, axis=0).T.astype(BF16)


def _mla(qm, km, vt):
    B, H, S, _ = qm.shape
    tq, tk, g = TQ_MLA, TK_MLA, MLA_GROUP
    return pl.pallas_call(
        _mla_kernel,
        grid=(B, H // g, S // tq),
        in_specs=[pl.BlockSpec((None, g, tq, LANES), lambda b, p, i: (b, p, i, 0)),
                  pl.BlockSpec((None, g, S, LANES), lambda b, p, i: (b, p, 0, 0)),
                  pl.BlockSpec((None, g // 2, S // tk, LANES, tk), lambda b, p, i: (b, p, 0, 0, 0))],
        out_specs=pl.BlockSpec((None, tq, g * MLA_V), lambda b, p, i: (b, i, p)),
        out_shape=jax.ShapeDtypeStruct((B, S, H * MLA_V), BF16),
        scratch_shapes=[pltpu.VMEM((g, LANES, tq), BF16), pltpu.VMEM((g * MLA_ACC_ROWS, tq), F32),
                        ] + [pltpu.VMEM((1, tk, tq + SCORE_PAD), F32)] * (MLA_LOOKAHEAD + 1),
        compiler_params=pltpu.CompilerParams(dimension_semantics=("arbitrary",) * 3,
                                             vmem_limit_bytes=VMEM_LIMIT),
        name="mla",
    )(qm, km, vt)


def _swa_kernel(sink_ref, q_ref, kp_ref, kc_ref, vp_ref, vc_ref, bias0_ref, bias_ref, o_ref, *st_slots):
    band = 2 * WINDOW
    pairs = SWA_GROUP // 2
    kband = jnp.concatenate([kp_ref[...], kc_ref[...]], axis=0).astype(F32)
    vband_t = jnp.concatenate([vp_ref[...], vc_ref[...]], axis=0).astype(F32).T
    lo = lax.broadcasted_iota(jnp.int32, kband.shape, 1) < SWA_HEAD
    kswap = pltpu.roll(kband, SWA_HEAD, 1)
    k_even = [jnp.where(lo, kband, 0.0).astype(BF16), jnp.where(lo, kswap, 0.0).astype(BF16)]
    k_odd = [jnp.where(lo, 0.0, kswap).astype(BF16), jnp.where(lo, 0.0, kband).astype(BF16)]
    ones_rows = (lax.broadcasted_iota(jnp.int32, (16, band), 0) == 0).astype(BF16)
    chains = [(j, kvh) for j in range(SWA_STEP_BLOCKS) for kvh in range(SWA_KV_HEADS)]

    def scores(n):
        j, kvh = chains[n]
        keys = slice(j * WINDOW, j * WINDOW + band)
        kcat = jnp.concatenate([k_even[kvh][keys], k_odd[kvh][keys]], axis=0)
        qstack = jnp.concatenate([q_ref[j * WINDOW:(j + 1) * WINDOW, (kvh * pairs + jj) * LANES:
                                        (kvh * pairs + jj + 1) * LANES] for jj in range(pairs)], axis=0)
        bias = bias0_ref if j == 0 else bias_ref
        st = _dot_nt(kcat, qstack)
        for parity in range(2):
            for jj in range(pairs):
                rows, cols = slice(parity * band, (parity + 1) * band), slice(jj * WINDOW, (jj + 1) * WINDOW)
                st_slots[n % len(st_slots)][0, rows, cols] = st[rows, cols] + bias[kvh * SWA_GROUP + 2 * jj + parity]

    for n in range(min(SWA_LOOKAHEAD, len(chains))):
        scores(n)
    for n, (j, kvh) in enumerate(chains):
        if n + SWA_LOOKAHEAD < len(chains):
            scores(n + SWA_LOOKAHEAD)
        st = st_slots[n % len(st_slots)].at[jnp.minimum(pl.program_id(1), 0)]
        v_t = vband_t[kvh * SWA_HEAD:(kvh + 1) * SWA_HEAD, j * WINDOW:j * WINDOW + band].astype(BF16)
        v_aug = jnp.concatenate([v_t, ones_rows], axis=0)
        halves = []
        for parity in range(2):
            ps, sink_terms = [], []
            for jj in range(pairs):
                blk = st[parity * band:(parity + 1) * band, jj * WINDOW:(jj + 1) * WINDOW]
                sink = sink_ref[kvh * SWA_GROUP + 2 * jj + parity] * LOG2E
                m = jnp.maximum(jnp.max(blk, axis=0, keepdims=True), sink)
                ps.append(jnp.exp2(blk - m).astype(BF16))
                sink_terms.append(jnp.exp2(sink - m))
            pv = _dot(v_aug, jnp.concatenate(ps, axis=1))
            halves.append(pv[:SWA_HEAD] / (pv[SWA_HEAD:SWA_HEAD + 1] + jnp.concatenate(sink_terms, axis=1)))
        o_t = jnp.concatenate(halves, axis=0)
        for jj in range(pairs):
            pair = kvh * pairs + jj
            o_ref[j * WINDOW:(j + 1) * WINDOW, pair * LANES:(pair + 1) * LANES] = (
                o_t[:, jj * WINDOW:(jj + 1) * WINDOW].T.astype(BF16))


def _swa(sinks, qs, ks, vs, bias_tbl):
    B, S, W = qs.shape
    nb = SWA_STEP_BLOCKS
    prev = lambda b, n: (b, jnp.maximum(nb * n - 1, 0), 0)
    cur = lambda b, n: (b, n, 0)
    tbl_block = (None,) + bias_tbl.shape[1:]
    return pl.pallas_call(
        _swa_kernel,
        grid=(B, S // (nb * WINDOW)),
        in_specs=[pl.BlockSpec(memory_space=pltpu.SMEM),
                  pl.BlockSpec((None, nb * WINDOW, W), cur),
                  pl.BlockSpec((None, WINDOW, LANES), prev), pl.BlockSpec((None, nb * WINDOW, LANES), cur),
                  pl.BlockSpec((None, WINDOW, LANES), prev), pl.BlockSpec((None, nb * WINDOW, LANES), cur),
                  pl.BlockSpec(tbl_block, lambda b, n: (jnp.minimum(n, 1), 0, 0, 0)),
                  pl.BlockSpec(tbl_block, lambda b, n: (1, 0, 0, 0))],
        out_specs=pl.BlockSpec((None, nb * WINDOW, W), cur),
        out_shape=jax.ShapeDtypeStruct((B, S, W), BF16),
        scratch_shapes=[pltpu.VMEM((1, 4 * WINDOW, SWA_GROUP // 2 * WINDOW + SCORE_PAD), F32)] * (SWA_LOOKAHEAD + 1),
        compiler_params=pltpu.CompilerParams(dimension_semantics=("arbitrary", "arbitrary"),
                                             vmem_limit_bytes=VMEM_LIMIT),
        name="swa",
    )(sinks, qs, ks, ks, vs, vs, bias_tbl, bias_tbl)


def _tail_kernel(x_ref, ym_ref, ys_ref, sh1_ref, sc1_ref, ga1_ref, sh2_ref, sc2_ref, ga2_ref,
                 gmix_ref, gmlp_ref, gfin_ref, bg_ref, wg_ref, wom_ref, wos_ref, wo_ref, w1_ref, w2_ref, o_ref):
    subs = [slice(k * TAIL_SUB, (k + 1) * TAIL_SUB) for k in range(TM_TAIL // TAIL_SUB)]
    n_ff = D_FF // FF_CHUNK
    xs = [x_ref[r, :] for r in subs]
    branches = [(_dot(ym_ref[r, :], wom_ref[...]), _dot(ys_ref[r, :], wos_ref[...])) for r in subs]
    hs = [(_rms(x, gmix_ref[...]) * (1 + sc1_ref[...]) + sh1_ref[...]).astype(BF16) for x in xs]
    logits = [_dot_nt(h, wg_ref[...]) for h in hs]
    x1s = []
    for x, (a, b), gl in zip(xs, branches, logits):
        gates = jax.nn.sigmoid(gl + bg_ref[...])
        merged = gates[:, :D_MODEL] * a + gates[:, D_MODEL:] * b
        x1s.append(x + ga1_ref[...] * _dot(merged.astype(BF16), wo_ref[...]))
    h2s = [(_rms(x1, gmlp_ref[...]) * (1 + sc2_ref[...]) + sh2_ref[...]).astype(BF16) for x1 in x1s]

    jobs = [(k, c) for k in range(len(subs)) for c in range(n_ff)]
    up = lambda k, c: _dot(h2s[k], w1_ref[:, c * FF_CHUNK:(c + 1) * FF_CHUNK])
    pending = {0: up(*jobs[0])}
    accs = [None] * len(subs)
    for n, (k, c) in enumerate(jobs):
        if n + 1 < len(jobs):
            pending[n + 1] = up(*jobs[n + 1])
        u = jnp.square(jnp.maximum(pending.pop(n), 0.0)).astype(BF16)
        down = _dot(u, w2_ref[c * FF_CHUNK:(c + 1) * FF_CHUNK, :])
        accs[k] = down if accs[k] is None else accs[k] + down
        if c == n_ff - 1:
            o_ref[subs[k], :] = _rms(x1s[k] + ga2_ref[...] * accs[k], gfin_ref[...])


def _tail(x, ym, ys, mods, gmix, gmlp, gfin, bg, wg, wom, wos, wo, w1, w2):
    B, S, D = x.shape
    tm = TM_TAIL
    tok = lambda width: pl.BlockSpec((None, tm, width), lambda b, i: (b, i, 0))
    per_b = pl.BlockSpec((None, 1, D), lambda b, i: (b, 0, 0))
    const = lambda a: pl.BlockSpec(a.shape, lambda b, i: (0,) * a.ndim, pipeline_mode=pl.Buffered(1))
    return pl.pallas_call(
        _tail_kernel,
        grid=(B, S // tm),
        in_specs=[tok(D), tok(ym.shape[-1]), tok(ys.shape[-1])] + [per_b] * 6
                 + [const(a) for a in (gmix, gmlp, gfin, bg, wg, wom, wos, wo, w1, w2)],
        out_specs=tok(D),
        out_shape=jax.ShapeDtypeStruct((B, S, D), F32),
        compiler_params=pltpu.CompilerParams(dimension_semantics=("arbitrary", "arbitrary"),
                                             vmem_limit_bytes=VMEM_LIMIT),
        name="tail",
    )(x, ym, ys, *mods, gmix, gmlp, gfin, bg, wg, wom, wos, wo, w1, w2)


def _head_slab(nope, a, b):
    r, h, _ = nope.shape
    pad = jnp.zeros((r, h, LANES - MLA_NOPE - MLA_ROPE - HALF_ROPE), nope.dtype)
    return jnp.concatenate([nope, a, b, a, pad], axis=-1).reshape(r, h * LANES)


def _rope_spread():
    m = np.zeros((LANES, 2 * LANES), np.float32)
    for term in range(3):
        for j in range(HALF_ROPE):
            m[term * MLA_ROPE + j, [ROPE_A + j, ROPE_B + j]] = 1.0
            m[term * MLA_ROPE + HALF_ROPE + j, LANES + ROPE_A + j] = -1.0
            m[term * MLA_ROPE + HALF_ROPE + j, LANES + ROPE_B + j] = 1.0
    return jnp.asarray(m, BF16)


def kernel(x, c, positions, rel_bias, ada_w, ada_b, ln_mix_g, w_in, b_gate, mla_q_norm_g, mla_kv_norm_g,
           w_uq, w_ukv, swa_sinks, w_o_mla, w_o_swa, w_o, ln_mlp_g, w_ff1, w_ff2, ln_final_g):
    B, S, D = x.shape
    assert (B, S, D) == (x.shape[0], 4096, D_MODEL) and ada_w.shape[0] == 1
    l = 0

    c_pad = jnp.pad(c, ((0, 8 - B), (0, 0)))
    mod, bias_tbl = _adaln_bias(rel_bias, c_pad, ada_w, ada_b)
    mod = mod[:B]
    mods = [m[:, None, :] for m in jnp.split(mod, 6, axis=-1)]

    wi = w_in[l].T
    o_kv = MLA_Q_RANK
    o_kr = o_kv + MLA_KV_RANK
    o_qs = o_kr + MLA_ROPE
    o_ks = o_qs + SWA_Q_HEADS * SWA_HEAD
    o_vs = o_ks + SWA_KV_HEADS * SWA_HEAD
    o_ga = o_vs + SWA_KV_HEADS * SWA_HEAD
    kr_a, kr_b = wi[o_kr:o_kr + HALF_ROPE], wi[o_kr + HALF_ROPE:o_qs]
    z = lambda n: jnp.zeros((n, D), wi.dtype)
    w1 = jnp.concatenate([
        wi[:o_kr],
        z(ROPE_A), kr_a, kr_b, kr_a, z(LANES - ROPE_A - MLA_ROPE - HALF_ROPE),
        wi[o_qs:o_ga]], axis=0).astype(BF16)
    assert w1.shape[0] == N_PROJ

    uq = w_uq[l]
    q_nope, q_a, q_b = uq[..., :MLA_NOPE], uq[..., MLA_NOPE:MLA_NOPE + HALF_ROPE], uq[..., MLA_NOPE + HALF_ROPE:]
    wq = _head_slab(q_nope, q_a, q_b).astype(BF16)
    ukv = w_ukv[l]
    k_nope, v_up = ukv[..., :MLA_NOPE], ukv[..., MLA_NOPE:]
    zr = jnp.zeros(k_nope.shape[:2] + (HALF_ROPE,), ukv.dtype)
    wkv = jnp.concatenate([_head_slab(k_nope, zr, zr),
                           v_up.reshape(MLA_KV_RANK, MLA_HEADS * MLA_V)], axis=1).astype(BF16)

    inv = (ROPE_THETA ** (-jnp.arange(HALF_ROPE, dtype=F32) / HALF_ROPE))[:, None]

    whole = lambda w: (w, 0, w.shape[1])
    gate_rows = (wi[None], o_ga, 2 * D)
    qm, km, vm, qs, ks, vs, wom, wos, wo, wf1, wf2, wg = _proj(
        x, mods[0], mods[1], ln_mix_g[l][None, :], positions[:, None, :], inv, _rope_spread(), w1,
        mla_q_norm_g[l][None, :], mla_kv_norm_g[l][None, :], wq, wkv,
        [whole(w_o_mla), whole(w_o_swa), whole(w_o), whole(w_ff1), whole(w_ff2), gate_rows])

    y_mla = _mla(qm, km, vm)
    y_swa = _swa(swa_sinks[l], qs, ks, vs, bias_tbl)

    return _tail(x, y_mla, y_swa, mods, ln_mix_g[l][None, :], ln_mlp_g[l][None, :], ln_final_g[None, :],
                 b_gate[l][None, :], wg, wom, wos, wo, wf1, wf2)
```

```python
import functools
import math

import jax
import jax.numpy as jnp
import numpy as np
from jax import lax
from jax.experimental import pallas as pl
from jax.experimental.pallas import tpu as pltpu

F32 = jnp.float32
BF16 = jnp.bfloat16

D_MODEL = 1024
MLA_HEADS = 8
MLA_Q_RANK = 256
MLA_KV_RANK = 128
MLA_NOPE = 64
MLA_ROPE = 32
MLA_V = 64
SWA_Q_HEADS = 16
SWA_KV_HEADS = 2
SWA_HEAD = 64
WINDOW = 128
REL_BUCKETS = 32
REL_MAX_DIST = 128
D_FF = 4 * D_MODEL
ROPE_THETA = 10000.0
EPS = 1e-6

LANES = 128
SCORE_PAD = LANES
HALF_ROPE = MLA_ROPE // 2
ROPE_A = MLA_NOPE
ROPE_B = MLA_NOPE + HALF_ROPE
N_PAIRS = SWA_Q_HEADS // 2
SWA_GROUP = SWA_Q_HEADS // SWA_KV_HEADS
SWA_STEP_BLOCKS = 16
SWA_LOOKAHEAD = 2
LOG2E = math.log2(math.e)

C_QLAT = 0
C_KVLAT = C_QLAT + MLA_Q_RANK
C_KR = C_KVLAT + MLA_KV_RANK
C_QS = C_KR + LANES
C_KS = C_QS + SWA_Q_HEADS * SWA_HEAD
C_VS = C_KS + SWA_KV_HEADS * SWA_HEAD
N_PROJ = C_VS + SWA_KV_HEADS * SWA_HEAD

VMEM_LIMIT = 56 * 1024 * 1024

TM_PROJ = 1024
PROJ_SUB = 1024
TQ_MLA = 1024
TK_MLA = 256
MLA_GROUP = 4
MLA_LOOKAHEAD = 3
MLA_ACC_ROWS = MLA_V + 16
TM_TAIL = 512
TAIL_SUB = 256
FF_CHUNK = 1024


def _rms(x, g):
    return x * lax.rsqrt(jnp.mean(x * x, axis=-1, keepdims=True) + EPS) * g


def _dot(a, b):
    return jnp.dot(a, b, preferred_element_type=F32)


def _dot_nt(a, b):
    return lax.dot_general(a, b, (((1,), (1,)), ((), ())), preferred_element_type=F32)


ADALN_STEPS = 8


def _adaln_bias_kernel(rb_ref, c_ref, w_ref, b_ref, o_ref, tbl_ref):
    @pl.when(pl.program_id(0) == 0)
    def _():
        o_ref[...] = jnp.broadcast_to(b_ref[...], o_ref.shape)

    c = c_ref[...]
    act = (c * jax.nn.sigmoid(c)).astype(BF16)
    o_ref[...] += _dot(act, w_ref[...].astype(BF16))

    a = lax.broadcasted_iota(jnp.int32, (2 * WINDOW, WINDOW), 1)
    b = lax.broadcasted_iota(jnp.int32, (2 * WINDOW, WINDOW), 0)
    dist = WINDOW + a - b
    n = jnp.maximum(dist, 0)
    max_exact = REL_BUCKETS // 2
    nf = jnp.maximum(n, 1).astype(F32)
    large = max_exact + jnp.floor(jnp.log(nf / max_exact) / math.log(REL_MAX_DIST / max_exact)
                                  * (REL_BUCKETS - max_exact)).astype(jnp.int32)
    large = jnp.minimum(large, REL_BUCKETS - 1)
    bucket = jnp.where(n < max_exact, n, large)
    band_ok = (dist >= 0) & (dist < WINDOW)
    has_prev = b >= WINDOW
    neg = jnp.float32(-jnp.inf)
    heads = SWA_Q_HEADS // ADALN_STEPS
    for h in range(heads):
        head = pl.program_id(0) * heads + h
        bias = jnp.zeros((2 * WINDOW, WINDOW), F32)
        for k in range(REL_BUCKETS):
            bias = jnp.where(bucket == k, rb_ref[head, k], bias)
        tbl_ref[0, h] = jnp.where(band_ok & has_prev, bias * LOG2E, neg)
        tbl_ref[1, h] = jnp.where(band_ok, bias * LOG2E, neg)


def _adaln_bias(rel_bias, c_pad, ada_w, ada_b):
    rows = c_pad.shape[0]
    n_out = ada_w.shape[2]
    slab = D_MODEL // ADALN_STEPS
    heads = SWA_Q_HEADS // ADALN_STEPS
    return pl.pallas_call(
        _adaln_bias_kernel,
        grid=(ADALN_STEPS,),
        in_specs=[pl.BlockSpec(memory_space=pltpu.SMEM),
                  pl.BlockSpec((rows, slab), lambda j: (0, j)),
                  pl.BlockSpec((None, slab, n_out), lambda j: (0, j, 0)),
                  pl.BlockSpec((1, n_out), lambda j: (0, 0))],
        out_specs=[pl.BlockSpec((rows, n_out), lambda j: (0, 0)),
                   pl.BlockSpec((2, heads, 2 * WINDOW, WINDOW), lambda j: (0, j, 0, 0))],
        out_shape=[jax.ShapeDtypeStruct((rows, n_out), F32),
                   jax.ShapeDtypeStruct((2, SWA_Q_HEADS, 2 * WINDOW, WINDOW), F32)],
        compiler_params=pltpu.CompilerParams(dimension_semantics=("arbitrary",)),
        name="adaln_bias",
    )(rel_bias, c_pad, ada_w, ada_b)


def _proj_kernel(cast_pieces, x_ref, sh_ref, sc_ref, g_ref, pos_ref, inv_ref, spread_ref, w1_ref, gq_ref, gkv_ref,
                 wq_ref, wkv_ref, *rest):
    n_in = sum(cast_pieces)
    cast_in, (qm_ref, km_ref, vm_ref, qs_ref, ks_ref, vs_ref), cast_out = (
        rest[:n_in], rest[n_in:n_in + 6], rest[n_in + 6:])
    srcs = iter(cast_in)
    for pieces, dst in zip(cast_pieces, cast_out):
        rows = dst.shape[0] // pieces
        for k in range(pieces):
            dst[k * rows:(k + 1) * rows, :] = next(srcs)[...].astype(BF16)
    subs = [slice(k * PROJ_SUB, (k + 1) * PROJ_SUB) for k in range(TM_PROJ // PROJ_SUB)]
    ps = []
    for r in subs:
        h = _rms(x_ref[r, :], g_ref[...]) * (1 + sc_ref[...]) + sh_ref[...]
        ps.append(_dot_nt(h.astype(BF16), w1_ref[...]))

    ang = inv_ref[...] * pos_ref[...].astype(F32)
    cs = jnp.concatenate([jnp.cos(ang), jnp.sin(ang)], axis=0)
    hi = cs.astype(BF16).astype(F32)
    mid = (cs - hi).astype(BF16).astype(F32)
    lo = (cs - hi - mid).astype(BF16).astype(F32)
    terms = jnp.concatenate([hi, mid, lo, jnp.zeros_like(hi)], axis=0)
    tables = _dot(terms.T.astype(BF16), spread_ref[...])
    lane_t = lax.broadcasted_iota(jnp.int32, (PROJ_SUB, LANES), 1)
    scale = (MLA_NOPE + MLA_ROPE) ** -0.5 * LOG2E
    half = MLA_HEADS * LANES

    ups = []
    for p in ps:
        qn = _rms(p[:, C_QLAT:C_QLAT + MLA_Q_RANK], gq_ref[...]).astype(BF16)
        kvn = _rms(p[:, C_KVLAT:C_KVLAT + MLA_KV_RANK], gkv_ref[...]).astype(BF16)
        ups.append((_dot(qn, wq_ref[...]), _dot(kvn, wkv_ref[...])))

    for k, (r, p, (qq, kk)) in enumerate(zip(subs, ps, ups)):
        cos_f, sin_f = tables[r, :LANES], tables[r, LANES:]
        cos_q = jnp.where(lane_t < MLA_NOPE, 1.0, cos_f)

        def rope(t, cos):
            return t * cos + pltpu.roll(t, LANES - HALF_ROPE, 1) * sin_f

        for hd in range(MLA_HEADS):
            qm_ref[hd, r, :] = (rope(qq[:, hd * LANES:(hd + 1) * LANES], cos_q) * scale).astype(BF16)
        k_pe = rope(p[:, C_KR:C_KR + LANES], cos_f)
        for hd in range(MLA_HEADS):
            km_ref[hd, r, :] = (kk[:, hd * LANES:(hd + 1) * LANES] + k_pe).astype(BF16)
        for pr in range(MLA_HEADS // 2):
            v_t = kk[:, half + pr * LANES:half + (pr + 1) * LANES].T.astype(BF16)
            for t in range(PROJ_SUB // TK_MLA):
                vm_ref[pr, k * (PROJ_SUB // TK_MLA) + t] = v_t[:, t * TK_MLA:(t + 1) * TK_MLA]
        qs_ref[r, :] = (p[:, C_QS:C_KS] * (SWA_HEAD ** -0.5 * LOG2E)).astype(BF16)
        ks_ref[r, :] = p[:, C_KS:C_VS].astype(BF16)
        vs_ref[r, :] = p[:, C_VS:N_PROJ].astype(BF16)


def _proj(x, sh1, sc1, ln_g, pos3, inv, spread, w1, gq, gkv, wq, wkv, tail_weights):
    B, S, D = x.shape
    tm = TM_PROJ
    steps = B * (S // tm)
    step = lambda b, i: b * (S // tm) + i
    const = lambda shape: pl.BlockSpec(shape, lambda b, i: (0,) * len(shape))
    cast_in, cast_args, cast_out, cast_shapes, cast_pieces = [], [], [], [], []
    for w, row0, rows in tail_weights:
        slab = rows // steps
        piece = math.gcd(row0, slab) if row0 else slab
        assert rows % steps == 0 and piece % 16 == 0
        for k in range(slab // piece):
            cast_in.append(pl.BlockSpec((None, piece, w.shape[2]),
                                        lambda b, i, o=row0 // piece + k, n=slab // piece: (0, o + step(b, i) * n, 0)))
            cast_args.append(w)
        cast_out.append(pl.BlockSpec((slab, w.shape[2]), lambda b, i: (step(b, i), 0)))
        cast_shapes.append(jax.ShapeDtypeStruct((rows, w.shape[2]), BF16))
        cast_pieces.append(slab // piece)
    return pl.pallas_call(
        functools.partial(_proj_kernel, tuple(cast_pieces)),
        grid=(B, S // tm),
        in_specs=[pl.BlockSpec((None, tm, D), lambda b, i: (b, i, 0)),
                  pl.BlockSpec((None, 1, D), lambda b, i: (b, 0, 0)),
                  pl.BlockSpec((None, 1, D), lambda b, i: (b, 0, 0)),
                  const((1, D)),
                  pl.BlockSpec((None, 1, tm), lambda b, i: (b, 0, i)),
                  const((HALF_ROPE, 1)), const(spread.shape),
                  const(w1.shape), const(gq.shape), const(gkv.shape), const(wq.shape), const(wkv.shape)] + cast_in,
        out_specs=[pl.BlockSpec((None, MLA_HEADS, tm, LANES), lambda b, i: (b, 0, i, 0)),
                   pl.BlockSpec((None, MLA_HEADS, tm, LANES), lambda b, i: (b, 0, i, 0)),
                   pl.BlockSpec((None, MLA_HEADS // 2, tm // TK_MLA, LANES, TK_MLA), lambda b, i: (b, 0, i, 0, 0)),
                   pl.BlockSpec((None, tm, SWA_Q_HEADS * SWA_HEAD), lambda b, i: (b, i, 0)),
                   pl.BlockSpec((None, tm, LANES), lambda b, i: (b, i, 0)),
                   pl.BlockSpec((None, tm, LANES), lambda b, i: (b, i, 0))] + cast_out,
        out_shape=[jax.ShapeDtypeStruct((B, MLA_HEADS, S, LANES), BF16),
                   jax.ShapeDtypeStruct((B, MLA_HEADS, S, LANES), BF16),
                   jax.ShapeDtypeStruct((B, MLA_HEADS // 2, S // TK_MLA, LANES, TK_MLA), BF16),
                   jax.ShapeDtypeStruct((B, S, SWA_Q_HEADS * SWA_HEAD), BF16),
                   jax.ShapeDtypeStruct((B, S, LANES), BF16),
                   jax.ShapeDtypeStruct((B, S, LANES), BF16)]
                  + cast_shapes,
        compiler_params=pltpu.CompilerParams(dimension_semantics=("arbitrary", "arbitrary"),
                                             vmem_limit_bytes=VMEM_LIMIT),
        name="proj",
    )(x, sh1, sc1, ln_g, pos3, inv, spread, w1, gq, gkv, wq, wkv, *cast_args)


def _mla_kernel(q_ref, k_ref, vt_ref, o_ref, qt_sc, acc_sc, *st_slots):
    tq, tk = TQ_MLA, TK_MLA
    nc = tq // LANES
    i = pl.program_id(2)
    for hd in range(MLA_GROUP):
        qt_sc[hd] = q_ref[hd].astype(F32).T.astype(BF16)
    acc_sc[...] = jnp.zeros_like(acc_sc)

    per_tile = tq // tk

    ones_rows = (lax.broadcasted_iota(jnp.int32, (MLA_ACC_ROWS - MLA_V, tk), 0) == 0).astype(BF16)
    r_iota = lax.broadcasted_iota(jnp.int32, (tk, LANES), 0)
    c_iota = lax.broadcasted_iota(jnp.int32, (tk, LANES), 1)

    blocks = [(d, hd) for d in range(per_tile) for hd in range(MLA_GROUP)]
    assert len(blocks) % len(st_slots) == 0 and MLA_LOOKAHEAD <= MLA_GROUP

    def scores(t, n, first_col):
        d, hd = blocks[n]
        k = k_ref[hd, pl.ds(pl.multiple_of((t * per_tile + d) * tk, tk), tk), :]
        st_slots[n % len(st_slots)][0, :, first_col:tq] = _dot(k, qt_sc[hd, :, first_col:])

    def tile(t, ms_all, diagonal):
        first_cc = [(d * tk) // LANES if diagonal else 0 for d, _ in blocks]
        ms_all = list(ms_all)
        for n, (d, hd) in enumerate(blocks):
            ahead = n + MLA_LOOKAHEAD
            if ahead < len(blocks):
                scores(t, ahead, first_cc[ahead] * LANES)
            elif not diagonal:
                scores(t + 1, ahead - len(blocks), 0)
            ms = list(ms_all[hd])
            ps, alphas = [], []
            for cc in range(first_cc[n], nc):
                blk = st_slots[n % len(st_slots)][0, :, cc * LANES:(cc + 1) * LANES]
                if diagonal and cc * LANES < (d + 1) * tk - 1:
                    blk = jnp.where(r_iota + (d * tk - cc * LANES) <= c_iota, blk, -jnp.inf)
                m_new = jnp.maximum(ms[cc], jnp.max(blk, axis=0, keepdims=True))
                alphas.append(jnp.exp2(ms[cc] - m_new))
                ps.append(jnp.exp2(blk - m_new).astype(BF16))
                ms[cc] = m_new
            v_t = vt_ref[hd // 2, t * per_tile + d, (hd % 2) * MLA_V:(hd % 2 + 1) * MLA_V, :]
            pv = _dot(jnp.concatenate([v_t, ones_rows], axis=0), jnp.concatenate(ps, axis=1))
            rows = slice(hd * MLA_ACC_ROWS, (hd + 1) * MLA_ACC_ROWS)
            cols = slice(first_cc[n] * LANES, tq)
            acc_sc[rows, cols] = acc_sc[rows, cols] * jnp.concatenate(alphas, axis=1) + pv
            ms_all[hd] = tuple(ms)
        return tuple(ms_all)

    init = tuple(tuple(jnp.full((1, LANES), -jnp.inf, F32) for _ in range(nc)) for _ in range(MLA_GROUP))
    for n in range(MLA_LOOKAHEAD):
        scores(0, n, 0)
    ms_all = lax.fori_loop(0, i, lambda t, c: tile(t, c, False), init)
    tile(i, ms_all, True)
    outs = []
    for hd in range(MLA_GROUP):
        acc = acc_sc[hd * MLA_ACC_ROWS:(hd + 1) * MLA_ACC_ROWS, :]
        outs.append(acc[:MLA_V] / acc[MLA_V:MLA_V + 1])
    o_ref[...] = jnp.concatenate(outs, axis=0).T.astype(BF16)


def _mla(qm, km, vt):
    B, H, S, _ = qm.shape
    tq, tk, g = TQ_MLA, TK_MLA, MLA_GROUP
    return pl.pallas_call(
        _mla_kernel,
        grid=(B, H // g, S // tq),
        in_specs=[pl.BlockSpec((None, g, tq, LANES), lambda b, p, i: (b, p, i, 0)),
                  pl.BlockSpec((None, g, S, LANES), lambda b, p, i: (b, p, 0, 0)),
                  pl.BlockSpec((None, g // 2, S // tk, LANES, tk), lambda b, p, i: (b, p, 0, 0, 0))],
        out_specs=pl.BlockSpec((None, tq, g * MLA_V), lambda b, p, i: (b, i, p)),
        out_shape=jax.ShapeDtypeStruct((B, S, H * MLA_V), BF16),
        scratch_shapes=[pltpu.VMEM((g, LANES, tq), BF16), pltpu.VMEM((g * MLA_ACC_ROWS, tq), F32),
                        ] + [pltpu.VMEM((1, tk, tq + SCORE_PAD), F32)] * (MLA_LOOKAHEAD + 1),
        compiler_params=pltpu.CompilerParams(dimension_semantics=("arbitrary",) * 3,
                                             vmem_limit_bytes=VMEM_LIMIT),
        name="mla",
    )(qm, km, vt)


def _swa_kernel(sink_ref, q_ref, kp_ref, kc_ref, vp_ref, vc_ref, bias0_ref, bias_ref, o_ref, *st_slots):
    band = 2 * WINDOW
    pairs = SWA_GROUP // 2
    kband = jnp.concatenate([kp_ref[...], kc_ref[...]], axis=0).astype(F32)
    vband_t = jnp.concatenate([vp_ref[...], vc_ref[...]], axis=0).astype(F32).T
    lo = lax.broadcasted_iota(jnp.int32, kband.shape, 1) < SWA_HEAD
    kswap = pltpu.roll(kband, SWA_HEAD, 1)
    k_even = [jnp.where(lo, kband, 0.0).astype(BF16), jnp.where(lo, kswap, 0.0).astype(BF16)]
    k_odd = [jnp.where(lo, 0.0, kswap).astype(BF16), jnp.where(lo, 0.0, kband).astype(BF16)]
    ones_rows = (lax.broadcasted_iota(jnp.int32, (16, band), 0) == 0).astype(BF16)
    chains = [(j, kvh) for j in range(SWA_STEP_BLOCKS) for kvh in range(SWA_KV_HEADS)]

    def scores(n):
        j, kvh = chains[n]
        keys = slice(j * WINDOW, j * WINDOW + band)
        kcat = jnp.concatenate([k_even[kvh][keys], k_odd[kvh][keys]], axis=0)
        qstack = jnp.concatenate([q_ref[j * WINDOW:(j + 1) * WINDOW, (kvh * pairs + jj) * LANES:
                                        (kvh * pairs + jj + 1) * LANES] for jj in range(pairs)], axis=0)
        bias = bias0_ref if j == 0 else bias_ref
        st = _dot_nt(kcat, qstack)
        for parity in range(2):
            for jj in range(pairs):
                rows, cols = slice(parity * band, (parity + 1) * band), slice(jj * WINDOW, (jj + 1) * WINDOW)
                st_slots[n % len(st_slots)][0, rows, cols] = st[rows, cols] + bias[kvh * SWA_GROUP + 2 * jj + parity]

    for n in range(min(SWA_LOOKAHEAD, len(chains))):
        scores(n)
    for n, (j, kvh) in enumerate(chains):
        if n + SWA_LOOKAHEAD < len(chains):
            scores(n + SWA_LOOKAHEAD)
        st = st_slots[n % len(st_slots)].at[jnp.minimum(pl.program_id(1), 0)]
        v_t = vband_t[kvh * SWA_HEAD:(kvh + 1) * SWA_HEAD, j * WINDOW:j * WINDOW + band].astype(BF16)
        v_aug = jnp.concatenate([v_t, ones_rows], axis=0)
        halves = []
        for parity in range(2):
            ps, sink_terms = [], []
            for jj in range(pairs):
                blk = st[parity * band:(parity + 1) * band, jj * WINDOW:(jj + 1) * WINDOW]
                sink = sink_ref[kvh * SWA_GROUP + 2 * jj + parity] * LOG2E
                m = jnp.maximum(jnp.max(blk, axis=0, keepdims=True), sink)
                ps.append(jnp.exp2(blk - m).astype(BF16))
                sink_terms.append(jnp.exp2(sink - m))
            pv = _dot(v_aug, jnp.concatenate(ps, axis=1))
            halves.append(pv[:SWA_HEAD] / (pv[SWA_HEAD:SWA_HEAD + 1] + jnp.concatenate(sink_terms, axis=1)))
        o_t = jnp.concatenate(halves, axis=0)
        for jj in range(pairs):
            pair = kvh * pairs + jj
            o_ref[j * WINDOW:(j + 1) * WINDOW, pair * LANES:(pair + 1) * LANES] = (
                o_t[:, jj * WINDOW:(jj + 1) * WINDOW].T.astype(BF16))


def _swa(sinks, qs, ks, vs, bias_tbl):
    B, S, W = qs.shape
    nb = SWA_STEP_BLOCKS
    prev = lambda b, n: (b, jnp.maximum(nb * n - 1, 0), 0)
    cur = lambda b, n: (b, n, 0)
    tbl_block = (None,) + bias_tbl.shape[1:]
    return pl.pallas_call(
        _swa_kernel,
        grid=(B, S // (nb * WINDOW)),
        in_specs=[pl.BlockSpec(memory_space=pltpu.SMEM),
                  pl.BlockSpec((None, nb * WINDOW, W), cur),
                  pl.BlockSpec((None, WINDOW, LANES), prev), pl.BlockSpec((None, nb * WINDOW, LANES), cur),
                  pl.BlockSpec((None, WINDOW, LANES), prev), pl.BlockSpec((None, nb * WINDOW, LANES), cur),
                  pl.BlockSpec(tbl_block, lambda b, n: (jnp.minimum(n, 1), 0, 0, 0)),
                  pl.BlockSpec(tbl_block, lambda b, n: (1, 0, 0, 0))],
        out_specs=pl.BlockSpec((None, nb * WINDOW, W), cur),
        out_shape=jax.ShapeDtypeStruct((B, S, W), BF16),
        scratch_shapes=[pltpu.VMEM((1, 4 * WINDOW, SWA_GROUP // 2 * WINDOW + SCORE_PAD), F32)] * (SWA_LOOKAHEAD + 1),
        compiler_params=pltpu.CompilerParams(dimension_semantics=("arbitrary", "arbitrary"),
                                             vmem_limit_bytes=VMEM_LIMIT),
        name="swa",
    )(sinks, qs, ks, ks, vs, vs, bias_tbl, bias_tbl)


def _tail_kernel(x_ref, ym_ref, ys_ref, sh1_ref, sc1_ref, ga1_ref, sh2_ref, sc2_ref, ga2_ref,
                 gmix_ref, gmlp_ref, gfin_ref, bg_ref, wg_ref, wom_ref, wos_ref, wo_ref, w1_ref, w2_ref, o_ref):
    subs = [slice(k * TAIL_SUB, (k + 1) * TAIL_SUB) for k in range(TM_TAIL // TAIL_SUB)]
    n_ff = D_FF // FF_CHUNK
    xs = [x_ref[r, :] for r in subs]
    branches = [(_dot(ym_ref[r, :], wom_ref[...]), _dot(ys_ref[r, :], wos_ref[...])) for r in subs]
    hs = [(_rms(x, gmix_ref[...]) * (1 + sc1_ref[...]) + sh1_ref[...]).astype(BF16) for x in xs]
    logits = [_dot_nt(h, wg_ref[...]) for h in hs]
    x1s = []
    for x, (a, b), gl in zip(xs, branches, logits):
        gates = jax.nn.sigmoid(gl + bg_ref[...])
        merged = gates[:, :D_MODEL] * a + gates[:, D_MODEL:] * b
        x1s.append(x + ga1_ref[...] * _dot(merged.astype(BF16), wo_ref[...]))
    h2s = [(_rms(x1, gmlp_ref[...]) * (1 + sc2_ref[...]) + sh2_ref[...]).astype(BF16) for x1 in x1s]

    jobs = [(k, c) for k in range(len(subs)) for c in range(n_ff)]
    up = lambda k, c: _dot(h2s[k], w1_ref[:, c * FF_CHUNK:(c + 1) * FF_CHUNK])
    pending = {0: up(*jobs[0])}
    accs = [None] * len(subs)
    for n, (k, c) in enumerate(jobs):
        if n + 1 < len(jobs):
            pending[n + 1] = up(*jobs[n + 1])
        u = jnp.square(jnp.maximum(pending.pop(n), 0.0)).astype(BF16)
        down = _dot(u, w2_ref[c * FF_CHUNK:(c + 1) * FF_CHUNK, :])
        accs[k] = down if accs[k] is None else accs[k] + down
        if c == n_ff - 1:
            o_ref[subs[k], :] = _rms(x1s[k] + ga2_ref[...] * accs[k], gfin_ref[...])


def _tail(x, ym, ys, mods, gmix, gmlp, gfin, bg, wg, wom, wos, wo, w1, w2):
    B, S, D = x.shape
    tm = TM_TAIL
    tok = lambda width: pl.BlockSpec((None, tm, width), lambda b, i: (b, i, 0))
    per_b = pl.BlockSpec((None, 1, D), lambda b, i: (b, 0, 0))
    const = lambda a: pl.BlockSpec(a.shape, lambda b, i: (0,) * a.ndim, pipeline_mode=pl.Buffered(1))
    return pl.pallas_call(
        _tail_kernel,
        grid=(B, S // tm),
        in_specs=[tok(D), tok(ym.shape[-1]), tok(ys.shape[-1])] + [per_b] * 6
                 + [const(a) for a in (gmix, gmlp, gfin, bg, wg, wom, wos, wo, w1, w2)],
        out_specs=tok(D),
        out_shape=jax.ShapeDtypeStruct((B, S, D), F32),
        compiler_params=pltpu.CompilerParams(dimension_semantics=("arbitrary", "arbitrary"),
                                             vmem_limit_bytes=VMEM_LIMIT),
        name="tail",
    )(x, ym, ys, *mods, gmix, gmlp, gfin, bg, wg, wom, wos, wo, w1, w2)


def _head_slab(nope, a, b):
    r, h, _ = nope.shape
    pad = jnp.zeros((r, h, LANES - MLA_NOPE - MLA_ROPE - HALF_ROPE), nope.dtype)
    return jnp.concatenate([nope, a, b, a, pad], axis=-1).reshape(r, h * LANES)


def _rope_spread():
    m = np.zeros((LANES, 2 * LANES), np.float32)
    for term in range(3):
        for j in range(HALF_ROPE):
            m[term * MLA_ROPE + j, [ROPE_A + j, ROPE_B + j]] = 1.0
            m[term * MLA_ROPE + HALF_ROPE + j, LANES + ROPE_A + j] = -1.0
            m[term * MLA_ROPE + HALF_ROPE + j, LANES + ROPE_B + j] = 1.0
    return jnp.asarray(m, BF16)


def kernel(x, c, positions, rel_bias, ada_w, ada_b, ln_mix_g, w_in, b_gate, mla_q_norm_g, mla_kv_norm_g,
           w_uq, w_ukv, swa_sinks, w_o_mla, w_o_swa, w_o, ln_mlp_g, w_ff1, w_ff2, ln_final_g):
    B, S, D = x.shape
    assert (B, S, D) == (x.shape[0], 4096, D_MODEL) and ada_w.shape[0] == 1
    l = 0

    c_pad = jnp.pad(c, ((0, 8 - B), (0, 0)))
    mod, bias_tbl = _adaln_bias(rel_bias, c_pad, ada_w, ada_b)
    mod = mod[:B]
    mods = [m[:, None, :] for m in jnp.split(mod, 6, axis=-1)]

    wi = w_in[l].T
    o_kv = MLA_Q_RANK
    o_kr = o_kv + MLA_KV_RANK
    o_qs = o_kr + MLA_ROPE
    o_ks = o_qs + SWA_Q_HEADS * SWA_HEAD
    o_vs = o_ks + SWA_KV_HEADS * SWA_HEAD
    o_ga = o_vs + SWA_KV_HEADS * SWA_HEAD
    kr_a, kr_b = wi[o_kr:o_kr + HALF_ROPE], wi[o_kr + HALF_ROPE:o_qs]
    z = lambda n: jnp.zeros((n, D), wi.dtype)
    w1 = jnp.concatenate([
        wi[:o_kr],
        z(ROPE_A), kr_a, kr_b, kr_a, z(LANES - ROPE_A - MLA_ROPE - HALF_ROPE),
        wi[o_qs:o_ga]], axis=0).astype(BF16)
    assert w1.shape[0] == N_PROJ

    uq = w_uq[l]
    q_nope, q_a, q_b = uq[..., :MLA_NOPE], uq[..., MLA_NOPE:MLA_NOPE + HALF_ROPE], uq[..., MLA_NOPE + HALF_ROPE:]
    wq = _head_slab(q_nope, q_a, q_b).astype(BF16)
    ukv = w_ukv[l]
    k_nope, v_up = ukv[..., :MLA_NOPE], ukv[..., MLA_NOPE:]
    zr = jnp.zeros(k_nope.shape[:2] + (HALF_ROPE,), ukv.dtype)
    wkv = jnp.concatenate([_head_slab(k_nope, zr, zr),
                           v_up.reshape(MLA_KV_RANK, MLA_HEADS * MLA_V)], axis=1).astype(BF16)

    inv = (ROPE_THETA ** (-jnp.arange(HALF_ROPE, dtype=F32) / HALF_ROPE))[:, None]

    whole = lambda w: (w, 0, w.shape[1])
    gate_rows = (wi[None], o_ga, 2 * D)
    qm, km, vm, qs, ks, vs, wom, wos, wo, wf1, wf2, wg = _proj(
        x, mods[0], mods[1], ln_mix_g[l][None, :], positions[:, None, :], inv, _rope_spread(), w1,
        mla_q_norm_g[l][None, :], mla_kv_norm_g[l][None, :], wq, wkv,
        [whole(w_o_mla), whole(w_o_swa), whole(w_o), whole(w_ff1), whole(w_ff2), gate_rows])

    y_mla = _mla(qm, km, vm)
    y_swa = _swa(swa_sinks[l], qs, ks, vs, bias_tbl)

    return _tail(x, y_mla, y_swa, mods, ln_mix_g[l][None, :], ln_mlp_g[l][None, :], ln_final_g[None, :],
                 b_gate[l][None, :], wg, wom, wos, wo, wf1, wf2)
```

```python
import functools
import math

import jax
import jax.numpy as jnp
import numpy as np
from jax import lax
from jax.experimental import pallas as pl
from jax.experimental.pallas import tpu as pltpu

F32 = jnp.float32
BF16 = jnp.bfloat16

D_MODEL = 1024
MLA_HEADS = 8
MLA_Q_RANK = 256
MLA_KV_RANK = 128
MLA_NOPE = 64
MLA_ROPE = 32
MLA_V = 64
SWA_Q_HEADS = 16
SWA_KV_HEADS = 2
SWA_HEAD = 64
WINDOW = 128
REL_BUCKETS = 32
REL_MAX_DIST = 128
D_FF = 4 * D_MODEL
ROPE_THETA = 10000.0
EPS = 1e-6

LANES = 128
SCORE_PAD = LANES
HALF_ROPE = MLA_ROPE // 2
ROPE_A = MLA_NOPE
ROPE_B = MLA_NOPE + HALF_ROPE
N_PAIRS = SWA_Q_HEADS // 2
SWA_GROUP = SWA_Q_HEADS // SWA_KV_HEADS
SWA_STEP_BLOCKS = 16
SWA_LOOKAHEAD = 1
LOG2E = math.log2(math.e)

C_QLAT = 0
C_KVLAT = C_QLAT + MLA_Q_RANK
C_KR = C_KVLAT + MLA_KV_RANK
C_QS = C_KR + LANES
C_KS = C_QS + SWA_Q_HEADS * SWA_HEAD
C_VS = C_KS + SWA_KV_HEADS * SWA_HEAD
N_PROJ = C_VS + SWA_KV_HEADS * SWA_HEAD

VMEM_LIMIT = 56 * 1024 * 1024

TM_PROJ = 1024
PROJ_SUB = 1024
TQ_MLA = 1024
TK_MLA = 256
MLA_GROUP = 4
MLA_LOOKAHEAD = 3
MLA_ACC_ROWS = MLA_V + 16
TM_TAIL = 512
TAIL_SUB = 256
FF_CHUNK = 1024


def _rms(x, g):
    return x * lax.rsqrt(jnp.mean(x * x, axis=-1, keepdims=True) + EPS) * g


def _dot(a, b):
    return jnp.dot(a, b, preferred_element_type=F32)


def _dot_nt(a, b):
    return lax.dot_general(a, b, (((1,), (1,)), ((), ())), preferred_element_type=F32)


ADALN_STEPS = 8


def _adaln_bias_kernel(rb_ref, c_ref, w_ref, b_ref, o_ref, tbl_ref):
    @pl.when(pl.program_id(0) == 0)
    def _():
        o_ref[...] = jnp.broadcast_to(b_ref[...], o_ref.shape)

    c = c_ref[...]
    act = (c * jax.nn.sigmoid(c)).astype(BF16)
    o_ref[...] += _dot(act, w_ref[...].astype(BF16))

    a = lax.broadcasted_iota(jnp.int32, (2 * WINDOW, WINDOW), 1)
    b = lax.broadcasted_iota(jnp.int32, (2 * WINDOW, WINDOW), 0)
    dist = WINDOW + a - b
    n = jnp.maximum(dist, 0)
    max_exact = REL_BUCKETS // 2
    nf = jnp.maximum(n, 1).astype(F32)
    large = max_exact + jnp.floor(jnp.log(nf / max_exact) / math.log(REL_MAX_DIST / max_exact)
                                  * (REL_BUCKETS - max_exact)).astype(jnp.int32)
    large = jnp.minimum(large, REL_BUCKETS - 1)
    bucket = jnp.where(n < max_exact, n, large)
    band_ok = (dist >= 0) & (dist < WINDOW)
    has_prev = b >= WINDOW
    neg = jnp.float32(-jnp.inf)
    heads = SWA_Q_HEADS // ADALN_STEPS
    for h in range(heads):
        head = pl.program_id(0) * heads + h
        bias = jnp.zeros((2 * WINDOW, WINDOW), F32)
        for k in range(REL_BUCKETS):
            bias = jnp.where(bucket == k, rb_ref[head, k], bias)
        tbl_ref[0, h] = jnp.where(band_ok & has_prev, bias * LOG2E, neg)
        tbl_ref[1, h] = jnp.where(band_ok, bias * LOG2E, neg)


def _adaln_bias(rel_bias, c_pad, ada_w, ada_b):
    rows = c_pad.shape[0]
    n_out = ada_w.shape[2]
    slab = D_MODEL // ADALN_STEPS
    heads = SWA_Q_HEADS // ADALN_STEPS
    return pl.pallas_call(
        _adaln_bias_kernel,
        grid=(ADALN_STEPS,),
        in_specs=[pl.BlockSpec(memory_space=pltpu.SMEM),
                  pl.BlockSpec((rows, slab), lambda j: (0, j)),
                  pl.BlockSpec((None, slab, n_out), lambda j: (0, j, 0)),
                  pl.BlockSpec((1, n_out), lambda j: (0, 0))],
        out_specs=[pl.BlockSpec((rows, n_out), lambda j: (0, 0)),
                   pl.BlockSpec((2, heads, 2 * WINDOW, WINDOW), lambda j: (0, j, 0, 0))],
        out_shape=[jax.ShapeDtypeStruct((rows, n_out), F32),
                   jax.ShapeDtypeStruct((2, SWA_Q_HEADS, 2 * WINDOW, WINDOW), F32)],
        compiler_params=pltpu.CompilerParams(dimension_semantics=("arbitrary",)),
        name="adaln_bias",
    )(rel_bias, c_pad, ada_w, ada_b)


def _proj_kernel(cast_pieces, x_ref, sh_ref, sc_ref, g_ref, pos_ref, inv_ref, spread_ref, w1_ref, gq_ref, gkv_ref,
                 wq_ref, wkv_ref, *rest):
    n_in = sum(cast_pieces)
    cast_in, (qm_ref, km_ref, vm_ref, qs_ref, ks_ref, vs_ref), cast_out = (
        rest[:n_in], rest[n_in:n_in + 6], rest[n_in + 6:])
    srcs = iter(cast_in)
    for pieces, dst in zip(cast_pieces, cast_out):
        rows = dst.shape[0] // pieces
        for k in range(pieces):
            dst[k * rows:(k + 1) * rows, :] = next(srcs)[...].astype(BF16)
    subs = [slice(k * PROJ_SUB, (k + 1) * PROJ_SUB) for k in range(TM_PROJ // PROJ_SUB)]
    ps = []
    for r in subs:
        h = _rms(x_ref[r, :], g_ref[...]) * (1 + sc_ref[...]) + sh_ref[...]
        ps.append(_dot_nt(h.astype(BF16), w1_ref[...]))

    ang = inv_ref[...] * pos_ref[...].astype(F32)
    cs = jnp.concatenate([jnp.cos(ang), jnp.sin(ang)], axis=0)
    hi = cs.astype(BF16).astype(F32)
    mid = (cs - hi).astype(BF16).astype(F32)
    lo = (cs - hi - mid).astype(BF16).astype(F32)
    terms = jnp.concatenate([hi, mid, lo, jnp.zeros_like(hi)], axis=0)
    tables = _dot(terms.T.astype(BF16), spread_ref[...])
    lane_t = lax.broadcasted_iota(jnp.int32, (PROJ_SUB, LANES), 1)
    scale = (MLA_NOPE + MLA_ROPE) ** -0.5 * LOG2E
    half = MLA_HEADS * LANES

    ups = []
    for p in ps:
        qn = _rms(p[:, C_QLAT:C_QLAT + MLA_Q_RANK], gq_ref[...]).astype(BF16)
        kvn = _rms(p[:, C_KVLAT:C_KVLAT + MLA_KV_RANK], gkv_ref[...]).astype(BF16)
        ups.append((_dot(qn, wq_ref[...]), _dot(kvn, wkv_ref[...])))

    for k, (r, p, (qq, kk)) in enumerate(zip(subs, ps, ups)):
        cos_f, sin_f = tables[r, :LANES], tables[r, LANES:]
        cos_q = jnp.where(lane_t < MLA_NOPE, 1.0, cos_f)

        def rope(t, cos):
            return t * cos + pltpu.roll(t, LANES - HALF_ROPE, 1) * sin_f

        for hd in range(MLA_HEADS):
            qm_ref[hd, r, :] = (rope(qq[:, hd * LANES:(hd + 1) * LANES], cos_q) * scale).astype(BF16)
        k_pe = rope(p[:, C_KR:C_KR + LANES], cos_f)
        for hd in range(MLA_HEADS):
            km_ref[hd, r, :] = (kk[:, hd * LANES:(hd + 1) * LANES] + k_pe).astype(BF16)
        for pr in range(MLA_HEADS // 2):
            v_t = kk[:, half + pr * LANES:half + (pr + 1) * LANES].T.astype(BF16)
            for t in range(PROJ_SUB // TK_MLA):
                vm_ref[pr, k * (PROJ_SUB // TK_MLA) + t] = v_t[:, t * TK_MLA:(t + 1) * TK_MLA]
        qs_ref[r, :] = (p[:, C_QS:C_KS] * (SWA_HEAD ** -0.5 * LOG2E)).astype(BF16)
        ks_ref[r, :] = p[:, C_KS:C_VS].astype(BF16)
        vs_ref[r, :] = p[:, C_VS:N_PROJ].astype(BF16)


def _proj(x, sh1, sc1, ln_g, pos3, inv, spread, w1, gq, gkv, wq, wkv, tail_weights):
    B, S, D = x.shape
    tm = TM_PROJ
    steps = B * (S // tm)
    step = lambda b, i: b * (S // tm) + i
    const = lambda shape: pl.BlockSpec(shape, lambda b, i: (0,) * len(shape))
    cast_in, cast_args, cast_out, cast_shapes, cast_pieces = [], [], [], [], []
    for w, row0, rows in tail_weights:
        slab = rows // steps
        piece = math.gcd(row0, slab) if row0 else slab
        assert rows % steps == 0 and piece % 16 == 0
        for k in range(slab // piece):
            cast_in.append(pl.BlockSpec((None, piece, w.shape[2]),
                                        lambda b, i, o=row0 // piece + k, n=slab // piece: (0, o + step(b, i) * n, 0)))
            cast_args.append(w)
        cast_out.append(pl.BlockSpec((slab, w.shape[2]), lambda b, i: (step(b, i), 0)))
        cast_shapes.append(jax.ShapeDtypeStruct((rows, w.shape[2]), BF16))
        cast_pieces.append(slab // piece)
    return pl.pallas_call(
        functools.partial(_proj_kernel, tuple(cast_pieces)),
        grid=(B, S // tm),
        in_specs=[pl.BlockSpec((None, tm, D), lambda b, i: (b, i, 0)),
                  pl.BlockSpec((None, 1, D), lambda b, i: (b, 0, 0)),
                  pl.BlockSpec((None, 1, D), lambda b, i: (b, 0, 0)),
                  const((1, D)),
                  pl.BlockSpec((None, 1, tm), lambda b, i: (b, 0, i)),
                  const((HALF_ROPE, 1)), const(spread.shape),
                  const(w1.shape), const(gq.shape), const(gkv.shape), const(wq.shape), const(wkv.shape)] + cast_in,
        out_specs=[pl.BlockSpec((None, MLA_HEADS, tm, LANES), lambda b, i: (b, 0, i, 0)),
                   pl.BlockSpec((None, MLA_HEADS, tm, LANES), lambda b, i: (b, 0, i, 0)),
                   pl.BlockSpec((None, MLA_HEADS // 2, tm // TK_MLA, LANES, TK_MLA), lambda b, i: (b, 0, i, 0, 0)),
                   pl.BlockSpec((None, tm, SWA_Q_HEADS * SWA_HEAD), lambda b, i: (b, i, 0)),
                   pl.BlockSpec((None, tm, LANES), lambda b, i: (b, i, 0)),
                   pl.BlockSpec((None, tm, LANES), lambda b, i: (b, i, 0))] + cast_out,
        out_shape=[jax.ShapeDtypeStruct((B, MLA_HEADS, S, LANES), BF16),
                   jax.ShapeDtypeStruct((B, MLA_HEADS, S, LANES), BF16),
                   jax.ShapeDtypeStruct((B, MLA_HEADS // 2, S // TK_MLA, LANES, TK_MLA), BF16),
                   jax.ShapeDtypeStruct((B, S, SWA_Q_HEADS * SWA_HEAD), BF16),
                   jax.ShapeDtypeStruct((B, S, LANES), BF16),
                   jax.ShapeDtypeStruct((B, S, LANES), BF16)]
                  + cast_shapes,
        compiler_params=pltpu.CompilerParams(dimension_semantics=("arbitrary", "arbitrary"),
                                             vmem_limit_bytes=VMEM_LIMIT),
        name="proj",
    )(x, sh1, sc1, ln_g, pos3, inv, spread, w1, gq, gkv, wq, wkv, *cast_args)


def _mla_kernel(q_ref, k_ref, vt_ref, o_ref, qt_sc, acc_sc, *st_slots):
    tq, tk = TQ_MLA, TK_MLA
    nc = tq // LANES
    i = pl.program_id(2)
    for hd in range(MLA_GROUP):
        qt_sc[hd] = q_ref[hd].astype(F32).T.astype(BF16)
    acc_sc[...] = jnp.zeros_like(acc_sc)

    per_tile = tq // tk

    ones_rows = (lax.broadcasted_iota(jnp.int32, (MLA_ACC_ROWS - MLA_V, tk), 0) == 0).astype(BF16)
    r_iota = lax.broadcasted_iota(jnp.int32, (tk, LANES), 0)
    c_iota = lax.broadcasted_iota(jnp.int32, (tk, LANES), 1)

    blocks = [(d, hd) for d in range(per_tile) for hd in range(MLA_GROUP)]
    assert len(blocks) % len(st_slots) == 0 and MLA_LOOKAHEAD <= MLA_GROUP

    def scores(t, n, first_col):
        d, hd = blocks[n]
        k = k_ref[hd, pl.ds(pl.multiple_of((t * per_tile + d) * tk, tk), tk), :]
        st_slots[n % len(st_slots)][0, :, first_col:tq] = _dot(k, qt_sc[hd, :, first_col:])

    def tile(t, ms_all, diagonal):
        first_cc = [(d * tk) // LANES if diagonal else 0 for d, _ in blocks]
        ms_all = list(ms_all)
        for n, (d, hd) in enumerate(blocks):
            ahead = n + MLA_LOOKAHEAD
            if ahead < len(blocks):
                scores(t, ahead, first_cc[ahead] * LANES)
            elif not diagonal:
                scores(t + 1, ahead - len(blocks), 0)
            ms = list(ms_all[hd])
            ps, alphas = [], []
            for cc in range(first_cc[n], nc):
                blk = st_slots[n % len(st_slots)][0, :, cc * LANES:(cc + 1) * LANES]
                if diagonal and cc * LANES < (d + 1) * tk - 1:
                    blk = jnp.where(r_iota + (d * tk - cc * LANES) <= c_iota, blk, -jnp.inf)
                m_new = jnp.maximum(ms[cc], jnp.max(blk, axis=0, keepdims=True))
                alphas.append(jnp.exp2(ms[cc] - m_new))
                ps.append(jnp.exp2(blk - m_new).astype(BF16))
                ms[cc] = m_new
            v_t = vt_ref[hd // 2, t * per_tile + d, (hd % 2) * MLA_V:(hd % 2 + 1) * MLA_V, :]
            pv = _dot(jnp.concatenate([v_t, ones_rows], axis=0), jnp.concatenate(ps, axis=1))
            rows = slice(hd * MLA_ACC_ROWS, (hd + 1) * MLA_ACC_ROWS)
            cols = slice(first_cc[n] * LANES, tq)
            acc_sc[rows, cols] = acc_sc[rows, cols] * jnp.concatenate(alphas, axis=1) + pv
            ms_all[hd] = tuple(ms)
        return tuple(ms_all)

    init = tuple(tuple(jnp.full((1, LANES), -jnp.inf, F32) for _ in range(nc)) for _ in range(MLA_GROUP))
    for n in range(MLA_LOOKAHEAD):
        scores(0, n, 0)
    ms_all = lax.fori_loop(0, i, lambda t, c: tile(t, c, False), init)
    tile(i, ms_all, True)
    outs = []
    for hd in range(MLA_GROUP):
        acc = acc_sc[hd * MLA_ACC_ROWS:(hd + 1) * MLA_ACC_ROWS, :]
        outs.append(acc[:MLA_V] / acc[MLA_V:MLA_V + 1])
    o_ref[...] = jnp.concatenate(outs, axis=0).T.astype(BF16)


def _mla(qm, km, vt):
    B, H, S, _ = qm.shape
    tq, tk, g = TQ_MLA, TK_MLA, MLA_GROUP
    return pl.pallas_call(
        _mla_kernel,
        grid=(B, H // g, S // tq),
        in_specs=[pl.BlockSpec((None, g, tq, LANES), lambda b, p, i: (b, p, i, 0)),
                  pl.BlockSpec((None, g, S, LANES), lambda b, p, i: (b, p, 0, 0)),
                  pl.BlockSpec((None, g // 2, S // tk, LANES, tk), lambda b, p, i: (b, p, 0, 0, 0))],
        out_specs=pl.BlockSpec((None, tq, g * MLA_V), lambda b, p, i: (b, i, p)),
        out_shape=jax.ShapeDtypeStruct((B, S, H * MLA_V), BF16),
        scratch_shapes=[pltpu.VMEM((g, LANES, tq), BF16), pltpu.VMEM((g * MLA_ACC_ROWS, tq), F32),
                        ] + [pltpu.VMEM((1, tk, tq + SCORE_PAD), F32)] * (MLA_LOOKAHEAD + 1),
        compiler_params=pltpu.CompilerParams(dimension_semantics=("arbitrary",) * 3,
                                             vmem_limit_bytes=VMEM_LIMIT),
        name="mla",
    )(qm, km, vt)


def _swa_kernel(sink_ref, q_ref, kp_ref, kc_ref, vp_ref, vc_ref, bias0_ref, bias_ref, o_ref, *st_slots):
    band = 2 * WINDOW
    pairs = SWA_GROUP // 2
    kband = jnp.concatenate([kp_ref[...], kc_ref[...]], axis=0).astype(F32)
    vband_t = jnp.concatenate([vp_ref[...], vc_ref[...]], axis=0).astype(F32).T
    lo = lax.broadcasted_iota(jnp.int32, kband.shape, 1) < SWA_HEAD
    kswap = pltpu.roll(kband, SWA_HEAD, 1)
    k_even = [jnp.where(lo, kband, 0.0).astype(BF16), jnp.where(lo, kswap, 0.0).astype(BF16)]
    k_odd = [jnp.where(lo, 0.0, kswap).astype(BF16), jnp.where(lo, 0.0, kband).astype(BF16)]
    ones_rows = (lax.broadcasted_iota(jnp.int32, (16, band), 0) == 0).astype(BF16)
    chains = [(j, kvh) for j in range(SWA_STEP_BLOCKS) for kvh in range(SWA_KV_HEADS)]

    def scores(n):
        j, kvh = chains[n]
        keys = slice(j * WINDOW, j * WINDOW + band)
        kcat = jnp.concatenate([k_even[kvh][keys], k_odd[kvh][keys]], axis=0)
        qstack = jnp.concatenate([q_ref[j * WINDOW:(j + 1) * WINDOW, (kvh * pairs + jj) * LANES:
                                        (kvh * pairs + jj + 1) * LANES] for jj in range(pairs)], axis=0)
        bias = bias0_ref if j == 0 else bias_ref
        st = _dot_nt(kcat, qstack)
        for parity in range(2):
            for jj in range(pairs):
                rows, cols = slice(parity * band, (parity + 1) * band), slice(jj * WINDOW, (jj + 1) * WINDOW)
                st_slots[n % len(st_slots)][0, rows, cols] = st[rows, cols] + bias[kvh * SWA_GROUP + 2 * jj + parity]

    for n in range(min(SWA_LOOKAHEAD, len(chains))):
        scores(n)
    for n, (j, kvh) in enumerate(chains):
        if n + SWA_LOOKAHEAD < len(chains):
            scores(n + SWA_LOOKAHEAD)
        st = st_slots[n % len(st_slots)].at[jnp.minimum(pl.program_id(1), 0)]
        v_t = vband_t[kvh * SWA_HEAD:(kvh + 1) * SWA_HEAD, j * WINDOW:j * WINDOW + band].astype(BF16)
        v_aug = jnp.concatenate([v_t, ones_rows], axis=0)
        halves = []
        for parity in range(2):
            ps, sink_terms = [], []
            for jj in range(pairs):
                blk = st[parity * band:(parity + 1) * band, jj * WINDOW:(jj + 1) * WINDOW]
                sink = sink_ref[kvh * SWA_GROUP + 2 * jj + parity] * LOG2E
                m = jnp.maximum(jnp.max(blk, axis=0, keepdims=True), sink)
                ps.append(jnp.exp2(blk - m).astype(BF16))
                sink_terms.append(jnp.exp2(sink - m))
            pv = _dot(v_aug, jnp.concatenate(ps, axis=1))
            halves.append(pv[:SWA_HEAD] / (pv[SWA_HEAD:SWA_HEAD + 1] + jnp.concatenate(sink_terms, axis=1)))
        o_t = jnp.concatenate(halves, axis=0)
        for jj in range(pairs):
            pair = kvh * pairs + jj
            o_ref[j * WINDOW:(j + 1) * WINDOW, pair * LANES:(pair + 1) * LANES] = (
                o_t[:, jj * WINDOW:(jj + 1) * WINDOW].T.astype(BF16))


def _swa(sinks, qs, ks, vs, bias_tbl):
    B, S, W = qs.shape
    nb = SWA_STEP_BLOCKS
    prev = lambda b, n: (b, jnp.maximum(nb * n - 1, 0), 0)
    cur = lambda b, n: (b, n, 0)
    tbl_block = (None,) + bias_tbl.shape[1:]
    return pl.pallas_call(
        _swa_kernel,
        grid=(B, S // (nb * WINDOW)),
        in_specs=[pl.BlockSpec(memory_space=pltpu.SMEM),
                  pl.BlockSpec((None, nb * WINDOW, W), cur),
                  pl.BlockSpec((None, WINDOW, LANES), prev), pl.BlockSpec((None, nb * WINDOW, LANES), cur),
                  pl.BlockSpec((None, WINDOW, LANES), prev), pl.BlockSpec((None, nb * WINDOW, LANES), cur),
                  pl.BlockSpec(tbl_block, lambda b, n: (jnp.minimum(n, 1), 0, 0, 0)),
                  pl.BlockSpec(tbl_block, lambda b, n: (1, 0, 0, 0))],
        out_specs=pl.BlockSpec((None, nb * WINDOW, W), cur),
        out_shape=jax.ShapeDtypeStruct((B, S, W), BF16),
        scratch_shapes=[pltpu.VMEM((1, 4 * WINDOW, SWA_GROUP // 2 * WINDOW + SCORE_PAD), F32)] * (SWA_LOOKAHEAD + 1),
        compiler_params=pltpu.CompilerParams(dimension_semantics=("arbitrary", "arbitrary"),
                                             vmem_limit_bytes=VMEM_LIMIT),
        name="swa",
    )(sinks, qs, ks, ks, vs, vs, bias_tbl, bias_tbl)


def _tail_kernel(x_ref, ym_ref, ys_ref, sh1_ref, sc1_ref, ga1_ref, sh2_ref, sc2_ref, ga2_ref,
                 gmix_ref, gmlp_ref, gfin_ref, bg_ref, wg_ref, wom_ref, wos_ref, wo_ref, w1_ref, w2_ref, o_ref):
    subs = [slice(k * TAIL_SUB, (k + 1) * TAIL_SUB) for k in range(TM_TAIL // TAIL_SUB)]
    n_ff = D_FF // FF_CHUNK
    xs = [x_ref[r, :] for r in subs]
    branches = [(_dot(ym_ref[r, :], wom_ref[...]), _dot(ys_ref[r, :], wos_ref[...])) for r in subs]
    hs = [(_rms(x, gmix_ref[...]) * (1 + sc1_ref[...]) + sh1_ref[...]).astype(BF16) for x in xs]
    logits = [_dot_nt(h, wg_ref[...]) for h in hs]
    x1s = []
    for x, (a, b), gl in zip(xs, branches, logits):
        gates = jax.nn.sigmoid(gl + bg_ref[...])
        merged = gates[:, :D_MODEL] * a + gates[:, D_MODEL:] * b
        x1s.append(x + ga1_ref[...] * _dot(merged.astype(BF16), wo_ref[...]))
    h2s = [(_rms(x1, gmlp_ref[...]) * (1 + sc2_ref[...]) + sh2_ref[...]).astype(BF16) for x1 in x1s]

    jobs = [(k, c) for k in range(len(subs)) for c in range(n_ff)]
    up = lambda k, c: _dot(h2s[k], w1_ref[:, c * FF_CHUNK:(c + 1) * FF_CHUNK])
    pending = {0: up(*jobs[0])}
    accs = [None] * len(subs)
    for n, (k, c) in enumerate(jobs):
        if n + 1 < len(jobs):
            pending[n + 1] = up(*jobs[n + 1])
        u = jnp.square(jnp.maximum(pending.pop(n), 0.0)).astype(BF16)
        down = _dot(u, w2_ref[c * FF_CHUNK:(c + 1) * FF_CHUNK, :])
        accs[k] = down if accs[k] is None else accs[k] + down
        if c == n_ff - 1:
            o_ref[subs[k], :] = _rms(x1s[k] + ga2_ref[...] * accs[k], gfin_ref[...])


def _tail(x, ym, ys, mods, gmix, gmlp, gfin, bg, wg, wom, wos, wo, w1, w2):
    B, S, D = x.shape
    tm = TM_TAIL
    tok = lambda width: pl.BlockSpec((None, tm, width), lambda b, i: (b, i, 0))
    per_b = pl.BlockSpec((None, 1, D), lambda b, i: (b, 0, 0))
    const = lambda a: pl.BlockSpec(a.shape, lambda b, i: (0,) * a.ndim, pipeline_mode=pl.Buffered(1))
    return pl.pallas_call(
        _tail_kernel,
        grid=(B, S // tm),
        in_specs=[tok(D), tok(ym.shape[-1]), tok(ys.shape[-1])] + [per_b] * 6
                 + [const(a) for a in (gmix, gmlp, gfin, bg, wg, wom, wos, wo, w1, w2)],
        out_specs=tok(D),
        out_shape=jax.ShapeDtypeStruct((B, S, D), F32),
        compiler_params=pltpu.CompilerParams(dimension_semantics=("arbitrary", "arbitrary"),
                                             vmem_limit_bytes=VMEM_LIMIT),
        name="tail",
    )(x, ym, ys, *mods, gmix, gmlp, gfin, bg, wg, wom, wos, wo, w1, w2)


def _head_slab(nope, a, b):
    r, h, _ = nope.shape
    pad = jnp.zeros((r, h, LANES - MLA_NOPE - MLA_ROPE - HALF_ROPE), nope.dtype)
    return jnp.concatenate([nope, a, b, a, pad], axis=-1).reshape(r, h * LANES)


def _rope_spread():
    m = np.zeros((LANES, 2 * LANES), np.float32)
    for term in range(3):
        for j in range(HALF_ROPE):
            m[term * MLA_ROPE + j, [ROPE_A + j, ROPE_B + j]] = 1.0
            m[term * MLA_ROPE + HALF_ROPE + j, LANES + ROPE_A + j] = -1.0
            m[term * MLA_ROPE + HALF_ROPE + j, LANES + ROPE_B + j] = 1.0
    return jnp.asarray(m, BF16)


def kernel(x, c, positions, rel_bias, ada_w, ada_b, ln_mix_g, w_in, b_gate, mla_q_norm_g, mla_kv_norm_g,
           w_uq, w_ukv, swa_sinks, w_o_mla, w_o_swa, w_o, ln_mlp_g, w_ff1, w_ff2, ln_final_g):
    B, S, D = x.shape
    assert (B, S, D) == (x.shape[0], 4096, D_MODEL) and ada_w.shape[0] == 1
    l = 0

    c_pad = jnp.pad(c, ((0, 8 - B), (0, 0)))
    mod, bias_tbl = _adaln_bias(rel_bias, c_pad, ada_w, ada_b)
    mod = mod[:B]
    mods = [m[:, None, :] for m in jnp.split(mod, 6, axis=-1)]

    wi = w_in[l].T
    o_kv = MLA_Q_RANK
    o_kr = o_kv + MLA_KV_RANK
    o_qs = o_kr + MLA_ROPE
    o_ks = o_qs + SWA_Q_HEADS * SWA_HEAD
    o_vs = o_ks + SWA_KV_HEADS * SWA_HEAD
    o_ga = o_vs + SWA_KV_HEADS * SWA_HEAD
    kr_a, kr_b = wi[o_kr:o_kr + HALF_ROPE], wi[o_kr + HALF_ROPE:o_qs]
    z = lambda n: jnp.zeros((n, D), wi.dtype)
    w1 = jnp.concatenate([
        wi[:o_kr],
        z(ROPE_A), kr_a, kr_b, kr_a, z(LANES - ROPE_A - MLA_ROPE - HALF_ROPE),
        wi[o_qs:o_ga]], axis=0).astype(BF16)
    assert w1.shape[0] == N_PROJ

    uq = w_uq[l]
    q_nope, q_a, q_b = uq[..., :MLA_NOPE], uq[..., MLA_NOPE:MLA_NOPE + HALF_ROPE], uq[..., MLA_NOPE + HALF_ROPE:]
    wq = _head_slab(q_nope, q_a, q_b).astype(BF16)
    ukv = w_ukv[l]
    k_nope, v_up = ukv[..., :MLA_NOPE], ukv[..., MLA_NOPE:]
    zr = jnp.zeros(k_nope.shape[:2] + (HALF_ROPE,), ukv.dtype)
    wkv = jnp.concatenate([_head_slab(k_nope, zr, zr),
                           v_up.reshape(MLA_KV_RANK, MLA_HEADS * MLA_V)], axis=1).astype(BF16)

    inv = (ROPE_THETA ** (-jnp.arange(HALF_ROPE, dtype=F32) / HALF_ROPE))[:, None]

    whole = lambda w: (w, 0, w.shape[1])
    gate_rows = (wi[None], o_ga, 2 * D)
    qm, km, vm, qs, ks, vs, wom, wos, wo, wf1, wf2, wg = _proj(
        x, mods[0], mods[1], ln_mix_g[l][None, :], positions[:, None, :], inv, _rope_spread(), w1,
        mla_q_norm_g[l][None, :], mla_kv_norm_g[l][None, :], wq, wkv,
        [whole(w_o_mla), whole(w_o_swa), whole(w_o), whole(w_ff1), whole(w_ff2), gate_rows])

    y_mla = _mla(qm, km, vm)
    y_swa = _swa(swa_sinks[l], qs, ks, vs, bias_tbl)

    return _tail(x, y_mla, y_swa, mods, ln_mix_g[l][None, :], ln_mlp_g[l][None, :], ln_final_g[None, :],
                 b_gate[l][None, :], wg, wom, wos, wo, wf1, wf2)
```

```python
import functools
import math

import jax
import jax.numpy as jnp
import numpy as np
from jax import lax
from jax.experimental import pallas as pl
from jax.experimental.pallas import tpu as pltpu

F32 = jnp.float32
BF16 = jnp.bfloat16

D_MODEL = 1024
MLA_HEADS = 8
MLA_Q_RANK = 256
MLA_KV_RANK = 128
MLA_NOPE = 64
MLA_ROPE = 32
MLA_V = 64
SWA_Q_HEADS = 16
SWA_KV_HEADS = 2
SWA_HEAD = 64
WINDOW = 128
REL_BUCKETS = 32
REL_MAX_DIST = 128
D_FF = 4 * D_MODEL
ROPE_THETA = 10000.0
EPS = 1e-6

LANES = 128
SCORE_PAD = LANES
HALF_ROPE = MLA_ROPE // 2
ROPE_A = MLA_NOPE
ROPE_B = MLA_NOPE + HALF_ROPE
SWA_GROUP = SWA_Q_HEADS // SWA_KV_HEADS
SWA_STEP_BLOCKS = 16
SWA_LOOKAHEAD = 1
LOG2E = math.log2(math.e)

C_QLAT = 0
C_KVLAT = C_QLAT + MLA_Q_RANK
C_KR = C_KVLAT + MLA_KV_RANK
C_QS = C_KR + LANES
C_KS = C_QS + SWA_Q_HEADS * SWA_HEAD
C_VS = C_KS + SWA_KV_HEADS * SWA_HEAD
N_PROJ = C_VS + SWA_KV_HEADS * SWA_HEAD

VMEM_LIMIT = 56 * 1024 * 1024

TM_PROJ = 1024
PROJ_SUB = 1024
TQ_MLA = 1024
TK_MLA = 256
MLA_GROUP = 4
MLA_LOOKAHEAD = 3
MLA_ACC_ROWS = MLA_V + 16
TM_TAIL = 512
TAIL_SUB = 256
FF_CHUNK = 1024


def _rms(x, g):
    return x * lax.rsqrt(jnp.mean(x * x, axis=-1, keepdims=True) + EPS) * g


def _dot(a, b):
    return jnp.dot(a, b, preferred_element_type=F32)


def _dot_nt(a, b):
    return lax.dot_general(a, b, (((1,), (1,)), ((), ())), preferred_element_type=F32)


ADALN_STEPS = 8


def _adaln_bias_kernel(rb_ref, c_ref, w_ref, b_ref, o_ref, tbl_ref):
    @pl.when(pl.program_id(0) == 0)
    def _():
        o_ref[...] = jnp.broadcast_to(b_ref[...], o_ref.shape)

    c = c_ref[...]
    act = (c * jax.nn.sigmoid(c)).astype(BF16)
    o_ref[...] += _dot(act, w_ref[...].astype(BF16))

    a = lax.broadcasted_iota(jnp.int32, (2 * WINDOW, WINDOW), 1)
    b = lax.broadcasted_iota(jnp.int32, (2 * WINDOW, WINDOW), 0)
    dist = WINDOW + a - b
    n = jnp.maximum(dist, 0)
    max_exact = REL_BUCKETS // 2
    nf = jnp.maximum(n, 1).astype(F32)
    large = max_exact + jnp.floor(jnp.log(nf / max_exact) / math.log(REL_MAX_DIST / max_exact)
                                  * (REL_BUCKETS - max_exact)).astype(jnp.int32)
    large = jnp.minimum(large, REL_BUCKETS - 1)
    bucket = jnp.where(n < max_exact, n, large)
    band_ok = (dist >= 0) & (dist < WINDOW)
    has_prev = b >= WINDOW
    neg = jnp.float32(-jnp.inf)
    heads = SWA_Q_HEADS // ADALN_STEPS
    for h in range(heads):
        head = pl.program_id(0) * heads + h
        bias = jnp.zeros((2 * WINDOW, WINDOW), F32)
        for k in range(REL_BUCKETS):
            bias = jnp.where(bucket == k, rb_ref[head, k], bias)
        tbl_ref[0, h] = jnp.where(band_ok & has_prev, bias * LOG2E, neg)
        tbl_ref[1, h] = jnp.where(band_ok, bias * LOG2E, neg)


def _adaln_bias(rel_bias, c_pad, ada_w, ada_b):
    rows = c_pad.shape[0]
    n_out = ada_w.shape[2]
    slab = D_MODEL // ADALN_STEPS
    heads = SWA_Q_HEADS // ADALN_STEPS
    return pl.pallas_call(
        _adaln_bias_kernel,
        grid=(ADALN_STEPS,),
        in_specs=[pl.BlockSpec(memory_space=pltpu.SMEM),
                  pl.BlockSpec((rows, slab), lambda j: (0, j)),
                  pl.BlockSpec((None, slab, n_out), lambda j: (0, j, 0)),
                  pl.BlockSpec((1, n_out), lambda j: (0, 0))],
        out_specs=[pl.BlockSpec((rows, n_out), lambda j: (0, 0)),
                   pl.BlockSpec((2, heads, 2 * WINDOW, WINDOW), lambda j: (0, j, 0, 0))],
        out_shape=[jax.ShapeDtypeStruct((rows, n_out), F32),
                   jax.ShapeDtypeStruct((2, SWA_Q_HEADS, 2 * WINDOW, WINDOW), F32)],
        compiler_params=pltpu.CompilerParams(dimension_semantics=("arbitrary",)),
        name="adaln_bias",
    )(rel_bias, c_pad, ada_w, ada_b)


def _proj_kernel(cast_pieces, x_ref, sh_ref, sc_ref, g_ref, pos_ref, inv_ref, spread_ref, w1_ref, gq_ref, gkv_ref,
                 wq_ref, wkv_ref, *rest):
    n_in = sum(cast_pieces)
    cast_in, (qm_ref, km_ref, vm_ref, qs_ref, ks_ref, vs_ref), cast_out = (
        rest[:n_in], rest[n_in:n_in + 6], rest[n_in + 6:])
    srcs = iter(cast_in)
    for pieces, dst in zip(cast_pieces, cast_out):
        rows = dst.shape[0] // pieces
        for k in range(pieces):
            dst[k * rows:(k + 1) * rows, :] = next(srcs)[...].astype(BF16)
    subs = [slice(k * PROJ_SUB, (k + 1) * PROJ_SUB) for k in range(TM_PROJ // PROJ_SUB)]
    ps = []
    for r in subs:
        h = _rms(x_ref[r, :], g_ref[...]) * (1 + sc_ref[...]) + sh_ref[...]
        ps.append(_dot_nt(h.astype(BF16), w1_ref[...]))

    ang = inv_ref[...] * pos_ref[...].astype(F32)
    cs = jnp.concatenate([jnp.cos(ang), jnp.sin(ang)], axis=0)
    hi = cs.astype(BF16).astype(F32)
    mid = (cs - hi).astype(BF16).astype(F32)
    lo = (cs - hi - mid).astype(BF16).astype(F32)
    terms = jnp.concatenate([hi, mid, lo, jnp.zeros_like(hi)], axis=0)
    tables = _dot(terms.T.astype(BF16), spread_ref[...])
    lane_t = lax.broadcasted_iota(jnp.int32, (PROJ_SUB, LANES), 1)
    scale = (MLA_NOPE + MLA_ROPE) ** -0.5 * LOG2E
    half = MLA_HEADS * LANES

    ups = []
    for p in ps:
        qn = _rms(p[:, C_QLAT:C_QLAT + MLA_Q_RANK], gq_ref[...]).astype(BF16)
        kvn = _rms(p[:, C_KVLAT:C_KVLAT + MLA_KV_RANK], gkv_ref[...]).astype(BF16)
        ups.append((_dot(qn, wq_ref[...]), _dot(kvn, wkv_ref[...])))

    for k, (r, p, (qq, kk)) in enumerate(zip(subs, ps, ups)):
        cos_f, sin_f = tables[r, :LANES], tables[r, LANES:]
        cos_q = jnp.where(lane_t < MLA_NOPE, 1.0, cos_f)

        def rope(t, cos):
            return t * cos + pltpu.roll(t, LANES - HALF_ROPE, 1) * sin_f

        for hd in range(MLA_HEADS):
            qm_ref[hd, r, :] = (rope(qq[:, hd * LANES:(hd + 1) * LANES], cos_q) * scale).astype(BF16)
        k_pe = rope(p[:, C_KR:C_KR + LANES], cos_f)
        for hd in range(MLA_HEADS):
            km_ref[hd, r, :] = (kk[:, hd * LANES:(hd + 1) * LANES] + k_pe).astype(BF16)
        for pr in range(MLA_HEADS // 2):
            v_t = kk[:, half + pr * LANES:half + (pr + 1) * LANES].T.astype(BF16)
            for t in range(PROJ_SUB // TK_MLA):
                vm_ref[pr, k * (PROJ_SUB // TK_MLA) + t] = v_t[:, t * TK_MLA:(t + 1) * TK_MLA]
        qs_ref[r, :] = (p[:, C_QS:C_KS] * (SWA_HEAD ** -0.5 * LOG2E)).astype(BF16)
        ks_ref[r, :] = p[:, C_KS:C_VS].astype(BF16)
        vs_ref[r, :] = p[:, C_VS:N_PROJ].astype(BF16)


def _proj(x, sh1, sc1, ln_g, pos3, inv, spread, w1, gq, gkv, wq, wkv, tail_weights):
    B, S, D = x.shape
    tm = TM_PROJ
    steps = B * (S // tm)
    step = lambda b, i: b * (S // tm) + i
    const = lambda shape: pl.BlockSpec(shape, lambda b, i: (0,) * len(shape))
    cast_in, cast_args, cast_out, cast_shapes, cast_pieces = [], [], [], [], []
    for w, row0, rows in tail_weights:
        slab = rows // steps
        piece = math.gcd(row0, slab) if row0 else slab
        assert rows % steps == 0 and piece % 16 == 0
        for k in range(slab // piece):
            cast_in.append(pl.BlockSpec((None, piece, w.shape[2]),
                                        lambda b, i, o=row0 // piece + k, n=slab // piece: (0, o + step(b, i) * n, 0)))
            cast_args.append(w)
        cast_out.append(pl.BlockSpec((slab, w.shape[2]), lambda b, i: (step(b, i), 0)))
        cast_shapes.append(jax.ShapeDtypeStruct((rows, w.shape[2]), BF16))
        cast_pieces.append(slab // piece)
    return pl.pallas_call(
        functools.partial(_proj_kernel, tuple(cast_pieces)),
        grid=(B, S // tm),
        in_specs=[pl.BlockSpec((None, tm, D), lambda b, i: (b, i, 0)),
                  pl.BlockSpec((None, 1, D), lambda b, i: (b, 0, 0)),
                  pl.BlockSpec((None, 1, D), lambda b, i: (b, 0, 0)),
                  const((1, D)),
                  pl.BlockSpec((None, 1, tm), lambda b, i: (b, 0, i)),
                  const((HALF_ROPE, 1)), const(spread.shape),
                  const(w1.shape), const(gq.shape), const(gkv.shape), const(wq.shape), const(wkv.shape)] + cast_in,
        out_specs=[pl.BlockSpec((None, MLA_HEADS, tm, LANES), lambda b, i: (b, 0, i, 0)),
                   pl.BlockSpec((None, MLA_HEADS, tm, LANES), lambda b, i: (b, 0, i, 0)),
                   pl.BlockSpec((None, MLA_HEADS // 2, tm // TK_MLA, LANES, TK_MLA), lambda b, i: (b, 0, i, 0, 0)),
                   pl.BlockSpec((None, tm, SWA_Q_HEADS * SWA_HEAD), lambda b, i: (b, i, 0)),
                   pl.BlockSpec((None, tm, LANES), lambda b, i: (b, i, 0)),
                   pl.BlockSpec((None, tm, LANES), lambda b, i: (b, i, 0))] + cast_out,
        out_shape=[jax.ShapeDtypeStruct((B, MLA_HEADS, S, LANES), BF16),
                   jax.ShapeDtypeStruct((B, MLA_HEADS, S, LANES), BF16),
                   jax.ShapeDtypeStruct((B, MLA_HEADS // 2, S // TK_MLA, LANES, TK_MLA), BF16),
                   jax.ShapeDtypeStruct((B, S, SWA_Q_HEADS * SWA_HEAD), BF16),
                   jax.ShapeDtypeStruct((B, S, LANES), BF16),
                   jax.ShapeDtypeStruct((B, S, LANES), BF16)]
                  + cast_shapes,
        compiler_params=pltpu.CompilerParams(dimension_semantics=("arbitrary", "arbitrary"),
                                             vmem_limit_bytes=VMEM_LIMIT),
        name="proj",
    )(x, sh1, sc1, ln_g, pos3, inv, spread, w1, gq, gkv, wq, wkv, *cast_args)


def _mla_kernel(q_ref, k_ref, vt_ref, o_ref, qt_sc, acc_sc, *st_slots):
    tq, tk = TQ_MLA, TK_MLA
    nc = tq // LANES
    i = pl.program_id(2)
    for hd in range(MLA_GROUP):
        qt_sc[hd] = q_ref[hd].astype(F32).T.astype(BF16)
    acc_sc[...] = jnp.zeros_like(acc_sc)

    per_tile = tq // tk

    ones_rows = (lax.broadcasted_iota(jnp.int32, (MLA_ACC_ROWS - MLA_V, tk), 0) == 0).astype(BF16)
    r_iota = lax.broadcasted_iota(jnp.int32, (tk, LANES), 0)
    c_iota = lax.broadcasted_iota(jnp.int32, (tk, LANES), 1)

    blocks = [(d, hd) for d in range(per_tile) for hd in range(MLA_GROUP)]
    assert len(blocks) % len(st_slots) == 0 and MLA_LOOKAHEAD <= MLA_GROUP

    def scores(t, n, first_col):
        d, hd = blocks[n]
        k = k_ref[hd, pl.ds(pl.multiple_of((t * per_tile + d) * tk, tk), tk), :]
        st_slots[n % len(st_slots)][0, :, first_col:tq] = _dot(k, qt_sc[hd, :, first_col:])

    def tile(t, ms_all, diagonal):
        first_cc = [(d * tk) // LANES if diagonal else 0 for d, _ in blocks]
        ms_all = list(ms_all)
        for n, (d, hd) in enumerate(blocks):
            ahead = n + MLA_LOOKAHEAD
            if ahead < len(blocks):
                scores(t, ahead, first_cc[ahead] * LANES)
            elif not diagonal:
                scores(t + 1, ahead - len(blocks), 0)
            ms = list(ms_all[hd])
            ps, alphas = [], []
            for cc in range(first_cc[n], nc):
                blk = st_slots[n % len(st_slots)][0, :, cc * LANES:(cc + 1) * LANES]
                if diagonal and cc * LANES < (d + 1) * tk - 1:
                    blk = jnp.where(r_iota + (d * tk - cc * LANES) <= c_iota, blk, -jnp.inf)
                m_new = jnp.maximum(ms[cc], jnp.max(blk, axis=0, keepdims=True))
                alphas.append(jnp.exp2(ms[cc] - m_new))
                ps.append(jnp.exp2(blk - m_new).astype(BF16))
                ms[cc] = m_new
            v_t = vt_ref[hd // 2, t * per_tile + d, (hd % 2) * MLA_V:(hd % 2 + 1) * MLA_V, :]
            pv = _dot(jnp.concatenate([v_t, ones_rows], axis=0), jnp.concatenate(ps, axis=1))
            rows = slice(hd * MLA_ACC_ROWS, (hd + 1) * MLA_ACC_ROWS)
            cols = slice(first_cc[n] * LANES, tq)
            acc_sc[rows, cols] = acc_sc[rows, cols] * jnp.concatenate(alphas, axis=1) + pv
            ms_all[hd] = tuple(ms)
        return tuple(ms_all)

    init = tuple(tuple(jnp.full((1, LANES), -jnp.inf, F32) for _ in range(nc)) for _ in range(MLA_GROUP))
    for n in range(MLA_LOOKAHEAD):
        scores(0, n, 0)
    ms_all = lax.fori_loop(0, i, lambda t, c: tile(t, c, False), init)
    tile(i, ms_all, True)
    outs = []
    for hd in range(MLA_GROUP):
        acc = acc_sc[hd * MLA_ACC_ROWS:(hd + 1) * MLA_ACC_ROWS, :]
        outs.append(acc[:MLA_V] / acc[MLA_V:MLA_V + 1])
    o_ref[...] = jnp.concatenate(outs, axis=0).T.astype(BF16)


def _mla(qm, km, vt):
    B, H, S, _ = qm.shape
    tq, tk, g = TQ_MLA, TK_MLA, MLA_GROUP
    return pl.pallas_call(
        _mla_kernel,
        grid=(B, H // g, S // tq),
        in_specs=[pl.BlockSpec((None, g, tq, LANES), lambda b, p, i: (b, p, i, 0)),
                  pl.BlockSpec((None, g, S, LANES), lambda b, p, i: (b, p, 0, 0)),
                  pl.BlockSpec((None, g // 2, S // tk, LANES, tk), lambda b, p, i: (b, p, 0, 0, 0))],
        out_specs=pl.BlockSpec((None, tq, g * MLA_V), lambda b, p, i: (b, i, p)),
        out_shape=jax.ShapeDtypeStruct((B, S, H * MLA_V), BF16),
        scratch_shapes=[pltpu.VMEM((g, LANES, tq), BF16), pltpu.VMEM((g * MLA_ACC_ROWS, tq), F32),
                        ] + [pltpu.VMEM((1, tk, tq + SCORE_PAD), F32)] * (MLA_LOOKAHEAD + 1),
        compiler_params=pltpu.CompilerParams(dimension_semantics=("arbitrary",) * 3,
                                             vmem_limit_bytes=VMEM_LIMIT),
        name="mla",
    )(qm, km, vt)


def _swa_kernel(sink_ref, q_ref, kp_ref, kc_ref, vp_ref, vc_ref, bias0_ref, bias_ref, o_ref, *st_slots):
    band = 2 * WINDOW
    pairs = SWA_GROUP // 2
    kband = jnp.concatenate([kp_ref[...], kc_ref[...]], axis=0).astype(F32)
    vband_t = jnp.concatenate([vp_ref[...], vc_ref[...]], axis=0).astype(F32).T
    lo = lax.broadcasted_iota(jnp.int32, kband.shape, 1) < SWA_HEAD
    kswap = pltpu.roll(kband, SWA_HEAD, 1)
    k_even = [jnp.where(lo, kband, 0.0).astype(BF16), jnp.where(lo, kswap, 0.0).astype(BF16)]
    k_odd = [jnp.where(lo, 0.0, kswap).astype(BF16), jnp.where(lo, 0.0, kband).astype(BF16)]
    ones_rows = (lax.broadcasted_iota(jnp.int32, (16, band), 0) == 0).astype(BF16)
    chains = [(j, kvh) for j in range(SWA_STEP_BLOCKS) for kvh in range(SWA_KV_HEADS)]

    def scores(n):
        j, kvh = chains[n]
        keys = slice(j * WINDOW, j * WINDOW + band)
        kcat = jnp.concatenate([k_even[kvh][keys], k_odd[kvh][keys]], axis=0)
        qstack = jnp.concatenate([q_ref[j * WINDOW:(j + 1) * WINDOW, (kvh * pairs + jj) * LANES:
                                        (kvh * pairs + jj + 1) * LANES] for jj in range(pairs)], axis=0)
        bias = bias0_ref if j == 0 else bias_ref
        st = _dot_nt(kcat, qstack)
        for parity in range(2):
            for jj in range(pairs):
                rows, cols = slice(parity * band, (parity + 1) * band), slice(jj * WINDOW, (jj + 1) * WINDOW)
                st_slots[n % len(st_slots)][0, rows, cols] = st[rows, cols] + bias[kvh * SWA_GROUP + 2 * jj + parity]

    for n in range(min(SWA_LOOKAHEAD, len(chains))):
        scores(n)
    for n, (j, kvh) in enumerate(chains):
        if n + SWA_LOOKAHEAD < len(chains):
            scores(n + SWA_LOOKAHEAD)
        st = st_slots[n % len(st_slots)].at[jnp.minimum(pl.program_id(1), 0)]
        v_t = vband_t[kvh * SWA_HEAD:(kvh + 1) * SWA_HEAD, j * WINDOW:j * WINDOW + band].astype(BF16)
        v_aug = jnp.concatenate([v_t, ones_rows], axis=0)
        halves = []
        for parity in range(2):
            ps, sink_terms = [], []
            for jj in range(pairs):
                blk = st[parity * band:(parity + 1) * band, jj * WINDOW:(jj + 1) * WINDOW]
                sink = sink_ref[kvh * SWA_GROUP + 2 * jj + parity] * LOG2E
                m = jnp.maximum(jnp.max(blk, axis=0, keepdims=True), sink)
                ps.append(jnp.exp2(blk - m).astype(BF16))
                sink_terms.append(jnp.exp2(sink - m))
            pv = _dot(v_aug, jnp.concatenate(ps, axis=1))
            halves.append(pv[:SWA_HEAD] / (pv[SWA_HEAD:SWA_HEAD + 1] + jnp.concatenate(sink_terms, axis=1)))
        o_t = jnp.concatenate(halves, axis=0)
        for jj in range(pairs):
            pair = kvh * pairs + jj
            o_ref[j * WINDOW:(j + 1) * WINDOW, pair * LANES:(pair + 1) * LANES] = (
                o_t[:, jj * WINDOW:(jj + 1) * WINDOW].T.astype(BF16))


def _swa(sinks, qs, ks, vs, bias_tbl):
    B, S, W = qs.shape
    nb = SWA_STEP_BLOCKS
    prev = lambda b, n: (b, jnp.maximum(nb * n - 1, 0), 0)
    cur = lambda b, n: (b, n, 0)
    tbl_block = (None,) + bias_tbl.shape[1:]
    return pl.pallas_call(
        _swa_kernel,
        grid=(B, S // (nb * WINDOW)),
        in_specs=[pl.BlockSpec(memory_space=pltpu.SMEM),
                  pl.BlockSpec((None, nb * WINDOW, W), cur),
                  pl.BlockSpec((None, WINDOW, LANES), prev), pl.BlockSpec((None, nb * WINDOW, LANES), cur),
                  pl.BlockSpec((None, WINDOW, LANES), prev), pl.BlockSpec((None, nb * WINDOW, LANES), cur),
                  pl.BlockSpec(tbl_block, lambda b, n: (jnp.minimum(n, 1), 0, 0, 0)),
                  pl.BlockSpec(tbl_block, lambda b, n: (1, 0, 0, 0))],
        out_specs=pl.BlockSpec((None, nb * WINDOW, W), cur),
        out_shape=jax.ShapeDtypeStruct((B, S, W), BF16),
        scratch_shapes=[pltpu.VMEM((1, 4 * WINDOW, SWA_GROUP // 2 * WINDOW + SCORE_PAD), F32)] * (SWA_LOOKAHEAD + 1),
        compiler_params=pltpu.CompilerParams(dimension_semantics=("arbitrary", "arbitrary"),
                                             vmem_limit_bytes=VMEM_LIMIT),
        name="swa",
    )(sinks, qs, ks, ks, vs, vs, bias_tbl, bias_tbl)


def _tail_kernel(x_ref, ym_ref, ys_ref, sh1_ref, sc1_ref, ga1_ref, sh2_ref, sc2_ref, ga2_ref,
                 gmix_ref, gmlp_ref, gfin_ref, bg_ref, wg_ref, wom_ref, wos_ref, wo_ref, w1_ref, w2_ref, o_ref):
    subs = [slice(k * TAIL_SUB, (k + 1) * TAIL_SUB) for k in range(TM_TAIL // TAIL_SUB)]
    n_ff = D_FF // FF_CHUNK
    xs = [x_ref[r, :] for r in subs]
    branches = [(_dot(ym_ref[r, :], wom_ref[...]), _dot(ys_ref[r, :], wos_ref[...])) for r in subs]
    hs = [(_rms(x, gmix_ref[...]) * (1 + sc1_ref[...]) + sh1_ref[...]).astype(BF16) for x in xs]
    logits = [_dot_nt(h, wg_ref[...]) for h in hs]
    x1s = []
    for x, (a, b), gl in zip(xs, branches, logits):
        gates = jax.nn.sigmoid(gl + bg_ref[...])
        merged = gates[:, :D_MODEL] * a + gates[:, D_MODEL:] * b
        x1s.append(x + ga1_ref[...] * _dot(merged.astype(BF16), wo_ref[...]))
    h2s = [(_rms(x1, gmlp_ref[...]) * (1 + sc2_ref[...]) + sh2_ref[...]).astype(BF16) for x1 in x1s]

    jobs = [(k, c) for k in range(len(subs)) for c in range(n_ff)]
    up = lambda k, c: _dot(h2s[k], w1_ref[:, c * FF_CHUNK:(c + 1) * FF_CHUNK])
    pending = {0: up(*jobs[0])}
    accs = [None] * len(subs)
    for n, (k, c) in enumerate(jobs):
        if n + 1 < len(jobs):
            pending[n + 1] = up(*jobs[n + 1])
        u = jnp.square(jnp.maximum(pending.pop(n), 0.0)).astype(BF16)
        down = _dot(u, w2_ref[c * FF_CHUNK:(c + 1) * FF_CHUNK, :])
        accs[k] = down if accs[k] is None else accs[k] + down
        if c == n_ff - 1:
            o_ref[subs[k], :] = _rms(x1s[k] + ga2_ref[...] * accs[k], gfin_ref[...])


def _tail(x, ym, ys, mods, gmix, gmlp, gfin, bg, wg, wom, wos, wo, w1, w2):
    B, S, D = x.shape
    tm = TM_TAIL
    tok = lambda width: pl.BlockSpec((None, tm, width), lambda b, i: (b, i, 0))
    per_b = pl.BlockSpec((None, 1, D), lambda b, i: (b, 0, 0))
    const = lambda a: pl.BlockSpec(a.shape, lambda b, i: (0,) * a.ndim, pipeline_mode=pl.Buffered(1))
    return pl.pallas_call(
        _tail_kernel,
        grid=(B, S // tm),
        in_specs=[tok(D), tok(ym.shape[-1]), tok(ys.shape[-1])] + [per_b] * 6
                 + [const(a) for a in (gmix, gmlp, gfin, bg, wg, wom, wos, wo, w1, w2)],
        out_specs=tok(D),
        out_shape=jax.ShapeDtypeStruct((B, S, D), F32),
        compiler_params=pltpu.CompilerParams(dimension_semantics=("arbitrary", "arbitrary"),
                                             vmem_limit_bytes=VMEM_LIMIT),
        name="tail",
    )(x, ym, ys, *mods, gmix, gmlp, gfin, bg, wg, wom, wos, wo, w1, w2)


def _head_slab(nope, a, b):
    r, h, _ = nope.shape
    pad = jnp.zeros((r, h, LANES - MLA_NOPE - MLA_ROPE - HALF_ROPE), nope.dtype)
    return jnp.concatenate([nope, a, b, a, pad], axis=-1).reshape(r, h * LANES)


def _rope_spread():
    m = np.zeros((LANES, 2 * LANES), np.float32)
    for term in range(3):
        for j in range(HALF_ROPE):
            m[term * MLA_ROPE + j, [ROPE_A + j, ROPE_B + j]] = 1.0
            m[term * MLA_ROPE + HALF_ROPE + j, LANES + ROPE_A + j] = -1.0
            m[term * MLA_ROPE + HALF_ROPE + j, LANES + ROPE_B + j] = 1.0
    return jnp.asarray(m, BF16)


def kernel(x, c, positions, rel_bias, ada_w, ada_b, ln_mix_g, w_in, b_gate, mla_q_norm_g, mla_kv_norm_g,
           w_uq, w_ukv, swa_sinks, w_o_mla, w_o_swa, w_o, ln_mlp_g, w_ff1, w_ff2, ln_final_g):
    B, S, D = x.shape
    assert (B, S, D) == (x.shape[0], 4096, D_MODEL) and ada_w.shape[0] == 1
    l = 0

    c_pad = jnp.pad(c, ((0, 8 - B), (0, 0)))
    mod, bias_tbl = _adaln_bias(rel_bias, c_pad, ada_w, ada_b)
    mod = mod[:B]
    mods = [m[:, None, :] for m in jnp.split(mod, 6, axis=-1)]

    wi = w_in[l].T
    o_kv = MLA_Q_RANK
    o_kr = o_kv + MLA_KV_RANK
    o_qs = o_kr + MLA_ROPE
    o_ks = o_qs + SWA_Q_HEADS * SWA_HEAD
    o_vs = o_ks + SWA_KV_HEADS * SWA_HEAD
    o_ga = o_vs + SWA_KV_HEADS * SWA_HEAD
    kr_a, kr_b = wi[o_kr:o_kr + HALF_ROPE], wi[o_kr + HALF_ROPE:o_qs]
    z = lambda n: jnp.zeros((n, D), wi.dtype)
    w1 = jnp.concatenate([
        wi[:o_kr],
        z(ROPE_A), kr_a, kr_b, kr_a, z(LANES - ROPE_A - MLA_ROPE - HALF_ROPE),
        wi[o_qs:o_ga]], axis=0).astype(BF16)
    assert w1.shape[0] == N_PROJ

    uq = w_uq[l]
    q_nope, q_a, q_b = uq[..., :MLA_NOPE], uq[..., MLA_NOPE:MLA_NOPE + HALF_ROPE], uq[..., MLA_NOPE + HALF_ROPE:]
    wq = _head_slab(q_nope, q_a, q_b).astype(BF16)
    ukv = w_ukv[l]
    k_nope, v_up = ukv[..., :MLA_NOPE], ukv[..., MLA_NOPE:]
    zr = jnp.zeros(k_nope.shape[:2] + (HALF_ROPE,), ukv.dtype)
    wkv = jnp.concatenate([_head_slab(k_nope, zr, zr),
                           v_up.reshape(MLA_KV_RANK, MLA_HEADS * MLA_V)], axis=1).astype(BF16)

    inv = (ROPE_THETA ** (-jnp.arange(HALF_ROPE, dtype=F32) / HALF_ROPE))[:, None]

    whole = lambda w: (w, 0, w.shape[1])
    gate_rows = (wi[None], o_ga, 2 * D)
    qm, km, vm, qs, ks, vs, wom, wos, wo, wf1, wf2, wg = _proj(
        x, mods[0], mods[1], ln_mix_g[l][None, :], positions[:, None, :], inv, _rope_spread(), w1,
        mla_q_norm_g[l][None, :], mla_kv_norm_g[l][None, :], wq, wkv,
        [whole(w_o_mla), whole(w_o_swa), whole(w_o), whole(w_ff1), whole(w_ff2), gate_rows])

    y_mla = _mla(qm, km, vm)
    y_swa = _swa(swa_sinks[l], qs, ks, vs, bias_tbl)

    return _tail(x, y_mla, y_swa, mods, ln_mix_g[l][None, :], ln_mlp_g[l][None, :], ln_final_g[None, :],
                 b_gate[l][None, :], wg, wom, wos, wo, wf1, wf2)
```

```python
import functools
import math

import jax
import jax.numpy as jnp
import numpy as np
from jax import lax
from jax.experimental import pallas as pl
from jax.experimental.pallas import tpu as pltpu

F32 = jnp.float32
BF16 = jnp.bfloat16

D_MODEL = 1024
MLA_HEADS = 8
MLA_Q_RANK = 256
MLA_KV_RANK = 128
MLA_NOPE = 64
MLA_ROPE = 32
MLA_V = 64
SWA_Q_HEADS = 16
SWA_KV_HEADS = 2
SWA_HEAD = 64
WINDOW = 128
REL_BUCKETS = 32
REL_MAX_DIST = 128
D_FF = 4 * D_MODEL
ROPE_THETA = 10000.0
EPS = 1e-6

LANES = 128
SCORE_PAD = LANES
HALF_ROPE = MLA_ROPE // 2
ROPE_A = MLA_NOPE
ROPE_B = MLA_NOPE + HALF_ROPE
SWA_GROUP = SWA_Q_HEADS // SWA_KV_HEADS
SWA_STEP_BLOCKS = 16
SWA_LOOKAHEAD = 1
LOG2E = math.log2(math.e)

C_QLAT = 0
C_KVLAT = C_QLAT + MLA_Q_RANK
C_KR = C_KVLAT + MLA_KV_RANK
C_QS = C_KR + LANES
C_KS = C_QS + SWA_Q_HEADS * SWA_HEAD
C_VS = C_KS + SWA_KV_HEADS * SWA_HEAD
N_PROJ = C_VS + SWA_KV_HEADS * SWA_HEAD

VMEM_LIMIT = 56 * 1024 * 1024

TM_PROJ = 1024
PROJ_SUB = 512
TQ_MLA = 1024
TK_MLA = 256
MLA_GROUP = 4
MLA_LOOKAHEAD = 3
MLA_ACC_ROWS = MLA_V + 16
TM_TAIL = 512
TAIL_SUB = 256
FF_CHUNK = 1024


def _rms(x, g):
    return x * lax.rsqrt(jnp.mean(x * x, axis=-1, keepdims=True) + EPS) * g


def _dot(a, b):
    return jnp.dot(a, b, preferred_element_type=F32)


def _dot_nt(a, b):
    return lax.dot_general(a, b, (((1,), (1,)), ((), ())), preferred_element_type=F32)


ADALN_STEPS = 8


def _adaln_bias_kernel(rb_ref, c_ref, w_ref, b_ref, o_ref, tbl_ref):
    @pl.when(pl.program_id(0) == 0)
    def _():
        o_ref[...] = jnp.broadcast_to(b_ref[...], o_ref.shape)

    c = c_ref[...]
    act = (c * jax.nn.sigmoid(c)).astype(BF16)
    o_ref[...] += _dot(act, w_ref[...].astype(BF16))

    a = lax.broadcasted_iota(jnp.int32, (2 * WINDOW, WINDOW), 1)
    b = lax.broadcasted_iota(jnp.int32, (2 * WINDOW, WINDOW), 0)
    dist = WINDOW + a - b
    n = jnp.maximum(dist, 0)
    max_exact = REL_BUCKETS // 2
    nf = jnp.maximum(n, 1).astype(F32)
    large = max_exact + jnp.floor(jnp.log(nf / max_exact) / math.log(REL_MAX_DIST / max_exact)
                                  * (REL_BUCKETS - max_exact)).astype(jnp.int32)
    large = jnp.minimum(large, REL_BUCKETS - 1)
    bucket = jnp.where(n < max_exact, n, large)
    band_ok = (dist >= 0) & (dist < WINDOW)
    has_prev = b >= WINDOW
    neg = jnp.float32(-jnp.inf)
    heads = SWA_Q_HEADS // ADALN_STEPS
    for h in range(heads):
        head = pl.program_id(0) * heads + h
        bias = jnp.zeros((2 * WINDOW, WINDOW), F32)
        for k in range(REL_BUCKETS):
            bias = jnp.where(bucket == k, rb_ref[head, k], bias)
        tbl_ref[0, h] = jnp.where(band_ok & has_prev, bias * LOG2E, neg)
        tbl_ref[1, h] = jnp.where(band_ok, bias * LOG2E, neg)


def _adaln_bias(rel_bias, c_pad, ada_w, ada_b):
    rows = c_pad.shape[0]
    n_out = ada_w.shape[2]
    slab = D_MODEL // ADALN_STEPS
    heads = SWA_Q_HEADS // ADALN_STEPS
    return pl.pallas_call(
        _adaln_bias_kernel,
        grid=(ADALN_STEPS,),
        in_specs=[pl.BlockSpec(memory_space=pltpu.SMEM),
                  pl.BlockSpec((rows, slab), lambda j: (0, j)),
                  pl.BlockSpec((None, slab, n_out), lambda j: (0, j, 0)),
                  pl.BlockSpec((1, n_out), lambda j: (0, 0))],
        out_specs=[pl.BlockSpec((rows, n_out), lambda j: (0, 0)),
                   pl.BlockSpec((2, heads, 2 * WINDOW, WINDOW), lambda j: (0, j, 0, 0))],
        out_shape=[jax.ShapeDtypeStruct((rows, n_out), F32),
                   jax.ShapeDtypeStruct((2, SWA_Q_HEADS, 2 * WINDOW, WINDOW), F32)],
        compiler_params=pltpu.CompilerParams(dimension_semantics=("arbitrary",)),
        name="adaln_bias",
    )(rel_bias, c_pad, ada_w, ada_b)


def _proj_kernel(cast_pieces, x_ref, sh_ref, sc_ref, g_ref, pos_ref, inv_ref, spread_ref, w1_ref, gq_ref, gkv_ref,
                 wq_ref, wkv_ref, *rest):
    n_in = sum(cast_pieces)
    cast_in, (qm_ref, km_ref, vm_ref, qs_ref, ks_ref, vs_ref), cast_out = (
        rest[:n_in], rest[n_in:n_in + 6], rest[n_in + 6:])
    srcs = iter(cast_in)
    for pieces, dst in zip(cast_pieces, cast_out):
        rows = dst.shape[0] // pieces
        for k in range(pieces):
            dst[k * rows:(k + 1) * rows, :] = next(srcs)[...].astype(BF16)
    subs = [slice(k * PROJ_SUB, (k + 1) * PROJ_SUB) for k in range(TM_PROJ // PROJ_SUB)]
    ps = []
    hold = 0.0
    for r in subs:
        h = _rms(x_ref[r, :], g_ref[...]) * (1 + (sc_ref[...] + hold)) + sh_ref[...]
        ps.append(_dot_nt(h.astype(BF16), w1_ref[...]))
        zero = jnp.minimum(jnp.abs(ps[-1][0:1, 0:LANES]) * 0.0, 0.0)
        hold = jnp.concatenate([zero] * (D_MODEL // LANES), axis=1)

    ang = inv_ref[...] * pos_ref[...].astype(F32) + hold[0:1, 0:1]
    cs = jnp.concatenate([jnp.cos(ang), jnp.sin(ang)], axis=0)
    hi = cs.astype(BF16).astype(F32)
    mid = (cs - hi).astype(BF16).astype(F32)
    lo = (cs - hi - mid).astype(BF16).astype(F32)
    terms = jnp.concatenate([hi, mid, lo, jnp.zeros_like(hi)], axis=0)
    tables = _dot(terms.T.astype(BF16), spread_ref[...])
    lane_t = lax.broadcasted_iota(jnp.int32, (PROJ_SUB, LANES), 1)
    scale = (MLA_NOPE + MLA_ROPE) ** -0.5 * LOG2E
    half = MLA_HEADS * LANES

    ups = []
    for p in ps:
        qn = _rms(p[:, C_QLAT:C_QLAT + MLA_Q_RANK], gq_ref[...]).astype(BF16)
        kvn = _rms(p[:, C_KVLAT:C_KVLAT + MLA_KV_RANK], gkv_ref[...]).astype(BF16)
        ups.append((_dot(qn, wq_ref[...]), _dot(kvn, wkv_ref[...])))

    for k, (r, p, (qq, kk)) in enumerate(zip(subs, ps, ups)):
        cos_f, sin_f = tables[r, :LANES], tables[r, LANES:]
        cos_q = jnp.where(lane_t < MLA_NOPE, 1.0, cos_f)

        def rope(t, cos):
            return t * cos + pltpu.roll(t, LANES - HALF_ROPE, 1) * sin_f

        for hd in range(MLA_HEADS):
            qm_ref[hd, r, :] = (rope(qq[:, hd * LANES:(hd + 1) * LANES], cos_q) * scale).astype(BF16)
        k_pe = rope(p[:, C_KR:C_KR + LANES], cos_f)
        for hd in range(MLA_HEADS):
            km_ref[hd, r, :] = (kk[:, hd * LANES:(hd + 1) * LANES] + k_pe).astype(BF16)
        for pr in range(MLA_HEADS // 2):
            v_t = kk[:, half + pr * LANES:half + (pr + 1) * LANES].T.astype(BF16)
            for t in range(PROJ_SUB // TK_MLA):
                vm_ref[pr, k * (PROJ_SUB // TK_MLA) + t] = v_t[:, t * TK_MLA:(t + 1) * TK_MLA]
        qs_ref[r, :] = (p[:, C_QS:C_KS] * (SWA_HEAD ** -0.5 * LOG2E)).astype(BF16)
        ks_ref[r, :] = p[:, C_KS:C_VS].astype(BF16)
        vs_ref[r, :] = p[:, C_VS:N_PROJ].astype(BF16)


def _proj(x, sh1, sc1, ln_g, pos3, inv, spread, w1, gq, gkv, wq, wkv, tail_weights):
    B, S, D = x.shape
    tm = TM_PROJ
    steps = B * (S // tm)
    step = lambda b, i: b * (S // tm) + i
    const = lambda shape: pl.BlockSpec(shape, lambda b, i: (0,) * len(shape))
    cast_in, cast_args, cast_out, cast_shapes, cast_pieces = [], [], [], [], []
    for w, row0, rows in tail_weights:
        slab = rows // steps
        piece = math.gcd(row0, slab) if row0 else slab
        assert rows % steps == 0 and piece % 16 == 0
        for k in range(slab // piece):
            cast_in.append(pl.BlockSpec((None, piece, w.shape[2]),
                                        lambda b, i, o=row0 // piece + k, n=slab // piece: (0, o + step(b, i) * n, 0)))
            cast_args.append(w)
        cast_out.append(pl.BlockSpec((slab, w.shape[2]), lambda b, i: (step(b, i), 0)))
        cast_shapes.append(jax.ShapeDtypeStruct((rows, w.shape[2]), BF16))
        cast_pieces.append(slab // piece)
    return pl.pallas_call(
        functools.partial(_proj_kernel, tuple(cast_pieces)),
        grid=(B, S // tm),
        in_specs=[pl.BlockSpec((None, tm, D), lambda b, i: (b, i, 0)),
                  pl.BlockSpec((None, 1, D), lambda b, i: (b, 0, 0)),
                  pl.BlockSpec((None, 1, D), lambda b, i: (b, 0, 0)),
                  const((1, D)),
                  pl.BlockSpec((None, 1, tm), lambda b, i: (b, 0, i)),
                  const((HALF_ROPE, 1)), const(spread.shape),
                  const(w1.shape), const(gq.shape), const(gkv.shape), const(wq.shape), const(wkv.shape)] + cast_in,
        out_specs=[pl.BlockSpec((None, MLA_HEADS, tm, LANES), lambda b, i: (b, 0, i, 0)),
                   pl.BlockSpec((None, MLA_HEADS, tm, LANES), lambda b, i: (b, 0, i, 0)),
                   pl.BlockSpec((None, MLA_HEADS // 2, tm // TK_MLA, LANES, TK_MLA), lambda b, i: (b, 0, i, 0, 0)),
                   pl.BlockSpec((None, tm, SWA_Q_HEADS * SWA_HEAD), lambda b, i: (b, i, 0)),
                   pl.BlockSpec((None, tm, LANES), lambda b, i: (b, i, 0)),
                   pl.BlockSpec((None, tm, LANES), lambda b, i: (b, i, 0))] + cast_out,
        out_shape=[jax.ShapeDtypeStruct((B, MLA_HEADS, S, LANES), BF16),
                   jax.ShapeDtypeStruct((B, MLA_HEADS, S, LANES), BF16),
                   jax.ShapeDtypeStruct((B, MLA_HEADS // 2, S // TK_MLA, LANES, TK_MLA), BF16),
                   jax.ShapeDtypeStruct((B, S, SWA_Q_HEADS * SWA_HEAD), BF16),
                   jax.ShapeDtypeStruct((B, S, LANES), BF16),
                   jax.ShapeDtypeStruct((B, S, LANES), BF16)]
                  + cast_shapes,
        compiler_params=pltpu.CompilerParams(dimension_semantics=("arbitrary", "arbitrary"),
                                             vmem_limit_bytes=VMEM_LIMIT),
        name="proj",
    )(x, sh1, sc1, ln_g, pos3, inv, spread, w1, gq, gkv, wq, wkv, *cast_args)


def _mla_kernel(q_ref, k_ref, vt_ref, o_ref, qt_sc, acc_sc, *st_slots):
    tq, tk = TQ_MLA, TK_MLA
    nc = tq // LANES
    i = pl.program_id(2)
    for hd in range(MLA_GROUP):
        qt_sc[hd] = q_ref[hd].astype(F32).T.astype(BF16)
    acc_sc[...] = jnp.zeros_like(acc_sc)

    per_tile = tq // tk

    ones_rows = (lax.broadcasted_iota(jnp.int32, (MLA_ACC_ROWS - MLA_V, tk), 0) == 0).astype(BF16)
    r_iota = lax.broadcasted_iota(jnp.int32, (tk, LANES), 0)
    c_iota = lax.broadcasted_iota(jnp.int32, (tk, LANES), 1)

    blocks = [(d, hd) for d in range(per_tile) for hd in range(MLA_GROUP)]
    assert len(blocks) % len(st_slots) == 0 and MLA_LOOKAHEAD <= MLA_GROUP

    def scores(t, n, first_col):
        d, hd = blocks[n]
        k = k_ref[hd, pl.ds(pl.multiple_of((t * per_tile + d) * tk, tk), tk), :]
        st_slots[n % len(st_slots)][0, :, first_col:tq] = _dot(k, qt_sc[hd, :, first_col:])

    def tile(t, ms_all, diagonal):
        first_cc = [(d * tk) // LANES if diagonal else 0 for d, _ in blocks]
        ms_all = list(ms_all)
        for n, (d, hd) in enumerate(blocks):
            ahead = n + MLA_LOOKAHEAD
            if ahead < len(blocks):
                scores(t, ahead, first_cc[ahead] * LANES)
            elif not diagonal:
                scores(t + 1, ahead - len(blocks), 0)
            ms = list(ms_all[hd])
            ps, alphas = [], []
            for cc in range(first_cc[n], nc):
                blk = st_slots[n % len(st_slots)][0, :, cc * LANES:(cc + 1) * LANES]
                if diagonal and cc * LANES < (d + 1) * tk - 1:
                    blk = jnp.where(r_iota + (d * tk - cc * LANES) <= c_iota, blk, -jnp.inf)
                m_new = jnp.maximum(ms[cc], jnp.max(blk, axis=0, keepdims=True))
                alphas.append(jnp.exp2(ms[cc] - m_new))
                ps.append(jnp.exp2(blk - m_new).astype(BF16))
                ms[cc] = m_new
            v_t = vt_ref[hd // 2, t * per_tile + d, (hd % 2) * MLA_V:(hd % 2 + 1) * MLA_V, :]
            pv = _dot(jnp.concatenate([v_t, ones_rows], axis=0), jnp.concatenate(ps, axis=1))
            rows = slice(hd * MLA_ACC_ROWS, (hd + 1) * MLA_ACC_ROWS)
            cols = slice(first_cc[n] * LANES, tq)
            acc_sc[rows, cols] = acc_sc[rows, cols] * jnp.concatenate(alphas, axis=1) + pv
            ms_all[hd] = tuple(ms)
        return tuple(ms_all)

    init = tuple(tuple(jnp.full((1, LANES), -jnp.inf, F32) for _ in range(nc)) for _ in range(MLA_GROUP))
    for n in range(MLA_LOOKAHEAD):
        scores(0, n, 0)
    ms_all = lax.fori_loop(0, i, lambda t, c: tile(t, c, False), init)
    tile(i, ms_all, True)
    outs = []
    for hd in range(MLA_GROUP):
        acc = acc_sc[hd * MLA_ACC_ROWS:(hd + 1) * MLA_ACC_ROWS, :]
        outs.append(acc[:MLA_V] / acc[MLA_V:MLA_V + 1])
    o_ref[...] = jnp.concatenate(outs, axis=0).T.astype(BF16)


def _mla(qm, km, vt):
    B, H, S, _ = qm.shape
    tq, tk, g = TQ_MLA, TK_MLA, MLA_GROUP
    return pl.pallas_call(
        _mla_kernel,
        grid=(B, H // g, S // tq),
        in_specs=[pl.BlockSpec((None, g, tq, LANES), lambda b, p, i: (b, p, i, 0)),
                  pl.BlockSpec((None, g, S, LANES), lambda b, p, i: (b, p, 0, 0)),
                  pl.BlockSpec((None, g // 2, S // tk, LANES, tk), lambda b, p, i: (b, p, 0, 0, 0))],
        out_specs=pl.BlockSpec((None, tq, g * MLA_V), lambda b, p, i: (b, i, p)),
        out_shape=jax.ShapeDtypeStruct((B, S, H * MLA_V), BF16),
        scratch_shapes=[pltpu.VMEM((g, LANES, tq), BF16), pltpu.VMEM((g * MLA_ACC_ROWS, tq), F32),
                        ] + [pltpu.VMEM((1, tk, tq + SCORE_PAD), F32)] * (MLA_LOOKAHEAD + 1),
        compiler_params=pltpu.CompilerParams(dimension_semantics=("arbitrary",) * 3,
                                             vmem_limit_bytes=VMEM_LIMIT),
        name="mla",
    )(qm, km, vt)


def _swa_kernel(sink_ref, q_ref, kp_ref, kc_ref, vp_ref, vc_ref, bias0_ref, bias_ref, o_ref, *st_slots):
    band = 2 * WINDOW
    pairs = SWA_GROUP // 2
    kband = jnp.concatenate([kp_ref[...], kc_ref[...]], axis=0).astype(F32)
    vband_t = jnp.concatenate([vp_ref[...], vc_ref[...]], axis=0).astype(F32).T
    lo = lax.broadcasted_iota(jnp.int32, kband.shape, 1) < SWA_HEAD
    kswap = pltpu.roll(kband, SWA_HEAD, 1)
    k_even = [jnp.where(lo, kband, 0.0).astype(BF16), jnp.where(lo, kswap, 0.0).astype(BF16)]
    k_odd = [jnp.where(lo, 0.0, kswap).astype(BF16), jnp.where(lo, 0.0, kband).astype(BF16)]
    ones_rows = (lax.broadcasted_iota(jnp.int32, (16, band), 0) == 0).astype(BF16)
    chains = [(j, kvh) for j in range(SWA_STEP_BLOCKS) for kvh in range(SWA_KV_HEADS)]

    def scores(n):
        j, kvh = chains[n]
        keys = slice(j * WINDOW, j * WINDOW + band)
        kcat = jnp.concatenate([k_even[kvh][keys], k_odd[kvh][keys]], axis=0)
        qstack = jnp.concatenate([q_ref[j * WINDOW:(j + 1) * WINDOW, (kvh * pairs + jj) * LANES:
                                        (kvh * pairs + jj + 1) * LANES] for jj in range(pairs)], axis=0)
        bias = bias0_ref if j == 0 else bias_ref
        st = _dot_nt(kcat, qstack)
        for parity in range(2):
            for jj in range(pairs):
                rows, cols = slice(parity * band, (parity + 1) * band), slice(jj * WINDOW, (jj + 1) * WINDOW)
                st_slots[n % len(st_slots)][0, rows, cols] = st[rows, cols] + bias[kvh * SWA_GROUP + 2 * jj + parity]

    for n in range(min(SWA_LOOKAHEAD, len(chains))):
        scores(n)
    for n, (j, kvh) in enumerate(chains):
        if n + SWA_LOOKAHEAD < len(chains):
            scores(n + SWA_LOOKAHEAD)
        st = st_slots[n % len(st_slots)].at[jnp.minimum(pl.program_id(1), 0)]
        v_t = vband_t[kvh * SWA_HEAD:(kvh + 1) * SWA_HEAD, j * WINDOW:j * WINDOW + band].astype(BF16)
        v_aug = jnp.concatenate([v_t, ones_rows], axis=0)
        halves = []
        for parity in range(2):
            ps, sink_terms = [], []
            for jj in range(pairs):
                blk = st[parity * band:(parity + 1) * band, jj * WINDOW:(jj + 1) * WINDOW]
                sink = sink_ref[kvh * SWA_GROUP + 2 * jj + parity] * LOG2E
                m = jnp.maximum(jnp.max(blk, axis=0, keepdims=True), sink)
                ps.append(jnp.exp2(blk - m).astype(BF16))
                sink_terms.append(jnp.exp2(sink - m))
            pv = _dot(v_aug, jnp.concatenate(ps, axis=1))
            halves.append(pv[:SWA_HEAD] / (pv[SWA_HEAD:SWA_HEAD + 1] + jnp.concatenate(sink_terms, axis=1)))
        o_t = jnp.concatenate(halves, axis=0)
        for jj in range(pairs):
            pair = kvh * pairs + jj
            o_ref[j * WINDOW:(j + 1) * WINDOW, pair * LANES:(pair + 1) * LANES] = (
                o_t[:, jj * WINDOW:(jj + 1) * WINDOW].T.astype(BF16))


def _swa(sinks, qs, ks, vs, bias_tbl):
    B, S, W = qs.shape
    nb = SWA_STEP_BLOCKS
    prev = lambda b, n: (b, jnp.maximum(nb * n - 1, 0), 0)
    cur = lambda b, n: (b, n, 0)
    tbl_block = (None,) + bias_tbl.shape[1:]
    return pl.pallas_call(
        _swa_kernel,
        grid=(B, S // (nb * WINDOW)),
        in_specs=[pl.BlockSpec(memory_space=pltpu.SMEM),
                  pl.BlockSpec((None, nb * WINDOW, W), cur),
                  pl.BlockSpec((None, WINDOW, LANES), prev), pl.BlockSpec((None, nb * WINDOW, LANES), cur),
                  pl.BlockSpec((None, WINDOW, LANES), prev), pl.BlockSpec((None, nb * WINDOW, LANES), cur),
                  pl.BlockSpec(tbl_block, lambda b, n: (jnp.minimum(n, 1), 0, 0, 0)),
                  pl.BlockSpec(tbl_block, lambda b, n: (1, 0, 0, 0))],
        out_specs=pl.BlockSpec((None, nb * WINDOW, W), cur),
        out_shape=jax.ShapeDtypeStruct((B, S, W), BF16),
        scratch_shapes=[pltpu.VMEM((1, 4 * WINDOW, SWA_GROUP // 2 * WINDOW + SCORE_PAD), F32)] * (SWA_LOOKAHEAD + 1),
        compiler_params=pltpu.CompilerParams(dimension_semantics=("arbitrary", "arbitrary"),
                                             vmem_limit_bytes=VMEM_LIMIT),
        name="swa",
    )(sinks, qs, ks, ks, vs, vs, bias_tbl, bias_tbl)


def _tail_kernel(x_ref, ym_ref, ys_ref, sh1_ref, sc1_ref, ga1_ref, sh2_ref, sc2_ref, ga2_ref,
                 gmix_ref, gmlp_ref, gfin_ref, bg_ref, wg_ref, wom_ref, wos_ref, wo_ref, w1_ref, w2_ref, o_ref):
    subs = [slice(k * TAIL_SUB, (k + 1) * TAIL_SUB) for k in range(TM_TAIL // TAIL_SUB)]
    n_ff = D_FF // FF_CHUNK
    xs = [x_ref[r, :] for r in subs]
    branches = [(_dot(ym_ref[r, :], wom_ref[...]), _dot(ys_ref[r, :], wos_ref[...])) for r in subs]
    hs = [(_rms(x, gmix_ref[...]) * (1 + sc1_ref[...]) + sh1_ref[...]).astype(BF16) for x in xs]
    logits = [_dot_nt(h, wg_ref[...]) for h in hs]
    x1s = []
    for x, (a, b), gl in zip(xs, branches, logits):
        gates = jax.nn.sigmoid(gl + bg_ref[...])
        merged = gates[:, :D_MODEL] * a + gates[:, D_MODEL:] * b
        x1s.append(x + ga1_ref[...] * _dot(merged.astype(BF16), wo_ref[...]))
    h2s = [(_rms(x1, gmlp_ref[...]) * (1 + sc2_ref[...]) + sh2_ref[...]).astype(BF16) for x1 in x1s]

    jobs = [(k, c) for k in range(len(subs)) for c in range(n_ff)]
    up = lambda k, c: _dot(h2s[k], w1_ref[:, c * FF_CHUNK:(c + 1) * FF_CHUNK])
    pending = {0: up(*jobs[0])}
    accs = [None] * len(subs)
    for n, (k, c) in enumerate(jobs):
        if n + 1 < len(jobs):
            pending[n + 1] = up(*jobs[n + 1])
        u = jnp.square(jnp.maximum(pending.pop(n), 0.0)).astype(BF16)
        down = _dot(u, w2_ref[c * FF_CHUNK:(c + 1) * FF_CHUNK, :])
        accs[k] = down if accs[k] is None else accs[k] + down
        if c == n_ff - 1:
            o_ref[subs[k], :] = _rms(x1s[k] + ga2_ref[...] * accs[k], gfin_ref[...])


def _tail(x, ym, ys, mods, gmix, gmlp, gfin, bg, wg, wom, wos, wo, w1, w2):
    B, S, D = x.shape
    tm = TM_TAIL
    tok = lambda width: pl.BlockSpec((None, tm, width), lambda b, i: (b, i, 0))
    per_b = pl.BlockSpec((None, 1, D), lambda b, i: (b, 0, 0))
    const = lambda a: pl.BlockSpec(a.shape, lambda b, i: (0,) * a.ndim, pipeline_mode=pl.Buffered(1))
    return pl.pallas_call(
        _tail_kernel,
        grid=(B, S // tm),
        in_specs=[tok(D), tok(ym.shape[-1]), tok(ys.shape[-1])] + [per_b] * 6
                 + [const(a) for a in (gmix, gmlp, gfin, bg, wg, wom, wos, wo, w1, w2)],
        out_specs=tok(D),
        out_shape=jax.ShapeDtypeStruct((B, S, D), F32),
        compiler_params=pltpu.CompilerParams(dimension_semantics=("arbitrary", "arbitrary"),
                                             vmem_limit_bytes=VMEM_LIMIT),
        name="tail",
    )(x, ym, ys, *mods, gmix, gmlp, gfin, bg, wg, wom, wos, wo, w1, w2)


def _head_slab(nope, a, b):
    r, h, _ = nope.shape
    pad = jnp.zeros((r, h, LANES - MLA_NOPE - MLA_ROPE - HALF_ROPE), nope.dtype)
    return jnp.concatenate([nope, a, b, a, pad], axis=-1).reshape(r, h * LANES)


def _rope_spread():
    m = np.zeros((LANES, 2 * LANES), np.float32)
    for term in range(3):
        for j in range(HALF_ROPE):
            m[term * MLA_ROPE + j, [ROPE_A + j, ROPE_B + j]] = 1.0
            m[term * MLA_ROPE + HALF_ROPE + j, LANES + ROPE_A + j] = -1.0
            m[term * MLA_ROPE + HALF_ROPE + j, LANES + ROPE_B + j] = 1.0
    return jnp.asarray(m, BF16)


def kernel(x, c, positions, rel_bias, ada_w, ada_b, ln_mix_g, w_in, b_gate, mla_q_norm_g, mla_kv_norm_g,
           w_uq, w_ukv, swa_sinks, w_o_mla, w_o_swa, w_o, ln_mlp_g, w_ff1, w_ff2, ln_final_g):
    B, S, D = x.shape
    assert (B, S, D) == (x.shape[0], 4096, D_MODEL) and ada_w.shape[0] == 1
    l = 0

    c_pad = jnp.pad(c, ((0, 8 - B), (0, 0)))
    mod, bias_tbl = _adaln_bias(rel_bias, c_pad, ada_w, ada_b)
    mod = mod[:B]
    mods = [m[:, None, :] for m in jnp.split(mod, 6, axis=-1)]

    wi = w_in[l].T
    o_kv = MLA_Q_RANK
    o_kr = o_kv + MLA_KV_RANK
    o_qs = o_kr + MLA_ROPE
    o_ks = o_qs + SWA_Q_HEADS * SWA_HEAD
    o_vs = o_ks + SWA_KV_HEADS * SWA_HEAD
    o_ga = o_vs + SWA_KV_HEADS * SWA_HEAD
    kr_a, kr_b = wi[o_kr:o_kr + HALF_ROPE], wi[o_kr + HALF_ROPE:o_qs]
    z = lambda n: jnp.zeros((n, D), wi.dtype)
    w1 = jnp.concatenate([
        wi[:o_kr],
        z(ROPE_A), kr_a, kr_b, kr_a, z(LANES - ROPE_A - MLA_ROPE - HALF_ROPE),
        wi[o_qs:o_ga]], axis=0).astype(BF16)
    assert w1.shape[0] == N_PROJ

    uq = w_uq[l]
    q_nope, q_a, q_b = uq[..., :MLA_NOPE], uq[..., MLA_NOPE:MLA_NOPE + HALF_ROPE], uq[..., MLA_NOPE + HALF_ROPE:]
    wq = _head_slab(q_nope, q_a, q_b).astype(BF16)
    ukv = w_ukv[l]
    k_nope, v_up = ukv[..., :MLA_NOPE], ukv[..., MLA_NOPE:]
    zr = jnp.zeros(k_nope.shape[:2] + (HALF_ROPE,), ukv.dtype)
    wkv = jnp.concatenate([_head_slab(k_nope, zr, zr),
                           v_up.reshape(MLA_KV_RANK, MLA_HEADS * MLA_V)], axis=1).astype(BF16)

    inv = (ROPE_THETA ** (-jnp.arange(HALF_ROPE, dtype=F32) / HALF_ROPE))[:, None]

    whole = lambda w: (w, 0, w.shape[1])
    gate_rows = (wi[None], o_ga, 2 * D)
    qm, km, vm, qs, ks, vs, wom, wos, wo, wf1, wf2, wg = _proj(
        x, mods[0], mods[1], ln_mix_g[l][None, :], positions[:, None, :], inv, _rope_spread(), w1,
        mla_q_norm_g[l][None, :], mla_kv_norm_g[l][None, :], wq, wkv,
        [whole(w_o_mla), whole(w_o_swa), whole(w_o), whole(w_ff1), whole(w_ff2), gate_rows])

    y_mla = _mla(qm, km, vm)
    y_swa = _swa(swa_sinks[l], qs, ks, vs, bias_tbl)

    return _tail(x, y_mla, y_swa, mods, ln_mix_g[l][None, :], ln_mlp_g[l][None, :], ln_final_g[None, :],
                 b_gate[l][None, :], wg, wom, wos, wo, wf1, wf2)
```
